```python
import math
import jax, jax.numpy as jnp
from jax import lax
import numpy as np

D_MODEL = 1024
BATCH = 4
SEQ = 4096
DEPTH = 2

GRID_W = 64
GLA_HEADS = 4
GLA_DK = 64
GLA_DV = 128
GLA_RANK = 16
GLA_TAU = 16.0
GLA_CHUNK = 64
NA_HEADS = 8
NA_DH = 64
WIN_H = 8
WIN_W = 16
Q_BLOCK_W = 16
K_BLOCK_W = 32
N_EXPERTS = 32
N_GROUPS = 4
EXPERTS_PER_GROUP = N_EXPERTS // N_GROUPS
TOP_K = 2
D_EXPERT = 512
MOE_BLOCK = 128
GLA_QK_W = GLA_HEADS * GLA_DK
GLA_V_W = GLA_HEADS * GLA_DV
NA_W = NA_HEADS * NA_DH
MIX_W = GLA_V_W + NA_W
IN_W = 2 * GLA_QK_W + 2 * GLA_V_W + 2 * GLA_RANK + 3 * NA_W
EPS = 1e-6
NEG = -1e30

kernel_name = "hybrid_gla_natten_grouped_moe_encoder"


def _split_points():
    widths = [GLA_QK_W, GLA_QK_W, GLA_V_W, GLA_V_W, GLA_RANK, GLA_RANK, NA_W, NA_W, NA_W]
    pts, acc = [], 0
    for w in widths[:-1]:
        acc += w
        pts.append(acc)
    return pts


def rms_norm(x, g):
    xf = x.astype(jnp.float32)
    y = xf * lax.rsqrt(jnp.mean(xf * xf, axis=-1, keepdims=True) + EPS)
    return (y * g.astype(jnp.float32)).astype(x.dtype)


def _heads(t, n):
    b, s, _ = t.shape
    return t.reshape(b, s, n, -1).transpose(0, 2, 1, 3)


def gla_scan(q, k, v, log_a, strict):
    B, H, T, dk = q.shape
    dv = v.shape[-1]
    C = GLA_CHUNK
    nc = T // C

    def chunks(t):
        return jnp.moveaxis(t.reshape(B, H, nc, C, t.shape[-1]), 2, 0)

    qc, kc, vc, ac = chunks(q), chunks(k), chunks(v), chunks(log_a)
    bc = jnp.cumsum(ac, axis=-2)
    mask = jnp.tril(jnp.ones((C, C), dtype=bool), -1 if strict else 0)

    def step(S, inp):
        q_, k_, v_, b_ = inp
        diff = b_[:, :, :, None, :] - b_[:, :, None, :, :]
        decay = jnp.exp(jnp.where(mask[:, :, None], diff, -jnp.inf))
        scores = jnp.sum(q_[:, :, :, None, :] * k_[:, :, None, :, :] * decay, axis=-1)
        o = (jnp.einsum('bhij,bhje->bhie', scores, v_)
             + jnp.einsum('bhid,bhde->bhie', q_ * jnp.exp(b_), S))
        b_last = b_[:, :, -1:, :]
        S_new = (jnp.exp(b_last[:, :, 0, :])[..., None] * S
                 + jnp.einsum('bhjd,bhje->bhde', k_ * jnp.exp(b_last - b_), v_))
        return S_new, o

    S0 = jnp.zeros((B, H, dk, dv), jnp.float32)
    _, o = lax.scan(step, S0, (qc, kc, vc, bc))
    return jnp.moveaxis(o, 0, 2).reshape(B, H, T, dv)


def _na_indices(rows):
    kh = min(WIN_H, rows)
    r = np.arange(rows)
    row_start = np.clip(r - kh // 2, 0, rows - kh)
    key_rows = row_start[:, None] + np.arange(kh)
    n_cb = GRID_W // Q_BLOCK_W
    q_cols = np.arange(n_cb)[:, None] * Q_BLOCK_W + np.arange(Q_BLOCK_W)
    col_start = np.clip(q_cols - WIN_W // 2, 0, GRID_W - WIN_W)
    blk_start = np.clip(np.arange(n_cb) * Q_BLOCK_W - WIN_W // 2, 0, GRID_W - K_BLOCK_W)
    key_cols = blk_start[:, None] + np.arange(K_BLOCK_W)
    col_ok = ((key_cols[:, None, :] >= col_start[:, :, None])
              & (key_cols[:, None, :] < col_start[:, :, None] + WIN_W))
    L = kh * K_BLOCK_W
    tok = (key_rows[:, None, :, None] * GRID_W + key_cols[None, :, None, :]).reshape(rows, n_cb, L)
    mask = np.broadcast_to(col_ok[:, :, None, :], (n_cb, Q_BLOCK_W, kh, K_BLOCK_W)).reshape(n_cb, Q_BLOCK_W, L)
    rel_r = key_rows - r[:, None] + WIN_H - 1
    rel_c = np.clip(key_cols[:, None, :] - q_cols[:, :, None] + WIN_W - 1, 0, 2 * WIN_W - 2)
    full = (rows, n_cb, Q_BLOCK_W, kh, K_BLOCK_W)
    ri = np.broadcast_to(rel_r[:, None, None, :, None], full).reshape(rows, n_cb, Q_BLOCK_W, L)
    ci = np.broadcast_to(rel_c[None, :, :, None, :], full).reshape(rows, n_cb, Q_BLOCK_W, L)
    return tok, mask, ri, ci


def neighborhood_attention(q, k, v, rpb):
    B, H, T, dh = q.shape
    rows = T // GRID_W
    tok, mask, ri, ci = _na_indices(rows)
    n_cb = GRID_W // Q_BLOCK_W
    qb = q.reshape(B, H, rows, n_cb, Q_BLOCK_W, dh)
    kg = k[:, :, tok]
    vg = v[:, :, tok]
    s = jnp.einsum('bhrnqd,bhrnkd->bhrnqk', qb, kg).astype(jnp.float32) * (dh ** -0.5)
    bias = rpb.astype(jnp.float32)[:, ri, ci]
    s = jnp.where(mask[None, None, None], s + bias[None], NEG)
    p = jax.nn.softmax(s, axis=-1)
    o = jnp.einsum('bhrnqk,bhrnkd->bhrnqd', p.astype(v.dtype), vg)
    return o.reshape(B, H, T, dh)


def mixer(h, w_in, gate_w, gate_b, gla_norm, q_norm, k_norm, rpb, w_out):
    B, T, _ = h.shape
    f32 = jnp.float32
    proj = h @ w_in
    qg, kg, vg, gg, lf, lb, qn, kn, vn = jnp.split(proj, _split_points(), axis=-1)
    q = _heads(qg, GLA_HEADS).astype(f32) * (GLA_DK ** -0.5)
    k = _heads(kg, GLA_HEADS).astype(f32)
    v = _heads(vg, GLA_HEADS).astype(f32)
    la_f = _heads(jax.nn.log_sigmoid((lf @ gate_w[0] + gate_b[0]).astype(f32)) / GLA_TAU, GLA_HEADS)
    la_b = _heads(jax.nn.log_sigmoid((lb @ gate_w[1] + gate_b[1]).astype(f32)) / GLA_TAU, GLA_HEADS)
    o_f = gla_scan(q, k, v, la_f, False)
    flip = lambda t: jnp.flip(t, axis=2)
    o_b = flip(gla_scan(flip(q), flip(k), flip(v), flip(la_b), True))
    o_gla = rms_norm(o_f + o_b, gla_norm)
    o_gla = o_gla.transpose(0, 2, 1, 3).reshape(B, T, GLA_V_W) * jax.nn.silu(gg.astype(f32))
    qh = rms_norm(_heads(qn, NA_HEADS), q_norm)
    kh = rms_norm(_heads(kn, NA_HEADS), k_norm)
    vh = _heads(vn, NA_HEADS)
    o_na = neighborhood_attention(qh, kh, vh, rpb).transpose(0, 2, 1, 3).reshape(B, T, NA_W)
    cat = jnp.concatenate([o_gla.astype(h.dtype), o_na.astype(h.dtype)], axis=-1)
    return cat @ w_out


def grouped_moe(h, router_w, router_b, w1, w3, w2):
    N, D = h.shape
    f32 = jnp.float32
    scores = jax.nn.sigmoid(h.astype(f32) @ router_w.astype(f32))
    sel = scores + router_b.astype(f32)
    grp = sel.reshape(N, N_GROUPS, EXPERTS_PER_GROUP)
    group_score = lax.top_k(grp, TOP_K)[0].sum(-1)
    best = jnp.argmax(group_score, axis=-1)
    in_grp = jnp.take_along_axis(grp, best[:, None, None], axis=1)[:, 0]
    _, local = lax.top_k(in_grp, TOP_K)
    eidx = best[:, None] * EXPERTS_PER_GROUP + local
    gw = jnp.take_along_axis(scores, eidx, axis=1)
    gw = gw / jnp.sum(gw, axis=-1, keepdims=True)
    M = N * TOP_K
    flat_e = eidx.reshape(-1)
    flat_tok = jnp.repeat(jnp.arange(N, dtype=jnp.int32), TOP_K)
    order = jnp.argsort(flat_e)
    sorted_e = flat_e[order]
    sorted_tok = flat_tok[order]
    sorted_w = gw.reshape(-1)[order]
    counts = jnp.bincount(flat_e, length=N_EXPERTS)
    padded = (counts + MOE_BLOCK - 1) // MOE_BLOCK * MOE_BLOCK
    pad_end = jnp.cumsum(padded)
    pad_start = pad_end - padded
    start = jnp.cumsum(counts) - counts
    dest = pad_start[sorted_e] + (jnp.arange(M) - start[sorted_e])
    P = (-(-M // MOE_BLOCK) + N_EXPERTS) * MOE_BLOCK
    NB = P // MOE_BLOCK
    buf_tok = jnp.full((P,), N, jnp.int32).at[dest].set(sorted_tok)
    h_pad = jnp.concatenate([h, jnp.zeros((1, D), h.dtype)], axis=0)
    xb = h_pad[buf_tok].reshape(NB, MOE_BLOCK, D)
    blk_e = jnp.minimum(jnp.searchsorted(pad_end, jnp.arange(NB) * MOE_BLOCK, side='right'), N_EXPERTS - 1)

    def expert_block(args):
        xblk, e = args
        return (jax.nn.silu(xblk @ w1[e]) * (xblk @ w3[e])) @ w2[e]

    y = lax.map(expert_block, (xb, blk_e)).reshape(P, D)
    y_assign = y[dest].astype(f32) * sorted_w[:, None]
    out = jax.ops.segment_sum(y_assign, sorted_tok, num_segments=N)
    return out.astype(h.dtype)


def setup_inputs(seed: int = 0) -> dict:
    key = jax.random.key(seed)
    ks = jax.random.split(key, 20)
    nrm = lambda k, s: jax.random.normal(k, s, jnp.float32)
    D = D_MODEL
    return {
        "x": nrm(ks[0], (BATCH, SEQ, D)),
        "c": nrm(ks[1], (BATCH, D)),
        "w_ada": nrm(ks[2], (DEPTH, D, 6 * D)) * (0.5 * D ** -0.5),
        "b_ada": nrm(ks[3], (DEPTH, 6 * D)) * 0.02,
        "attn_norm": 1.0 + 0.05 * nrm(ks[4], (DEPTH, D)),
        "ffn_norm": 1.0 + 0.05 * nrm(ks[5], (DEPTH, D)),
        "w_in": nrm(ks[6], (DEPTH, D, IN_W)) * D ** -0.5,
        "gla_gate_w": nrm(ks[7], (DEPTH, 2, GLA_RANK, GLA_QK_W)) * GLA_RANK ** -0.5,
        "gla_gate_b": 0.1 * nrm(ks[8], (DEPTH, 2, GLA_QK_W)),
        "gla_out_norm": 1.0 + 0.05 * nrm(ks[9], (DEPTH, GLA_DV)),
        "na_q_norm": 1.0 + 0.05 * nrm(ks[10], (DEPTH, NA_DH)),
        "na_k_norm": 1.0 + 0.05 * nrm(ks[11], (DEPTH, NA_DH)),
        "na_rpb": 0.02 * nrm(ks[12], (DEPTH, NA_HEADS, 2 * WIN_H - 1, 2 * WIN_W - 1)),
        "w_out": nrm(ks[13], (DEPTH, MIX_W, D)) * MIX_W ** -0.5,
        "router_w": nrm(ks[14], (D, N_EXPERTS)) * D ** -0.5,
        "router_b": 0.01 * nrm(ks[15], (N_EXPERTS,)),
        "w1": nrm(ks[16], (DEPTH, N_EXPERTS, D, D_EXPERT)) * D ** -0.5,
        "w3": nrm(ks[17], (DEPTH, N_EXPERTS, D, D_EXPERT)) * D ** -0.5,
        "w2": nrm(ks[18], (DEPTH, N_EXPERTS, D_EXPERT, D)) * D_EXPERT ** -0.5,
    }


def reference(x, c, w_ada, b_ada, attn_norm, ffn_norm, w_in, gla_gate_w, gla_gate_b,
              gla_out_norm, na_q_norm, na_k_norm, na_rpb, w_out, router_w, router_b,
              w1, w3, w2):
    B, T, D = x.shape
    c_act = jax.nn.silu(c)
    for l in range(DEPTH):
        mod = c_act @ w_ada[l] + b_ada[l]
        sh1, sc1, g1, sh2, sc2, g2 = jnp.split(mod, 6, axis=-1)
        h = rms_norm(x, attn_norm[l]) * (1.0 + sc1[:, None]) + sh1[:, None]
        x = x + g1[:, None] * mixer(h, w_in[l], gla_gate_w[l], gla_gate_b[l], gla_out_norm[l],
                                    na_q_norm[l], na_k_norm[l], na_rpb[l], w_out[l])
        h = rms_norm(x, ffn_norm[l]) * (1.0 + sc2[:, None]) + sh2[:, None]
        y = grouped_moe(h.reshape(B * T, D), router_w, router_b, w1[l], w3[l], w2[l])
        x = x + g2[:, None] * y.reshape(B, T, D)
    return x
```

```python
import functools

import numpy as np
import jax
import jax.numpy as jnp
from jax import lax
from jax.experimental import pallas as pl
from jax.experimental.pallas import tpu as pltpu

f32 = jnp.float32
bf16 = jnp.bfloat16
i32 = jnp.int32
HIGHEST = lax.Precision.HIGHEST

D = 1024
GRID_W = 64
GLA_H, GLA_DK, GLA_DV = 4, 64, 128
GLA_RANK = 16
GLA_TAU = 16.0
CHUNK = 64
NA_H, NA_DH = 8, 64
WIN_H, WIN_W = 8, 16
N_EXP, N_GRP, EPG = 32, 4, 8
D_EXP = 512
QK_W = GLA_H * GLA_DK
V_W = GLA_H * GLA_DV
NA_W = NA_H * NA_DH
EPS = 1e-6
NEG = -1e30

TM_PROJ = 512
TB_GLA = 256
NA_QROWS = 4
NA_KROWS = 12
TM_RANK = 512
TM_DISP = 256
TM_COMB = 128
MOE_BLK = 256
N_LEVELS = 6


def _nt(a, b):
    return lax.dot_general(a, b, (((1,), (1,)), ((), ())), preferred_element_type=f32)


def _tn(a, b):
    return lax.dot_general(a, b, (((0,), (0,)), ((), ())), preferred_element_type=f32)


def _split3(x):
    hi = x.astype(bf16)
    r = x - hi.astype(f32)
    mid = r.astype(bf16)
    lo = (r - mid.astype(f32)).astype(bf16)
    return hi, mid, lo


def _sel_l(m01, x):
    hi, mid, lo = _split3(x)
    d = lambda p: jnp.dot(m01, p, preferred_element_type=f32)
    return d(hi) + d(mid) + d(lo)


def _sel_r(x, m01):
    hi, mid, lo = _split3(x)
    d = lambda p: jnp.dot(p, m01, preferred_element_type=f32)
    return d(hi) + d(mid) + d(lo)


def _mod_kernel(c_ref, w_ref, b_ref, o_ref):
    c = c_ref[...]
    ca = c * jax.nn.sigmoid(c)
    o_ref[0, 0] = jnp.dot(ca, w_ref[0], precision=HIGHEST, preferred_element_type=f32) + b_ref[0, 0]


def _adaln_mod(c, w_ada, b_ada):
    depth = w_ada.shape[0]
    bsz = c.shape[0]
    cp = jnp.zeros((8, D), f32).at[:bsz].set(c)
    out = pl.pallas_call(
        _mod_kernel,
        grid=(depth, 6),
        in_specs=[pl.BlockSpec((8, D), lambda l, j: (0, 0)),
                  pl.BlockSpec((1, D, D), lambda l, j: (l, 0, j)),
                  pl.BlockSpec((1, 1, 1, D), lambda l, j: (l, j, 0, 0))],
        out_specs=pl.BlockSpec((1, 1, 8, D), lambda l, j: (l, j, 0, 0)),
        out_shape=jax.ShapeDtypeStruct((depth, 6, 8, D), f32),
        name="adaln_mod",
    )(cp, w_ada, b_ada.reshape(depth, 6, 1, D))
    return out[:, :, :bsz].reshape(depth, 6, bsz, 1, D)


def _log_sigmoid(z):
    return jnp.minimum(z, 0.0) - jnp.log1p(jnp.exp(-jnp.abs(z)))


def _inproj_kernel(x_ref, mod_ref, nrm_ref, wm_ref, wlr_ref, gf_ref, gb_ref, bf_ref, bb_ref,
                   qg_ref, kg_ref, vg_ref, sg_ref, laf_ref, lab_ref, qn_ref, kn_ref, vn_ref):
    x = x_ref[...]
    y = x * lax.rsqrt(jnp.mean(x * x, axis=-1, keepdims=True) + EPS) * nrm_ref[...]
    h = y * (1.0 + mod_ref[1]) + mod_ref[0]
    hb = h.astype(bf16)
    main = jnp.dot(hb, wm_ref[...], preferred_element_type=f32)
    qg_ref[...] = (main[:, 0:256] * (GLA_DK ** -0.5)).astype(bf16)
    kg_ref[...] = main[:, 256:512].astype(bf16)
    vg_ref[...] = main[:, 512:1024].astype(bf16)
    gg = main[:, 1024:1536]
    sg_ref[...] = (gg * jax.nn.sigmoid(gg)).astype(bf16)
    qn_ref[...] = main[:, 1536:2048].astype(bf16)
    kn_ref[...] = main[:, 2048:2560].astype(bf16)
    vn_ref[...] = main[:, 2560:3072].astype(bf16)
    lr = jnp.dot(hb, wlr_ref[...], preferred_element_type=f32)
    zf = jnp.dot(lr, gf_ref[...], precision=HIGHEST, preferred_element_type=f32) + bf_ref[...]
    zb = jnp.dot(lr, gb_ref[...], precision=HIGHEST, preferred_element_type=f32) + bb_ref[...]
    laf_ref[...] = _log_sigmoid(zf) * (1.0 / GLA_TAU)
    lab_ref[...] = _log_sigmoid(zb) * (1.0 / GLA_TAU)


def _inproj(x2d, mod_l, nrm, w_in, gate_w, gate_b, bsz, seq):
    n = x2d.shape[0]
    tm = TM_PROJ
    nt = seq // tm
    lo, hi = 2 * QK_W + 2 * V_W, 2 * QK_W + 2 * V_W + 2 * GLA_RANK
    wm = jnp.concatenate([w_in[:, :lo], w_in[:, hi:]], axis=1).astype(bf16)
    wlr = w_in[:, lo:hi].astype(bf16)
    zpad = jnp.zeros((GLA_RANK, QK_W), f32)
    gf = jnp.concatenate([gate_w[0], zpad], axis=0)
    gb = jnp.concatenate([zpad, gate_w[1]], axis=0)
    row = lambda b, i: (b * nt + i, 0)
    const = lambda b, i: (0, 0)
    widths = [(QK_W, bf16), (QK_W, bf16), (V_W, bf16), (V_W, bf16), (QK_W, f32), (QK_W, f32),
              (NA_W, bf16), (NA_W, bf16), (NA_W, bf16)]
    return pl.pallas_call(
        _inproj_kernel,
        grid=(bsz, nt),
        in_specs=[pl.BlockSpec((tm, D), row),
                  pl.BlockSpec((6, None, 1, D), lambda b, i: (0, b, 0, 0)),
                  pl.BlockSpec((1, D), const),
                  pl.BlockSpec(wm.shape, const),
                  pl.BlockSpec(wlr.shape, const),
                  pl.BlockSpec(gf.shape, const),
                  pl.BlockSpec(gb.shape, const),
                  pl.BlockSpec((1, QK_W), const),
                  pl.BlockSpec((1, QK_W), const)],
        out_specs=[pl.BlockSpec((tm, w), row) for w, _ in widths],
        out_shape=[jax.ShapeDtypeStruct((n, w), dt) for w, dt in widths],
        compiler_params=pltpu.CompilerParams(dimension_semantics=("arbitrary", "arbitrary")),
        name="inproj",
    )(x2d, mod_l, nrm.reshape(1, D), wm, wlr, gf, gb, gate_b[0].reshape(1, QK_W), gate_b[1].reshape(1, QK_W))


def _gla_consts(rev, tb):
    c = CHUNK
    idx = np.arange(c)
    if not rev:
        cum = (idx[None, :] <= idx[:, None]).astype(np.float32)
    else:
        cum = (idx[None, :] >= idx[:, None]).astype(np.float32)
    mats = [cum]
    masks = []
    for lvl in range(N_LEVELS):
        s = c >> (lvl + 1)
        blk = idx // (2 * s)
        second = (idx % (2 * s)) >= s
        ref_row = blk * 2 * s + (s if rev else s - 1)
        mats.append(cum[ref_row])
        same = blk[:, None] == blk[None, :]
        if not rev:
            m = same & second[:, None] & (~second[None, :])
        else:
            m = same & (~second[:, None]) & second[None, :]
        masks.append(m)
    masks.append(np.eye(c, dtype=bool) & (not rev))
    mst = np.concatenate(mats, axis=0)
    msk = np.stack([np.tile(m, (1, GLA_H)) for m in masks]).astype(np.float32)
    nch = tb // c
    selb = np.zeros((tb, nch * 128), np.float32)
    for ch in range(nch):
        selb[ch * c:(ch + 1) * c, ch * 128:(ch + 1) * 128] = 1.0
    hk = np.arange(QK_W) // GLA_DK
    hv = np.arange(V_W) // GLA_DV
    kmask = (hk[:, None] == hk[None, :]).astype(np.float32)
    vmask = (hk[:, None] == hv[None, :]).astype(np.float32)
    return (jnp.asarray(mst, bf16), jnp.asarray(msk, f32), jnp.asarray(selb, bf16),
            jnp.asarray(kmask, bf16), jnp.asarray(vmask, bf16), jnp.asarray(vmask, f32))


def _gla_kernel(rev, tb, *refs):
    if rev:
        (q_ref, k_ref, v_ref, la_ref, mst_ref, msk_ref, selb_ref, kmask_ref, vmask_ref, smask_ref,
         of_ref, sg_ref, gn_ref, out_ref, s_scr, o_scr) = refs
    else:
        (q_ref, k_ref, v_ref, la_ref, mst_ref, msk_ref, selb_ref, kmask_ref, vmask_ref, smask_ref,
         out_ref, s_scr) = refs
        o_scr = out_ref
    c = CHUNK
    nch = tb // c

    @pl.when(pl.program_id(1) == 0)
    def _():
        s_scr[...] = jnp.zeros_like(s_scr)

    tot = _sel_r(la_ref[...].T, selb_ref[...])
    mst = mst_ref[...]
    kmask = kmask_ref[...]
    vmask = vmask_ref[...]
    order = range(nch - 1, -1, -1) if rev else range(nch)
    for ch in order:
        rows = slice(ch * c, (ch + 1) * c)
        q = q_ref[rows, :].astype(f32)
        k = k_ref[rows, :].astype(f32)
        v = v_ref[rows, :]
        r_all = _sel_l(mst, la_ref[rows, :])
        bcum = r_all[0:c]
        qe = (q * jnp.exp(bcum)).astype(bf16)
        o = jnp.dot(qe, s_scr[...].astype(bf16), preferred_element_type=f32)
        sall = jnp.zeros((c, QK_W), f32)
        for lvl in range(N_LEVELS):
            ref = r_all[(lvl + 1) * c:(lvl + 2) * c]
            a = (q * jnp.exp(jnp.minimum(bcum - ref, 0.0))).astype(bf16)
            bm = (k * jnp.exp(jnp.minimum(ref - bcum, 0.0))).astype(bf16)
            bbd = jnp.concatenate([bm] * GLA_H, axis=0) * kmask
            sall = sall + jnp.where(msk_ref[lvl] > 0.0, _nt(a, bbd), 0.0)
        if not rev:
            bbd = jnp.concatenate([k_ref[rows, :]] * GLA_H, axis=0) * kmask
            sall = sall + jnp.where(msk_ref[N_LEVELS] > 0.0, _nt(q_ref[rows, :], bbd), 0.0)
        vbd = jnp.concatenate([v] * GLA_H, axis=0) * vmask
        o = o + jnp.dot(sall.astype(bf16), vbd, preferred_element_type=f32)
        o_scr[rows, :] = o
        blast = bcum[0:1] if rev else bcum[c - 1:c]
        ke = (k * jnp.exp(blast - bcum)).astype(bf16)
        dec = jnp.exp(tot[:, ch * 128:(ch + 1) * 128])
        s_scr[...] = (s_scr[...] * jnp.concatenate([dec] * GLA_H, axis=1)
                      + _tn(ke, v) * smask_ref[...])

    if rev:
        gn = gn_ref[...]
        for hd in range(GLA_H):
            cols = slice(hd * GLA_DV, (hd + 1) * GLA_DV)
            t = of_ref[:, cols] + o_scr[:, cols]
            y = t * lax.rsqrt(jnp.mean(t * t, axis=-1, keepdims=True) + EPS) * gn
            out_ref[:, cols] = (y * sg_ref[:, cols].astype(f32)).astype(bf16)


def _gla_dir(rev, q, k, v, la, bsz, seq, extra=None):
    n = q.shape[0]
    tb = TB_GLA
    nb = seq // tb
    consts = _gla_consts(rev, tb)
    if rev:
        row = lambda b, i: (b * nb + nb - 1 - i, 0)
    else:
        row = lambda b, i: (b * nb + i, 0)
    const2 = lambda b, i: (0, 0)
    const3 = lambda b, i: (0, 0, 0)
    in_specs = [pl.BlockSpec((tb, QK_W), row), pl.BlockSpec((tb, QK_W), row),
                pl.BlockSpec((tb, V_W), row), pl.BlockSpec((tb, QK_W), row),
                pl.BlockSpec(consts[0].shape, const2), pl.BlockSpec(consts[1].shape, const3),
                pl.BlockSpec(consts[2].shape, const2), pl.BlockSpec(consts[3].shape, const2),
                pl.BlockSpec(consts[4].shape, const2), pl.BlockSpec(consts[5].shape, const2)]
    args = [q, k, v, la, *consts]
    scratch = [pltpu.VMEM((QK_W, V_W), f32)]
    if rev:
        o_f, sg, gn = extra
        in_specs += [pl.BlockSpec((tb, V_W), row), pl.BlockSpec((tb, V_W), row),
                     pl.BlockSpec((1, GLA_DV), const2)]
        args += [o_f, sg, gn.reshape(1, GLA_DV)]
        scratch.append(pltpu.VMEM((tb, V_W), f32))
        out_dtype = bf16
    else:
        out_dtype = f32
    return pl.pallas_call(
        functools.partial(_gla_kernel, rev, tb),
        grid=(bsz, nb),
        in_specs=in_specs,
        out_specs=pl.BlockSpec((tb, V_W), row),
        out_shape=jax.ShapeDtypeStruct((n, V_W), out_dtype),
        scratch_shapes=scratch,
        compiler_params=pltpu.CompilerParams(dimension_semantics=("arbitrary", "arbitrary")),
        name="gla_bwd" if rev else "gla_fwd",
    )(*args)


def _rpb_expand_kernel(rpb_ref, sel_ref, valid_ref, o_ref):
    e = _sel_r(rpb_ref[...], sel_ref[...])
    o_ref[...] = jnp.where(valid_ref[...] > 0.0, e, NEG)


def _na_bias_table(rpb):
    nri, nci = 2 * WIN_H - 1, 2 * WIN_W - 1
    qc = np.arange(GRID_W)
    col_start = np.clip(qc - WIN_W // 2, 0, GRID_W - WIN_W)
    kc = np.arange(GRID_W)
    valid = (kc[None, :] >= col_start[:, None]) & (kc[None, :] < col_start[:, None] + WIN_W)
    ci = kc[None, :] - qc[:, None] + WIN_W - 1
    sel = np.zeros((32, GRID_W * GRID_W), np.float32)
    flat_ci = ci.reshape(-1)
    ok = (flat_ci >= 0) & (flat_ci < nci)
    sel[flat_ci[ok], np.nonzero(ok)[0]] = 1.0
    rpb2 = jnp.zeros((NA_H * nri + 8 - (NA_H * nri) % 8, 32), f32).at[:NA_H * nri, :nci].set(
        rpb.reshape(NA_H * nri, nci))
    nrow = rpb2.shape[0]
    e = pl.pallas_call(
        _rpb_expand_kernel,
        out_shape=jax.ShapeDtypeStruct((nrow, GRID_W * GRID_W), f32),
        name="rpb_expand",
    )(rpb2, jnp.asarray(sel, bf16), jnp.asarray(valid.reshape(1, -1), f32))
    e4 = e[:NA_H * nri].reshape(NA_H, nri, GRID_W, GRID_W)
    rows = GRID_W
    ri_idx = np.zeros((3, NA_QROWS, NA_KROWS), np.int32)
    ri_ok = np.zeros((3, NA_QROWS, NA_KROWS), bool)
    for cls, rb in enumerate((0, 1, rows // NA_QROWS - 1)):
        r0 = rb * NA_QROWS
        kb = int(np.clip(r0 - WIN_H // 2, 0, rows - NA_KROWS))
        for rq in range(NA_QROWS):
            r = r0 + rq
            rs = int(np.clip(r - WIN_H // 2, 0, rows - WIN_H))
            for m in range(NA_KROWS):
                kr = kb + m
                if rs <= kr < rs + WIN_H:
                    ri_ok[cls, rq, m] = True
                    ri_idx[cls, rq, m] = kr - r + WIN_H - 1
    g = e4[:, ri_idx.reshape(-1)]
    g = jnp.where(jnp.asarray(ri_ok.reshape(1, -1, 1, 1)), g, NEG)
    g = g.reshape(NA_H, 3, NA_QROWS, NA_KROWS, GRID_W, GRID_W).transpose(0, 1, 2, 4, 3, 5)
    return g.reshape(NA_H // 2, 2, 3, NA_QROWS * GRID_W, NA_KROWS * GRID_W)


def _natten_kernel(q_ref, k_ref, v_ref, qn_ref, kn_ref, gsum_ref, tbl_ref, o_ref, kn_scr):
    rb = pl.program_id(2)
    nrb = pl.num_programs(2)
    gsum = gsum_ref[...]

    def headnorm(x, g):
        ms = _sel_r(x * x, gsum) * (1.0 / NA_DH)
        return x * lax.rsqrt(ms + EPS) * g

    @pl.when(rb == 0)
    def _():
        kn_scr[...] = headnorm(k_ref[...].astype(f32), kn_ref[...]).astype(bf16)

    q = (headnorm(q_ref[...].astype(f32), qn_ref[...]) * (NA_DH ** -0.5))
    kb = jnp.clip(rb * NA_QROWS - WIN_H // 2, 0, GRID_W - NA_KROWS)
    start = pl.multiple_of(kb * GRID_W, GRID_W)
    nk = NA_KROWS * GRID_W
    kwin = kn_scr[pl.ds(start, nk), :]
    vwin = v_ref[pl.ds(start, nk), :]
    cls = jnp.where(rb == 0, 0, jnp.where(rb == nrb - 1, 2, 1))
    lane = lax.broadcasted_iota(i32, q.shape, 1)
    out = jnp.zeros(q.shape, f32)
    for hh in range(2):
        sel = (lane >= hh * NA_DH) & (lane < (hh + 1) * NA_DH)
        qh = jnp.where(sel, q, 0.0).astype(bf16)
        s = _nt(qh, kwin) + tbl_ref[hh, cls]
        m = jnp.max(s, axis=-1, keepdims=True)
        p = jnp.exp(s - m)
        l = jnp.sum(p, axis=-1, keepdims=True)
        o = jnp.dot(p.astype(bf16), vwin, preferred_element_type=f32) / l
        out = jnp.where(sel, o, out)
    o_ref[...] = out.astype(bf16)


def _natten(qn, kn, vn, q_norm, k_norm, rpb, bsz, seq):
    n = qn.shape[0]
    tbl = _na_bias_table(rpb)
    nq = NA_QROWS * GRID_W
    nrb = seq // nq
    lane_h = np.arange(128) // NA_DH
    gsum = jnp.asarray(lane_h[:, None] == lane_h[None, :], bf16)
    qn2 = jnp.tile(q_norm.reshape(1, NA_DH), (1, 2))
    kn2 = jnp.tile(k_norm.reshape(1, NA_DH), (1, 2))
    const2 = lambda b, p, r: (0, 0)
    return pl.pallas_call(
        _natten_kernel,
        grid=(bsz, NA_H // 2, nrb),
        in_specs=[pl.BlockSpec((nq, 128), lambda b, p, r: (b * nrb + r, p)),
                  pl.BlockSpec((seq, 128), lambda b, p, r: (b, p)),
                  pl.BlockSpec((seq, 128), lambda b, p, r: (b, p)),
                  pl.BlockSpec((1, 128), const2),
                  pl.BlockSpec((1, 128), const2),
                  pl.BlockSpec((128, 128), const2),
                  pl.BlockSpec((None,) + tbl.shape[1:], lambda b, p, r: (p, 0, 0, 0, 0))],
        out_specs=pl.BlockSpec((nq, 128), lambda b, p, r: (b * nrb + r, p)),
        out_shape=jax.ShapeDtypeStruct((n, NA_W), bf16),
        scratch_shapes=[pltpu.VMEM((seq, 128), bf16)],
        compiler_params=pltpu.CompilerParams(dimension_semantics=("arbitrary", "arbitrary", "arbitrary")),
        name="natten",
    )(qn, kn, vn, qn2, kn2, gsum, tbl)


def _top2(vals):
    io = lax.broadcasted_iota(i32, vals.shape, 0)
    m1 = jnp.max(vals, axis=0, keepdims=True)
    i1 = jnp.min(jnp.where(vals == m1, io, EPG), axis=0, keepdims=True)
    v2 = jnp.where(io == i1, -jnp.inf, vals)
    m2 = jnp.max(v2, axis=0, keepdims=True)
    i2 = jnp.min(jnp.where(v2 == m2, io, EPG), axis=0, keepdims=True)
    return m1, i1, m2, i2


def _outproj_kernel(og_ref, on_ref, wo_ref, x_ref, mod_ref, nrm_ref, rw_ref, rb_ref,
                    x1_ref, h2_ref, eidx_ref, gw_ref):
    mix = (jnp.dot(og_ref[...], wo_ref[0:V_W, :], preferred_element_type=f32)
           + jnp.dot(on_ref[...], wo_ref[V_W:V_W + NA_W, :], preferred_element_type=f32))
    x1 = x_ref[...] + mod_ref[2] * mix
    x1_ref[...] = x1
    y = x1 * lax.rsqrt(jnp.mean(x1 * x1, axis=-1, keepdims=True) + EPS) * nrm_ref[...]
    h2 = y * (1.0 + mod_ref[4]) + mod_ref[3]
    h2_ref[...] = h2
    rw = rw_ref[...]
    rw_hi = rw.astype(bf16)
    rw_lo = (rw - rw_hi.astype(f32)).astype(bf16)
    h_hi = h2.astype(bf16)
    h_lo = (h2 - h_hi.astype(f32)).astype(bf16)
    logits = _nt(rw_hi, h_hi) + _nt(rw_hi, h_lo) + _nt(rw_lo, h_hi)
    scores = jax.nn.sigmoid(logits)
    sel = scores + rb_ref[...]
    tops = [_top2(sel[g * EPG:(g + 1) * EPG]) for g in range(N_GRP)]
    best = jnp.zeros_like(tops[0][1])
    bs = tops[0][0] + tops[0][2]
    for g in range(1, N_GRP):
        gs = tops[g][0] + tops[g][2]
        take = gs > bs
        best = jnp.where(take, g, best)
        bs = jnp.where(take, gs, bs)
    io = lax.broadcasted_iota(i32, (EPG, sel.shape[1]), 0)
    i1 = jnp.zeros_like(best)
    i2 = jnp.zeros_like(best)
    s1 = jnp.zeros(best.shape, f32)
    s2 = jnp.zeros(best.shape, f32)
    for g in range(N_GRP):
        sc = scores[g * EPG:(g + 1) * EPG]
        pick = best == g
        i1 = jnp.where(pick, tops[g][1], i1)
        i2 = jnp.where(pick, tops[g][3], i2)
        s1 = jnp.where(pick, jnp.sum(jnp.where(io == tops[g][1], sc, 0.0), axis=0, keepdims=True), s1)
        s2 = jnp.where(pick, jnp.sum(jnp.where(io == tops[g][3], sc, 0.0), axis=0, keepdims=True), s2)
    eidx_ref[0:1, :] = best * EPG + i1
    eidx_ref[1:2, :] = best * EPG + i2
    tot = s1 + s2
    gw_ref[0:1, :] = s1 / tot
    gw_ref[1:2, :] = s2 / tot


def _outproj(o_gla, o_na, w_out, x2d, mod_l, nrm, router_w, router_b, bsz, seq):
    n = x2d.shape[0]
    tm = TM_PROJ
    nt = seq // tm
    row = lambda b, i: (b * nt + i, 0)
    col = lambda b, i: (0, b * nt + i)
    const = lambda b, i: (0, 0)
    return pl.pallas_call(
        _outproj_kernel,
        grid=(bsz, nt),
        in_specs=[pl.BlockSpec((tm, V_W), row),
                  pl.BlockSpec((tm, NA_W), row),
                  pl.BlockSpec((V_W + NA_W, D), const),
                  pl.BlockSpec((tm, D), row),
                  pl.BlockSpec((6, None, 1, D), lambda b, i: (0, b, 0, 0)),
                  pl.BlockSpec((1, D), const),
                  pl.BlockSpec((N_EXP, D), const),
                  pl.BlockSpec((N_EXP, 1), const)],
        out_specs=[pl.BlockSpec((tm, D), row), pl.BlockSpec((tm, D), row),
                   pl.BlockSpec((2, tm), col), pl.BlockSpec((2, tm), col)],
        out_shape=[jax.ShapeDtypeStruct((n, D), f32), jax.ShapeDtypeStruct((n, D), f32),
                   jax.ShapeDtypeStruct((2, n), i32), jax.ShapeDtypeStruct((2, n), f32)],
        compiler_params=pltpu.CompilerParams(dimension_semantics=("arbitrary", "arbitrary")),
        name="outproj_router",
    )(o_gla, o_na, w_out.astype(bf16), x2d, mod_l, nrm.reshape(1, D),
      router_w.T, router_b.reshape(N_EXP, 1))


def _rank_kernel(eidx_ref, tri_ref, rank_ref, cnt_ref, carry):
    @pl.when(pl.program_id(0) == 0)
    def _():
        carry[...] = jnp.zeros_like(carry)

    e = eidx_ref[...]
    tm = e.shape[1]
    io = lax.broadcasted_iota(i32, (N_EXP, tm), 0)
    run = carry[...]
    for k in range(2):
        oh = io == e[k:k + 1, :]
        ohf = oh.astype(f32)
        pre = jnp.dot(ohf.astype(bf16), tri_ref[...], preferred_element_type=f32) + run[:, 0:1]
        rank_ref[k:k + 1, :] = jnp.sum(jnp.where(oh, pre, 0.0), axis=0, keepdims=True).astype(i32)
        run = run + jnp.sum(ohf, axis=1, keepdims=True)
    carry[...] = run
    cnt_ref[...] = run


def _ranks(eidx):
    n = eidx.shape[1]
    tm = TM_RANK
    t = np.arange(tm)
    tri = jnp.asarray(t[:, None] < t[None, :], bf16)
    return pl.pallas_call(
        _rank_kernel,
        grid=(n // tm,),
        in_specs=[pl.BlockSpec((2, tm), lambda i: (0, i)), pl.BlockSpec((tm, tm), lambda i: (0, 0))],
        out_specs=[pl.BlockSpec((2, tm), lambda i: (0, i)), pl.BlockSpec((N_EXP, 128), lambda i: (0, 0))],
        out_shape=[jax.ShapeDtypeStruct((2, n), i32), jax.ShapeDtypeStruct((N_EXP, 128), f32)],
        scratch_shapes=[pltpu.VMEM((N_EXP, 128), f32)],
        compiler_params=pltpu.CompilerParams(dimension_semantics=("arbitrary",)),
        name="moe_rank",
    )(eidx, tri)


def _dispatch_kernel(dest_ref, tails_ref, h_ref, xb_ref, zbuf, sem, zsem):
    i = pl.program_id(0)
    tm = h_ref.shape[0]
    n = pl.num_programs(0) * tm

    def tail_copy(e):
        start = pl.multiple_of(tails_ref[e], MOE_BLK)
        return pltpu.make_async_copy(zbuf, xb_ref.at[pl.ds(start, MOE_BLK)], zsem)

    @pl.when(i == 0)
    def _():
        zbuf[...] = jnp.zeros_like(zbuf)
        for e in range(2 * N_EXP):
            @pl.when(tails_ref[e] >= 0)
            def _():
                tail_copy(e).start()
        for e in range(2 * N_EXP):
            @pl.when(tails_ref[e] >= 0)
            def _():
                tail_copy(e).wait()

    def row_copy(t, k):
        d = dest_ref[k * n + i * tm + t]
        return pltpu.make_async_copy(h_ref.at[pl.ds(t, 1)], xb_ref.at[pl.ds(d, 1)], sem)

    def issue(t, carry):
        row_copy(t, 0).start()
        row_copy(t, 1).start()
        return carry

    lax.fori_loop(0, tm, issue, 0)

    def drain(t, carry):
        row_copy(t, 0).wait()
        row_copy(t, 1).wait()
        return carry

    lax.fori_loop(0, tm, drain, 0)


def _dispatch(h2, dest_flat, tails, p_rows):
    n = h2.shape[0]
    tm = TM_DISP
    return pl.pallas_call(
        _dispatch_kernel,
        grid_spec=pltpu.PrefetchScalarGridSpec(
            num_scalar_prefetch=2,
            grid=(n // tm,),
            in_specs=[pl.BlockSpec((tm, D), lambda i, *_: (i, 0))],
            out_specs=pl.BlockSpec(memory_space=pl.ANY),
            scratch_shapes=[pltpu.VMEM((MOE_BLK, D), f32), pltpu.SemaphoreType.DMA(()),
                            pltpu.SemaphoreType.DMA(())],
        ),
        out_shape=jax.ShapeDtypeStruct((p_rows, D), f32),
        compiler_params=pltpu.CompilerParams(dimension_semantics=("arbitrary",)),
        name="moe_dispatch",
    )(dest_flat, tails, h2)


def _expert_kernel(blk_e_ref, nb_ref, xb_ref, w1_ref, w3_ref, w2_ref, y_ref):
    i = pl.program_id(0)

    @pl.when(i < nb_ref[0])
    def _():
        x = xb_ref[...].astype(bf16)
        a = jnp.dot(x, w1_ref[0].astype(bf16), preferred_element_type=f32)
        b = jnp.dot(x, w3_ref[0].astype(bf16), preferred_element_type=f32)
        h = (a * jax.nn.sigmoid(a) * b).astype(bf16)
        y_ref[...] = jnp.dot(h, w2_ref[0].astype(bf16), preferred_element_type=f32)

    @pl.when(i >= nb_ref[0])
    def _():
        y_ref[...] = jnp.zeros_like(y_ref)


def _experts(xb, blk_e, nb_used, w1, w3, w2):
    p_rows = xb.shape[0]
    nb = p_rows // MOE_BLK

    def xmap(i, be, nbu):
        return (jnp.minimum(i, nbu[0] - 1), 0)

    def wmap(i, be, nbu):
        return (be[i], 0, 0)

    return pl.pallas_call(
        _expert_kernel,
        grid_spec=pltpu.PrefetchScalarGridSpec(
            num_scalar_prefetch=2,
            grid=(nb,),
            in_specs=[pl.BlockSpec((MOE_BLK, D), xmap),
                      pl.BlockSpec((1, D, D_EXP), wmap),
                      pl.BlockSpec((1, D, D_EXP), wmap),
                      pl.BlockSpec((1, D_EXP, D), wmap)],
            out_specs=pl.BlockSpec((MOE_BLK, D), lambda i, be, nbu: (i, 0)),
        ),
        out_shape=jax.ShapeDtypeStruct((p_rows, D), f32),
        compiler_params=pltpu.CompilerParams(dimension_semantics=("arbitrary",),
                                             vmem_limit_bytes=48 * 1024 * 1024),
        name="moe_experts",
    )(blk_e, nb_used, xb, w1, w3, w2)


def _combine_kernel(dest_ref, x1_ref, gw_ref, mod_ref, y_ref, o_ref, ybuf, sem):
    i = pl.program_id(0)
    tm = x1_ref.shape[0]
    n = pl.num_programs(0) * tm

    def row_copy(t, k):
        d = dest_ref[k * n + i * tm + t]
        return pltpu.make_async_copy(y_ref.at[pl.ds(d, 1)], ybuf.at[k, pl.ds(t, 1)], sem)

    def issue(t, carry):
        row_copy(t, 0).start()
        row_copy(t, 1).start()
        return carry

    lax.fori_loop(0, tm, issue, 0)

    def drain(t, carry):
        row_copy(t, 0).wait()
        row_copy(t, 1).wait()
        return carry

    lax.fori_loop(0, tm, drain, 0)
    gw = gw_ref[...]
    y = ybuf[0] * gw[:, 0:1] + ybuf[1] * gw[:, 1:2]
    o_ref[...] = x1_ref[...] + mod_ref[5] * y


def _combine(y, dest_flat, x1, gw_t, mod_l, bsz, seq):
    n = x1.shape[0]
    tm = TM_COMB
    nt = seq // tm
    return pl.pallas_call(
        _combine_kernel,
        grid_spec=pltpu.PrefetchScalarGridSpec(
            num_scalar_prefetch=1,
            grid=(n // tm,),
            in_specs=[pl.BlockSpec((tm, D), lambda i, *_: (i, 0)),
                      pl.BlockSpec((tm, 2), lambda i, *_: (i, 0)),
                      pl.BlockSpec((6, None, 1, D), lambda i, *_: (0, i // nt, 0, 0)),
                      pl.BlockSpec(memory_space=pl.ANY)],
            out_specs=pl.BlockSpec((tm, D), lambda i, *_: (i, 0)),
            scratch_shapes=[pltpu.VMEM((2, tm, D), f32), pltpu.SemaphoreType.DMA(())],
        ),
        out_shape=jax.ShapeDtypeStruct((n, D), f32),
        compiler_params=pltpu.CompilerParams(dimension_semantics=("arbitrary",)),
        name="moe_combine",
    )(dest_flat, x1, gw_t, mod_l, y)


def _grouped_moe(h2, eidx, gw, x1, mod_l, w1, w3, w2, bsz, seq):
    n = h2.shape[0]
    p_rows = (n * 2 // MOE_BLK + N_EXP) * MOE_BLK
    nb = p_rows // MOE_BLK
    rank, cnt = _ranks(eidx)
    counts = cnt[:, 0].astype(i32)
    padded = (counts + MOE_BLK - 1) // MOE_BLK * MOE_BLK
    pad_end = jnp.cumsum(padded)
    pad_start = pad_end - padded
    dest_flat = (pad_start[eidx] + rank).reshape(-1)
    nb_used = (pad_end[-1] // MOE_BLK).astype(i32)
    spare = nb_used + jnp.arange(N_EXP, dtype=i32)
    tails = jnp.concatenate([jnp.where(padded > 0, pad_end - MOE_BLK, -1),
                             jnp.where(spare < nb, spare * MOE_BLK, -1)]).astype(i32)
    blk = jnp.minimum(jnp.arange(nb, dtype=i32), nb_used - 1)
    seg_done = (pad_end[None, :] <= (blk * MOE_BLK)[:, None]).astype(i32)
    blk_e = jnp.minimum(jnp.sum(seg_done, axis=1), N_EXP - 1).astype(i32)
    xb = _dispatch(h2, dest_flat, tails, p_rows)
    y = _experts(xb, blk_e, nb_used.reshape(1), w1, w3, w2)
    return _combine(y, dest_flat, x1, gw.T, mod_l, bsz, seq)


def kernel(x, c, w_ada, b_ada, attn_norm, ffn_norm, w_in, gla_gate_w, gla_gate_b, gla_out_norm,
           na_q_norm, na_k_norm, na_rpb, w_out, router_w, router_b, w1, w3, w2):
    bsz, seq, _ = x.shape
    depth = w_ada.shape[0]
    mod = _adaln_mod(c, w_ada, b_ada)
    xc = x.reshape(bsz * seq, D)
    for l in range(depth):
        qg, kg, vg, sg, la_f, la_b, qn, kn, vn = _inproj(
            xc, mod[l], attn_norm[l], w_in[l], gla_gate_w[l], gla_gate_b[l], bsz, seq)
        o_f = _gla_dir(False, qg, kg, vg, la_f, bsz, seq)
        o_gla = _gla_dir(True, qg, kg, vg, la_b, bsz, seq, extra=(o_f, sg, gla_out_norm[l]))
        o_na = _natten(qn, kn, vn, na_q_norm[l], na_k_norm[l], na_rpb[l], bsz, seq)
        x1, h2, eidx, gw = _outproj(o_gla, o_na, w_out[l], xc, mod[l], ffn_norm[l],
                                    router_w, router_b, bsz, seq)
        xc = _grouped_moe(h2, eidx, gw, x1, mod[l], w1[l], w3[l], w2[l], bsz, seq)
    return xc.reshape(bsz, seq, D)
```

```python
import functools

import numpy as np
import jax
import jax.numpy as jnp
from jax import lax
from jax.experimental import pallas as pl
from jax.experimental.pallas import tpu as pltpu

f32 = jnp.float32
bf16 = jnp.bfloat16
i32 = jnp.int32
HIGHEST = lax.Precision.HIGHEST

D = 1024
GRID_W = 64
GLA_H, GLA_DK, GLA_DV = 4, 64, 128
GLA_RANK = 16
GLA_TAU = 16.0
CHUNK = 64
NA_H, NA_DH = 8, 64
WIN_H, WIN_W = 8, 16
N_EXP, N_GRP, EPG = 32, 4, 8
D_EXP = 512
QK_W = GLA_H * GLA_DK
V_W = GLA_H * GLA_DV
NA_W = NA_H * NA_DH
EPS = 1e-6
NEG = -1e30

TM_PROJ = 512
TB_GLA = 256
NA_QROWS = 4
NA_KROWS = 12
TM_RANK = 512
TM_DISP = 256
TM_COMB = 256
MOE_BLK = 256
N_LEVELS = 6
GLA_SAFE_RANGE = 40.0


def _nt(a, b):
    return lax.dot_general(a, b, (((1,), (1,)), ((), ())), preferred_element_type=f32)


def _tn(a, b):
    return lax.dot_general(a, b, (((0,), (0,)), ((), ())), preferred_element_type=f32)


def _split3(x):
    hi = x.astype(bf16)
    r = x - hi.astype(f32)
    mid = r.astype(bf16)
    lo = (r - mid.astype(f32)).astype(bf16)
    return hi, mid, lo


def _sel_l(m01, x):
    hi, mid, lo = _split3(x)
    d = lambda p: jnp.dot(m01, p, preferred_element_type=f32)
    return d(hi) + d(mid) + d(lo)


def _sel_r(x, m01):
    hi, mid, lo = _split3(x)
    d = lambda p: jnp.dot(p, m01, preferred_element_type=f32)
    return d(hi) + d(mid) + d(lo)


def _mod_kernel(c_ref, w_ref, b_ref, o_ref):
    c = c_ref[...]
    ca = c * jax.nn.sigmoid(c)
    o_ref[0, 0] = jnp.dot(ca, w_ref[0], precision=HIGHEST, preferred_element_type=f32) + b_ref[0, 0]


def _adaln_mod(c, w_ada, b_ada):
    depth = w_ada.shape[0]
    bsz = c.shape[0]
    cp = jnp.zeros((8, D), f32).at[:bsz].set(c)
    out = pl.pallas_call(
        _mod_kernel,
        grid=(depth, 6),
        in_specs=[pl.BlockSpec((8, D), lambda l, j: (0, 0)),
                  pl.BlockSpec((1, D, D), lambda l, j: (l, 0, j)),
                  pl.BlockSpec((1, 1, 1, D), lambda l, j: (l, j, 0, 0))],
        out_specs=pl.BlockSpec((1, 1, 8, D), lambda l, j: (l, j, 0, 0)),
        out_shape=jax.ShapeDtypeStruct((depth, 6, 8, D), f32),
        name="adaln_mod",
    )(cp, w_ada, b_ada.reshape(depth, 6, 1, D))
    return out[:, :, :bsz].reshape(depth, 6, bsz, 1, D)


def _log_sigmoid(z):
    return jnp.minimum(z, 0.0) - jnp.log1p(jnp.exp(-jnp.abs(z)))


def _inproj_kernel(x_ref, mod_ref, nrm_ref, wm_ref, wlr_ref, gf_ref, gb_ref, bf_ref, bb_ref,
                   qg_ref, kg_ref, vg_ref, sg_ref, laf_ref, lab_ref, qn_ref, kn_ref, vn_ref):
    x = x_ref[...]
    y = x * lax.rsqrt(jnp.mean(x * x, axis=-1, keepdims=True) + EPS) * nrm_ref[...]
    h = y * (1.0 + mod_ref[1]) + mod_ref[0]
    hb = h.astype(bf16)
    main = jnp.dot(hb, wm_ref[...], preferred_element_type=f32)
    qg_ref[...] = (main[:, 0:256] * (GLA_DK ** -0.5)).astype(bf16)
    kg_ref[...] = main[:, 256:512].astype(bf16)
    vg_ref[...] = main[:, 512:1024].astype(bf16)
    gg = main[:, 1024:1536]
    sg_ref[...] = (gg * jax.nn.sigmoid(gg)).astype(bf16)
    qn_ref[...] = main[:, 1536:2048].astype(bf16)
    kn_ref[...] = main[:, 2048:2560].astype(bf16)
    vn_ref[...] = main[:, 2560:3072].astype(bf16)
    lr = jnp.dot(hb, wlr_ref[...], preferred_element_type=f32)
    zf = jnp.dot(lr, gf_ref[...], precision=HIGHEST, preferred_element_type=f32) + bf_ref[...]
    zb = jnp.dot(lr, gb_ref[...], precision=HIGHEST, preferred_element_type=f32) + bb_ref[...]
    laf_ref[...] = _log_sigmoid(zf) * (1.0 / GLA_TAU)
    lab_ref[...] = _log_sigmoid(zb) * (1.0 / GLA_TAU)


def _inproj(x2d, mod_l, nrm, w_in, gate_w, gate_b, bsz, seq):
    n = x2d.shape[0]
    tm = TM_PROJ
    nt = seq // tm
    lo, hi = 2 * QK_W + 2 * V_W, 2 * QK_W + 2 * V_W + 2 * GLA_RANK
    wm = jnp.concatenate([w_in[:, :lo], w_in[:, hi:]], axis=1).astype(bf16)
    wlr = w_in[:, lo:hi].astype(bf16)
    zpad = jnp.zeros((GLA_RANK, QK_W), f32)
    gf = jnp.concatenate([gate_w[0], zpad], axis=0)
    gb = jnp.concatenate([zpad, gate_w[1]], axis=0)
    row = lambda b, i: (b * nt + i, 0)
    const = lambda b, i: (0, 0)
    widths = [(QK_W, bf16), (QK_W, bf16), (V_W, bf16), (V_W, bf16), (QK_W, f32), (QK_W, f32),
              (NA_W, bf16), (NA_W, bf16), (NA_W, bf16)]
    return pl.pallas_call(
        _inproj_kernel,
        grid=(bsz, nt),
        in_specs=[pl.BlockSpec((tm, D), row),
                  pl.BlockSpec((6, None, 1, D), lambda b, i: (0, b, 0, 0)),
                  pl.BlockSpec((1, D), const),
                  pl.BlockSpec(wm.shape, const),
                  pl.BlockSpec(wlr.shape, const),
                  pl.BlockSpec(gf.shape, const),
                  pl.BlockSpec(gb.shape, const),
                  pl.BlockSpec((1, QK_W), const),
                  pl.BlockSpec((1, QK_W), const)],
        out_specs=[pl.BlockSpec((tm, w), row) for w, _ in widths],
        out_shape=[jax.ShapeDtypeStruct((n, w), dt) for w, dt in widths],
        compiler_params=pltpu.CompilerParams(dimension_semantics=("arbitrary", "arbitrary")),
        name="inproj",
    )(x2d, mod_l, nrm.reshape(1, D), wm, wlr, gf, gb, gate_b[0].reshape(1, QK_W), gate_b[1].reshape(1, QK_W))


def _gla_consts(rev, tb):
    c = CHUNK
    idx = np.arange(c)
    if not rev:
        cum = (idx[None, :] <= idx[:, None]).astype(np.float32)
    else:
        cum = (idx[None, :] >= idx[:, None]).astype(np.float32)
    mats = [cum]
    masks = []
    for lvl in range(N_LEVELS):
        s = c >> (lvl + 1)
        blk = idx // (2 * s)
        second = (idx % (2 * s)) >= s
        ref_row = blk * 2 * s + (s if rev else s - 1)
        mats.append(cum[ref_row])
        same = blk[:, None] == blk[None, :]
        if not rev:
            m = same & second[:, None] & (~second[None, :])
        else:
            m = same & (~second[:, None]) & second[None, :]
        masks.append(m)
    masks.append(np.eye(c, dtype=bool) & (not rev))
    masks.append((idx[:, None] < idx[None, :]) if rev else (idx[:, None] >= idx[None, :]))
    mst = np.concatenate(mats, axis=0)
    msk = np.stack([np.tile(m, (1, GLA_H)) for m in masks]).astype(np.float32)
    nch = tb // c
    selb = np.zeros((tb, nch * 128), np.float32)
    for ch in range(nch):
        selb[ch * c:(ch + 1) * c, ch * 128:(ch + 1) * 128] = 1.0
    hk = np.arange(QK_W) // GLA_DK
    hv = np.arange(V_W) // GLA_DV
    kmask = (hk[:, None] == hk[None, :]).astype(np.float32)
    vmask = (hk[:, None] == hv[None, :]).astype(np.float32)
    return (jnp.asarray(mst, bf16), jnp.asarray(msk, f32), jnp.asarray(selb, bf16),
            jnp.asarray(kmask, bf16), jnp.asarray(vmask, bf16), jnp.asarray(vmask, f32))


def _gla_kernel(rev, tb, *refs):
    if rev:
        (q_ref, k_ref, v_ref, la_ref, mst_ref, msk_ref, selb_ref, kmask_ref, vmask_ref, smask_ref,
         of_ref, sg_ref, gn_ref, out_ref, s_scr, o_scr) = refs
    else:
        (q_ref, k_ref, v_ref, la_ref, mst_ref, msk_ref, selb_ref, kmask_ref, vmask_ref, smask_ref,
         out_ref, s_scr) = refs
        o_scr = out_ref
    c = CHUNK
    nch = tb // c

    @pl.when(pl.program_id(1) == 0)
    def _():
        s_scr[...] = jnp.zeros_like(s_scr)

    tot = _sel_r(la_ref[...].T, selb_ref[...])
    mst = mst_ref[...]
    kmask = kmask_ref[...]
    vmask = vmask_ref[...]
    order = range(nch - 1, -1, -1) if rev else range(nch)

    def chunk(ch, bounded):
        rows = slice(ch * c, (ch + 1) * c)
        q = q_ref[rows, :].astype(f32)
        k = k_ref[rows, :].astype(f32)
        v = v_ref[rows, :]
        r_all = _sel_l(mst[0:c] if bounded else mst, la_ref[rows, :])
        bcum = r_all[0:c]
        qe = (q * jnp.exp(bcum)).astype(bf16)
        o = jnp.dot(qe, s_scr[...].astype(bf16), preferred_element_type=f32)
        if bounded:
            bm = (k * jnp.exp(-bcum)).astype(bf16)
            bbd = jnp.concatenate([bm] * GLA_H, axis=0) * kmask
            sall = jnp.where(msk_ref[N_LEVELS + 1] > 0.0, _nt(qe, bbd), 0.0)
        else:
            sall = jnp.zeros((c, QK_W), f32)
            for lvl in range(N_LEVELS):
                ref = r_all[(lvl + 1) * c:(lvl + 2) * c]
                a = (q * jnp.exp(jnp.minimum(bcum - ref, 0.0))).astype(bf16)
                bm = (k * jnp.exp(jnp.minimum(ref - bcum, 0.0))).astype(bf16)
                bbd = jnp.concatenate([bm] * GLA_H, axis=0) * kmask
                sall = sall + jnp.where(msk_ref[lvl] > 0.0, _nt(a, bbd), 0.0)
            if not rev:
                bbd = jnp.concatenate([k_ref[rows, :]] * GLA_H, axis=0) * kmask
                sall = sall + jnp.where(msk_ref[N_LEVELS] > 0.0, _nt(q_ref[rows, :], bbd), 0.0)
        vbd = jnp.concatenate([v] * GLA_H, axis=0) * vmask
        o = o + jnp.dot(sall.astype(bf16), vbd, preferred_element_type=f32)
        o_scr[rows, :] = o
        blast = bcum[0:1] if rev else bcum[c - 1:c]
        ke = (k * jnp.exp(blast - bcum)).astype(bf16)
        dec = jnp.exp(tot[:, ch * 128:(ch + 1) * 128])
        s_scr[...] = (s_scr[...] * jnp.concatenate([dec] * GLA_H, axis=1)
                      + _tn(ke, v) * smask_ref[...])

    bounded = jnp.min(tot) > -GLA_SAFE_RANGE

    @pl.when(bounded)
    def _():
        for ch in order:
            chunk(ch, True)

    @pl.when(jnp.logical_not(bounded))
    def _():
        for ch in order:
            chunk(ch, False)

    if rev:
        gn = gn_ref[...]
        for hd in range(GLA_H):
            cols = slice(hd * GLA_DV, (hd + 1) * GLA_DV)
            t = of_ref[:, cols] + o_scr[:, cols]
            y = t * lax.rsqrt(jnp.mean(t * t, axis=-1, keepdims=True) + EPS) * gn
            out_ref[:, cols] = (y * sg_ref[:, cols].astype(f32)).astype(bf16)


def _gla_dir(rev, q, k, v, la, bsz, seq, extra=None):
    n = q.shape[0]
    tb = TB_GLA
    nb = seq // tb
    consts = _gla_consts(rev, tb)
    if rev:
        row = lambda b, i: (b * nb + nb - 1 - i, 0)
    else:
        row = lambda b, i: (b * nb + i, 0)
    const2 = lambda b, i: (0, 0)
    const3 = lambda b, i: (0, 0, 0)
    in_specs = [pl.BlockSpec((tb, QK_W), row), pl.BlockSpec((tb, QK_W), row),
                pl.BlockSpec((tb, V_W), row), pl.BlockSpec((tb, QK_W), row),
                pl.BlockSpec(consts[0].shape, const2), pl.BlockSpec(consts[1].shape, const3),
                pl.BlockSpec(consts[2].shape, const2), pl.BlockSpec(consts[3].shape, const2),
                pl.BlockSpec(consts[4].shape, const2), pl.BlockSpec(consts[5].shape, const2)]
    args = [q, k, v, la, *consts]
    scratch = [pltpu.VMEM((QK_W, V_W), f32)]
    if rev:
        o_f, sg, gn = extra
        in_specs += [pl.BlockSpec((tb, V_W), row), pl.BlockSpec((tb, V_W), row),
                     pl.BlockSpec((1, GLA_DV), const2)]
        args += [o_f, sg, gn.reshape(1, GLA_DV)]
        scratch.append(pltpu.VMEM((tb, V_W), f32))
        out_dtype = bf16
    else:
        out_dtype = f32
    return pl.pallas_call(
        functools.partial(_gla_kernel, rev, tb),
        grid=(bsz, nb),
        in_specs=in_specs,
        out_specs=pl.BlockSpec((tb, V_W), row),
        out_shape=jax.ShapeDtypeStruct((n, V_W), out_dtype),
        scratch_shapes=scratch,
        compiler_params=pltpu.CompilerParams(dimension_semantics=("arbitrary", "arbitrary")),
        name="gla_bwd" if rev else "gla_fwd",
    )(*args)


def _rpb_expand_kernel(rpb_ref, sel_ref, valid_ref, o_ref):
    e = _sel_r(rpb_ref[...], sel_ref[...])
    o_ref[...] = jnp.where(valid_ref[...] > 0.0, e, NEG)


def _na_bias_table(rpb):
    nri, nci = 2 * WIN_H - 1, 2 * WIN_W - 1
    qc = np.arange(GRID_W)
    col_start = np.clip(qc - WIN_W // 2, 0, GRID_W - WIN_W)
    kc = np.arange(GRID_W)
    valid = (kc[None, :] >= col_start[:, None]) & (kc[None, :] < col_start[:, None] + WIN_W)
    ci = kc[None, :] - qc[:, None] + WIN_W - 1
    sel = np.zeros((32, GRID_W * GRID_W), np.float32)
    flat_ci = ci.reshape(-1)
    ok = (flat_ci >= 0) & (flat_ci < nci)
    sel[flat_ci[ok], np.nonzero(ok)[0]] = 1.0
    rpb2 = jnp.zeros((NA_H * nri + 8 - (NA_H * nri) % 8, 32), f32).at[:NA_H * nri, :nci].set(
        rpb.reshape(NA_H * nri, nci))
    nrow = rpb2.shape[0]
    e = pl.pallas_call(
        _rpb_expand_kernel,
        out_shape=jax.ShapeDtypeStruct((nrow, GRID_W * GRID_W), f32),
        name="rpb_expand",
    )(rpb2, jnp.asarray(sel, bf16), jnp.asarray(valid.reshape(1, -1), f32))
    return e[:NA_H * nri].reshape(NA_H // 2, 2, nri, GRID_W, GRID_W)


def _na_row_classes():
    rows = GRID_W
    ri = -np.ones((3, NA_QROWS, NA_KROWS), np.int64)
    for cls, rb in enumerate((0, 1, rows // NA_QROWS - 1)):
        r0 = rb * NA_QROWS
        kb = int(np.clip(r0 - WIN_H // 2, 0, rows - NA_KROWS))
        for rq in range(NA_QROWS):
            r = r0 + rq
            rs = int(np.clip(r - WIN_H // 2, 0, rows - WIN_H))
            for m in range(NA_KROWS):
                kr = kb + m
                if rs <= kr < rs + WIN_H:
                    ri[cls, rq, m] = kr - r + WIN_H - 1
    return ri


def _natten_kernel(q_ref, k_ref, v_ref, qn_ref, kn_ref, gsum_ref, e_ref, o_ref, kn_scr, tbl_scr):
    b = pl.program_id(1)
    rb = pl.program_id(2)
    nrb = pl.num_programs(2)
    gsum = gsum_ref[...]
    log2e = 1.4426950408889634

    def headnorm(x, g):
        ms = _sel_r(x * x, gsum) * (1.0 / NA_DH)
        return x * lax.rsqrt(ms + EPS) * g

    @pl.when((b == 0) & (rb == 0))
    def _():
        ri = _na_row_classes()
        neg = jnp.full((GRID_W, GRID_W), NEG, f32)
        for hh in range(2):
            for cls in range(3):
                for rq in range(NA_QROWS):
                    for mp in range(NA_KROWS // 2):
                        parts = []
                        for m in (2 * mp, 2 * mp + 1):
                            r = int(ri[cls, rq, m])
                            parts.append(neg if r < 0 else e_ref[hh, r] * log2e)
                        tbl_scr[hh, cls, rq * GRID_W:(rq + 1) * GRID_W, mp * 128:(mp + 1) * 128] = (
                            jnp.concatenate(parts, axis=1))

    @pl.when(rb == 0)
    def _():
        kn_scr[...] = headnorm(k_ref[...].astype(f32), kn_ref[...]).astype(bf16)

    q = headnorm(q_ref[...].astype(f32), qn_ref[...]) * (NA_DH ** -0.5 * log2e)
    kb = jnp.clip(rb * NA_QROWS - WIN_H // 2, 0, GRID_W - NA_KROWS)
    start = pl.multiple_of(kb * GRID_W, GRID_W)
    nk = NA_KROWS * GRID_W
    kwin = kn_scr[pl.ds(start, nk), :]
    vwin = v_ref[pl.ds(start, nk), :]
    cls = jnp.where(rb == 0, 0, jnp.where(rb == nrb - 1, 2, 1))
    lane = lax.broadcasted_iota(i32, q.shape, 1)
    out = jnp.zeros(q.shape, f32)
    for hh in range(2):
        sel = (lane >= hh * NA_DH) & (lane < (hh + 1) * NA_DH)
        qh = jnp.where(sel, q, 0.0).astype(bf16)
        s = _nt(qh, kwin) + tbl_scr[hh, cls]
        m = jnp.max(s, axis=-1, keepdims=True)
        p = jnp.exp2(s - m)
        l = jnp.sum(p, axis=-1, keepdims=True)
        o = jnp.dot(p.astype(bf16), vwin, preferred_element_type=f32) / l
        out = jnp.where(sel, o, out)
    o_ref[...] = out.astype(bf16)


def _natten(qn, kn, vn, q_norm, k_norm, rpb, bsz, seq):
    n = qn.shape[0]
    e5 = _na_bias_table(rpb)
    nq = NA_QROWS * GRID_W
    nrb = seq // nq
    lane_h = np.arange(128) // NA_DH
    gsum = jnp.asarray(lane_h[:, None] == lane_h[None, :], bf16)
    qn2 = jnp.tile(q_norm.reshape(1, NA_DH), (1, 2))
    kn2 = jnp.tile(k_norm.reshape(1, NA_DH), (1, 2))
    const2 = lambda p, b, r: (0, 0)
    return pl.pallas_call(
        _natten_kernel,
        grid=(NA_H // 2, bsz, nrb),
        in_specs=[pl.BlockSpec((nq, 128), lambda p, b, r: (b * nrb + r, p)),
                  pl.BlockSpec((seq, 128), lambda p, b, r: (b, p)),
                  pl.BlockSpec((seq, 128), lambda p, b, r: (b, p)),
                  pl.BlockSpec((1, 128), const2),
                  pl.BlockSpec((1, 128), const2),
                  pl.BlockSpec((128, 128), const2),
                  pl.BlockSpec((None,) + e5.shape[1:], lambda p, b, r: (p, 0, 0, 0, 0))],
        out_specs=pl.BlockSpec((nq, 128), lambda p, b, r: (b * nrb + r, p)),
        out_shape=jax.ShapeDtypeStruct((n, NA_W), bf16),
        scratch_shapes=[pltpu.VMEM((seq, 128), bf16),
                        pltpu.VMEM((2, 3, nq, NA_KROWS * GRID_W), f32)],
        compiler_params=pltpu.CompilerParams(dimension_semantics=("arbitrary", "arbitrary", "arbitrary")),
        name="natten",
    )(qn, kn, vn, qn2, kn2, gsum, e5)


def _top2(vals):
    io = lax.broadcasted_iota(i32, vals.shape, 0)
    m1 = jnp.max(vals, axis=0, keepdims=True)
    i1 = jnp.min(jnp.where(vals == m1, io, EPG), axis=0, keepdims=True)
    v2 = jnp.where(io == i1, -jnp.inf, vals)
    m2 = jnp.max(v2, axis=0, keepdims=True)
    i2 = jnp.min(jnp.where(v2 == m2, io, EPG), axis=0, keepdims=True)
    return m1, i1, m2, i2


def _outproj_kernel(og_ref, on_ref, wo_ref, x_ref, mod_ref, nrm_ref, rw_ref, rb_ref,
                    x1_ref, h2_ref, eidx_ref, gw_ref):
    mix = (jnp.dot(og_ref[...], wo_ref[0:V_W, :], preferred_element_type=f32)
           + jnp.dot(on_ref[...], wo_ref[V_W:V_W + NA_W, :], preferred_element_type=f32))
    x1 = x_ref[...] + mod_ref[2] * mix
    x1_ref[...] = x1
    y = x1 * lax.rsqrt(jnp.mean(x1 * x1, axis=-1, keepdims=True) + EPS) * nrm_ref[...]
    h2 = y * (1.0 + mod_ref[4]) + mod_ref[3]
    h2_ref[...] = h2
    rw = rw_ref[...]
    rw_hi = rw.astype(bf16)
    rw_lo = (rw - rw_hi.astype(f32)).astype(bf16)
    h_hi = h2.astype(bf16)
    h_lo = (h2 - h_hi.astype(f32)).astype(bf16)
    logits = _nt(rw_hi, h_hi) + _nt(rw_hi, h_lo) + _nt(rw_lo, h_hi)
    scores = jax.nn.sigmoid(logits)
    sel = scores + rb_ref[...]
    tops = [_top2(sel[g * EPG:(g + 1) * EPG]) for g in range(N_GRP)]
    best = jnp.zeros_like(tops[0][1])
    bs = tops[0][0] + tops[0][2]
    for g in range(1, N_GRP):
        gs = tops[g][0] + tops[g][2]
        take = gs > bs
        best = jnp.where(take, g, best)
        bs = jnp.where(take, gs, bs)
    io = lax.broadcasted_iota(i32, (EPG, sel.shape[1]), 0)
    i1 = jnp.zeros_like(best)
    i2 = jnp.zeros_like(best)
    s1 = jnp.zeros(best.shape, f32)
    s2 = jnp.zeros(best.shape, f32)
    for g in range(N_GRP):
        sc = scores[g * EPG:(g + 1) * EPG]
        pick = best == g
        i1 = jnp.where(pick, tops[g][1], i1)
        i2 = jnp.where(pick, tops[g][3], i2)
        s1 = jnp.where(pick, jnp.sum(jnp.where(io == tops[g][1], sc, 0.0), axis=0, keepdims=True), s1)
        s2 = jnp.where(pick, jnp.sum(jnp.where(io == tops[g][3], sc, 0.0), axis=0, keepdims=True), s2)
    eidx_ref[0:1, :] = best * EPG + i1
    eidx_ref[1:2, :] = best * EPG + i2
    tot = s1 + s2
    gw_ref[0:1, :] = s1 / tot
    gw_ref[1:2, :] = s2 / tot


def _outproj(o_gla, o_na, w_out, x2d, mod_l, nrm, router_w, router_b, bsz, seq):
    n = x2d.shape[0]
    tm = TM_PROJ
    nt = seq // tm
    row = lambda b, i: (b * nt + i, 0)
    col = lambda b, i: (0, b * nt + i)
    const = lambda b, i: (0, 0)
    return pl.pallas_call(
        _outproj_kernel,
        grid=(bsz, nt),
        in_specs=[pl.BlockSpec((tm, V_W), row),
                  pl.BlockSpec((tm, NA_W), row),
                  pl.BlockSpec((V_W + NA_W, D), const),
                  pl.BlockSpec((tm, D), row),
                  pl.BlockSpec((6, None, 1, D), lambda b, i: (0, b, 0, 0)),
                  pl.BlockSpec((1, D), const),
                  pl.BlockSpec((N_EXP, D), const),
                  pl.BlockSpec((N_EXP, 1), const)],
        out_specs=[pl.BlockSpec((tm, D), row), pl.BlockSpec((tm, D), row),
                   pl.BlockSpec((2, tm), col), pl.BlockSpec((2, tm), col)],
        out_shape=[jax.ShapeDtypeStruct((n, D), f32), jax.ShapeDtypeStruct((n, D), f32),
                   jax.ShapeDtypeStruct((2, n), i32), jax.ShapeDtypeStruct((2, n), f32)],
        compiler_params=pltpu.CompilerParams(dimension_semantics=("arbitrary", "arbitrary")),
        name="outproj_router",
    )(o_gla, o_na, w_out.astype(bf16), x2d, mod_l, nrm.reshape(1, D),
      router_w.T, router_b.reshape(N_EXP, 1))


def _rank_kernel(eidx_ref, tri_ref, rank_ref, cnt_ref, carry):
    @pl.when(pl.program_id(0) == 0)
    def _():
        carry[...] = jnp.zeros_like(carry)

    e = eidx_ref[...]
    tm = e.shape[1]
    io = lax.broadcasted_iota(i32, (N_EXP, tm), 0)
    run = carry[...]
    for k in range(2):
        oh = io == e[k:k + 1, :]
        ohf = oh.astype(f32)
        pre = jnp.dot(ohf.astype(bf16), tri_ref[...], preferred_element_type=f32) + run[:, 0:1]
        rank_ref[k:k + 1, :] = jnp.sum(jnp.where(oh, pre, 0.0), axis=0, keepdims=True).astype(i32)
        run = run + jnp.sum(ohf, axis=1, keepdims=True)
    carry[...] = run
    cnt_ref[...] = run


def _ranks(eidx):
    n = eidx.shape[1]
    tm = TM_RANK
    t = np.arange(tm)
    tri = jnp.asarray(t[:, None] < t[None, :], bf16)
    return pl.pallas_call(
        _rank_kernel,
        grid=(n // tm,),
        in_specs=[pl.BlockSpec((2, tm), lambda i: (0, i)), pl.BlockSpec((tm, tm), lambda i: (0, 0))],
        out_specs=[pl.BlockSpec((2, tm), lambda i: (0, i)), pl.BlockSpec((N_EXP, 128), lambda i: (0, 0))],
        out_shape=[jax.ShapeDtypeStruct((2, n), i32), jax.ShapeDtypeStruct((N_EXP, 128), f32)],
        scratch_shapes=[pltpu.VMEM((N_EXP, 128), f32)],
        compiler_params=pltpu.CompilerParams(dimension_semantics=("arbitrary",)),
        name="moe_rank",
    )(eidx, tri)


def _dispatch_kernel(dest_ref, tails_ref, h_ref, xb_ref, zbuf, sem, zsem):
    i = pl.program_id(0)
    nsteps = pl.num_programs(0)
    tm = TM_DISP
    n = nsteps * tm

    def tail_copy(e):
        start = pl.multiple_of(tails_ref[e], MOE_BLK)
        return pltpu.make_async_copy(zbuf, xb_ref.at[pl.ds(start, MOE_BLK)], zsem)

    @pl.when(i == 0)
    def _():
        zbuf[...] = jnp.zeros_like(zbuf)
        for e in range(2 * N_EXP):
            @pl.when(tails_ref[e] >= 0)
            def _():
                tail_copy(e).start()
        for e in range(2 * N_EXP):
            @pl.when(tails_ref[e] >= 0)
            def _():
                tail_copy(e).wait()

    def issue(t, carry):
        src = h_ref.at[pl.ds(i * tm + t, 1)]
        for k in range(2):
            d = dest_ref[k * n + i * tm + t]
            pltpu.make_async_copy(src, xb_ref.at[pl.ds(d, 1)], sem).start()
        return carry

    lax.fori_loop(0, tm, issue, 0, unroll=8)

    def drain_step():
        for _ in range(2):
            pltpu.make_async_copy(h_ref.at[pl.ds(0, tm)], xb_ref.at[pl.ds(0, tm)], sem).wait()

    @pl.when(i > 0)
    def _():
        drain_step()

    @pl.when(i == nsteps - 1)
    def _():
        drain_step()


def _dispatch(h2, dest_flat, tails, p_rows):
    n = h2.shape[0]
    tm = TM_DISP
    return pl.pallas_call(
        _dispatch_kernel,
        grid_spec=pltpu.PrefetchScalarGridSpec(
            num_scalar_prefetch=2,
            grid=(n // tm,),
            in_specs=[pl.BlockSpec(memory_space=pl.ANY)],
            out_specs=pl.BlockSpec(memory_space=pl.ANY),
            scratch_shapes=[pltpu.VMEM((MOE_BLK, D), f32), pltpu.SemaphoreType.DMA(()),
                            pltpu.SemaphoreType.DMA(())],
        ),
        out_shape=jax.ShapeDtypeStruct((p_rows, D), f32),
        compiler_params=pltpu.CompilerParams(dimension_semantics=("arbitrary",)),
        name="moe_dispatch",
    )(dest_flat, tails, h2)


def _expert_kernel(blk_e_ref, nb_ref, xb_ref, w1_ref, w3_ref, w2_ref, y_ref):
    i = pl.program_id(0)

    @pl.when(i < nb_ref[0])
    def _():
        x = xb_ref[...].astype(bf16)
        a = jnp.dot(x, w1_ref[0].astype(bf16), preferred_element_type=f32)
        b = jnp.dot(x, w3_ref[0].astype(bf16), preferred_element_type=f32)
        h = (a * jax.nn.sigmoid(a) * b).astype(bf16)
        y_ref[...] = jnp.dot(h, w2_ref[0].astype(bf16), preferred_element_type=f32)

    @pl.when(i >= nb_ref[0])
    def _():
        y_ref[...] = jnp.zeros_like(y_ref)


def _experts(xb, blk_e, nb_used, w1, w3, w2):
    p_rows = xb.shape[0]
    nb = p_rows // MOE_BLK

    def xmap(i, be, nbu):
        return (jnp.minimum(i, nbu[0] - 1), 0)

    def wmap(i, be, nbu):
        return (be[i], 0, 0)

    return pl.pallas_call(
        _expert_kernel,
        grid_spec=pltpu.PrefetchScalarGridSpec(
            num_scalar_prefetch=2,
            grid=(nb,),
            in_specs=[pl.BlockSpec((MOE_BLK, D), xmap),
                      pl.BlockSpec((1, D, D_EXP), wmap),
                      pl.BlockSpec((1, D, D_EXP), wmap),
                      pl.BlockSpec((1, D_EXP, D), wmap)],
            out_specs=pl.BlockSpec((MOE_BLK, D), lambda i, be, nbu: (i, 0)),
        ),
        out_shape=jax.ShapeDtypeStruct((p_rows, D), f32),
        compiler_params=pltpu.CompilerParams(dimension_semantics=("arbitrary",),
                                             vmem_limit_bytes=48 * 1024 * 1024),
        name="moe_experts",
    )(blk_e, nb_used, xb, w1, w3, w2)


def _combine_kernel(dest_ref, x1_ref, gw_ref, mod_ref, y_ref, o_ref, ybuf, sems):
    i = pl.program_id(0)
    nsteps = pl.num_programs(0)
    tm = x1_ref.shape[0]
    n = nsteps * tm

    def issue_tile(j):
        slot = j % 2

        def issue(t, carry):
            for k in range(2):
                d = dest_ref[k * n + j * tm + t]
                pltpu.make_async_copy(y_ref.at[pl.ds(d, 1)], ybuf.at[slot, k, pl.ds(t, 1)],
                                      sems.at[slot]).start()
            return carry

        lax.fori_loop(0, tm, issue, 0, unroll=8)

    @pl.when(i == 0)
    def _():
        issue_tile(i)

    @pl.when(i + 1 < nsteps)
    def _():
        issue_tile(i + 1)

    slot = i % 2
    for k in range(2):
        pltpu.make_async_copy(y_ref.at[pl.ds(0, tm)], ybuf.at[slot, k], sems.at[slot]).wait()
    gw = gw_ref[...]
    y = ybuf[slot, 0] * gw[:, 0:1] + ybuf[slot, 1] * gw[:, 1:2]
    o_ref[...] = x1_ref[...] + mod_ref[5] * y


def _combine(y, dest_flat, x1, gw_t, mod_l, bsz, seq):
    n = x1.shape[0]
    tm = TM_COMB
    nt = seq // tm
    return pl.pallas_call(
        _combine_kernel,
        grid_spec=pltpu.PrefetchScalarGridSpec(
            num_scalar_prefetch=1,
            grid=(n // tm,),
            in_specs=[pl.BlockSpec((tm, D), lambda i, *_: (i, 0)),
                      pl.BlockSpec((tm, 2), lambda i, *_: (i, 0)),
                      pl.BlockSpec((6, None, 1, D), lambda i, *_: (0, i // nt, 0, 0)),
                      pl.BlockSpec(memory_space=pl.ANY)],
            out_specs=pl.BlockSpec((tm, D), lambda i, *_: (i, 0)),
            scratch_shapes=[pltpu.VMEM((2, 2, tm, D), f32), pltpu.SemaphoreType.DMA((2,))],
        ),
        out_shape=jax.ShapeDtypeStruct((n, D), f32),
        compiler_params=pltpu.CompilerParams(dimension_semantics=("arbitrary",)),
        name="moe_combine",
    )(dest_flat, x1, gw_t, mod_l, y)


def _grouped_moe(h2, eidx, gw, x1, mod_l, w1, w3, w2, bsz, seq):
    n = h2.shape[0]
    p_rows = (n * 2 // MOE_BLK + N_EXP) * MOE_BLK
    nb = p_rows // MOE_BLK
    rank, cnt = _ranks(eidx)
    counts = cnt[:, 0].astype(i32)
    padded = (counts + MOE_BLK - 1) // MOE_BLK * MOE_BLK
    pad_end = jnp.cumsum(padded)
    pad_start = pad_end - padded
    dest_flat = (pad_start[eidx] + rank).reshape(-1)
    nb_used = (pad_end[-1] // MOE_BLK).astype(i32)
    spare = nb_used + jnp.arange(N_EXP, dtype=i32)
    tails = jnp.concatenate([jnp.where(padded > 0, pad_end - MOE_BLK, -1),
                             jnp.where(spare < nb, spare * MOE_BLK, -1)]).astype(i32)
    blk = jnp.minimum(jnp.arange(nb, dtype=i32), nb_used - 1)
    seg_done = (pad_end[None, :] <= (blk * MOE_BLK)[:, None]).astype(i32)
    blk_e = jnp.minimum(jnp.sum(seg_done, axis=1), N_EXP - 1).astype(i32)
    xb = _dispatch(h2, dest_flat, tails, p_rows)
    y = _experts(xb, blk_e, nb_used.reshape(1), w1, w3, w2)
    return _combine(y, dest_flat, x1, gw.T, mod_l, bsz, seq)


def kernel(x, c, w_ada, b_ada, attn_norm, ffn_norm, w_in, gla_gate_w, gla_gate_b, gla_out_norm,
           na_q_norm, na_k_norm, na_rpb, w_out, router_w, router_b, w1, w3, w2):
    bsz, seq, _ = x.shape
    depth = w_ada.shape[0]
    mod = _adaln_mod(c, w_ada, b_ada)
    xc = x.reshape(bsz * seq, D)
    for l in range(depth):
        qg, kg, vg, sg, la_f, la_b, qn, kn, vn = _inproj(
            xc, mod[l], attn_norm[l], w_in[l], gla_gate_w[l], gla_gate_b[l], bsz, seq)
        o_f = _gla_dir(False, qg, kg, vg, la_f, bsz, seq)
        o_gla = _gla_dir(True, qg, kg, vg, la_b, bsz, seq, extra=(o_f, sg, gla_out_norm[l]))
        o_na = _natten(qn, kn, vn, na_q_norm[l], na_k_norm[l], na_rpb[l], bsz, seq)
        x1, h2, eidx, gw = _outproj(o_gla, o_na, w_out[l], xc, mod[l], ffn_norm[l],
                                    router_w, router_b, bsz, seq)
        xc = _grouped_moe(h2, eidx, gw, x1, mod[l], w1[l], w3[l], w2[l], bsz, seq)
    return xc.reshape(bsz, seq, D)
```

```python
import functools

import numpy as np
import jax
import jax.numpy as jnp
from jax import lax
from jax.experimental import pallas as pl
from jax.experimental.pallas import tpu as pltpu

f32 = jnp.float32
bf16 = jnp.bfloat16
i32 = jnp.int32
HIGHEST = lax.Precision.HIGHEST

D = 1024
GRID_W = 64
GLA_H, GLA_DK, GLA_DV = 4, 64, 128
GLA_RANK = 16
GLA_TAU = 16.0
CHUNK = 64
NA_H, NA_DH = 8, 64
WIN_H, WIN_W = 8, 16
N_EXP, N_GRP, EPG = 32, 4, 8
D_EXP = 512
QK_W = GLA_H * GLA_DK
V_W = GLA_H * GLA_DV
NA_W = NA_H * NA_DH
EPS = 1e-6
NEG = -1e30

TM_PROJ = 512
TB_GLA = 256
NA_QROWS = 4
NA_KROWS = 12
TM_RANK = 512
TM_DISP = 256
TM_COMB = 256
MOE_BLK = 512
N_LEVELS = 6
GLA_SAFE_RANGE = 40.0


def _nt(a, b):
    return lax.dot_general(a, b, (((1,), (1,)), ((), ())), preferred_element_type=f32)


def _tn(a, b):
    return lax.dot_general(a, b, (((0,), (0,)), ((), ())), preferred_element_type=f32)


def _split3(x):
    hi = x.astype(bf16)
    r = x - hi.astype(f32)
    mid = r.astype(bf16)
    lo = (r - mid.astype(f32)).astype(bf16)
    return hi, mid, lo


def _sel_l(m01, x):
    hi, mid, lo = _split3(x)
    d = lambda p: jnp.dot(m01, p, preferred_element_type=f32)
    return d(hi) + d(mid) + d(lo)


def _sel_r(x, m01):
    hi, mid, lo = _split3(x)
    d = lambda p: jnp.dot(p, m01, preferred_element_type=f32)
    return d(hi) + d(mid) + d(lo)


def _mod_kernel(c_ref, w_ref, b_ref, o_ref):
    c = c_ref[...]
    ca = c * jax.nn.sigmoid(c)
    o_ref[0, 0] = jnp.dot(ca, w_ref[0], precision=HIGHEST, preferred_element_type=f32) + b_ref[0, 0]


def _adaln_mod(c, w_ada, b_ada):
    depth = w_ada.shape[0]
    bsz = c.shape[0]
    cp = jnp.zeros((8, D), f32).at[:bsz].set(c)
    out = pl.pallas_call(
        _mod_kernel,
        grid=(depth, 6),
        in_specs=[pl.BlockSpec((8, D), lambda l, j: (0, 0)),
                  pl.BlockSpec((1, D, D), lambda l, j: (l, 0, j)),
                  pl.BlockSpec((1, 1, 1, D), lambda l, j: (l, j, 0, 0))],
        out_specs=pl.BlockSpec((1, 1, 8, D), lambda l, j: (l, j, 0, 0)),
        out_shape=jax.ShapeDtypeStruct((depth, 6, 8, D), f32),
        name="adaln_mod",
    )(cp, w_ada, b_ada.reshape(depth, 6, 1, D))
    return out[:, :, :bsz].reshape(depth, 6, bsz, 1, D)


def _log_sigmoid(z):
    return jnp.minimum(z, 0.0) - jnp.log1p(jnp.exp(-jnp.abs(z)))


def _inproj_kernel(x_ref, mod_ref, nrm_ref, wm_ref, wlr_ref, gf_ref, gb_ref, bf_ref, bb_ref,
                   qg_ref, kg_ref, vg_ref, sg_ref, laf_ref, lab_ref, qn_ref, kn_ref, vn_ref):
    x = x_ref[...]
    y = x * lax.rsqrt(jnp.mean(x * x, axis=-1, keepdims=True) + EPS) * nrm_ref[...]
    h = y * (1.0 + mod_ref[1]) + mod_ref[0]
    hb = h.astype(bf16)
    main = jnp.dot(hb, wm_ref[...], preferred_element_type=f32)
    qg_ref[...] = (main[:, 0:256] * (GLA_DK ** -0.5)).astype(bf16)
    kg_ref[...] = main[:, 256:512].astype(bf16)
    vg_ref[...] = main[:, 512:1024].astype(bf16)
    gg = main[:, 1024:1536]
    sg_ref[...] = (gg * jax.nn.sigmoid(gg)).astype(bf16)
    qn_ref[...] = main[:, 1536:2048].astype(bf16)
    kn_ref[...] = main[:, 2048:2560].astype(bf16)
    vn_ref[...] = main[:, 2560:3072].astype(bf16)
    lr = jnp.dot(hb, wlr_ref[...], preferred_element_type=f32)
    zf = jnp.dot(lr, gf_ref[...], precision=HIGHEST, preferred_element_type=f32) + bf_ref[...]
    zb = jnp.dot(lr, gb_ref[...], precision=HIGHEST, preferred_element_type=f32) + bb_ref[...]
    laf_ref[...] = _log_sigmoid(zf) * (1.0 / GLA_TAU)
    lab_ref[...] = _log_sigmoid(zb) * (1.0 / GLA_TAU)


def _inproj(x2d, mod_l, nrm, w_in, gate_w, gate_b, bsz, seq):
    n = x2d.shape[0]
    tm = TM_PROJ
    nt = seq // tm
    lo, hi = 2 * QK_W + 2 * V_W, 2 * QK_W + 2 * V_W + 2 * GLA_RANK
    wm = jnp.concatenate([w_in[:, :lo], w_in[:, hi:]], axis=1).astype(bf16)
    wlr = w_in[:, lo:hi].astype(bf16)
    zpad = jnp.zeros((GLA_RANK, QK_W), f32)
    gf = jnp.concatenate([gate_w[0], zpad], axis=0)
    gb = jnp.concatenate([zpad, gate_w[1]], axis=0)
    row = lambda b, i: (b * nt + i, 0)
    const = lambda b, i: (0, 0)
    widths = [(QK_W, bf16), (QK_W, bf16), (V_W, bf16), (V_W, bf16), (QK_W, f32), (QK_W, f32),
              (NA_W, bf16), (NA_W, bf16), (NA_W, bf16)]
    return pl.pallas_call(
        _inproj_kernel,
        grid=(bsz, nt),
        in_specs=[pl.BlockSpec((tm, D), row),
                  pl.BlockSpec((6, None, 1, D), lambda b, i: (0, b, 0, 0)),
                  pl.BlockSpec((1, D), const),
                  pl.BlockSpec(wm.shape, const),
                  pl.BlockSpec(wlr.shape, const),
                  pl.BlockSpec(gf.shape, const),
                  pl.BlockSpec(gb.shape, const),
                  pl.BlockSpec((1, QK_W), const),
                  pl.BlockSpec((1, QK_W), const)],
        out_specs=[pl.BlockSpec((tm, w), row) for w, _ in widths],
        out_shape=[jax.ShapeDtypeStruct((n, w), dt) for w, dt in widths],
        compiler_params=pltpu.CompilerParams(dimension_semantics=("arbitrary", "arbitrary")),
        name="inproj",
    )(x2d, mod_l, nrm.reshape(1, D), wm, wlr, gf, gb, gate_b[0].reshape(1, QK_W), gate_b[1].reshape(1, QK_W))


def _gla_consts(rev, tb):
    c = CHUNK
    idx = np.arange(c)
    if not rev:
        cum = (idx[None, :] <= idx[:, None]).astype(np.float32)
    else:
        cum = (idx[None, :] >= idx[:, None]).astype(np.float32)
    mats = [cum]
    masks = []
    for lvl in range(N_LEVELS):
        s = c >> (lvl + 1)
        blk = idx // (2 * s)
        second = (idx % (2 * s)) >= s
        ref_row = blk * 2 * s + (s if rev else s - 1)
        mats.append(cum[ref_row])
        same = blk[:, None] == blk[None, :]
        if not rev:
            m = same & second[:, None] & (~second[None, :])
        else:
            m = same & (~second[:, None]) & second[None, :]
        masks.append(m)
    masks.append(np.eye(c, dtype=bool) & (not rev))
    masks.append((idx[:, None] < idx[None, :]) if rev else (idx[:, None] >= idx[None, :]))
    mst = np.concatenate(mats, axis=0)
    msk = np.stack([np.tile(m, (1, GLA_H)) for m in masks]).astype(np.float32)
    nch = tb // c
    selb = np.zeros((tb, nch * 128), np.float32)
    for ch in range(nch):
        selb[ch * c:(ch + 1) * c, ch * 128:(ch + 1) * 128] = 1.0
    hk = np.arange(QK_W) // GLA_DK
    hv = np.arange(V_W) // GLA_DV
    kmask = (hk[:, None] == hk[None, :]).astype(np.float32)
    vmask = (hk[:, None] == hv[None, :]).astype(np.float32)
    return (jnp.asarray(mst, bf16), jnp.asarray(msk, f32), jnp.asarray(selb, bf16),
            jnp.asarray(kmask, bf16), jnp.asarray(vmask, bf16), jnp.asarray(vmask, f32))


def _gla_kernel(rev, tb, *refs):
    if rev:
        (q_ref, k_ref, v_ref, la_ref, mst_ref, msk_ref, selb_ref, kmask_ref, vmask_ref, smask_ref,
         of_ref, sg_ref, gn_ref, out_ref, s_scr, o_scr) = refs
    else:
        (q_ref, k_ref, v_ref, la_ref, mst_ref, msk_ref, selb_ref, kmask_ref, vmask_ref, smask_ref,
         out_ref, s_scr) = refs
        o_scr = out_ref
    c = CHUNK
    nch = tb // c

    @pl.when(pl.program_id(1) == 0)
    def _():
        s_scr[...] = jnp.zeros_like(s_scr)

    tot = _sel_r(la_ref[...].T, selb_ref[...])
    mst = mst_ref[...]
    kmask = kmask_ref[...]
    vmask = vmask_ref[...]
    order = range(nch - 1, -1, -1) if rev else range(nch)

    def chunk(ch, bounded):
        rows = slice(ch * c, (ch + 1) * c)
        q = q_ref[rows, :].astype(f32)
        k = k_ref[rows, :].astype(f32)
        v = v_ref[rows, :]
        r_all = _sel_l(mst[0:c] if bounded else mst, la_ref[rows, :])
        bcum = r_all[0:c]
        qe = (q * jnp.exp(bcum)).astype(bf16)
        o = jnp.dot(qe, s_scr[...].astype(bf16), preferred_element_type=f32)
        if bounded:
            bm = (k * jnp.exp(-bcum)).astype(bf16)
            bbd = jnp.concatenate([bm] * GLA_H, axis=0) * kmask
            sall = jnp.where(msk_ref[N_LEVELS + 1] > 0.0, _nt(qe, bbd), 0.0)
        else:
            sall = jnp.zeros((c, QK_W), f32)
            for lvl in range(N_LEVELS):
                ref = r_all[(lvl + 1) * c:(lvl + 2) * c]
                a = (q * jnp.exp(jnp.minimum(bcum - ref, 0.0))).astype(bf16)
                bm = (k * jnp.exp(jnp.minimum(ref - bcum, 0.0))).astype(bf16)
                bbd = jnp.concatenate([bm] * GLA_H, axis=0) * kmask
                sall = sall + jnp.where(msk_ref[lvl] > 0.0, _nt(a, bbd), 0.0)
            if not rev:
                bbd = jnp.concatenate([k_ref[rows, :]] * GLA_H, axis=0) * kmask
                sall = sall + jnp.where(msk_ref[N_LEVELS] > 0.0, _nt(q_ref[rows, :], bbd), 0.0)
        vbd = jnp.concatenate([v] * GLA_H, axis=0) * vmask
        o = o + jnp.dot(sall.astype(bf16), vbd, preferred_element_type=f32)
        o_scr[rows, :] = o
        blast = bcum[0:1] if rev else bcum[c - 1:c]
        ke = (k * jnp.exp(blast - bcum)).astype(bf16)
        dec = jnp.exp(tot[:, ch * 128:(ch + 1) * 128])
        s_scr[...] = (s_scr[...] * jnp.concatenate([dec] * GLA_H, axis=1)
                      + _tn(ke, v) * smask_ref[...])

    bounded = jnp.min(tot) > -GLA_SAFE_RANGE

    @pl.when(bounded)
    def _():
        for ch in order:
            chunk(ch, True)

    @pl.when(jnp.logical_not(bounded))
    def _():
        for ch in order:
            chunk(ch, False)

    if rev:
        gn = gn_ref[...]
        for hd in range(GLA_H):
            cols = slice(hd * GLA_DV, (hd + 1) * GLA_DV)
            t = of_ref[:, cols] + o_scr[:, cols]
            y = t * lax.rsqrt(jnp.mean(t * t, axis=-1, keepdims=True) + EPS) * gn
            out_ref[:, cols] = (y * sg_ref[:, cols].astype(f32)).astype(bf16)


def _gla_dir(rev, q, k, v, la, bsz, seq, extra=None):
    n = q.shape[0]
    tb = TB_GLA
    nb = seq // tb
    consts = _gla_consts(rev, tb)
    if rev:
        row = lambda b, i: (b * nb + nb - 1 - i, 0)
    else:
        row = lambda b, i: (b * nb + i, 0)
    const2 = lambda b, i: (0, 0)
    const3 = lambda b, i: (0, 0, 0)
    in_specs = [pl.BlockSpec((tb, QK_W), row), pl.BlockSpec((tb, QK_W), row),
                pl.BlockSpec((tb, V_W), row), pl.BlockSpec((tb, QK_W), row),
                pl.BlockSpec(consts[0].shape, const2), pl.BlockSpec(consts[1].shape, const3),
                pl.BlockSpec(consts[2].shape, const2), pl.BlockSpec(consts[3].shape, const2),
                pl.BlockSpec(consts[4].shape, const2), pl.BlockSpec(consts[5].shape, const2)]
    args = [q, k, v, la, *consts]
    scratch = [pltpu.VMEM((QK_W, V_W), f32)]
    if rev:
        o_f, sg, gn = extra
        in_specs += [pl.BlockSpec((tb, V_W), row), pl.BlockSpec((tb, V_W), row),
                     pl.BlockSpec((1, GLA_DV), const2)]
        args += [o_f, sg, gn.reshape(1, GLA_DV)]
        scratch.append(pltpu.VMEM((tb, V_W), f32))
        out_dtype = bf16
    else:
        out_dtype = f32
    return pl.pallas_call(
        functools.partial(_gla_kernel, rev, tb),
        grid=(bsz, nb),
        in_specs=in_specs,
        out_specs=pl.BlockSpec((tb, V_W), row),
        out_shape=jax.ShapeDtypeStruct((n, V_W), out_dtype),
        scratch_shapes=scratch,
        compiler_params=pltpu.CompilerParams(dimension_semantics=("arbitrary", "arbitrary")),
        name="gla_bwd" if rev else "gla_fwd",
    )(*args)


def _rpb_expand_kernel(rpb_ref, sel_ref, valid_ref, o_ref):
    e = _sel_r(rpb_ref[...], sel_ref[...])
    o_ref[...] = jnp.where(valid_ref[...] > 0.0, e, NEG)


def _na_bias_table(rpb):
    nri, nci = 2 * WIN_H - 1, 2 * WIN_W - 1
    qc = np.arange(GRID_W)
    col_start = np.clip(qc - WIN_W // 2, 0, GRID_W - WIN_W)
    kc = np.arange(GRID_W)
    valid = (kc[None, :] >= col_start[:, None]) & (kc[None, :] < col_start[:, None] + WIN_W)
    ci = kc[None, :] - qc[:, None] + WIN_W - 1
    sel = np.zeros((32, GRID_W * GRID_W), np.float32)
    flat_ci = ci.reshape(-1)
    ok = (flat_ci >= 0) & (flat_ci < nci)
    sel[flat_ci[ok], np.nonzero(ok)[0]] = 1.0
    rpb2 = jnp.zeros((NA_H * nri + 8 - (NA_H * nri) % 8, 32), f32).at[:NA_H * nri, :nci].set(
        rpb.reshape(NA_H * nri, nci))
    nrow = rpb2.shape[0]
    e = pl.pallas_call(
        _rpb_expand_kernel,
        out_shape=jax.ShapeDtypeStruct((nrow, GRID_W * GRID_W), f32),
        name="rpb_expand",
    )(rpb2, jnp.asarray(sel, bf16), jnp.asarray(valid.reshape(1, -1), f32))
    return e[:NA_H * nri].reshape(NA_H // 2, 2, nri, GRID_W, GRID_W)


def _na_row_classes():
    rows = GRID_W
    ri = -np.ones((3, NA_QROWS, NA_KROWS), np.int64)
    for cls, rb in enumerate((0, 1, rows // NA_QROWS - 1)):
        r0 = rb * NA_QROWS
        kb = int(np.clip(r0 - WIN_H // 2, 0, rows - NA_KROWS))
        for rq in range(NA_QROWS):
            r = r0 + rq
            rs = int(np.clip(r - WIN_H // 2, 0, rows - WIN_H))
            for m in range(NA_KROWS):
                kr = kb + m
                if rs <= kr < rs + WIN_H:
                    ri[cls, rq, m] = kr - r + WIN_H - 1
    return ri


def _natten_kernel(q_ref, k_ref, v_ref, qn_ref, kn_ref, gsum_ref, e_ref, o_ref, kn_scr, tbl_scr):
    b = pl.program_id(1)
    rb = pl.program_id(2)
    nrb = pl.num_programs(2)
    gsum = gsum_ref[...]
    log2e = 1.4426950408889634

    def headnorm(x, g):
        ms = _sel_r(x * x, gsum) * (1.0 / NA_DH)
        return x * lax.rsqrt(ms + EPS) * g

    @pl.when((b == 0) & (rb == 0))
    def _():
        ri = _na_row_classes()
        neg = jnp.full((GRID_W, GRID_W), NEG, f32)
        for hh in range(2):
            for cls in range(3):
                for rq in range(NA_QROWS):
                    for mp in range(NA_KROWS // 2):
                        parts = []
                        for m in (2 * mp, 2 * mp + 1):
                            r = int(ri[cls, rq, m])
                            parts.append(neg if r < 0 else e_ref[hh, r] * log2e)
                        tbl_scr[hh, cls, rq * GRID_W:(rq + 1) * GRID_W, mp * 128:(mp + 1) * 128] = (
                            jnp.concatenate(parts, axis=1))

    @pl.when(rb == 0)
    def _():
        kn_scr[...] = headnorm(k_ref[...].astype(f32), kn_ref[...]).astype(bf16)

    q = headnorm(q_ref[...].astype(f32), qn_ref[...]) * (NA_DH ** -0.5 * log2e)
    kb = jnp.clip(rb * NA_QROWS - WIN_H // 2, 0, GRID_W - NA_KROWS)
    start = pl.multiple_of(kb * GRID_W, GRID_W)
    nk = NA_KROWS * GRID_W
    kwin = kn_scr[pl.ds(start, nk), :]
    vwin = v_ref[pl.ds(start, nk), :]
    cls = jnp.where(rb == 0, 0, jnp.where(rb == nrb - 1, 2, 1))
    lane = lax.broadcasted_iota(i32, q.shape, 1)
    out = jnp.zeros(q.shape, f32)
    for hh in range(2):
        sel = (lane >= hh * NA_DH) & (lane < (hh + 1) * NA_DH)
        qh = jnp.where(sel, q, 0.0).astype(bf16)
        s = _nt(qh, kwin) + tbl_scr[hh, cls]
        m = jnp.max(s, axis=-1, keepdims=True)
        p = jnp.exp2(s - m)
        l = jnp.sum(p, axis=-1, keepdims=True)
        o = jnp.dot(p.astype(bf16), vwin, preferred_element_type=f32) / l
        out = jnp.where(sel, o, out)
    o_ref[...] = out.astype(bf16)


def _natten(qn, kn, vn, q_norm, k_norm, rpb, bsz, seq):
    n = qn.shape[0]
    e5 = _na_bias_table(rpb)
    nq = NA_QROWS * GRID_W
    nrb = seq // nq
    lane_h = np.arange(128) // NA_DH
    gsum = jnp.asarray(lane_h[:, None] == lane_h[None, :], bf16)
    qn2 = jnp.tile(q_norm.reshape(1, NA_DH), (1, 2))
    kn2 = jnp.tile(k_norm.reshape(1, NA_DH), (1, 2))
    const2 = lambda p, b, r: (0, 0)
    return pl.pallas_call(
        _natten_kernel,
        grid=(NA_H // 2, bsz, nrb),
        in_specs=[pl.BlockSpec((nq, 128), lambda p, b, r: (b * nrb + r, p)),
                  pl.BlockSpec((seq, 128), lambda p, b, r: (b, p)),
                  pl.BlockSpec((seq, 128), lambda p, b, r: (b, p)),
                  pl.BlockSpec((1, 128), const2),
                  pl.BlockSpec((1, 128), const2),
                  pl.BlockSpec((128, 128), const2),
                  pl.BlockSpec((None,) + e5.shape[1:], lambda p, b, r: (p, 0, 0, 0, 0))],
        out_specs=pl.BlockSpec((nq, 128), lambda p, b, r: (b * nrb + r, p)),
        out_shape=jax.ShapeDtypeStruct((n, NA_W), bf16),
        scratch_shapes=[pltpu.VMEM((seq, 128), bf16),
                        pltpu.VMEM((2, 3, nq, NA_KROWS * GRID_W), f32)],
        compiler_params=pltpu.CompilerParams(dimension_semantics=("arbitrary", "arbitrary", "arbitrary")),
        name="natten",
    )(qn, kn, vn, qn2, kn2, gsum, e5)


def _top2(vals):
    io = lax.broadcasted_iota(i32, vals.shape, 0)
    m1 = jnp.max(vals, axis=0, keepdims=True)
    i1 = jnp.min(jnp.where(vals == m1, io, EPG), axis=0, keepdims=True)
    v2 = jnp.where(io == i1, -jnp.inf, vals)
    m2 = jnp.max(v2, axis=0, keepdims=True)
    i2 = jnp.min(jnp.where(v2 == m2, io, EPG), axis=0, keepdims=True)
    return m1, i1, m2, i2


def _outproj_kernel(og_ref, on_ref, wo_ref, x_ref, mod_ref, nrm_ref, rw_ref, rb_ref,
                    x1_ref, h2_ref, eidx_ref, gw_ref):
    mix = (jnp.dot(og_ref[...], wo_ref[0:V_W, :], preferred_element_type=f32)
           + jnp.dot(on_ref[...], wo_ref[V_W:V_W + NA_W, :], preferred_element_type=f32))
    x1 = x_ref[...] + mod_ref[2] * mix
    x1_ref[...] = x1
    y = x1 * lax.rsqrt(jnp.mean(x1 * x1, axis=-1, keepdims=True) + EPS) * nrm_ref[...]
    h2 = y * (1.0 + mod_ref[4]) + mod_ref[3]
    h2_ref[...] = h2
    rw = rw_ref[...]
    rw_hi = rw.astype(bf16)
    rw_lo = (rw - rw_hi.astype(f32)).astype(bf16)
    h_hi = h2.astype(bf16)
    h_lo = (h2 - h_hi.astype(f32)).astype(bf16)
    logits = _nt(rw_hi, h_hi) + _nt(rw_hi, h_lo) + _nt(rw_lo, h_hi)
    scores = jax.nn.sigmoid(logits)
    sel = scores + rb_ref[...]
    tops = [_top2(sel[g * EPG:(g + 1) * EPG]) for g in range(N_GRP)]
    best = jnp.zeros_like(tops[0][1])
    bs = tops[0][0] + tops[0][2]
    for g in range(1, N_GRP):
        gs = tops[g][0] + tops[g][2]
        take = gs > bs
        best = jnp.where(take, g, best)
        bs = jnp.where(take, gs, bs)
    io = lax.broadcasted_iota(i32, (EPG, sel.shape[1]), 0)
    i1 = jnp.zeros_like(best)
    i2 = jnp.zeros_like(best)
    s1 = jnp.zeros(best.shape, f32)
    s2 = jnp.zeros(best.shape, f32)
    for g in range(N_GRP):
        sc = scores[g * EPG:(g + 1) * EPG]
        pick = best == g
        i1 = jnp.where(pick, tops[g][1], i1)
        i2 = jnp.where(pick, tops[g][3], i2)
        s1 = jnp.where(pick, jnp.sum(jnp.where(io == tops[g][1], sc, 0.0), axis=0, keepdims=True), s1)
        s2 = jnp.where(pick, jnp.sum(jnp.where(io == tops[g][3], sc, 0.0), axis=0, keepdims=True), s2)
    eidx_ref[0:1, :] = best * EPG + i1
    eidx_ref[1:2, :] = best * EPG + i2
    tot = s1 + s2
    gw_ref[0:1, :] = s1 / tot
    gw_ref[1:2, :] = s2 / tot


def _outproj(o_gla, o_na, w_out, x2d, mod_l, nrm, router_w, router_b, bsz, seq):
    n = x2d.shape[0]
    tm = TM_PROJ
    nt = seq // tm
    row = lambda b, i: (b * nt + i, 0)
    col = lambda b, i: (0, b * nt + i)
    const = lambda b, i: (0, 0)
    return pl.pallas_call(
        _outproj_kernel,
        grid=(bsz, nt),
        in_specs=[pl.BlockSpec((tm, V_W), row),
                  pl.BlockSpec((tm, NA_W), row),
                  pl.BlockSpec((V_W + NA_W, D), const),
                  pl.BlockSpec((tm, D), row),
                  pl.BlockSpec((6, None, 1, D), lambda b, i: (0, b, 0, 0)),
                  pl.BlockSpec((1, D), const),
                  pl.BlockSpec((N_EXP, D), const),
                  pl.BlockSpec((N_EXP, 1), const)],
        out_specs=[pl.BlockSpec((tm, D), row), pl.BlockSpec((tm, D), row),
                   pl.BlockSpec((2, tm), col), pl.BlockSpec((2, tm), col)],
        out_shape=[jax.ShapeDtypeStruct((n, D), f32), jax.ShapeDtypeStruct((n, D), f32),
                   jax.ShapeDtypeStruct((2, n), i32), jax.ShapeDtypeStruct((2, n), f32)],
        compiler_params=pltpu.CompilerParams(dimension_semantics=("arbitrary", "arbitrary")),
        name="outproj_router",
    )(o_gla, o_na, w_out.astype(bf16), x2d, mod_l, nrm.reshape(1, D),
      router_w.T, router_b.reshape(N_EXP, 1))


def _rank_kernel(eidx_ref, tri_ref, rank_ref, cnt_ref, carry):
    @pl.when(pl.program_id(0) == 0)
    def _():
        carry[...] = jnp.zeros_like(carry)

    e = eidx_ref[...]
    tm = e.shape[1]
    io = lax.broadcasted_iota(i32, (N_EXP, tm), 0)
    run = carry[...]
    for k in range(2):
        oh = io == e[k:k + 1, :]
        ohf = oh.astype(f32)
        pre = jnp.dot(ohf.astype(bf16), tri_ref[...], preferred_element_type=f32) + run[:, 0:1]
        rank_ref[k:k + 1, :] = jnp.sum(jnp.where(oh, pre, 0.0), axis=0, keepdims=True).astype(i32)
        run = run + jnp.sum(ohf, axis=1, keepdims=True)
    carry[...] = run
    cnt_ref[...] = run


def _ranks(eidx):
    n = eidx.shape[1]
    tm = TM_RANK
    t = np.arange(tm)
    tri = jnp.asarray(t[:, None] < t[None, :], bf16)
    return pl.pallas_call(
        _rank_kernel,
        grid=(n // tm,),
        in_specs=[pl.BlockSpec((2, tm), lambda i: (0, i)), pl.BlockSpec((tm, tm), lambda i: (0, 0))],
        out_specs=[pl.BlockSpec((2, tm), lambda i: (0, i)), pl.BlockSpec((N_EXP, 128), lambda i: (0, 0))],
        out_shape=[jax.ShapeDtypeStruct((2, n), i32), jax.ShapeDtypeStruct((N_EXP, 128), f32)],
        scratch_shapes=[pltpu.VMEM((N_EXP, 128), f32)],
        compiler_params=pltpu.CompilerParams(dimension_semantics=("arbitrary",)),
        name="moe_rank",
    )(eidx, tri)


def _dispatch_kernel(dest_ref, tails_ref, h_ref, xb_ref, zbuf, hbuf, lsem, rsem, zsem):
    i = pl.program_id(0)
    nsteps = pl.num_programs(0)
    tm = TM_DISP
    n = nsteps * tm
    nbuf = hbuf.shape[0]

    def load(j):
        start = pl.multiple_of(j * tm, tm)
        return pltpu.make_async_copy(h_ref.at[pl.ds(start, tm)], hbuf.at[j % nbuf], lsem.at[j % nbuf])

    def drain_rows(j):
        for _ in range(2):
            pltpu.make_async_copy(hbuf.at[j % nbuf], xb_ref.at[pl.ds(0, tm)], rsem.at[j % nbuf]).wait()

    def tail_copy(e):
        start = pl.multiple_of(tails_ref[e], MOE_BLK)
        return pltpu.make_async_copy(zbuf, xb_ref.at[pl.ds(start, MOE_BLK)], zsem)

    @pl.when(i == 0)
    def _():
        zbuf[...] = jnp.zeros_like(zbuf)
        for e in range(2 * N_EXP):
            @pl.when(tails_ref[e] >= 0)
            def _():
                tail_copy(e).start()
        for e in range(2 * N_EXP):
            @pl.when(tails_ref[e] >= 0)
            def _():
                tail_copy(e).wait()
        load(i).start()

    @pl.when(i >= nbuf - 1)
    def _():
        drain_rows(i + 1 - nbuf)

    @pl.when(i + 1 < nsteps)
    def _():
        load(i + 1).start()

    load(i).wait()
    slot = i % nbuf

    def issue(t, carry):
        src = hbuf.at[slot, pl.ds(t, 1)]
        for k in range(2):
            d = dest_ref[k * n + i * tm + t]
            pltpu.make_async_copy(src, xb_ref.at[pl.ds(d, 1)], rsem.at[slot]).start()
        return carry

    lax.fori_loop(0, tm, issue, 0, unroll=4)

    @pl.when(i == nsteps - 1)
    def _():
        for back in range(nbuf - 2, -1, -1):
            drain_rows(i - back)


def _dispatch(h2, dest_flat, tails, p_rows):
    n = h2.shape[0]
    tm = TM_DISP
    return pl.pallas_call(
        _dispatch_kernel,
        grid_spec=pltpu.PrefetchScalarGridSpec(
            num_scalar_prefetch=2,
            grid=(n // tm,),
            in_specs=[pl.BlockSpec(memory_space=pl.ANY)],
            out_specs=pl.BlockSpec(memory_space=pl.ANY),
            scratch_shapes=[pltpu.VMEM((MOE_BLK, D), f32), pltpu.VMEM((3, tm, D), f32),
                            pltpu.SemaphoreType.DMA((3,)), pltpu.SemaphoreType.DMA((3,)),
                            pltpu.SemaphoreType.DMA(())],
        ),
        out_shape=jax.ShapeDtypeStruct((p_rows, D), f32),
        compiler_params=pltpu.CompilerParams(dimension_semantics=("arbitrary",)),
        name="moe_dispatch",
    )(dest_flat, tails, h2)


def _expert_kernel(blk_e_ref, nb_ref, xb_ref, w1_ref, w3_ref, w2_ref, y_ref, w1b, w3b, w2b):
    i = pl.program_id(0)

    @pl.when((i < nb_ref[0]) & ((i == 0) | (blk_e_ref[i] != blk_e_ref[jnp.maximum(i - 1, 0)])))
    def _():
        w1b[...] = w1_ref[...].astype(bf16)
        w3b[...] = w3_ref[...].astype(bf16)
        w2b[...] = w2_ref[...].astype(bf16)

    @pl.when(i < nb_ref[0])
    def _():
        x = xb_ref[...].astype(bf16)
        a = jnp.dot(x, w1b[...], preferred_element_type=f32)
        b = jnp.dot(x, w3b[...], preferred_element_type=f32)
        h = (a * jax.nn.sigmoid(a) * b).astype(bf16)
        y_ref[...] = jnp.dot(h, w2b[...], preferred_element_type=f32)

    @pl.when(i >= nb_ref[0])
    def _():
        y_ref[...] = jnp.zeros_like(y_ref)


def _experts(xb, blk_e, nb_used, w1, w3, w2, layer):
    p_rows = xb.shape[0]
    nb = p_rows // MOE_BLK

    def xmap(i, be, nbu):
        return (jnp.minimum(i, nbu[0] - 1), 0)

    def wmap(i, be, nbu):
        return (layer, be[i], 0, 0)

    return pl.pallas_call(
        _expert_kernel,
        grid_spec=pltpu.PrefetchScalarGridSpec(
            num_scalar_prefetch=2,
            grid=(nb,),
            in_specs=[pl.BlockSpec((MOE_BLK, D), xmap),
                      pl.BlockSpec((None, None, D, D_EXP), wmap),
                      pl.BlockSpec((None, None, D, D_EXP), wmap),
                      pl.BlockSpec((None, None, D_EXP, D), wmap)],
            out_specs=pl.BlockSpec((MOE_BLK, D), lambda i, be, nbu: (i, 0)),
            scratch_shapes=[pltpu.VMEM((D, D_EXP), bf16), pltpu.VMEM((D, D_EXP), bf16),
                            pltpu.VMEM((D_EXP, D), bf16)],
        ),
        out_shape=jax.ShapeDtypeStruct((p_rows, D), f32),
        compiler_params=pltpu.CompilerParams(dimension_semantics=("arbitrary",),
                                             vmem_limit_bytes=48 * 1024 * 1024),
        name="moe_experts",
    )(blk_e, nb_used, xb, w1, w3, w2)


def _combine_kernel(dest_ref, x1_ref, gw_ref, mod_ref, y_ref, o_ref, ybuf, sems):
    i = pl.program_id(0)
    nsteps = pl.num_programs(0)
    tm = x1_ref.shape[0]
    n = nsteps * tm

    def issue_tile(j):
        slot = j % 2

        def issue(t, carry):
            for k in range(2):
                d = dest_ref[k * n + j * tm + t]
                pltpu.make_async_copy(y_ref.at[pl.ds(d, 1)], ybuf.at[slot, k, pl.ds(t, 1)],
                                      sems.at[slot]).start()
            return carry

        lax.fori_loop(0, tm, issue, 0, unroll=8)

    @pl.when(i == 0)
    def _():
        issue_tile(i)

    @pl.when(i + 1 < nsteps)
    def _():
        issue_tile(i + 1)

    slot = i % 2
    for k in range(2):
        pltpu.make_async_copy(y_ref.at[pl.ds(0, tm)], ybuf.at[slot, k], sems.at[slot]).wait()
    gw = gw_ref[...]
    y = ybuf[slot, 0] * gw[:, 0:1] + ybuf[slot, 1] * gw[:, 1:2]
    o_ref[...] = x1_ref[...] + mod_ref[5] * y


def _combine(y, dest_flat, x1, gw_t, mod_l, bsz, seq):
    n = x1.shape[0]
    tm = TM_COMB
    nt = seq // tm
    return pl.pallas_call(
        _combine_kernel,
        grid_spec=pltpu.PrefetchScalarGridSpec(
            num_scalar_prefetch=1,
            grid=(n // tm,),
            in_specs=[pl.BlockSpec((tm, D), lambda i, *_: (i, 0)),
                      pl.BlockSpec((tm, 2), lambda i, *_: (i, 0)),
                      pl.BlockSpec((6, None, 1, D), lambda i, *_: (0, i // nt, 0, 0)),
                      pl.BlockSpec(memory_space=pl.ANY)],
            out_specs=pl.BlockSpec((tm, D), lambda i, *_: (i, 0)),
            scratch_shapes=[pltpu.VMEM((2, 2, tm, D), f32), pltpu.SemaphoreType.DMA((2,))],
        ),
        out_shape=jax.ShapeDtypeStruct((n, D), f32),
        compiler_params=pltpu.CompilerParams(dimension_semantics=("arbitrary",)),
        name="moe_combine",
    )(dest_flat, x1, gw_t, mod_l, y)


def _grouped_moe(h2, eidx, gw, x1, mod_l, w1, w3, w2, layer, bsz, seq):
    n = h2.shape[0]
    p_rows = (n * 2 // MOE_BLK + N_EXP) * MOE_BLK
    nb = p_rows // MOE_BLK
    rank, cnt = _ranks(eidx)
    counts = cnt[:, 0].astype(i32)
    padded = (counts + MOE_BLK - 1) // MOE_BLK * MOE_BLK
    pad_end = jnp.cumsum(padded)
    pad_start = pad_end - padded
    start_of = jnp.sum(jnp.where(eidx[..., None] == jnp.arange(N_EXP, dtype=i32), pad_start, 0), axis=-1)
    dest_flat = (start_of + rank).reshape(-1)
    nb_used = (pad_end[-1] // MOE_BLK).astype(i32)
    spare = nb_used + jnp.arange(N_EXP, dtype=i32)
    tails = jnp.concatenate([jnp.where(padded > 0, pad_end - MOE_BLK, -1),
                             jnp.where(spare < nb, spare * MOE_BLK, -1)]).astype(i32)
    blk = jnp.minimum(jnp.arange(nb, dtype=i32), nb_used - 1)
    seg_done = (pad_end[None, :] <= (blk * MOE_BLK)[:, None]).astype(i32)
    blk_e = jnp.minimum(jnp.sum(seg_done, axis=1), N_EXP - 1).astype(i32)
    xb = _dispatch(h2, dest_flat, tails, p_rows)
    y = _experts(xb, blk_e, nb_used.reshape(1), w1, w3, w2, layer)
    return _combine(y, dest_flat, x1, gw.T, mod_l, bsz, seq)


def kernel(x, c, w_ada, b_ada, attn_norm, ffn_norm, w_in, gla_gate_w, gla_gate_b, gla_out_norm,
           na_q_norm, na_k_norm, na_rpb, w_out, router_w, router_b, w1, w3, w2):
    bsz, seq, _ = x.shape
    depth = w_ada.shape[0]
    mod = _adaln_mod(c, w_ada, b_ada)
    xc = x.reshape(bsz * seq, D)
    for l in range(depth):
        qg, kg, vg, sg, la_f, la_b, qn, kn, vn = _inproj(
            xc, mod[l], attn_norm[l], w_in[l], gla_gate_w[l], gla_gate_b[l], bsz, seq)
        o_f = _gla_dir(False, qg, kg, vg, la_f, bsz, seq)
        o_gla = _gla_dir(True, qg, kg, vg, la_b, bsz, seq, extra=(o_f, sg, gla_out_norm[l]))
        o_na = _natten(qn, kn, vn, na_q_norm[l], na_k_norm[l], na_rpb[l], bsz, seq)
        x1, h2, eidx, gw = _outproj(o_gla, o_na, w_out[l], xc, mod[l], ffn_norm[l],
                                    router_w, router_b, bsz, seq)
        xc = _grouped_moe(h2, eidx, gw, x1, mod[l], w1, w3, w2, l, bsz, seq)
    return xc.reshape(bsz, seq, D)
```

```python
import functools

import numpy as np
import jax
import jax.numpy as jnp
from jax import lax
from jax.experimental import pallas as pl
from jax.experimental.pallas import tpu as pltpu

f32 = jnp.float32
bf16 = jnp.bfloat16
i32 = jnp.int32
HIGHEST = lax.Precision.HIGHEST

D = 1024
GRID_W = 64
GLA_H, GLA_DK, GLA_DV = 4, 64, 128
GLA_RANK = 16
GLA_TAU = 16.0
CHUNK = 64
NA_H, NA_DH = 8, 64
WIN_H, WIN_W = 8, 16
N_EXP, N_GRP, EPG = 32, 4, 8
D_EXP = 512
QK_W = GLA_H * GLA_DK
V_W = GLA_H * GLA_DV
NA_W = NA_H * NA_DH
EPS = 1e-6
NEG = -1e30

TM_PROJ = 512
TB_GLA = 256
NA_QROWS = 4
NA_KROWS = 12
TM_RANK = 512
TM_DISP = 256
TM_COMB = 256
MOE_BLK = 512
N_LEVELS = 6
GLA_SAFE_RANGE = 40.0


def _nt(a, b):
    return lax.dot_general(a, b, (((1,), (1,)), ((), ())), preferred_element_type=f32)


def _tn(a, b):
    return lax.dot_general(a, b, (((0,), (0,)), ((), ())), preferred_element_type=f32)


def _split3(x):
    hi = x.astype(bf16)
    r = x - hi.astype(f32)
    mid = r.astype(bf16)
    lo = (r - mid.astype(f32)).astype(bf16)
    return hi, mid, lo


def _sel_l(m01, x):
    hi, mid, lo = _split3(x)
    d = lambda p: jnp.dot(m01, p, preferred_element_type=f32)
    return d(hi) + d(mid) + d(lo)


def _sel_r(x, m01):
    hi, mid, lo = _split3(x)
    d = lambda p: jnp.dot(p, m01, preferred_element_type=f32)
    return d(hi) + d(mid) + d(lo)


def _mod_kernel(c_ref, w_ref, b_ref, o_ref):
    c = c_ref[...]
    ca = c * jax.nn.sigmoid(c)
    o_ref[0, 0] = jnp.dot(ca, w_ref[0], precision=HIGHEST, preferred_element_type=f32) + b_ref[0, 0]


def _adaln_mod(c, w_ada, b_ada):
    depth = w_ada.shape[0]
    bsz = c.shape[0]
    cp = jnp.zeros((8, D), f32).at[:bsz].set(c)
    out = pl.pallas_call(
        _mod_kernel,
        grid=(depth, 6),
        in_specs=[pl.BlockSpec((8, D), lambda l, j: (0, 0)),
                  pl.BlockSpec((1, D, D), lambda l, j: (l, 0, j)),
                  pl.BlockSpec((1, 1, 1, D), lambda l, j: (l, j, 0, 0))],
        out_specs=pl.BlockSpec((1, 1, 8, D), lambda l, j: (l, j, 0, 0)),
        out_shape=jax.ShapeDtypeStruct((depth, 6, 8, D), f32),
        name="adaln_mod",
    )(cp, w_ada, b_ada.reshape(depth, 6, 1, D))
    return out[:, :, :bsz].reshape(depth, 6, bsz, 1, D)


def _log_sigmoid(z):
    return jnp.minimum(z, 0.0) - jnp.log1p(jnp.exp(-jnp.abs(z)))


def _inproj_kernel(x_ref, mod_ref, nrm_ref, wm_ref, wlr_ref, gf_ref, gb_ref, bf_ref, bb_ref,
                   qg_ref, kg_ref, vg_ref, sg_ref, laf_ref, lab_ref, qn_ref, kn_ref, vn_ref, lamin_ref):
    x = x_ref[...]
    y = x * lax.rsqrt(jnp.mean(x * x, axis=-1, keepdims=True) + EPS) * nrm_ref[...]
    h = y * (1.0 + mod_ref[1]) + mod_ref[0]
    hb = h.astype(bf16)
    main = jnp.dot(hb, wm_ref[...], preferred_element_type=f32)
    qg_ref[...] = (main[:, 0:256] * (GLA_DK ** -0.5)).astype(bf16)
    kg_ref[...] = main[:, 256:512].astype(bf16)
    vg_ref[...] = main[:, 512:1024].astype(bf16)
    gg = main[:, 1024:1536]
    sg_ref[...] = (gg * jax.nn.sigmoid(gg)).astype(bf16)
    qn_ref[...] = main[:, 1536:2048].astype(bf16)
    kn_ref[...] = main[:, 2048:2560].astype(bf16)
    vn_ref[...] = main[:, 2560:3072].astype(bf16)
    lr = jnp.dot(hb, wlr_ref[...], preferred_element_type=f32)
    zf = jnp.dot(lr, gf_ref[...], precision=HIGHEST, preferred_element_type=f32) + bf_ref[...]
    zb = jnp.dot(lr, gb_ref[...], precision=HIGHEST, preferred_element_type=f32) + bb_ref[...]
    la_f = _log_sigmoid(zf) * (1.0 / GLA_TAU)
    la_b = _log_sigmoid(zb) * (1.0 / GLA_TAU)
    laf_ref[...] = la_f
    lab_ref[...] = la_b
    rows = []
    for la in (la_f, la_b):
        for blk in range(la.shape[0] // TB_GLA):
            tot = jnp.sum(la[blk * TB_GLA:(blk + 1) * TB_GLA], axis=0, keepdims=True)
            rows.append(jnp.broadcast_to(jnp.min(tot, axis=1, keepdims=True), (1, 128)))
    rows.append(jnp.zeros((8 - len(rows), 128), f32))
    lamin_ref[...] = jnp.concatenate(rows, axis=0)


def _inproj(x2d, mod_l, nrm, w_in, gate_w, gate_b, bsz, seq):
    n = x2d.shape[0]
    tm = TM_PROJ
    nt = seq // tm
    lo, hi = 2 * QK_W + 2 * V_W, 2 * QK_W + 2 * V_W + 2 * GLA_RANK
    wm = jnp.concatenate([w_in[:, :lo], w_in[:, hi:]], axis=1).astype(bf16)
    wlr = w_in[:, lo:hi].astype(bf16)
    zpad = jnp.zeros((GLA_RANK, QK_W), f32)
    gf = jnp.concatenate([gate_w[0], zpad], axis=0)
    gb = jnp.concatenate([zpad, gate_w[1]], axis=0)
    row = lambda b, i: (b * nt + i, 0)
    const = lambda b, i: (0, 0)
    widths = [(QK_W, bf16), (QK_W, bf16), (V_W, bf16), (V_W, bf16), (QK_W, f32), (QK_W, f32),
              (NA_W, bf16), (NA_W, bf16), (NA_W, bf16)]
    outs = pl.pallas_call(
        _inproj_kernel,
        grid=(bsz, nt),
        in_specs=[pl.BlockSpec((tm, D), row),
                  pl.BlockSpec((6, None, 1, D), lambda b, i: (0, b, 0, 0)),
                  pl.BlockSpec((1, D), const),
                  pl.BlockSpec(wm.shape, const),
                  pl.BlockSpec(wlr.shape, const),
                  pl.BlockSpec(gf.shape, const),
                  pl.BlockSpec(gb.shape, const),
                  pl.BlockSpec((1, QK_W), const),
                  pl.BlockSpec((1, QK_W), const)],
        out_specs=[pl.BlockSpec((tm, w), row) for w, _ in widths]
        + [pl.BlockSpec((None, 8, 128), lambda b, i: (b * nt + i, 0, 0))],
        out_shape=[jax.ShapeDtypeStruct((n, w), dt) for w, dt in widths]
        + [jax.ShapeDtypeStruct((n // tm, 8, 128), f32)],
        compiler_params=pltpu.CompilerParams(dimension_semantics=("arbitrary", "arbitrary")),
        name="inproj",
    )(x2d, mod_l, nrm.reshape(1, D), wm, wlr, gf, gb, gate_b[0].reshape(1, QK_W), gate_b[1].reshape(1, QK_W))
    per = tm // TB_GLA
    lamin = outs[-1][:, :2 * per, 0]
    bounded_f = (lamin[:, :per].reshape(-1) > -GLA_SAFE_RANGE).astype(i32)
    bounded_b = (lamin[:, per:].reshape(-1) > -GLA_SAFE_RANGE).astype(i32)
    return outs[:-1], bounded_f, bounded_b


def _gla_consts(rev, tb):
    c = CHUNK
    idx = np.arange(c)
    if not rev:
        cum = (idx[None, :] <= idx[:, None]).astype(np.float32)
    else:
        cum = (idx[None, :] >= idx[:, None]).astype(np.float32)
    mats = [cum]
    masks = []
    for lvl in range(N_LEVELS):
        s = c >> (lvl + 1)
        blk = idx // (2 * s)
        second = (idx % (2 * s)) >= s
        ref_row = blk * 2 * s + (s if rev else s - 1)
        mats.append(cum[ref_row])
        same = blk[:, None] == blk[None, :]
        if not rev:
            m = same & second[:, None] & (~second[None, :])
        else:
            m = same & (~second[:, None]) & second[None, :]
        masks.append(m)
    masks.append(np.eye(c, dtype=bool) & (not rev))
    mst = np.concatenate(mats, axis=0)
    msk = np.stack([np.tile(m, (1, GLA_H)) for m in masks]).astype(np.float32)
    tix = np.arange(tb)
    tri = (tix[None, :] >= tix[:, None]) if rev else (tix[None, :] <= tix[:, None])
    cmask = (tix[:, None] < tix[None, :]) if rev else (tix[:, None] >= tix[None, :])
    nch = tb // c
    selb = np.zeros((tb, nch * 128), np.float32)
    for ch in range(nch):
        selb[ch * c:(ch + 1) * c, ch * 128:(ch + 1) * 128] = 1.0
    hk = np.arange(QK_W) // GLA_DK
    hv = np.arange(V_W) // GLA_DV
    kmask = (hk[:, None] == hk[None, :]).astype(np.float32)
    vmask = (hk[:, None] == hv[None, :]).astype(np.float32)
    return (jnp.asarray(mst, bf16), jnp.asarray(msk, f32), jnp.asarray(selb, bf16),
            jnp.asarray(kmask, bf16), jnp.asarray(vmask, bf16), jnp.asarray(vmask, f32),
            jnp.asarray(tri, bf16), jnp.asarray(cmask, f32), jnp.ones((tb, 128), bf16))


def _gla_kernel(rev, tb, *refs):
    if rev:
        (bounded_ref, q_ref, k_ref, v_ref, la_ref, mst_ref, msk_ref, selb_ref, kmask_ref, vmask_ref,
         smask_ref, tri_ref, cmask_ref, ones_ref, of_ref, sg_ref, gn_ref, out_ref, s_scr, o_scr) = refs
    else:
        (bounded_ref, q_ref, k_ref, v_ref, la_ref, mst_ref, msk_ref, selb_ref, kmask_ref, vmask_ref,
         smask_ref, tri_ref, cmask_ref, ones_ref, out_ref, s_scr) = refs
        o_scr = out_ref
    c = CHUNK
    nch = tb // c

    @pl.when(pl.program_id(1) == 0)
    def _():
        s_scr[...] = jnp.zeros_like(s_scr)

    def block_bounded():
        la = la_ref[...]
        tot_blk = _sel_r(la.T, ones_ref[...])
        la_hi = la.astype(bf16)
        la_lo = (la - la_hi.astype(f32)).astype(bf16)
        tri = tri_ref[...]
        bcum = (jnp.dot(tri, la_hi, preferred_element_type=f32)
                + jnp.dot(tri, la_lo, preferred_element_type=f32))
        q = q_ref[...].astype(f32)
        k = k_ref[...].astype(f32)
        v = v_ref[...]
        qe = (q * jnp.exp(bcum)).astype(bf16)
        kx = (k * jnp.exp(-bcum)).astype(bf16)
        s_prev = s_scr[...]
        o_inter = jnp.dot(qe, s_prev.astype(bf16), preferred_element_type=f32)
        lane_head = lax.broadcasted_iota(i32, qe.shape, 1) // GLA_DK
        cmask = cmask_ref[...]
        for hd in range(GLA_H):
            cols = slice(hd * GLA_DV, (hd + 1) * GLA_DV)
            qh = jnp.where(lane_head == hd, qe, jnp.zeros_like(qe))
            ph = jnp.where(cmask > 0.0, _nt(qh, kx), 0.0).astype(bf16)
            o_scr[:, cols] = o_inter[:, cols] + jnp.dot(ph, v[:, cols], preferred_element_type=f32)
        blast = bcum[0:1] if rev else bcum[tb - 1:tb]
        ke = (k * jnp.exp(blast - bcum)).astype(bf16)
        dec = jnp.exp(tot_blk)
        s_scr[...] = s_prev * jnp.concatenate([dec] * GLA_H, axis=1) + _tn(ke, v) * smask_ref[...]

    def chunks_robust():
        tot = _sel_r(la_ref[...].T, selb_ref[...])
        mst = mst_ref[...]
        kmask = kmask_ref[...]
        vmask = vmask_ref[...]
        for ch in (range(nch - 1, -1, -1) if rev else range(nch)):
            rows = slice(ch * c, (ch + 1) * c)
            q = q_ref[rows, :].astype(f32)
            k = k_ref[rows, :].astype(f32)
            v = v_ref[rows, :]
            r_all = _sel_l(mst, la_ref[rows, :])
            bcum = r_all[0:c]
            qe = (q * jnp.exp(bcum)).astype(bf16)
            o = jnp.dot(qe, s_scr[...].astype(bf16), preferred_element_type=f32)
            sall = jnp.zeros((c, QK_W), f32)
            for lvl in range(N_LEVELS):
                ref = r_all[(lvl + 1) * c:(lvl + 2) * c]
                a = (q * jnp.exp(jnp.minimum(bcum - ref, 0.0))).astype(bf16)
                bm = (k * jnp.exp(jnp.minimum(ref - bcum, 0.0))).astype(bf16)
                bbd = jnp.concatenate([bm] * GLA_H, axis=0) * kmask
                sall = sall + jnp.where(msk_ref[lvl] > 0.0, _nt(a, bbd), 0.0)
            if not rev:
                bbd = jnp.concatenate([k_ref[rows, :]] * GLA_H, axis=0) * kmask
                sall = sall + jnp.where(msk_ref[N_LEVELS] > 0.0, _nt(q_ref[rows, :], bbd), 0.0)
            vbd = jnp.concatenate([v] * GLA_H, axis=0) * vmask
            o_scr[rows, :] = o + jnp.dot(sall.astype(bf16), vbd, preferred_element_type=f32)
            blast = bcum[0:1] if rev else bcum[c - 1:c]
            ke = (k * jnp.exp(blast - bcum)).astype(bf16)
            dec = jnp.exp(tot[:, ch * 128:(ch + 1) * 128])
            s_scr[...] = (s_scr[...] * jnp.concatenate([dec] * GLA_H, axis=1)
                          + _tn(ke, v) * smask_ref[...])

    nb = pl.num_programs(1)
    blk = pl.program_id(0) * nb + (nb - 1 - pl.program_id(1) if rev else pl.program_id(1))
    bounded = bounded_ref[blk] > 0
    pl.when(bounded)(block_bounded)
    pl.when(jnp.logical_not(bounded))(chunks_robust)

    if rev:
        gn = gn_ref[...]
        for hd in range(GLA_H):
            cols = slice(hd * GLA_DV, (hd + 1) * GLA_DV)
            t = of_ref[:, cols] + o_scr[:, cols]
            y = t * lax.rsqrt(jnp.mean(t * t, axis=-1, keepdims=True) + EPS) * gn
            out_ref[:, cols] = (y * sg_ref[:, cols].astype(f32)).astype(bf16)


def _gla_dir(rev, q, k, v, la, bounded, bsz, seq, extra=None):
    n = q.shape[0]
    tb = TB_GLA
    nb = seq // tb
    consts = _gla_consts(rev, tb)
    if rev:
        row = lambda b, i, flags: (b * nb + nb - 1 - i, 0)
    else:
        row = lambda b, i, flags: (b * nb + i, 0)
    const2 = lambda b, i, flags: (0, 0)
    const3 = lambda b, i, flags: (0, 0, 0)
    in_specs = [pl.BlockSpec((tb, QK_W), row), pl.BlockSpec((tb, QK_W), row),
                pl.BlockSpec((tb, V_W), row), pl.BlockSpec((tb, QK_W), row),
                pl.BlockSpec(consts[0].shape, const2), pl.BlockSpec(consts[1].shape, const3),
                pl.BlockSpec(consts[2].shape, const2), pl.BlockSpec(consts[3].shape, const2),
                pl.BlockSpec(consts[4].shape, const2), pl.BlockSpec(consts[5].shape, const2),
                pl.BlockSpec(consts[6].shape, const2), pl.BlockSpec(consts[7].shape, const2),
                pl.BlockSpec(consts[8].shape, const2)]
    args = [q, k, v, la, *consts]
    scratch = [pltpu.VMEM((QK_W, V_W), f32)]
    if rev:
        o_f, sg, gn = extra
        in_specs += [pl.BlockSpec((tb, V_W), row), pl.BlockSpec((tb, V_W), row),
                     pl.BlockSpec((1, GLA_DV), const2)]
        args += [o_f, sg, gn.reshape(1, GLA_DV)]
        scratch.append(pltpu.VMEM((tb, V_W), f32))
        out_dtype = bf16
    else:
        out_dtype = f32
    return pl.pallas_call(
        functools.partial(_gla_kernel, rev, tb),
        grid_spec=pltpu.PrefetchScalarGridSpec(
            num_scalar_prefetch=1,
            grid=(bsz, nb),
            in_specs=in_specs,
            out_specs=pl.BlockSpec((tb, V_W), row),
            scratch_shapes=scratch,
        ),
        out_shape=jax.ShapeDtypeStruct((n, V_W), out_dtype),
        compiler_params=pltpu.CompilerParams(dimension_semantics=("arbitrary", "arbitrary")),
        name="gla_bwd" if rev else "gla_fwd",
    )(bounded, *args)


def _rpb_expand_kernel(rpb_ref, sel_ref, valid_ref, o_ref):
    e = _sel_r(rpb_ref[...], sel_ref[...])
    o_ref[...] = jnp.where(valid_ref[...] > 0.0, e, NEG)


def _na_bias_table(rpb):
    nri, nci = 2 * WIN_H - 1, 2 * WIN_W - 1
    qc = np.arange(GRID_W)
    col_start = np.clip(qc - WIN_W // 2, 0, GRID_W - WIN_W)
    kc = np.arange(GRID_W)
    valid = (kc[None, :] >= col_start[:, None]) & (kc[None, :] < col_start[:, None] + WIN_W)
    ci = kc[None, :] - qc[:, None] + WIN_W - 1
    sel = np.zeros((32, GRID_W * GRID_W), np.float32)
    flat_ci = ci.reshape(-1)
    ok = (flat_ci >= 0) & (flat_ci < nci)
    sel[flat_ci[ok], np.nonzero(ok)[0]] = 1.0
    rpb2 = jnp.zeros((NA_H * nri + 8 - (NA_H * nri) % 8, 32), f32).at[:NA_H * nri, :nci].set(
        rpb.reshape(NA_H * nri, nci))
    nrow = rpb2.shape[0]
    e = pl.pallas_call(
        _rpb_expand_kernel,
        out_shape=jax.ShapeDtypeStruct((nrow, GRID_W * GRID_W), f32),
        name="rpb_expand",
    )(rpb2, jnp.asarray(sel, bf16), jnp.asarray(valid.reshape(1, -1), f32))
    return e[:NA_H * nri].reshape(NA_H // 2, 2, nri, GRID_W, GRID_W)


def _na_row_classes():
    rows = GRID_W
    ri = -np.ones((3, NA_QROWS, NA_KROWS), np.int64)
    for cls, rb in enumerate((0, 1, rows // NA_QROWS - 1)):
        r0 = rb * NA_QROWS
        kb = int(np.clip(r0 - WIN_H // 2, 0, rows - NA_KROWS))
        for rq in range(NA_QROWS):
            r = r0 + rq
            rs = int(np.clip(r - WIN_H // 2, 0, rows - WIN_H))
            for m in range(NA_KROWS):
                kr = kb + m
                if rs <= kr < rs + WIN_H:
                    ri[cls, rq, m] = kr - r + WIN_H - 1
    return ri


def _natten_kernel(q_ref, k_ref, v_ref, qn_ref, kn_ref, gsum_ref, e_ref, o_ref, kn_scr, tbl_scr):
    b = pl.program_id(1)
    rb = pl.program_id(2)
    nrb = pl.num_programs(2)
    gsum = gsum_ref[...]
    log2e = 1.4426950408889634

    def headnorm(x, g):
        ms = _sel_r(x * x, gsum) * (1.0 / NA_DH)
        return x * lax.rsqrt(ms + EPS) * g

    @pl.when((b == 0) & (rb == 0))
    def _():
        ri = _na_row_classes()
        neg = jnp.full((GRID_W, GRID_W), NEG, f32)
        for hh in range(2):
            for cls in range(3):
                for rq in range(NA_QROWS):
                    for mp in range(NA_KROWS // 2):
                        parts = []
                        for m in (2 * mp, 2 * mp + 1):
                            r = int(ri[cls, rq, m])
                            parts.append(neg if r < 0 else e_ref[hh, r] * log2e)
                        tbl_scr[hh, cls, rq * GRID_W:(rq + 1) * GRID_W, mp * 128:(mp + 1) * 128] = (
                            jnp.concatenate(parts, axis=1))

    @pl.when(rb == 0)
    def _():
        kn_scr[...] = headnorm(k_ref[...].astype(f32), kn_ref[...]).astype(bf16)

    q = headnorm(q_ref[...].astype(f32), qn_ref[...]) * (NA_DH ** -0.5 * log2e)
    kb = jnp.clip(rb * NA_QROWS - WIN_H // 2, 0, GRID_W - NA_KROWS)
    start = pl.multiple_of(kb * GRID_W, GRID_W)
    nk = NA_KROWS * GRID_W
    kwin = kn_scr[pl.ds(start, nk), :]
    vwin = v_ref[pl.ds(start, nk), :]
    cls = jnp.where(rb == 0, 0, jnp.where(rb == nrb - 1, 2, 1))
    lane = lax.broadcasted_iota(i32, q.shape, 1)
    first = lane < NA_DH
    qh = [jnp.where(first, q, 0.0).astype(bf16), jnp.where(first, 0.0, q).astype(bf16)]
    scores = [_nt(qh[hh], kwin) + tbl_scr[hh, cls] for hh in range(2)]
    outs = []
    for s in scores:
        m = jnp.max(s, axis=-1, keepdims=True)
        p = jnp.exp2(s - m)
        l = jnp.sum(p, axis=-1, keepdims=True)
        outs.append(jnp.dot(p.astype(bf16), vwin, preferred_element_type=f32) / l)
    o_ref[...] = jnp.where(first, outs[0], outs[1]).astype(bf16)


def _natten(qn, kn, vn, q_norm, k_norm, rpb, bsz, seq):
    n = qn.shape[0]
    e5 = _na_bias_table(rpb)
    nq = NA_QROWS * GRID_W
    nrb = seq // nq
    lane_h = np.arange(128) // NA_DH
    gsum = jnp.asarray(lane_h[:, None] == lane_h[None, :], bf16)
    qn2 = jnp.tile(q_norm.reshape(1, NA_DH), (1, 2))
    kn2 = jnp.tile(k_norm.reshape(1, NA_DH), (1, 2))
    const2 = lambda p, b, r: (0, 0)
    return pl.pallas_call(
        _natten_kernel,
        grid=(NA_H // 2, bsz, nrb),
        in_specs=[pl.BlockSpec((nq, 128), lambda p, b, r: (b * nrb + r, p)),
                  pl.BlockSpec((seq, 128), lambda p, b, r: (b, p)),
                  pl.BlockSpec((seq, 128), lambda p, b, r: (b, p)),
                  pl.BlockSpec((1, 128), const2),
                  pl.BlockSpec((1, 128), const2),
                  pl.BlockSpec((128, 128), const2),
                  pl.BlockSpec((None,) + e5.shape[1:], lambda p, b, r: (p, 0, 0, 0, 0))],
        out_specs=pl.BlockSpec((nq, 128), lambda p, b, r: (b * nrb + r, p)),
        out_shape=jax.ShapeDtypeStruct((n, NA_W), bf16),
        scratch_shapes=[pltpu.VMEM((seq, 128), bf16),
                        pltpu.VMEM((2, 3, nq, NA_KROWS * GRID_W), f32)],
        compiler_params=pltpu.CompilerParams(dimension_semantics=("arbitrary", "arbitrary", "arbitrary")),
        name="natten",
    )(qn, kn, vn, qn2, kn2, gsum, e5)


def _top2(vals):
    io = lax.broadcasted_iota(i32, vals.shape, 0)
    m1 = jnp.max(vals, axis=0, keepdims=True)
    i1 = jnp.min(jnp.where(vals == m1, io, EPG), axis=0, keepdims=True)
    v2 = jnp.where(io == i1, -jnp.inf, vals)
    m2 = jnp.max(v2, axis=0, keepdims=True)
    i2 = jnp.min(jnp.where(v2 == m2, io, EPG), axis=0, keepdims=True)
    return m1, i1, m2, i2


def _outproj_kernel(og_ref, on_ref, wo_ref, x_ref, mod_ref, nrm_ref, rw_ref, rb_ref,
                    x1_ref, h2_ref, eidx_ref, gw_ref):
    mix = (jnp.dot(og_ref[...], wo_ref[0:V_W, :], preferred_element_type=f32)
           + jnp.dot(on_ref[...], wo_ref[V_W:V_W + NA_W, :], preferred_element_type=f32))
    x1 = x_ref[...] + mod_ref[2] * mix
    x1_ref[...] = x1
    y = x1 * lax.rsqrt(jnp.mean(x1 * x1, axis=-1, keepdims=True) + EPS) * nrm_ref[...]
    h2 = y * (1.0 + mod_ref[4]) + mod_ref[3]
    h2_ref[...] = h2
    rw = rw_ref[...]
    rw_hi = rw.astype(bf16)
    rw_lo = (rw - rw_hi.astype(f32)).astype(bf16)
    h_hi = h2.astype(bf16)
    h_lo = (h2 - h_hi.astype(f32)).astype(bf16)
    logits = _nt(rw_hi, h_hi) + _nt(rw_hi, h_lo) + _nt(rw_lo, h_hi)
    scores = jax.nn.sigmoid(logits)
    sel = scores + rb_ref[...]
    tops = [_top2(sel[g * EPG:(g + 1) * EPG]) for g in range(N_GRP)]
    best = jnp.zeros_like(tops[0][1])
    bs = tops[0][0] + tops[0][2]
    for g in range(1, N_GRP):
        gs = tops[g][0] + tops[g][2]
        take = gs > bs
        best = jnp.where(take, g, best)
        bs = jnp.where(take, gs, bs)
    io = lax.broadcasted_iota(i32, (EPG, sel.shape[1]), 0)
    i1 = jnp.zeros_like(best)
    i2 = jnp.zeros_like(best)
    s1 = jnp.zeros(best.shape, f32)
    s2 = jnp.zeros(best.shape, f32)
    for g in range(N_GRP):
        sc = scores[g * EPG:(g + 1) * EPG]
        pick = best == g
        i1 = jnp.where(pick, tops[g][1], i1)
        i2 = jnp.where(pick, tops[g][3], i2)
        s1 = jnp.where(pick, jnp.sum(jnp.where(io == tops[g][1], sc, 0.0), axis=0, keepdims=True), s1)
        s2 = jnp.where(pick, jnp.sum(jnp.where(io == tops[g][3], sc, 0.0), axis=0, keepdims=True), s2)
    eidx_ref[0:1, :] = best * EPG + i1
    eidx_ref[1:2, :] = best * EPG + i2
    tot = s1 + s2
    gw_ref[0:1, :] = s1 / tot
    gw_ref[1:2, :] = s2 / tot


def _outproj(o_gla, o_na, w_out, x2d, mod_l, nrm, router_w, router_b, bsz, seq):
    n = x2d.shape[0]
    tm = TM_PROJ
    nt = seq // tm
    row = lambda b, i: (b * nt + i, 0)
    col = lambda b, i: (0, b * nt + i)
    const = lambda b, i: (0, 0)
    return pl.pallas_call(
        _outproj_kernel,
        grid=(bsz, nt),
        in_specs=[pl.BlockSpec((tm, V_W), row),
                  pl.BlockSpec((tm, NA_W), row),
                  pl.BlockSpec((V_W + NA_W, D), const),
                  pl.BlockSpec((tm, D), row),
                  pl.BlockSpec((6, None, 1, D), lambda b, i: (0, b, 0, 0)),
                  pl.BlockSpec((1, D), const),
                  pl.BlockSpec((N_EXP, D), const),
                  pl.BlockSpec((N_EXP, 1), const)],
        out_specs=[pl.BlockSpec((tm, D), row), pl.BlockSpec((tm, D), row),
                   pl.BlockSpec((2, tm), col), pl.BlockSpec((2, tm), col)],
        out_shape=[jax.ShapeDtypeStruct((n, D), f32), jax.ShapeDtypeStruct((n, D), f32),
                   jax.ShapeDtypeStruct((2, n), i32), jax.ShapeDtypeStruct((2, n), f32)],
        compiler_params=pltpu.CompilerParams(dimension_semantics=("arbitrary", "arbitrary")),
        name="outproj_router",
    )(o_gla, o_na, w_out.astype(bf16), x2d, mod_l, nrm.reshape(1, D),
      router_w.T, router_b.reshape(N_EXP, 1))


def _rank_kernel(eidx_ref, tri_ref, rank_ref, cnt_ref, carry):
    @pl.when(pl.program_id(0) == 0)
    def _():
        carry[...] = jnp.zeros_like(carry)

    e = eidx_ref[...]
    tm = e.shape[1]
    io = lax.broadcasted_iota(i32, (N_EXP, tm), 0)
    run = carry[...]
    for k in range(2):
        oh = io == e[k:k + 1, :]
        ohf = oh.astype(f32)
        pre = jnp.dot(ohf.astype(bf16), tri_ref[...], preferred_element_type=f32) + run[:, 0:1]
        rank_ref[k:k + 1, :] = jnp.sum(jnp.where(oh, pre, 0.0), axis=0, keepdims=True).astype(i32)
        run = run + jnp.sum(ohf, axis=1, keepdims=True)
    carry[...] = run
    cnt_ref[...] = run


def _ranks(eidx):
    n = eidx.shape[1]
    tm = TM_RANK
    t = np.arange(tm)
    tri = jnp.asarray(t[:, None] < t[None, :], bf16)
    return pl.pallas_call(
        _rank_kernel,
        grid=(n // tm,),
        in_specs=[pl.BlockSpec((2, tm), lambda i: (0, i)), pl.BlockSpec((tm, tm), lambda i: (0, 0))],
        out_specs=[pl.BlockSpec((2, tm), lambda i: (0, i)), pl.BlockSpec((N_EXP, 128), lambda i: (0, 0))],
        out_shape=[jax.ShapeDtypeStruct((2, n), i32), jax.ShapeDtypeStruct((N_EXP, 128), f32)],
        scratch_shapes=[pltpu.VMEM((N_EXP, 128), f32)],
        compiler_params=pltpu.CompilerParams(dimension_semantics=("arbitrary",)),
        name="moe_rank",
    )(eidx, tri)


def _dispatch_kernel(dest_ref, tails_ref, h_ref, xb_ref, zbuf, hbuf, lsem, rsem, zsem):
    i = pl.program_id(0)
    nsteps = pl.num_programs(0)
    tm = TM_DISP
    n = nsteps * tm
    nbuf = hbuf.shape[0]

    def load(j):
        start = pl.multiple_of(j * tm, tm)
        return pltpu.make_async_copy(h_ref.at[pl.ds(start, tm)], hbuf.at[j % nbuf], lsem.at[j % nbuf])

    def drain_rows(j):
        for _ in range(2):
            pltpu.make_async_copy(hbuf.at[j % nbuf], xb_ref.at[pl.ds(0, tm)], rsem.at[j % nbuf]).wait()

    def tail_copy(e):
        start = pl.multiple_of(tails_ref[e], MOE_BLK)
        return pltpu.make_async_copy(zbuf, xb_ref.at[pl.ds(start, MOE_BLK)], zsem)

    @pl.when(i == 0)
    def _():
        zbuf[...] = jnp.zeros_like(zbuf)
        for e in range(2 * N_EXP):
            @pl.when(tails_ref[e] >= 0)
            def _():
                tail_copy(e).start()
        for e in range(2 * N_EXP):
            @pl.when(tails_ref[e] >= 0)
            def _():
                tail_copy(e).wait()
        load(i).start()

    @pl.when(i >= nbuf - 1)
    def _():
        drain_rows(i + 1 - nbuf)

    @pl.when(i + 1 < nsteps)
    def _():
        load(i + 1).start()

    load(i).wait()
    slot = i % nbuf

    def issue(t, carry):
        src = hbuf.at[slot, pl.ds(t, 1)]
        for k in range(2):
            d = dest_ref[k * n + i * tm + t]
            pltpu.make_async_copy(src, xb_ref.at[pl.ds(d, 1)], rsem.at[slot]).start()
        return carry

    lax.fori_loop(0, tm, issue, 0, unroll=4)

    @pl.when(i == nsteps - 1)
    def _():
        for back in range(nbuf - 2, -1, -1):
            drain_rows(i - back)


def _dispatch(h2, dest_flat, tails, p_rows):
    n = h2.shape[0]
    tm = TM_DISP
    return pl.pallas_call(
        _dispatch_kernel,
        grid_spec=pltpu.PrefetchScalarGridSpec(
            num_scalar_prefetch=2,
            grid=(n // tm,),
            in_specs=[pl.BlockSpec(memory_space=pl.ANY)],
            out_specs=pl.BlockSpec(memory_space=pl.ANY),
            scratch_shapes=[pltpu.VMEM((MOE_BLK, D), f32), pltpu.VMEM((3, tm, D), f32),
                            pltpu.SemaphoreType.DMA((3,)), pltpu.SemaphoreType.DMA((3,)),
                            pltpu.SemaphoreType.DMA(())],
        ),
        out_shape=jax.ShapeDtypeStruct((p_rows, D), f32),
        compiler_params=pltpu.CompilerParams(dimension_semantics=("arbitrary",)),
        name="moe_dispatch",
    )(dest_flat, tails, h2)


def _expert_kernel(blk_e_ref, nb_ref, xb_ref, w1_ref, w3_ref, w2_ref, y_ref, w1b, w3b, w2b):
    i = pl.program_id(0)

    @pl.when((i < nb_ref[0]) & ((i == 0) | (blk_e_ref[i] != blk_e_ref[jnp.maximum(i - 1, 0)])))
    def _():
        w1b[...] = w1_ref[...].astype(bf16)
        w3b[...] = w3_ref[...].astype(bf16)
        w2b[...] = w2_ref[...].astype(bf16)

    @pl.when(i < nb_ref[0])
    def _():
        x = xb_ref[...].astype(bf16)
        a = jnp.dot(x, w1b[...], preferred_element_type=f32)
        b = jnp.dot(x, w3b[...], preferred_element_type=f32)
        h = (a * jax.nn.sigmoid(a) * b).astype(bf16)
        y_ref[...] = jnp.dot(h, w2b[...], preferred_element_type=f32)

    @pl.when(i >= nb_ref[0])
    def _():
        y_ref[...] = jnp.zeros_like(y_ref)


def _experts(xb, blk_e, nb_used, w1, w3, w2, layer):
    p_rows = xb.shape[0]
    nb = p_rows // MOE_BLK

    def xmap(i, be, nbu):
        return (jnp.minimum(i, nbu[0] - 1), 0)

    def wmap(i, be, nbu):
        return (layer, be[i], 0, 0)

    return pl.pallas_call(
        _expert_kernel,
        grid_spec=pltpu.PrefetchScalarGridSpec(
            num_scalar_prefetch=2,
            grid=(nb,),
            in_specs=[pl.BlockSpec((MOE_BLK, D), xmap),
                      pl.BlockSpec((None, None, D, D_EXP), wmap),
                      pl.BlockSpec((None, None, D, D_EXP), wmap),
                      pl.BlockSpec((None, None, D_EXP, D), wmap)],
            out_specs=pl.BlockSpec((MOE_BLK, D), lambda i, be, nbu: (i, 0)),
            scratch_shapes=[pltpu.VMEM((D, D_EXP), bf16), pltpu.VMEM((D, D_EXP), bf16),
                            pltpu.VMEM((D_EXP, D), bf16)],
        ),
        out_shape=jax.ShapeDtypeStruct((p_rows, D), f32),
        compiler_params=pltpu.CompilerParams(dimension_semantics=("arbitrary",),
                                             vmem_limit_bytes=48 * 1024 * 1024),
        name="moe_experts",
    )(blk_e, nb_used, xb, w1, w3, w2)


def _combine_kernel(dest_ref, x1_ref, gw_ref, mod_ref, y_ref, o_ref, ybuf, sems):
    i = pl.program_id(0)
    nsteps = pl.num_programs(0)
    tm = x1_ref.shape[0]
    n = nsteps * tm

    def issue_tile(j):
        slot = j % 2

        def issue(t, carry):
            for k in range(2):
                d = dest_ref[k * n + j * tm + t]
                pltpu.make_async_copy(y_ref.at[pl.ds(d, 1)], ybuf.at[slot, k, pl.ds(t, 1)],
                                      sems.at[slot]).start()
            return carry

        lax.fori_loop(0, tm, issue, 0, unroll=8)

    @pl.when(i == 0)
    def _():
        issue_tile(i)

    @pl.when(i + 1 < nsteps)
    def _():
        issue_tile(i + 1)

    slot = i % 2
    for k in range(2):
        pltpu.make_async_copy(y_ref.at[pl.ds(0, tm)], ybuf.at[slot, k], sems.at[slot]).wait()
    gw = gw_ref[...]
    y = ybuf[slot, 0] * gw[:, 0:1] + ybuf[slot, 1] * gw[:, 1:2]
    o_ref[...] = x1_ref[...] + mod_ref[5] * y


def _combine(y, dest_flat, x1, gw_t, mod_l, bsz, seq):
    n = x1.shape[0]
    tm = TM_COMB
    nt = seq // tm
    return pl.pallas_call(
        _combine_kernel,
        grid_spec=pltpu.PrefetchScalarGridSpec(
            num_scalar_prefetch=1,
            grid=(n // tm,),
            in_specs=[pl.BlockSpec((tm, D), lambda i, *_: (i, 0)),
                      pl.BlockSpec((tm, 2), lambda i, *_: (i, 0)),
                      pl.BlockSpec((6, None, 1, D), lambda i, *_: (0, i // nt, 0, 0)),
                      pl.BlockSpec(memory_space=pl.ANY)],
            out_specs=pl.BlockSpec((tm, D), lambda i, *_: (i, 0)),
            scratch_shapes=[pltpu.VMEM((2, 2, tm, D), f32), pltpu.SemaphoreType.DMA((2,))],
        ),
        out_shape=jax.ShapeDtypeStruct((n, D), f32),
        compiler_params=pltpu.CompilerParams(dimension_semantics=("arbitrary",)),
        name="moe_combine",
    )(dest_flat, x1, gw_t, mod_l, y)


def _grouped_moe(h2, eidx, gw, x1, mod_l, w1, w3, w2, layer, bsz, seq):
    n = h2.shape[0]
    p_rows = (n * 2 // MOE_BLK + N_EXP) * MOE_BLK
    nb = p_rows // MOE_BLK
    rank, cnt = _ranks(eidx)
    counts = cnt[:, 0].astype(i32)
    padded = (counts + MOE_BLK - 1) // MOE_BLK * MOE_BLK
    pad_end = jnp.cumsum(padded)
    pad_start = pad_end - padded
    start_of = jnp.sum(jnp.where(eidx[..., None] == jnp.arange(N_EXP, dtype=i32), pad_start, 0), axis=-1)
    dest_flat = (start_of + rank).reshape(-1)
    nb_used = (pad_end[-1] // MOE_BLK).astype(i32)
    spare = nb_used + jnp.arange(N_EXP, dtype=i32)
    tails = jnp.concatenate([jnp.where(padded > 0, pad_end - MOE_BLK, -1),
                             jnp.where(spare < nb, spare * MOE_BLK, -1)]).astype(i32)
    blk = jnp.minimum(jnp.arange(nb, dtype=i32), nb_used - 1)
    seg_done = (pad_end[None, :] <= (blk * MOE_BLK)[:, None]).astype(i32)
    blk_e = jnp.minimum(jnp.sum(seg_done, axis=1), N_EXP - 1).astype(i32)
    xb = _dispatch(h2, dest_flat, tails, p_rows)
    y = _experts(xb, blk_e, nb_used.reshape(1), w1, w3, w2, layer)
    return _combine(y, dest_flat, x1, gw.T, mod_l, bsz, seq)


def kernel(x, c, w_ada, b_ada, attn_norm, ffn_norm, w_in, gla_gate_w, gla_gate_b, gla_out_norm,
           na_q_norm, na_k_norm, na_rpb, w_out, router_w, router_b, w1, w3, w2):
    bsz, seq, _ = x.shape
    depth = w_ada.shape[0]
    mod = _adaln_mod(c, w_ada, b_ada)
    xc = x.reshape(bsz * seq, D)
    for l in range(depth):
        (qg, kg, vg, sg, la_f, la_b, qn, kn, vn), bounded_f, bounded_b = _inproj(
            xc, mod[l], attn_norm[l], w_in[l], gla_gate_w[l], gla_gate_b[l], bsz, seq)
        o_f = _gla_dir(False, qg, kg, vg, la_f, bounded_f, bsz, seq)
        o_gla = _gla_dir(True, qg, kg, vg, la_b, bounded_b, bsz, seq, extra=(o_f, sg, gla_out_norm[l]))
        o_na = _natten(qn, kn, vn, na_q_norm[l], na_k_norm[l], na_rpb[l], bsz, seq)
        x1, h2, eidx, gw = _outproj(o_gla, o_na, w_out[l], xc, mod[l], ffn_norm[l],
                                    router_w, router_b, bsz, seq)
        xc = _grouped_moe(h2, eidx, gw, x1, mod[l], w1, w3, w2, l, bsz, seq)
    return xc.reshape(bsz, seq, D)
```

```python
import functools

import numpy as np
import jax
import jax.numpy as jnp
from jax import lax
from jax.experimental import pallas as pl
from jax.experimental.pallas import tpu as pltpu

f32 = jnp.float32
bf16 = jnp.bfloat16
i32 = jnp.int32
HIGHEST = lax.Precision.HIGHEST

D = 1024
GRID_W = 64
GLA_H, GLA_DK, GLA_DV = 4, 64, 128
GLA_RANK = 16
GLA_TAU = 16.0
CHUNK = 64
NA_H, NA_DH = 8, 64
WIN_H, WIN_W = 8, 16
N_EXP, N_GRP, EPG = 32, 4, 8
D_EXP = 512
QK_W = GLA_H * GLA_DK
V_W = GLA_H * GLA_DV
NA_W = NA_H * NA_DH
EPS = 1e-6
NEG = -1e30

TM_PROJ = 512
TB_GLA = 256
NA_QROWS = 4
NA_KROWS = 12
NA_SUB = 4
TM_RANK = 512
TM_DISP = 256
TM_COMB = 256
MOE_BLK = 512
N_LEVELS = 6
GLA_SAFE_RANGE = 40.0


def _nt(a, b):
    return lax.dot_general(a, b, (((1,), (1,)), ((), ())), preferred_element_type=f32)


def _tn(a, b):
    return lax.dot_general(a, b, (((0,), (0,)), ((), ())), preferred_element_type=f32)


def _split3(x):
    hi = x.astype(bf16)
    r = x - hi.astype(f32)
    mid = r.astype(bf16)
    lo = (r - mid.astype(f32)).astype(bf16)
    return hi, mid, lo


def _sel_l(m01, x):
    hi, mid, lo = _split3(x)
    d = lambda p: jnp.dot(m01, p, preferred_element_type=f32)
    return d(hi) + d(mid) + d(lo)


def _sel_r(x, m01):
    hi, mid, lo = _split3(x)
    d = lambda p: jnp.dot(p, m01, preferred_element_type=f32)
    return d(hi) + d(mid) + d(lo)


def _mod_kernel(c_ref, w_ref, b_ref, o_ref):
    c = c_ref[...]
    ca = c * jax.nn.sigmoid(c)
    o_ref[0, 0] = jnp.dot(ca, w_ref[0], precision=HIGHEST, preferred_element_type=f32) + b_ref[0, 0]


def _adaln_mod(c, w_ada, b_ada):
    depth = w_ada.shape[0]
    bsz = c.shape[0]
    cp = jnp.zeros((8, D), f32).at[:bsz].set(c)
    out = pl.pallas_call(
        _mod_kernel,
        grid=(depth, 6),
        in_specs=[pl.BlockSpec((8, D), lambda l, j: (0, 0)),
                  pl.BlockSpec((1, D, D), lambda l, j: (l, 0, j)),
                  pl.BlockSpec((1, 1, 1, D), lambda l, j: (l, j, 0, 0))],
        out_specs=pl.BlockSpec((1, 1, 8, D), lambda l, j: (l, j, 0, 0)),
        out_shape=jax.ShapeDtypeStruct((depth, 6, 8, D), f32),
        name="adaln_mod",
    )(cp, w_ada, b_ada.reshape(depth, 6, 1, D))
    return out[:, :, :bsz].reshape(depth, 6, bsz, 1, D)


def _log_sigmoid(z):
    return jnp.minimum(z, 0.0) - jnp.log1p(jnp.exp(-jnp.abs(z)))


def _inproj_kernel(x_ref, mod_ref, nrm_ref, wm_ref, wlr_ref, gf_ref, gb_ref, bf_ref, bb_ref,
                   qg_ref, kg_ref, vg_ref, sg_ref, laf_ref, lab_ref, qn_ref, kn_ref, vn_ref, lamin_ref):
    x = x_ref[...]
    y = x * lax.rsqrt(jnp.mean(x * x, axis=-1, keepdims=True) + EPS) * nrm_ref[...]
    h = y * (1.0 + mod_ref[1]) + mod_ref[0]
    hb = h.astype(bf16)
    proj = lambda lo, hi: jnp.dot(hb, wm_ref[:, lo:hi], preferred_element_type=f32)
    qg_ref[...] = (proj(0, 256) * (GLA_DK ** -0.5)).astype(bf16)
    kg_ref[...] = proj(256, 512).astype(bf16)
    vg_ref[...] = proj(512, 1024).astype(bf16)
    gg = proj(1024, 1536)
    sg_ref[...] = (gg * jax.nn.sigmoid(gg)).astype(bf16)
    qn_ref[...] = proj(1536, 2048).astype(bf16)
    kn_ref[...] = proj(2048, 2560).astype(bf16)
    vn_ref[...] = proj(2560, 3072).astype(bf16)
    lr = jnp.dot(hb, wlr_ref[...], preferred_element_type=f32)
    zf = jnp.dot(lr, gf_ref[...], precision=HIGHEST, preferred_element_type=f32) + bf_ref[...]
    zb = jnp.dot(lr, gb_ref[...], precision=HIGHEST, preferred_element_type=f32) + bb_ref[...]
    la_f = _log_sigmoid(zf) * (1.0 / GLA_TAU)
    la_b = _log_sigmoid(zb) * (1.0 / GLA_TAU)
    laf_ref[...] = la_f
    lab_ref[...] = la_b
    rows = []
    for la in (la_f, la_b):
        for blk in range(la.shape[0] // TB_GLA):
            tot = jnp.sum(la[blk * TB_GLA:(blk + 1) * TB_GLA], axis=0, keepdims=True)
            rows.append(jnp.broadcast_to(jnp.min(tot, axis=1, keepdims=True), (1, 128)))
    rows.append(jnp.zeros((8 - len(rows), 128), f32))
    lamin_ref[...] = jnp.concatenate(rows, axis=0)


def _inproj(x2d, mod_l, nrm, w_in, gate_w, gate_b, bsz, seq):
    n = x2d.shape[0]
    tm = TM_PROJ
    nt = seq // tm
    lo, hi = 2 * QK_W + 2 * V_W, 2 * QK_W + 2 * V_W + 2 * GLA_RANK
    wm = jnp.concatenate([w_in[:, :lo], w_in[:, hi:]], axis=1).astype(bf16)
    wlr = w_in[:, lo:hi].astype(bf16)
    zpad = jnp.zeros((GLA_RANK, QK_W), f32)
    gf = jnp.concatenate([gate_w[0], zpad], axis=0)
    gb = jnp.concatenate([zpad, gate_w[1]], axis=0)
    row = lambda b, i: (b * nt + i, 0)
    const = lambda b, i: (0, 0)
    widths = [(QK_W, bf16), (QK_W, bf16), (V_W, bf16), (V_W, bf16), (QK_W, f32), (QK_W, f32),
              (NA_W, bf16), (NA_W, bf16), (NA_W, bf16)]
    outs = pl.pallas_call(
        _inproj_kernel,
        grid=(bsz, nt),
        in_specs=[pl.BlockSpec((tm, D), row),
                  pl.BlockSpec((6, None, 1, D), lambda b, i: (0, b, 0, 0)),
                  pl.BlockSpec((1, D), const),
                  pl.BlockSpec(wm.shape, const),
                  pl.BlockSpec(wlr.shape, const),
                  pl.BlockSpec(gf.shape, const),
                  pl.BlockSpec(gb.shape, const),
                  pl.BlockSpec((1, QK_W), const),
                  pl.BlockSpec((1, QK_W), const)],
        out_specs=[pl.BlockSpec((tm, w), row) for w, _ in widths]
        + [pl.BlockSpec((None, 8, 128), lambda b, i: (b * nt + i, 0, 0))],
        out_shape=[jax.ShapeDtypeStruct((n, w), dt) for w, dt in widths]
        + [jax.ShapeDtypeStruct((n // tm, 8, 128), f32)],
        compiler_params=pltpu.CompilerParams(dimension_semantics=("arbitrary", "arbitrary")),
        name="inproj",
    )(x2d, mod_l, nrm.reshape(1, D), wm, wlr, gf, gb, gate_b[0].reshape(1, QK_W), gate_b[1].reshape(1, QK_W))
    per = tm // TB_GLA
    lamin = outs[-1][:, :2 * per, 0]
    bounded_f = (lamin[:, :per].reshape(-1) > -GLA_SAFE_RANGE).astype(i32)
    bounded_b = (lamin[:, per:].reshape(-1) > -GLA_SAFE_RANGE).astype(i32)
    return outs[:-1], bounded_f, bounded_b


def _gla_consts(rev, tb):
    c = CHUNK
    idx = np.arange(c)
    if not rev:
        cum = (idx[None, :] <= idx[:, None]).astype(np.float32)
    else:
        cum = (idx[None, :] >= idx[:, None]).astype(np.float32)
    mats = [cum]
    masks = []
    for lvl in range(N_LEVELS):
        s = c >> (lvl + 1)
        blk = idx // (2 * s)
        second = (idx % (2 * s)) >= s
        ref_row = blk * 2 * s + (s if rev else s - 1)
        mats.append(cum[ref_row])
        same = blk[:, None] == blk[None, :]
        if not rev:
            m = same & second[:, None] & (~second[None, :])
        else:
            m = same & (~second[:, None]) & second[None, :]
        masks.append(m)
    masks.append(np.eye(c, dtype=bool) & (not rev))
    mst = np.concatenate(mats, axis=0)
    msk = np.stack([np.tile(m, (1, GLA_H)) for m in masks]).astype(np.float32)
    tix = np.arange(tb)
    tri = (tix[None, :] >= tix[:, None]) if rev else (tix[None, :] <= tix[:, None])
    cmask = (tix[:, None] < tix[None, :]) if rev else (tix[:, None] >= tix[None, :])
    nch = tb // c
    selb = np.zeros((tb, nch * 128), np.float32)
    for ch in range(nch):
        selb[ch * c:(ch + 1) * c, ch * 128:(ch + 1) * 128] = 1.0
    hk = np.arange(QK_W) // GLA_DK
    hv = np.arange(V_W) // GLA_DV
    kmask = (hk[:, None] == hk[None, :]).astype(np.float32)
    vmask = (hk[:, None] == hv[None, :]).astype(np.float32)
    return (jnp.asarray(mst, bf16), jnp.asarray(msk, f32), jnp.asarray(selb, bf16),
            jnp.asarray(kmask, bf16), jnp.asarray(vmask, bf16), jnp.asarray(vmask, f32),
            jnp.asarray(tri, bf16), jnp.asarray(cmask, f32), jnp.ones((tb, 128), bf16))


def _gla_kernel(rev, tb, *refs):
    if rev:
        (bounded_ref, q_ref, k_ref, v_ref, la_ref, mst_ref, msk_ref, selb_ref, kmask_ref, vmask_ref,
         smask_ref, tri_ref, cmask_ref, ones_ref, of_ref, sg_ref, gn_ref, out_ref, s_scr, o_scr) = refs
    else:
        (bounded_ref, q_ref, k_ref, v_ref, la_ref, mst_ref, msk_ref, selb_ref, kmask_ref, vmask_ref,
         smask_ref, tri_ref, cmask_ref, ones_ref, out_ref, s_scr) = refs
        o_scr = out_ref
    c = CHUNK
    nch = tb // c

    @pl.when(pl.program_id(1) == 0)
    def _():
        s_scr[...] = jnp.zeros_like(s_scr)

    def block_bounded():
        la = la_ref[...]
        tot_blk = _sel_r(la.T, ones_ref[...])
        la_hi = la.astype(bf16)
        la_lo = (la - la_hi.astype(f32)).astype(bf16)
        tri = tri_ref[...]
        bcum = (jnp.dot(tri, la_hi, preferred_element_type=f32)
                + jnp.dot(tri, la_lo, preferred_element_type=f32))
        q = q_ref[...].astype(f32)
        k = k_ref[...].astype(f32)
        v = v_ref[...]
        qe = (q * jnp.exp(bcum)).astype(bf16)
        kx = (k * jnp.exp(-bcum)).astype(bf16)
        s_prev = s_scr[...]
        o_inter = jnp.dot(qe, s_prev.astype(bf16), preferred_element_type=f32)
        lane_head = lax.broadcasted_iota(i32, qe.shape, 1) // GLA_DK
        cmask = cmask_ref[...]
        for hd in range(GLA_H):
            cols = slice(hd * GLA_DV, (hd + 1) * GLA_DV)
            qh = jnp.where(lane_head == hd, qe, jnp.zeros_like(qe))
            ph = jnp.where(cmask > 0.0, _nt(qh, kx), 0.0).astype(bf16)
            o_scr[:, cols] = o_inter[:, cols] + jnp.dot(ph, v[:, cols], preferred_element_type=f32)
        blast = bcum[0:1] if rev else bcum[tb - 1:tb]
        ke = (k * jnp.exp(blast - bcum)).astype(bf16)
        dec = jnp.exp(tot_blk)
        s_scr[...] = s_prev * jnp.concatenate([dec] * GLA_H, axis=1) + _tn(ke, v) * smask_ref[...]

    def chunks_robust():
        tot = _sel_r(la_ref[...].T, selb_ref[...])
        mst = mst_ref[...]
        kmask = kmask_ref[...]
        vmask = vmask_ref[...]
        for ch in (range(nch - 1, -1, -1) if rev else range(nch)):
            rows = slice(ch * c, (ch + 1) * c)
            q = q_ref[rows, :].astype(f32)
            k = k_ref[rows, :].astype(f32)
            v = v_ref[rows, :]
            r_all = _sel_l(mst, la_ref[rows, :])
            bcum = r_all[0:c]
            qe = (q * jnp.exp(bcum)).astype(bf16)
            o = jnp.dot(qe, s_scr[...].astype(bf16), preferred_element_type=f32)
            sall = jnp.zeros((c, QK_W), f32)
            for lvl in range(N_LEVELS):
                ref = r_all[(lvl + 1) * c:(lvl + 2) * c]
                a = (q * jnp.exp(jnp.minimum(bcum - ref, 0.0))).astype(bf16)
                bm = (k * jnp.exp(jnp.minimum(ref - bcum, 0.0))).astype(bf16)
                bbd = jnp.concatenate([bm] * GLA_H, axis=0) * kmask
                sall = sall + jnp.where(msk_ref[lvl] > 0.0, _nt(a, bbd), 0.0)
            if not rev:
                bbd = jnp.concatenate([k_ref[rows, :]] * GLA_H, axis=0) * kmask
                sall = sall + jnp.where(msk_ref[N_LEVELS] > 0.0, _nt(q_ref[rows, :], bbd), 0.0)
            vbd = jnp.concatenate([v] * GLA_H, axis=0) * vmask
            o_scr[rows, :] = o + jnp.dot(sall.astype(bf16), vbd, preferred_element_type=f32)
            blast = bcum[0:1] if rev else bcum[c - 1:c]
            ke = (k * jnp.exp(blast - bcum)).astype(bf16)
            dec = jnp.exp(tot[:, ch * 128:(ch + 1) * 128])
            s_scr[...] = (s_scr[...] * jnp.concatenate([dec] * GLA_H, axis=1)
                          + _tn(ke, v) * smask_ref[...])

    nb = pl.num_programs(1)
    blk = pl.program_id(0) * nb + (nb - 1 - pl.program_id(1) if rev else pl.program_id(1))
    bounded = bounded_ref[blk] > 0
    pl.when(bounded)(block_bounded)
    pl.when(jnp.logical_not(bounded))(chunks_robust)

    if rev:
        gn = gn_ref[...]
        for hd in range(GLA_H):
            cols = slice(hd * GLA_DV, (hd + 1) * GLA_DV)
            t = of_ref[:, cols] + o_scr[:, cols]
            y = t * lax.rsqrt(jnp.mean(t * t, axis=-1, keepdims=True) + EPS) * gn
            out_ref[:, cols] = (y * sg_ref[:, cols].astype(f32)).astype(bf16)


def _gla_dir(rev, q, k, v, la, bounded, bsz, seq, extra=None):
    n = q.shape[0]
    tb = TB_GLA
    nb = seq // tb
    consts = _gla_consts(rev, tb)
    if rev:
        row = lambda b, i, flags: (b * nb + nb - 1 - i, 0)
    else:
        row = lambda b, i, flags: (b * nb + i, 0)
    const2 = lambda b, i, flags: (0, 0)
    const3 = lambda b, i, flags: (0, 0, 0)
    in_specs = [pl.BlockSpec((tb, QK_W), row), pl.BlockSpec((tb, QK_W), row),
                pl.BlockSpec((tb, V_W), row), pl.BlockSpec((tb, QK_W), row),
                pl.BlockSpec(consts[0].shape, const2), pl.BlockSpec(consts[1].shape, const3),
                pl.BlockSpec(consts[2].shape, const2), pl.BlockSpec(consts[3].shape, const2),
                pl.BlockSpec(consts[4].shape, const2), pl.BlockSpec(consts[5].shape, const2),
                pl.BlockSpec(consts[6].shape, const2), pl.BlockSpec(consts[7].shape, const2),
                pl.BlockSpec(consts[8].shape, const2)]
    args = [q, k, v, la, *consts]
    scratch = [pltpu.VMEM((QK_W, V_W), f32)]
    if rev:
        o_f, sg, gn = extra
        in_specs += [pl.BlockSpec((tb, V_W), row), pl.BlockSpec((tb, V_W), row),
                     pl.BlockSpec((1, GLA_DV), const2)]
        args += [o_f, sg, gn.reshape(1, GLA_DV)]
        scratch.append(pltpu.VMEM((tb, V_W), f32))
        out_dtype = bf16
    else:
        out_dtype = f32
    return pl.pallas_call(
        functools.partial(_gla_kernel, rev, tb),
        grid_spec=pltpu.PrefetchScalarGridSpec(
            num_scalar_prefetch=1,
            grid=(bsz, nb),
            in_specs=in_specs,
            out_specs=pl.BlockSpec((tb, V_W), row),
            scratch_shapes=scratch,
        ),
        out_shape=jax.ShapeDtypeStruct((n, V_W), out_dtype),
        compiler_params=pltpu.CompilerParams(dimension_semantics=("arbitrary", "arbitrary")),
        name="gla_bwd" if rev else "gla_fwd",
    )(bounded, *args)


def _rpb_expand_kernel(rpb_ref, sel_ref, valid_ref, o_ref):
    e = _sel_r(rpb_ref[...], sel_ref[...])
    o_ref[...] = jnp.where(valid_ref[...] > 0.0, e, NEG)


def _na_bias_table(rpb):
    nri, nci = 2 * WIN_H - 1, 2 * WIN_W - 1
    qc = np.arange(GRID_W)
    col_start = np.clip(qc - WIN_W // 2, 0, GRID_W - WIN_W)
    kc = np.arange(GRID_W)
    valid = (kc[None, :] >= col_start[:, None]) & (kc[None, :] < col_start[:, None] + WIN_W)
    ci = kc[None, :] - qc[:, None] + WIN_W - 1
    sel = np.zeros((32, GRID_W * GRID_W), np.float32)
    flat_ci = ci.reshape(-1)
    ok = (flat_ci >= 0) & (flat_ci < nci)
    sel[flat_ci[ok], np.nonzero(ok)[0]] = 1.0
    rpb2 = jnp.zeros((NA_H * nri + 8 - (NA_H * nri) % 8, 32), f32).at[:NA_H * nri, :nci].set(
        rpb.reshape(NA_H * nri, nci))
    nrow = rpb2.shape[0]
    e = pl.pallas_call(
        _rpb_expand_kernel,
        out_shape=jax.ShapeDtypeStruct((nrow, GRID_W * GRID_W), f32),
        name="rpb_expand",
    )(rpb2, jnp.asarray(sel, bf16), jnp.asarray(valid.reshape(1, -1), f32))
    return e[:NA_H * nri].reshape(NA_H // 2, 2, nri, GRID_W, GRID_W)


def _na_row_classes():
    rows = GRID_W
    ri = -np.ones((3, NA_QROWS, NA_KROWS), np.int64)
    for cls, rb in enumerate((0, 1, rows // NA_QROWS - 1)):
        r0 = rb * NA_QROWS
        kb = int(np.clip(r0 - WIN_H // 2, 0, rows - NA_KROWS))
        for rq in range(NA_QROWS):
            r = r0 + rq
            rs = int(np.clip(r - WIN_H // 2, 0, rows - WIN_H))
            for m in range(NA_KROWS):
                kr = kb + m
                if rs <= kr < rs + WIN_H:
                    ri[cls, rq, m] = kr - r + WIN_H - 1
    return ri


def _natten_kernel(q_ref, k_ref, v_ref, qn_ref, kn_ref, gsum_ref, e_ref, o_ref, kn_scr, tbl_scr):
    b = pl.program_id(1)
    rb = pl.program_id(2)
    gsum = gsum_ref[...]
    log2e = 1.4426950408889634

    def headnorm(x, g):
        ms = _sel_r(x * x, gsum) * (1.0 / NA_DH)
        return x * lax.rsqrt(ms + EPS) * g

    @pl.when((b == 0) & (rb == 0))
    def _():
        ri = _na_row_classes()
        neg = jnp.full((GRID_W, GRID_W), NEG, f32)
        for hh in range(2):
            for cls in range(3):
                for rq in range(NA_QROWS):
                    for mp in range(NA_KROWS // 2):
                        parts = []
                        for m in (2 * mp, 2 * mp + 1):
                            r = int(ri[cls, rq, m])
                            parts.append(neg if r < 0 else e_ref[hh, r] * log2e)
                        r0 = (hh * NA_QROWS + rq) * GRID_W
                        tbl_scr[cls, r0:r0 + GRID_W, mp * 128:(mp + 1) * 128] = (
                            jnp.concatenate(parts, axis=1))

    @pl.when(rb == 0)
    def _():
        kn_scr[...] = headnorm(k_ref[...].astype(f32), kn_ref[...]).astype(bf16)

    nq = NA_QROWS * GRID_W
    nk = NA_KROWS * GRID_W
    nblk = GRID_W // NA_QROWS
    first = lax.broadcasted_iota(i32, (nq, 128), 1) < NA_DH
    qs, wins, tbls = [], [], []
    for sub in range(NA_SUB):
        qb = rb * NA_SUB + sub
        q = headnorm(q_ref[sub * nq:(sub + 1) * nq, :].astype(f32), qn_ref[...]) * (NA_DH ** -0.5 * log2e)
        kb = jnp.clip(qb * NA_QROWS - WIN_H // 2, 0, GRID_W - NA_KROWS)
        start = pl.multiple_of(kb * GRID_W, GRID_W)
        cls = jnp.where(qb == 0, 0, jnp.where(qb == nblk - 1, 2, 1))
        qs.append([jnp.where(first, q, 0.0).astype(bf16), jnp.where(first, 0.0, q).astype(bf16)])
        wins.append((kn_scr[pl.ds(start, nk), :], v_ref[pl.ds(start, nk), :]))
        tbls.append(cls)
    scores = [[_nt(qs[sub][hh], wins[sub][0]) + tbl_scr[tbls[sub], hh * nq:(hh + 1) * nq, :]
               for hh in range(2)] for sub in range(NA_SUB)]
    for sub in range(NA_SUB):
        outs = []
        for s in scores[sub]:
            m = jnp.max(s, axis=-1, keepdims=True)
            p = jnp.exp2(s - m)
            l = jnp.sum(p, axis=-1, keepdims=True)
            outs.append(jnp.dot(p.astype(bf16), wins[sub][1], preferred_element_type=f32) / l)
        o_ref[sub * nq:(sub + 1) * nq, :] = jnp.where(first, outs[0], outs[1]).astype(bf16)


def _natten(qn, kn, vn, q_norm, k_norm, rpb, bsz, seq):
    n = qn.shape[0]
    e5 = _na_bias_table(rpb)
    nq = NA_QROWS * GRID_W
    nqs = NA_SUB * nq
    nrb = seq // nqs
    lane_h = np.arange(128) // NA_DH
    gsum = jnp.asarray(lane_h[:, None] == lane_h[None, :], bf16)
    qn2 = jnp.tile(q_norm.reshape(1, NA_DH), (1, 2))
    kn2 = jnp.tile(k_norm.reshape(1, NA_DH), (1, 2))
    const2 = lambda p, b, r: (0, 0)
    return pl.pallas_call(
        _natten_kernel,
        grid=(NA_H // 2, bsz, nrb),
        in_specs=[pl.BlockSpec((nqs, 128), lambda p, b, r: (b * nrb + r, p)),
                  pl.BlockSpec((seq, 128), lambda p, b, r: (b, p)),
                  pl.BlockSpec((seq, 128), lambda p, b, r: (b, p)),
                  pl.BlockSpec((1, 128), const2),
                  pl.BlockSpec((1, 128), const2),
                  pl.BlockSpec((128, 128), const2),
                  pl.BlockSpec((None,) + e5.shape[1:], lambda p, b, r: (p, 0, 0, 0, 0))],
        out_specs=pl.BlockSpec((nqs, 128), lambda p, b, r: (b * nrb + r, p)),
        out_shape=jax.ShapeDtypeStruct((n, NA_W), bf16),
        scratch_shapes=[pltpu.VMEM((seq, 128), bf16),
                        pltpu.VMEM((3, 2 * nq, NA_KROWS * GRID_W), f32)],
        compiler_params=pltpu.CompilerParams(dimension_semantics=("arbitrary", "arbitrary", "arbitrary")),
        name="natten",
    )(qn, kn, vn, qn2, kn2, gsum, e5)


def _top2(vals):
    io = lax.broadcasted_iota(i32, vals.shape, 0)
    m1 = jnp.max(vals, axis=0, keepdims=True)
    i1 = jnp.min(jnp.where(vals == m1, io, EPG), axis=0, keepdims=True)
    v2 = jnp.where(io == i1, -jnp.inf, vals)
    m2 = jnp.max(v2, axis=0, keepdims=True)
    i2 = jnp.min(jnp.where(v2 == m2, io, EPG), axis=0, keepdims=True)
    return m1, i1, m2, i2


def _outproj_kernel(og_ref, on_ref, wo_ref, x_ref, mod_ref, nrm_ref, rw_ref, rb_ref,
                    x1_ref, h2_ref, eidx_ref, gw_ref):
    mix = (jnp.dot(og_ref[...], wo_ref[0:V_W, :], preferred_element_type=f32)
           + jnp.dot(on_ref[...], wo_ref[V_W:V_W + NA_W, :], preferred_element_type=f32))
    x1 = x_ref[...] + mod_ref[2] * mix
    x1_ref[...] = x1
    y = x1 * lax.rsqrt(jnp.mean(x1 * x1, axis=-1, keepdims=True) + EPS) * nrm_ref[...]
    h2 = y * (1.0 + mod_ref[4]) + mod_ref[3]
    h2_ref[...] = h2
    rw = rw_ref[...]
    rw_hi = rw.astype(bf16)
    rw_lo = (rw - rw_hi.astype(f32)).astype(bf16)
    h_hi = h2.astype(bf16)
    h_lo = (h2 - h_hi.astype(f32)).astype(bf16)
    logits = _nt(rw_hi, h_hi) + _nt(rw_hi, h_lo) + _nt(rw_lo, h_hi)
    scores = jax.nn.sigmoid(logits)
    sel = scores + rb_ref[...]
    tops = [_top2(sel[g * EPG:(g + 1) * EPG]) for g in range(N_GRP)]
    best = jnp.zeros_like(tops[0][1])
    bs = tops[0][0] + tops[0][2]
    for g in range(1, N_GRP):
        gs = tops[g][0] + tops[g][2]
        take = gs > bs
        best = jnp.where(take, g, best)
        bs = jnp.where(take, gs, bs)
    io = lax.broadcasted_iota(i32, (EPG, sel.shape[1]), 0)
    i1 = jnp.zeros_like(best)
    i2 = jnp.zeros_like(best)
    s1 = jnp.zeros(best.shape, f32)
    s2 = jnp.zeros(best.shape, f32)
    for g in range(N_GRP):
        sc = scores[g * EPG:(g + 1) * EPG]
        pick = best == g
        i1 = jnp.where(pick, tops[g][1], i1)
        i2 = jnp.where(pick, tops[g][3], i2)
        s1 = jnp.where(pick, jnp.sum(jnp.where(io == tops[g][1], sc, 0.0), axis=0, keepdims=True), s1)
        s2 = jnp.where(pick, jnp.sum(jnp.where(io == tops[g][3], sc, 0.0), axis=0, keepdims=True), s2)
    eidx_ref[0:1, :] = best * EPG + i1
    eidx_ref[1:2, :] = best * EPG + i2
    tot = s1 + s2
    gw_ref[0:1, :] = s1 / tot
    gw_ref[1:2, :] = s2 / tot


def _outproj(o_gla, o_na, w_out, x2d, mod_l, nrm, router_w, router_b, bsz, seq):
    n = x2d.shape[0]
    tm = TM_PROJ
    nt = seq // tm
    row = lambda b, i: (b * nt + i, 0)
    col = lambda b, i: (0, b * nt + i)
    const = lambda b, i: (0, 0)
    return pl.pallas_call(
        _outproj_kernel,
        grid=(bsz, nt),
        in_specs=[pl.BlockSpec((tm, V_W), row),
                  pl.BlockSpec((tm, NA_W), row),
                  pl.BlockSpec((V_W + NA_W, D), const),
                  pl.BlockSpec((tm, D), row),
                  pl.BlockSpec((6, None, 1, D), lambda b, i: (0, b, 0, 0)),
                  pl.BlockSpec((1, D), const),
                  pl.BlockSpec((N_EXP, D), const),
                  pl.BlockSpec((N_EXP, 1), const)],
        out_specs=[pl.BlockSpec((tm, D), row), pl.BlockSpec((tm, D), row),
                   pl.BlockSpec((2, tm), col), pl.BlockSpec((2, tm), col)],
        out_shape=[jax.ShapeDtypeStruct((n, D), f32), jax.ShapeDtypeStruct((n, D), f32),
                   jax.ShapeDtypeStruct((2, n), i32), jax.ShapeDtypeStruct((2, n), f32)],
        compiler_params=pltpu.CompilerParams(dimension_semantics=("arbitrary", "arbitrary")),
        name="outproj_router",
    )(o_gla, o_na, w_out.astype(bf16), x2d, mod_l, nrm.reshape(1, D),
      router_w.T, router_b.reshape(N_EXP, 1))


def _rank_kernel(eidx_ref, tri_ref, rank_ref, cnt_ref, carry):
    @pl.when(pl.program_id(0) == 0)
    def _():
        carry[...] = jnp.zeros_like(carry)

    e = eidx_ref[...]
    tm = e.shape[1]
    io = lax.broadcasted_iota(i32, (N_EXP, tm), 0)
    run = carry[...]
    for k in range(2):
        oh = io == e[k:k + 1, :]
        ohf = oh.astype(f32)
        pre = jnp.dot(ohf.astype(bf16), tri_ref[...], preferred_element_type=f32) + run[:, 0:1]
        rank_ref[k:k + 1, :] = jnp.sum(jnp.where(oh, pre, 0.0), axis=0, keepdims=True).astype(i32)
        run = run + jnp.sum(ohf, axis=1, keepdims=True)
    carry[...] = run
    cnt_ref[...] = run


def _ranks(eidx):
    n = eidx.shape[1]
    tm = TM_RANK
    t = np.arange(tm)
    tri = jnp.asarray(t[:, None] < t[None, :], bf16)
    return pl.pallas_call(
        _rank_kernel,
        grid=(n // tm,),
        in_specs=[pl.BlockSpec((2, tm), lambda i: (0, i)), pl.BlockSpec((tm, tm), lambda i: (0, 0))],
        out_specs=[pl.BlockSpec((2, tm), lambda i: (0, i)), pl.BlockSpec((N_EXP, 128), lambda i: (0, 0))],
        out_shape=[jax.ShapeDtypeStruct((2, n), i32), jax.ShapeDtypeStruct((N_EXP, 128), f32)],
        scratch_shapes=[pltpu.VMEM((N_EXP, 128), f32)],
        compiler_params=pltpu.CompilerParams(dimension_semantics=("arbitrary",)),
        name="moe_rank",
    )(eidx, tri)


def _dispatch_kernel(dest_ref, tails_ref, h_ref, xb_ref, zbuf, hbuf, lsem, rsem, zsem):
    i = pl.program_id(0)
    nsteps = pl.num_programs(0)
    tm = TM_DISP
    n = nsteps * tm
    nbuf = hbuf.shape[0]

    def load(j):
        start = pl.multiple_of(j * tm, tm)
        return pltpu.make_async_copy(h_ref.at[pl.ds(start, tm)], hbuf.at[j % nbuf], lsem.at[j % nbuf])

    def drain_rows(j):
        for _ in range(2):
            pltpu.make_async_copy(hbuf.at[j % nbuf], xb_ref.at[pl.ds(0, tm)], rsem.at[j % nbuf]).wait()

    def tail_copy(e):
        start = pl.multiple_of(tails_ref[e], MOE_BLK)
        return pltpu.make_async_copy(zbuf, xb_ref.at[pl.ds(start, MOE_BLK)], zsem)

    @pl.when(i == 0)
    def _():
        zbuf[...] = jnp.zeros_like(zbuf)
        for e in range(2 * N_EXP):
            @pl.when(tails_ref[e] >= 0)
            def _():
                tail_copy(e).start()
        for e in range(2 * N_EXP):
            @pl.when(tails_ref[e] >= 0)
            def _():
                tail_copy(e).wait()
        load(i).start()

    @pl.when(i >= nbuf - 1)
    def _():
        drain_rows(i + 1 - nbuf)

    @pl.when(i + 1 < nsteps)
    def _():
        load(i + 1).start()

    load(i).wait()
    slot = i % nbuf

    def issue(t, carry):
        src = hbuf.at[slot, pl.ds(t, 1)]
        for k in range(2):
            d = dest_ref[k * n + i * tm + t]
            pltpu.make_async_copy(src, xb_ref.at[pl.ds(d, 1)], rsem.at[slot]).start()
        return carry

    lax.fori_loop(0, tm, issue, 0, unroll=4)

    @pl.when(i == nsteps - 1)
    def _():
        for back in range(nbuf - 2, -1, -1):
            drain_rows(i - back)


def _dispatch(h2, dest_flat, tails, p_rows):
    n = h2.shape[0]
    tm = TM_DISP
    return pl.pallas_call(
        _dispatch_kernel,
        grid_spec=pltpu.PrefetchScalarGridSpec(
            num_scalar_prefetch=2,
            grid=(n // tm,),
            in_specs=[pl.BlockSpec(memory_space=pl.ANY)],
            out_specs=pl.BlockSpec(memory_space=pl.ANY),
            scratch_shapes=[pltpu.VMEM((MOE_BLK, D), f32), pltpu.VMEM((3, tm, D), f32),
                            pltpu.SemaphoreType.DMA((3,)), pltpu.SemaphoreType.DMA((3,)),
                            pltpu.SemaphoreType.DMA(())],
        ),
        out_shape=jax.ShapeDtypeStruct((p_rows, D), f32),
        compiler_params=pltpu.CompilerParams(dimension_semantics=("arbitrary",)),
        name="moe_dispatch",
    )(dest_flat, tails, h2)


def _expert_kernel(blk_e_ref, nb_ref, xb_ref, w1_ref, w3_ref, w2_ref, y_ref, w1b, w3b, w2b):
    i = pl.program_id(0)

    @pl.when((i < nb_ref[0]) & ((i == 0) | (blk_e_ref[i] != blk_e_ref[jnp.maximum(i - 1, 0)])))
    def _():
        w1b[...] = w1_ref[...].astype(bf16)
        w3b[...] = w3_ref[...].astype(bf16)
        w2b[...] = w2_ref[...].astype(bf16)

    @pl.when(i < nb_ref[0])
    def _():
        x = xb_ref[...].astype(bf16)
        a = jnp.dot(x, w1b[...], preferred_element_type=f32)
        b = jnp.dot(x, w3b[...], preferred_element_type=f32)
        h = (a * jax.nn.sigmoid(a) * b).astype(bf16)
        y_ref[...] = jnp.dot(h, w2b[...], preferred_element_type=f32)

    @pl.when(i >= nb_ref[0])
    def _():
        y_ref[...] = jnp.zeros_like(y_ref)


def _experts(xb, blk_e, nb_used, w1, w3, w2, layer):
    p_rows = xb.shape[0]
    nb = p_rows // MOE_BLK

    def xmap(i, be, nbu):
        return (jnp.minimum(i, nbu[0] - 1), 0)

    def wmap(i, be, nbu):
        return (layer, be[i], 0, 0)

    return pl.pallas_call(
        _expert_kernel,
        grid_spec=pltpu.PrefetchScalarGridSpec(
            num_scalar_prefetch=2,
            grid=(nb,),
            in_specs=[pl.BlockSpec((MOE_BLK, D), xmap),
                      pl.BlockSpec((None, None, D, D_EXP), wmap),
                      pl.BlockSpec((None, None, D, D_EXP), wmap),
                      pl.BlockSpec((None, None, D_EXP, D), wmap)],
            out_specs=pl.BlockSpec((MOE_BLK, D), lambda i, be, nbu: (i, 0)),
            scratch_shapes=[pltpu.VMEM((D, D_EXP), bf16), pltpu.VMEM((D, D_EXP), bf16),
                            pltpu.VMEM((D_EXP, D), bf16)],
        ),
        out_shape=jax.ShapeDtypeStruct((p_rows, D), f32),
        compiler_params=pltpu.CompilerParams(dimension_semantics=("arbitrary",),
                                             vmem_limit_bytes=48 * 1024 * 1024),
        name="moe_experts",
    )(blk_e, nb_used, xb, w1, w3, w2)


def _combine_kernel(dest_ref, x1_ref, gw_ref, mod_ref, y_ref, o_ref, ybuf, sems):
    i = pl.program_id(0)
    nsteps = pl.num_programs(0)
    tm = x1_ref.shape[0]
    n = nsteps * tm

    def issue_tile(j):
        slot = j % 2

        def issue(t, carry):
            for k in range(2):
                d = dest_ref[k * n + j * tm + t]
                pltpu.make_async_copy(y_ref.at[pl.ds(d, 1)], ybuf.at[slot, k, pl.ds(t, 1)],
                                      sems.at[slot]).start()
            return carry

        lax.fori_loop(0, tm, issue, 0, unroll=8)

    @pl.when(i == 0)
    def _():
        issue_tile(i)

    @pl.when(i + 1 < nsteps)
    def _():
        issue_tile(i + 1)

    slot = i % 2
    for k in range(2):
        pltpu.make_async_copy(y_ref.at[pl.ds(0, tm)], ybuf.at[slot, k], sems.at[slot]).wait()
    gw = gw_ref[...]
    y = ybuf[slot, 0] * gw[:, 0:1] + ybuf[slot, 1] * gw[:, 1:2]
    o_ref[...] = x1_ref[...] + mod_ref[5] * y


def _combine(y, dest_flat, x1, gw_t, mod_l, bsz, seq):
    n = x1.shape[0]
    tm = TM_COMB
    nt = seq // tm
    return pl.pallas_call(
        _combine_kernel,
        grid_spec=pltpu.PrefetchScalarGridSpec(
            num_scalar_prefetch=1,
            grid=(n // tm,),
            in_specs=[pl.BlockSpec((tm, D), lambda i, *_: (i, 0)),
                      pl.BlockSpec((tm, 2), lambda i, *_: (i, 0)),
                      pl.BlockSpec((6, None, 1, D), lambda i, *_: (0, i // nt, 0, 0)),
                      pl.BlockSpec(memory_space=pl.ANY)],
            out_specs=pl.BlockSpec((tm, D), lambda i, *_: (i, 0)),
            scratch_shapes=[pltpu.VMEM((2, 2, tm, D), f32), pltpu.SemaphoreType.DMA((2,))],
        ),
        out_shape=jax.ShapeDtypeStruct((n, D), f32),
        compiler_params=pltpu.CompilerParams(dimension_semantics=("arbitrary",)),
        name="moe_combine",
    )(dest_flat, x1, gw_t, mod_l, y)


def _grouped_moe(h2, eidx, gw, x1, mod_l, w1, w3, w2, layer, bsz, seq):
    n = h2.shape[0]
    p_rows = (n * 2 // MOE_BLK + N_EXP) * MOE_BLK
    nb = p_rows // MOE_BLK
    rank, cnt = _ranks(eidx)
    counts = cnt[:, 0].astype(i32)
    padded = (counts + MOE_BLK - 1) // MOE_BLK * MOE_BLK
    pad_end = jnp.cumsum(padded)
    pad_start = pad_end - padded
    start_of = jnp.sum(jnp.where(eidx[..., None] == jnp.arange(N_EXP, dtype=i32), pad_start, 0), axis=-1)
    dest_flat = (start_of + rank).reshape(-1)
    nb_used = (pad_end[-1] // MOE_BLK).astype(i32)
    spare = nb_used + jnp.arange(N_EXP, dtype=i32)
    tails = jnp.concatenate([jnp.where(padded > 0, pad_end - MOE_BLK, -1),
                             jnp.where(spare < nb, spare * MOE_BLK, -1)]).astype(i32)
    blk = jnp.minimum(jnp.arange(nb, dtype=i32), nb_used - 1)
    seg_done = (pad_end[None, :] <= (blk * MOE_BLK)[:, None]).astype(i32)
    blk_e = jnp.minimum(jnp.sum(seg_done, axis=1), N_EXP - 1).astype(i32)
    xb = _dispatch(h2, dest_flat, tails, p_rows)
    y = _experts(xb, blk_e, nb_used.reshape(1), w1, w3, w2, layer)
    return _combine(y, dest_flat, x1, gw.T, mod_l, bsz, seq)


def kernel(x, c, w_ada, b_ada, attn_norm, ffn_norm, w_in, gla_gate_w, gla_gate_b, gla_out_norm,
           na_q_norm, na_k_norm, na_rpb, w_out, router_w, router_b, w1, w3, w2):
    bsz, seq, _ = x.shape
    depth = w_ada.shape[0]
    mod = _adaln_mod(c, w_ada, b_ada)
    xc = x.reshape(bsz * seq, D)
    for l in range(depth):
        (qg, kg, vg, sg, la_f, la_b, qn, kn, vn), bounded_f, bounded_b = _inproj(
            xc, mod[l], attn_norm[l], w_in[l], gla_gate_w[l], gla_gate_b[l], bsz, seq)
        o_f = _gla_dir(False, qg, kg, vg, la_f, bounded_f, bsz, seq)
        o_gla = _gla_dir(True, qg, kg, vg, la_b, bounded_b, bsz, seq, extra=(o_f, sg, gla_out_norm[l]))
        o_na = _natten(qn, kn, vn, na_q_norm[l], na_k_norm[l], na_rpb[l], bsz, seq)
        x1, h2, eidx, gw = _outproj(o_gla, o_na, w_out[l], xc, mod[l], ffn_norm[l],
                                    router_w, router_b, bsz, seq)
        xc = _grouped_moe(h2, eidx, gw, x1, mod[l], w1, w3, w2, l, bsz, seq)
    return xc.reshape(bsz, seq, D)
```

```python
import functools

import numpy as np
import jax
import jax.numpy as jnp
from jax import lax
from jax.experimental import pallas as pl
from jax.experimental.pallas import tpu as pltpu

f32 = jnp.float32
bf16 = jnp.bfloat16
i32 = jnp.int32

D = 1024
GRID_W = 64
GLA_H, GLA_DK, GLA_DV = 4, 64, 128
GLA_RANK = 16
GLA_TAU = 16.0
CHUNK = 64
NA_H, NA_DH = 8, 64
WIN_H, WIN_W = 8, 16
N_EXP, N_GRP, EPG = 32, 4, 8
D_EXP = 512
QK_W = GLA_H * GLA_DK
V_W = GLA_H * GLA_DV
NA_W = NA_H * NA_DH
EPS = 1e-6
NEG = -1e30

TM_PROJ = 512
TB_GLA = 256
NA_QROWS = 4
NA_KROWS = 12
NA_SUB = 4
TM_RANK = 512
TM_DISP = 256
TM_COMB = 256
MOE_BLK = 256
N_LEVELS = 6
GLA_SAFE_RANGE = 40.0


def _nt(a, b):
    return lax.dot_general(a, b, (((1,), (1,)), ((), ())), preferred_element_type=f32)


def _tn(a, b):
    return lax.dot_general(a, b, (((0,), (0,)), ((), ())), preferred_element_type=f32)


def _split3(x):
    hi = x.astype(bf16)
    r = x - hi.astype(f32)
    mid = r.astype(bf16)
    lo = (r - mid.astype(f32)).astype(bf16)
    return hi, mid, lo


def _sel_l(m01, x):
    hi, mid, lo = _split3(x)
    d = lambda p: jnp.dot(m01, p, preferred_element_type=f32)
    return d(hi) + d(mid) + d(lo)


def _sel_r(x, m01):
    hi, mid, lo = _split3(x)
    d = lambda p: jnp.dot(p, m01, preferred_element_type=f32)
    return d(hi) + d(mid) + d(lo)


def _mod_kernel(c_ref, w_ref, b_ref, o_ref):
    c = c_ref[...]
    ca = c * jax.nn.sigmoid(c)
    w = w_ref[0]
    c_hi = ca.astype(bf16)
    c_lo = (ca - c_hi.astype(f32)).astype(bf16)
    w_hi = w.astype(bf16)
    w_lo = (w - w_hi.astype(f32)).astype(bf16)
    acc = (jnp.dot(c_hi, w_hi, preferred_element_type=f32) + jnp.dot(c_lo, w_hi, preferred_element_type=f32)
           + jnp.dot(c_hi, w_lo, preferred_element_type=f32))
    o_ref[0, 0] = acc + b_ref[0, 0]


def _adaln_mod(c, w_ada, b_ada):
    depth = w_ada.shape[0]
    bsz = c.shape[0]
    cp = jnp.zeros((8, D), f32).at[:bsz].set(c)
    out = pl.pallas_call(
        _mod_kernel,
        grid=(depth, 6),
        in_specs=[pl.BlockSpec((8, D), lambda l, j: (0, 0)),
                  pl.BlockSpec((1, D, D), lambda l, j: (l, 0, j)),
                  pl.BlockSpec((1, 1, 1, D), lambda l, j: (l, j, 0, 0))],
        out_specs=pl.BlockSpec((1, 1, 8, D), lambda l, j: (l, j, 0, 0)),
        out_shape=jax.ShapeDtypeStruct((depth, 6, 8, D), f32),
        name="adaln_mod",
    )(cp, w_ada, b_ada.reshape(depth, 6, 1, D))
    return out[:, :, :bsz].reshape(depth, 6, bsz, 1, D)


def _log_sigmoid(z):
    return jnp.minimum(z, 0.0) - jnp.log1p(jnp.exp(-jnp.abs(z)))


def _inproj_kernel(x_ref, mod_ref, nrm_ref, wm_ref, wlr_ref, gf_ref, bf_ref,
                   qg_ref, kg_ref, vg_ref, sg_ref, laf_ref, lab_ref, qn_ref, kn_ref, vn_ref, lamin_ref):
    x = x_ref[...]
    y = x * lax.rsqrt(jnp.mean(x * x, axis=-1, keepdims=True) + EPS) * nrm_ref[...]
    h = y * (1.0 + mod_ref[1]) + mod_ref[0]
    hb = h.astype(bf16)
    proj = lambda lo, hi: jnp.dot(hb, wm_ref[:, lo:hi], preferred_element_type=f32)
    qg_ref[...] = (proj(0, 256) * (GLA_DK ** -0.5)).astype(bf16)
    kg_ref[...] = proj(256, 512).astype(bf16)
    vg_ref[...] = proj(512, 1024).astype(bf16)
    gg = proj(1024, 1536)
    sg_ref[...] = (gg * jax.nn.sigmoid(gg)).astype(bf16)
    qn_ref[...] = proj(1536, 2048).astype(bf16)
    kn_ref[...] = proj(2048, 2560).astype(bf16)
    vn_ref[...] = proj(2560, 3072).astype(bf16)
    lr = jnp.dot(hb, wlr_ref[...], preferred_element_type=f32)
    g = gf_ref[...]
    g_hi = g.astype(bf16)
    g_lo = (g - g_hi.astype(f32)).astype(bf16)
    lr_hi = lr.astype(bf16)
    lr_lo = (lr - lr_hi.astype(f32)).astype(bf16)
    z = (jnp.dot(lr_hi, g_hi, preferred_element_type=f32) + jnp.dot(lr_lo, g_hi, preferred_element_type=f32)
         + jnp.dot(lr_hi, g_lo, preferred_element_type=f32)) + bf_ref[...]
    la = _log_sigmoid(z) * (1.0 / GLA_TAU)
    la_f = la[:, 0:QK_W]
    la_b = la[:, QK_W:2 * QK_W]
    laf_ref[...] = la_f
    lab_ref[...] = la_b
    rows = []
    for la in (la_f, la_b):
        for blk in range(la.shape[0] // TB_GLA):
            tot = jnp.sum(la[blk * TB_GLA:(blk + 1) * TB_GLA], axis=0, keepdims=True)
            rows.append(jnp.broadcast_to(jnp.min(tot, axis=1, keepdims=True), (1, 128)))
    rows.append(jnp.zeros((8 - len(rows), 128), f32))
    lamin_ref[...] = jnp.concatenate(rows, axis=0)


def _inproj(x2d, mod_l, nrm, w_in, gate_w, gate_b, bsz, seq):
    n = x2d.shape[0]
    tm = TM_PROJ
    nt = seq // tm
    lo, hi = 2 * QK_W + 2 * V_W, 2 * QK_W + 2 * V_W + 2 * GLA_RANK
    wm = jnp.concatenate([w_in[:, :lo], w_in[:, hi:]], axis=1).astype(bf16)
    wlr = w_in[:, lo:hi].astype(bf16)
    zpad = jnp.zeros((GLA_RANK, QK_W), f32)
    gmat = jnp.concatenate([jnp.concatenate([gate_w[0], zpad], axis=1),
                            jnp.concatenate([zpad, gate_w[1]], axis=1)], axis=0)
    gbias = jnp.concatenate([gate_b[0], gate_b[1]]).reshape(1, 2 * QK_W)
    row = lambda b, i: (b * nt + i, 0)
    const = lambda b, i: (0, 0)
    widths = [(QK_W, bf16), (QK_W, bf16), (V_W, bf16), (V_W, bf16), (QK_W, f32), (QK_W, f32),
              (NA_W, bf16), (NA_W, bf16), (NA_W, bf16)]
    outs = pl.pallas_call(
        _inproj_kernel,
        grid=(bsz, nt),
        in_specs=[pl.BlockSpec((tm, D), row),
                  pl.BlockSpec((6, None, 1, D), lambda b, i: (0, b, 0, 0)),
                  pl.BlockSpec((1, D), const),
                  pl.BlockSpec(wm.shape, const),
                  pl.BlockSpec(wlr.shape, const),
                  pl.BlockSpec(gmat.shape, const),
                  pl.BlockSpec(gbias.shape, const)],
        out_specs=[pl.BlockSpec((tm, w), row) for w, _ in widths]
        + [pl.BlockSpec((None, 8, 128), lambda b, i: (b * nt + i, 0, 0))],
        out_shape=[jax.ShapeDtypeStruct((n, w), dt) for w, dt in widths]
        + [jax.ShapeDtypeStruct((n // tm, 8, 128), f32)],
        compiler_params=pltpu.CompilerParams(dimension_semantics=("arbitrary", "arbitrary")),
        name="inproj",
    )(x2d, mod_l, nrm.reshape(1, D), wm, wlr, gmat, gbias)
    per = tm // TB_GLA
    lamin = outs[-1][:, :2 * per, 0]
    bounded_f = (lamin[:, :per].reshape(-1) > -GLA_SAFE_RANGE).astype(i32)
    bounded_b = (lamin[:, per:].reshape(-1) > -GLA_SAFE_RANGE).astype(i32)
    return outs[:-1], bounded_f, bounded_b


def _gla_consts(rev, tb):
    c = CHUNK
    idx = np.arange(c)
    if not rev:
        cum = (idx[None, :] <= idx[:, None]).astype(np.float32)
    else:
        cum = (idx[None, :] >= idx[:, None]).astype(np.float32)
    mats = [cum]
    masks = []
    for lvl in range(N_LEVELS):
        s = c >> (lvl + 1)
        blk = idx // (2 * s)
        second = (idx % (2 * s)) >= s
        ref_row = blk * 2 * s + (s if rev else s - 1)
        mats.append(cum[ref_row])
        same = blk[:, None] == blk[None, :]
        if not rev:
            m = same & second[:, None] & (~second[None, :])
        else:
            m = same & (~second[:, None]) & second[None, :]
        masks.append(m)
    masks.append(np.eye(c, dtype=bool) & (not rev))
    mst = np.concatenate(mats, axis=0)
    msk = np.stack([np.tile(m, (1, GLA_H)) for m in masks]).astype(np.float32)
    tix = np.arange(tb)
    tri = (tix[None, :] >= tix[:, None]) if rev else (tix[None, :] <= tix[:, None])
    cmask = (tix[:, None] < tix[None, :]) if rev else (tix[:, None] >= tix[None, :])
    nch = tb // c
    selb = np.zeros((tb, nch * 128), np.float32)
    for ch in range(nch):
        selb[ch * c:(ch + 1) * c, ch * 128:(ch + 1) * 128] = 1.0
    hk = np.arange(QK_W) // GLA_DK
    hv = np.arange(V_W) // GLA_DV
    kmask = (hk[:, None] == hk[None, :]).astype(np.float32)
    vmask = (hk[:, None] == hv[None, :]).astype(np.float32)
    return (jnp.asarray(mst, bf16), jnp.asarray(msk, f32), jnp.asarray(selb, bf16),
            jnp.asarray(kmask, bf16), jnp.asarray(vmask, bf16), jnp.asarray(vmask, f32),
            jnp.asarray(tri, bf16), jnp.asarray(cmask, f32), jnp.ones((tb, 128), bf16))


def _gla_kernel(rev, tb, *refs):
    if rev:
        (bounded_ref, q_ref, k_ref, v_ref, la_ref, mst_ref, msk_ref, selb_ref, kmask_ref, vmask_ref,
         smask_ref, tri_ref, cmask_ref, ones_ref, of_ref, sg_ref, gn_ref, out_ref, s_scr, o_scr) = refs
    else:
        (bounded_ref, q_ref, k_ref, v_ref, la_ref, mst_ref, msk_ref, selb_ref, kmask_ref, vmask_ref,
         smask_ref, tri_ref, cmask_ref, ones_ref, out_ref, s_scr) = refs
        o_scr = out_ref
    c = CHUNK
    nch = tb // c

    @pl.when(pl.program_id(1) == 0)
    def _():
        s_scr[...] = jnp.zeros_like(s_scr)

    def block_bounded():
        la = la_ref[...]
        tot_blk = _sel_r(la.T, ones_ref[...])
        la_hi = la.astype(bf16)
        la_lo = (la - la_hi.astype(f32)).astype(bf16)
        tri = tri_ref[...]
        bcum = (jnp.dot(tri, la_hi, preferred_element_type=f32)
                + jnp.dot(tri, la_lo, preferred_element_type=f32))
        q = q_ref[...].astype(f32)
        k = k_ref[...].astype(f32)
        v = v_ref[...]
        qe = (q * jnp.exp(bcum)).astype(bf16)
        kx = (k * jnp.exp(-bcum)).astype(bf16)
        s_prev = s_scr[...]
        o_inter = jnp.dot(qe, s_prev.astype(bf16), preferred_element_type=f32)
        lane_head = lax.broadcasted_iota(i32, qe.shape, 1) // GLA_DK
        cmask = cmask_ref[...]
        for hd in range(GLA_H):
            cols = slice(hd * GLA_DV, (hd + 1) * GLA_DV)
            qh = jnp.where(lane_head == hd, qe, jnp.zeros_like(qe))
            ph = jnp.where(cmask > 0.0, _nt(qh, kx), 0.0).astype(bf16)
            o_scr[:, cols] = o_inter[:, cols] + jnp.dot(ph, v[:, cols], preferred_element_type=f32)
        blast = bcum[0:1] if rev else bcum[tb - 1:tb]
        ke = (k * jnp.exp(blast - bcum)).astype(bf16)
        dec = jnp.exp(tot_blk)
        s_scr[...] = s_prev * jnp.concatenate([dec] * GLA_H, axis=1) + _tn(ke, v) * smask_ref[...]

    def chunks_robust():
        tot = _sel_r(la_ref[...].T, selb_ref[...])
        mst = mst_ref[...]
        kmask = kmask_ref[...]
        vmask = vmask_ref[...]
        for ch in (range(nch - 1, -1, -1) if rev else range(nch)):
            rows = slice(ch * c, (ch + 1) * c)
            q = q_ref[rows, :].astype(f32)
            k = k_ref[rows, :].astype(f32)
            v = v_ref[rows, :]
            r_all = _sel_l(mst, la_ref[rows, :])
            bcum = r_all[0:c]
            qe = (q * jnp.exp(bcum)).astype(bf16)
            o = jnp.dot(qe, s_scr[...].astype(bf16), preferred_element_type=f32)
            sall = jnp.zeros((c, QK_W), f32)
            for lvl in range(N_LEVELS):
                ref = r_all[(lvl + 1) * c:(lvl + 2) * c]
                a = (q * jnp.exp(jnp.minimum(bcum - ref, 0.0))).astype(bf16)
                bm = (k * jnp.exp(jnp.minimum(ref - bcum, 0.0))).astype(bf16)
                bbd = jnp.concatenate([bm] * GLA_H, axis=0) * kmask
                sall = sall + jnp.where(msk_ref[lvl] > 0.0, _nt(a, bbd), 0.0)
            if not rev:
                bbd = jnp.concatenate([k_ref[rows, :]] * GLA_H, axis=0) * kmask
                sall = sall + jnp.where(msk_ref[N_LEVELS] > 0.0, _nt(q_ref[rows, :], bbd), 0.0)
            vbd = jnp.concatenate([v] * GLA_H, axis=0) * vmask
            o_scr[rows, :] = o + jnp.dot(sall.astype(bf16), vbd, preferred_element_type=f32)
            blast = bcum[0:1] if rev else bcum[c - 1:c]
            ke = (k * jnp.exp(blast - bcum)).astype(bf16)
            dec = jnp.exp(tot[:, ch * 128:(ch + 1) * 128])
            s_scr[...] = (s_scr[...] * jnp.concatenate([dec] * GLA_H, axis=1)
                          + _tn(ke, v) * smask_ref[...])

    nb = pl.num_programs(1)
    blk = pl.program_id(0) * nb + (nb - 1 - pl.program_id(1) if rev else pl.program_id(1))
    bounded = bounded_ref[blk] > 0
    pl.when(bounded)(block_bounded)
    pl.when(jnp.logical_not(bounded))(chunks_robust)

    if rev:
        gn = gn_ref[...]
        for hd in range(GLA_H):
            cols = slice(hd * GLA_DV, (hd + 1) * GLA_DV)
            t = of_ref[:, cols] + o_scr[:, cols]
            y = t * lax.rsqrt(jnp.mean(t * t, axis=-1, keepdims=True) + EPS) * gn
            out_ref[:, cols] = (y * sg_ref[:, cols].astype(f32)).astype(bf16)


def _gla_dir(rev, q, k, v, la, bounded, bsz, seq, extra=None):
    n = q.shape[0]
    tb = TB_GLA
    nb = seq // tb
    consts = _gla_consts(rev, tb)
    if rev:
        row = lambda b, i, flags: (b * nb + nb - 1 - i, 0)
    else:
        row = lambda b, i, flags: (b * nb + i, 0)
    const2 = lambda b, i, flags: (0, 0)
    const3 = lambda b, i, flags: (0, 0, 0)
    in_specs = [pl.BlockSpec((tb, QK_W), row), pl.BlockSpec((tb, QK_W), row),
                pl.BlockSpec((tb, V_W), row), pl.BlockSpec((tb, QK_W), row),
                pl.BlockSpec(consts[0].shape, const2), pl.BlockSpec(consts[1].shape, const3),
                pl.BlockSpec(consts[2].shape, const2), pl.BlockSpec(consts[3].shape, const2),
                pl.BlockSpec(consts[4].shape, const2), pl.BlockSpec(consts[5].shape, const2),
                pl.BlockSpec(consts[6].shape, const2), pl.BlockSpec(consts[7].shape, const2),
                pl.BlockSpec(consts[8].shape, const2)]
    args = [q, k, v, la, *consts]
    scratch = [pltpu.VMEM((QK_W, V_W), f32)]
    if rev:
        o_f, sg, gn = extra
        in_specs += [pl.BlockSpec((tb, V_W), row), pl.BlockSpec((tb, V_W), row),
                     pl.BlockSpec((1, GLA_DV), const2)]
        args += [o_f, sg, gn.reshape(1, GLA_DV)]
        scratch.append(pltpu.VMEM((tb, V_W), f32))
        out_dtype = bf16
    else:
        out_dtype = f32
    return pl.pallas_call(
        functools.partial(_gla_kernel, rev, tb),
        grid_spec=pltpu.PrefetchScalarGridSpec(
            num_scalar_prefetch=1,
            grid=(bsz, nb),
            in_specs=in_specs,
            out_specs=pl.BlockSpec((tb, V_W), row),
            scratch_shapes=scratch,
        ),
        out_shape=jax.ShapeDtypeStruct((n, V_W), out_dtype),
        compiler_params=pltpu.CompilerParams(dimension_semantics=("arbitrary", "arbitrary")),
        name="gla_bwd" if rev else "gla_fwd",
    )(bounded, *args)


def _rpb_expand_kernel(rpb_ref, sel_ref, valid_ref, o_ref):
    e = _sel_r(rpb_ref[...], sel_ref[...])
    o_ref[...] = jnp.where(valid_ref[...] > 0.0, e, NEG)


def _na_bias_table(rpb):
    nri, nci = 2 * WIN_H - 1, 2 * WIN_W - 1
    qc = np.arange(GRID_W)
    col_start = np.clip(qc - WIN_W // 2, 0, GRID_W - WIN_W)
    kc = np.arange(GRID_W)
    valid = (kc[None, :] >= col_start[:, None]) & (kc[None, :] < col_start[:, None] + WIN_W)
    ci = kc[None, :] - qc[:, None] + WIN_W - 1
    sel = np.zeros((32, GRID_W * GRID_W), np.float32)
    flat_ci = ci.reshape(-1)
    ok = (flat_ci >= 0) & (flat_ci < nci)
    sel[flat_ci[ok], np.nonzero(ok)[0]] = 1.0
    rpb2 = jnp.zeros((NA_H * nri + 8 - (NA_H * nri) % 8, 32), f32).at[:NA_H * nri, :nci].set(
        rpb.reshape(NA_H * nri, nci))
    nrow = rpb2.shape[0]
    e = pl.pallas_call(
        _rpb_expand_kernel,
        out_shape=jax.ShapeDtypeStruct((nrow, GRID_W * GRID_W), f32),
        name="rpb_expand",
    )(rpb2, jnp.asarray(sel, bf16), jnp.asarray(valid.reshape(1, -1), f32))
    return e[:NA_H * nri].reshape(NA_H // 2, 2, nri, GRID_W, GRID_W)


def _na_row_classes():
    rows = GRID_W
    ri = -np.ones((3, NA_QROWS, NA_KROWS), np.int64)
    for cls, rb in enumerate((0, 1, rows // NA_QROWS - 1)):
        r0 = rb * NA_QROWS
        kb = int(np.clip(r0 - WIN_H // 2, 0, rows - NA_KROWS))
        for rq in range(NA_QROWS):
            r = r0 + rq
            rs = int(np.clip(r - WIN_H // 2, 0, rows - WIN_H))
            for m in range(NA_KROWS):
                kr = kb + m
                if rs <= kr < rs + WIN_H:
                    ri[cls, rq, m] = kr - r + WIN_H - 1
    return ri


def _natten_kernel(q_ref, k_ref, v_ref, qn_ref, kn_ref, gsum_ref, e_ref, o_ref, kn_scr, tbl_scr):
    b = pl.program_id(1)
    rb = pl.program_id(2)
    gsum = gsum_ref[...]
    log2e = 1.4426950408889634

    def headnorm(x, g):
        ms = _sel_r(x * x, gsum) * (1.0 / NA_DH)
        return x * lax.rsqrt(ms + EPS) * g

    @pl.when((b == 0) & (rb == 0))
    def _():
        ri = _na_row_classes()
        neg = jnp.full((GRID_W, GRID_W), NEG, f32)
        for hh in range(2):
            for cls in range(3):
                for rq in range(NA_QROWS):
                    for mp in range(NA_KROWS // 2):
                        parts = []
                        for m in (2 * mp, 2 * mp + 1):
                            r = int(ri[cls, rq, m])
                            parts.append(neg if r < 0 else e_ref[hh, r] * log2e)
                        r0 = (hh * NA_QROWS + rq) * GRID_W
                        tbl_scr[cls, r0:r0 + GRID_W, mp * 128:(mp + 1) * 128] = (
                            jnp.concatenate(parts, axis=1))

    @pl.when(rb == 0)
    def _():
        kn_scr[...] = headnorm(k_ref[...].astype(f32), kn_ref[...]).astype(bf16)

    nq = NA_QROWS * GRID_W
    nk = NA_KROWS * GRID_W
    nblk = GRID_W // NA_QROWS
    first = lax.broadcasted_iota(i32, (nq, 128), 1) < NA_DH
    qs, wins, tbls = [], [], []
    for sub in range(NA_SUB):
        qb = rb * NA_SUB + sub
        q = headnorm(q_ref[sub * nq:(sub + 1) * nq, :].astype(f32), qn_ref[...]) * (NA_DH ** -0.5 * log2e)
        kb = jnp.clip(qb * NA_QROWS - WIN_H // 2, 0, GRID_W - NA_KROWS)
        start = pl.multiple_of(kb * GRID_W, GRID_W)
        cls = jnp.where(qb == 0, 0, jnp.where(qb == nblk - 1, 2, 1))
        qs.append([jnp.where(first, q, 0.0).astype(bf16), jnp.where(first, 0.0, q).astype(bf16)])
        wins.append((kn_scr[pl.ds(start, nk), :], v_ref[pl.ds(start, nk), :]))
        tbls.append(cls)
    scores = [[_nt(qs[sub][hh], wins[sub][0]) + tbl_scr[tbls[sub], hh * nq:(hh + 1) * nq, :]
               for hh in range(2)] for sub in range(NA_SUB)]
    for sub in range(NA_SUB):
        outs = []
        for s in scores[sub]:
            m = jnp.max(s, axis=-1, keepdims=True)
            p = jnp.exp2(s - m)
            l = jnp.sum(p, axis=-1, keepdims=True)
            outs.append(jnp.dot(p.astype(bf16), wins[sub][1], preferred_element_type=f32) / l)
        o_ref[sub * nq:(sub + 1) * nq, :] = jnp.where(first, outs[0], outs[1]).astype(bf16)


def _natten(qn, kn, vn, q_norm, k_norm, rpb, bsz, seq):
    n = qn.shape[0]
    e5 = _na_bias_table(rpb)
    nq = NA_QROWS * GRID_W
    nqs = NA_SUB * nq
    nrb = seq // nqs
    lane_h = np.arange(128) // NA_DH
    gsum = jnp.asarray(lane_h[:, None] == lane_h[None, :], bf16)
    qn2 = jnp.tile(q_norm.reshape(1, NA_DH), (1, 2))
    kn2 = jnp.tile(k_norm.reshape(1, NA_DH), (1, 2))
    const2 = lambda p, b, r: (0, 0)
    return pl.pallas_call(
        _natten_kernel,
        grid=(NA_H // 2, bsz, nrb),
        in_specs=[pl.BlockSpec((nqs, 128), lambda p, b, r: (b * nrb + r, p)),
                  pl.BlockSpec((seq, 128), lambda p, b, r: (b, p)),
                  pl.BlockSpec((seq, 128), lambda p, b, r: (b, p)),
                  pl.BlockSpec((1, 128), const2),
                  pl.BlockSpec((1, 128), const2),
                  pl.BlockSpec((128, 128), const2),
                  pl.BlockSpec((None,) + e5.shape[1:], lambda p, b, r: (p, 0, 0, 0, 0))],
        out_specs=pl.BlockSpec((nqs, 128), lambda p, b, r: (b * nrb + r, p)),
        out_shape=jax.ShapeDtypeStruct((n, NA_W), bf16),
        scratch_shapes=[pltpu.VMEM((seq, 128), bf16),
                        pltpu.VMEM((3, 2 * nq, NA_KROWS * GRID_W), f32)],
        compiler_params=pltpu.CompilerParams(dimension_semantics=("arbitrary", "arbitrary", "arbitrary")),
        name="natten",
    )(qn, kn, vn, qn2, kn2, gsum, e5)


def _top2(vals):
    io = lax.broadcasted_iota(i32, vals.shape, 0)
    m1 = jnp.max(vals, axis=0, keepdims=True)
    i1 = jnp.min(jnp.where(vals == m1, io, EPG), axis=0, keepdims=True)
    v2 = jnp.where(io == i1, -jnp.inf, vals)
    m2 = jnp.max(v2, axis=0, keepdims=True)
    i2 = jnp.min(jnp.where(v2 == m2, io, EPG), axis=0, keepdims=True)
    return m1, i1, m2, i2


def _outproj_kernel(og_ref, on_ref, wo_ref, x_ref, mod_ref, nrm_ref, rw_ref, rb_ref,
                    x1_ref, h2_ref, eidx_ref, gw_ref):
    mix = (jnp.dot(og_ref[...], wo_ref[0:V_W, :], preferred_element_type=f32)
           + jnp.dot(on_ref[...], wo_ref[V_W:V_W + NA_W, :], preferred_element_type=f32))
    x1 = x_ref[...] + mod_ref[2] * mix
    x1_ref[...] = x1
    y = x1 * lax.rsqrt(jnp.mean(x1 * x1, axis=-1, keepdims=True) + EPS) * nrm_ref[...]
    h2 = y * (1.0 + mod_ref[4]) + mod_ref[3]
    h2_ref[...] = h2
    rw = rw_ref[...]
    rw_hi = rw.astype(bf16)
    rw_lo = (rw - rw_hi.astype(f32)).astype(bf16)
    h_hi = h2.astype(bf16)
    h_lo = (h2 - h_hi.astype(f32)).astype(bf16)
    logits = _nt(rw_hi, h_hi) + _nt(rw_hi, h_lo) + _nt(rw_lo, h_hi)
    scores = jax.nn.sigmoid(logits)
    sel = scores + rb_ref[...]
    tops = [_top2(sel[g * EPG:(g + 1) * EPG]) for g in range(N_GRP)]
    best = jnp.zeros_like(tops[0][1])
    bs = tops[0][0] + tops[0][2]
    for g in range(1, N_GRP):
        gs = tops[g][0] + tops[g][2]
        take = gs > bs
        best = jnp.where(take, g, best)
        bs = jnp.where(take, gs, bs)
    io = lax.broadcasted_iota(i32, (EPG, sel.shape[1]), 0)
    i1 = jnp.zeros_like(best)
    i2 = jnp.zeros_like(best)
    s1 = jnp.zeros(best.shape, f32)
    s2 = jnp.zeros(best.shape, f32)
    for g in range(N_GRP):
        sc = scores[g * EPG:(g + 1) * EPG]
        pick = best == g
        i1 = jnp.where(pick, tops[g][1], i1)
        i2 = jnp.where(pick, tops[g][3], i2)
        s1 = jnp.where(pick, jnp.sum(jnp.where(io == tops[g][1], sc, 0.0), axis=0, keepdims=True), s1)
        s2 = jnp.where(pick, jnp.sum(jnp.where(io == tops[g][3], sc, 0.0), axis=0, keepdims=True), s2)
    eidx_ref[0:1, :] = best * EPG + i1
    eidx_ref[1:2, :] = best * EPG + i2
    tot = s1 + s2
    gw_ref[0:1, :] = s1 / tot
    gw_ref[1:2, :] = s2 / tot


def _outproj(o_gla, o_na, w_out, x2d, mod_l, nrm, router_w, router_b, bsz, seq):
    n = x2d.shape[0]
    tm = TM_PROJ
    nt = seq // tm
    row = lambda b, i: (b * nt + i, 0)
    col = lambda b, i: (0, b * nt + i)
    const = lambda b, i: (0, 0)
    return pl.pallas_call(
        _outproj_kernel,
        grid=(bsz, nt),
        in_specs=[pl.BlockSpec((tm, V_W), row),
                  pl.BlockSpec((tm, NA_W), row),
                  pl.BlockSpec((V_W + NA_W, D), const),
                  pl.BlockSpec((tm, D), row),
                  pl.BlockSpec((6, None, 1, D), lambda b, i: (0, b, 0, 0)),
                  pl.BlockSpec((1, D), const),
                  pl.BlockSpec((N_EXP, D), const),
                  pl.BlockSpec((N_EXP, 1), const)],
        out_specs=[pl.BlockSpec((tm, D), row), pl.BlockSpec((tm, D), row),
                   pl.BlockSpec((2, tm), col), pl.BlockSpec((2, tm), col)],
        out_shape=[jax.ShapeDtypeStruct((n, D), f32), jax.ShapeDtypeStruct((n, D), f32),
                   jax.ShapeDtypeStruct((2, n), i32), jax.ShapeDtypeStruct((2, n), f32)],
        compiler_params=pltpu.CompilerParams(dimension_semantics=("arbitrary", "arbitrary")),
        name="outproj_router",
    )(o_gla, o_na, w_out.astype(bf16), x2d, mod_l, nrm.reshape(1, D),
      router_w.T, router_b.reshape(N_EXP, 1))


def _rank_kernel(eidx_ref, tri_ref, rank_ref, cnt_ref, carry):
    @pl.when(pl.program_id(0) == 0)
    def _():
        carry[...] = jnp.zeros_like(carry)

    e = eidx_ref[...]
    tm = e.shape[1]
    io = lax.broadcasted_iota(i32, (N_EXP, tm), 0)
    run = carry[...]
    for k in range(2):
        oh = io == e[k:k + 1, :]
        ohf = oh.astype(f32)
        pre = jnp.dot(ohf.astype(bf16), tri_ref[...], preferred_element_type=f32) + run[:, 0:1]
        rank_ref[k:k + 1, :] = jnp.sum(jnp.where(oh, pre, 0.0), axis=0, keepdims=True).astype(i32)
        run = run + jnp.sum(ohf, axis=1, keepdims=True)
    carry[...] = run
    cnt_ref[...] = run


def _ranks(eidx):
    n = eidx.shape[1]
    tm = TM_RANK
    t = np.arange(tm)
    tri = jnp.asarray(t[:, None] < t[None, :], bf16)
    return pl.pallas_call(
        _rank_kernel,
        grid=(n // tm,),
        in_specs=[pl.BlockSpec((2, tm), lambda i: (0, i)), pl.BlockSpec((tm, tm), lambda i: (0, 0))],
        out_specs=[pl.BlockSpec((2, tm), lambda i: (0, i)), pl.BlockSpec((N_EXP, 128), lambda i: (0, 0))],
        out_shape=[jax.ShapeDtypeStruct((2, n), i32), jax.ShapeDtypeStruct((N_EXP, 128), f32)],
        scratch_shapes=[pltpu.VMEM((N_EXP, 128), f32)],
        compiler_params=pltpu.CompilerParams(dimension_semantics=("arbitrary",)),
        name="moe_rank",
    )(eidx, tri)


def _dispatch_kernel(dest_ref, tails_ref, h_ref, xb_ref, zbuf, hbuf, lsem, rsem, zsem):
    i = pl.program_id(0)
    nsteps = pl.num_programs(0)
    tm = TM_DISP
    n = nsteps * tm
    nbuf = hbuf.shape[0]

    def load(j):
        start = pl.multiple_of(j * tm, tm)
        return pltpu.make_async_copy(h_ref.at[pl.ds(start, tm)], hbuf.at[j % nbuf], lsem.at[j % nbuf])

    def drain_rows(j):
        for _ in range(2):
            pltpu.make_async_copy(hbuf.at[j % nbuf], xb_ref.at[pl.ds(0, tm)], rsem.at[j % nbuf]).wait()

    def tail_copy(e):
        start = pl.multiple_of(tails_ref[e], MOE_BLK)
        return pltpu.make_async_copy(zbuf, xb_ref.at[pl.ds(start, MOE_BLK)], zsem)

    @pl.when(i == 0)
    def _():
        zbuf[...] = jnp.zeros_like(zbuf)
        for e in range(2 * N_EXP):
            @pl.when(tails_ref[e] >= 0)
            def _():
                tail_copy(e).start()
        for e in range(2 * N_EXP):
            @pl.when(tails_ref[e] >= 0)
            def _():
                tail_copy(e).wait()
        load(i).start()

    @pl.when(i >= nbuf - 1)
    def _():
        drain_rows(i + 1 - nbuf)

    @pl.when(i + 1 < nsteps)
    def _():
        load(i + 1).start()

    load(i).wait()
    slot = i % nbuf

    def issue(t, carry):
        src = hbuf.at[slot, pl.ds(t, 1)]
        for k in range(2):
            d = dest_ref[k * n + i * tm + t]
            pltpu.make_async_copy(src, xb_ref.at[pl.ds(d, 1)], rsem.at[slot]).start()
        return carry

    lax.fori_loop(0, tm, issue, 0, unroll=4)

    @pl.when(i == nsteps - 1)
    def _():
        for back in range(nbuf - 2, -1, -1):
            drain_rows(i - back)


def _dispatch(h2, dest_flat, tails, p_rows):
    n = h2.shape[0]
    tm = TM_DISP
    return pl.pallas_call(
        _dispatch_kernel,
        grid_spec=pltpu.PrefetchScalarGridSpec(
            num_scalar_prefetch=2,
            grid=(n // tm,),
            in_specs=[pl.BlockSpec(memory_space=pl.ANY)],
            out_specs=pl.BlockSpec(memory_space=pl.ANY),
            scratch_shapes=[pltpu.VMEM((MOE_BLK, D), f32), pltpu.VMEM((3, tm, D), f32),
                            pltpu.SemaphoreType.DMA((3,)), pltpu.SemaphoreType.DMA((3,)),
                            pltpu.SemaphoreType.DMA(())],
        ),
        out_shape=jax.ShapeDtypeStruct((p_rows, D), f32),
        compiler_params=pltpu.CompilerParams(dimension_semantics=("arbitrary",)),
        name="moe_dispatch",
    )(dest_flat, tails, h2)


def _expert_kernel(blk_e_ref, nb_ref, xb_ref, w1_ref, w3_ref, w2_ref, y_ref, w1b, w3b, w2b):
    i = pl.program_id(0)

    @pl.when((i < nb_ref[0]) & ((i == 0) | (blk_e_ref[i] != blk_e_ref[jnp.maximum(i - 1, 0)])))
    def _():
        w1b[...] = w1_ref[...].astype(bf16)
        w3b[...] = w3_ref[...].astype(bf16)
        w2b[...] = w2_ref[...].astype(bf16)

    @pl.when(i < nb_ref[0])
    def _():
        x = xb_ref[...].astype(bf16)
        a = jnp.dot(x, w1b[...], preferred_element_type=f32)
        b = jnp.dot(x, w3b[...], preferred_element_type=f32)
        h = (a * jax.nn.sigmoid(a) * b).astype(bf16)
        y_ref[...] = jnp.dot(h, w2b[...], preferred_element_type=f32)

    @pl.when(i >= nb_ref[0])
    def _():
        y_ref[...] = jnp.zeros_like(y_ref)


def _experts(xb, blk_e, nb_used, w1, w3, w2, layer):
    p_rows = xb.shape[0]
    nb = p_rows // MOE_BLK

    def xmap(i, be, nbu):
        return (jnp.minimum(i, nbu[0] - 1), 0)

    def wmap(i, be, nbu):
        return (layer, be[i], 0, 0)

    return pl.pallas_call(
        _expert_kernel,
        grid_spec=pltpu.PrefetchScalarGridSpec(
            num_scalar_prefetch=2,
            grid=(nb,),
            in_specs=[pl.BlockSpec((MOE_BLK, D), xmap),
                      pl.BlockSpec((None, None, D, D_EXP), wmap),
                      pl.BlockSpec((None, None, D, D_EXP), wmap),
                      pl.BlockSpec((None, None, D_EXP, D), wmap)],
            out_specs=pl.BlockSpec((MOE_BLK, D), lambda i, be, nbu: (i, 0)),
            scratch_shapes=[pltpu.VMEM((D, D_EXP), bf16), pltpu.VMEM((D, D_EXP), bf16),
                            pltpu.VMEM((D_EXP, D), bf16)],
        ),
        out_shape=jax.ShapeDtypeStruct((p_rows, D), f32),
        compiler_params=pltpu.CompilerParams(dimension_semantics=("arbitrary",),
                                             vmem_limit_bytes=48 * 1024 * 1024),
        name="moe_experts",
    )(blk_e, nb_used, xb, w1, w3, w2)


def _combine_kernel(dest_ref, x1_ref, gw_ref, mod_ref, y_ref, o_ref, ybuf, sems):
    i = pl.program_id(0)
    nsteps = pl.num_programs(0)
    tm = x1_ref.shape[0]
    n = nsteps * tm

    def issue_tile(j):
        slot = j % 2

        def issue(t, carry):
            for k in range(2):
                d = dest_ref[k * n + j * tm + t]
                pltpu.make_async_copy(y_ref.at[pl.ds(d, 1)], ybuf.at[slot, k, pl.ds(t, 1)],
                                      sems.at[slot]).start()
            return carry

        lax.fori_loop(0, tm, issue, 0, unroll=8)

    @pl.when(i == 0)
    def _():
        issue_tile(i)

    @pl.when(i + 1 < nsteps)
    def _():
        issue_tile(i + 1)

    slot = i % 2
    for k in range(2):
        pltpu.make_async_copy(y_ref.at[pl.ds(0, tm)], ybuf.at[slot, k], sems.at[slot]).wait()
    gw = gw_ref[...]
    y = ybuf[slot, 0] * gw[:, 0:1] + ybuf[slot, 1] * gw[:, 1:2]
    o_ref[...] = x1_ref[...] + mod_ref[5] * y


def _combine(y, dest_flat, x1, gw_t, mod_l, bsz, seq):
    n = x1.shape[0]
    tm = TM_COMB
    nt = seq // tm
    return pl.pallas_call(
        _combine_kernel,
        grid_spec=pltpu.PrefetchScalarGridSpec(
            num_scalar_prefetch=1,
            grid=(n // tm,),
            in_specs=[pl.BlockSpec((tm, D), lambda i, *_: (i, 0)),
                      pl.BlockSpec((tm, 2), lambda i, *_: (i, 0)),
                      pl.BlockSpec((6, None, 1, D), lambda i, *_: (0, i // nt, 0, 0)),
                      pl.BlockSpec(memory_space=pl.ANY)],
            out_specs=pl.BlockSpec((tm, D), lambda i, *_: (i, 0)),
            scratch_shapes=[pltpu.VMEM((2, 2, tm, D), f32), pltpu.SemaphoreType.DMA((2,))],
        ),
        out_shape=jax.ShapeDtypeStruct((n, D), f32),
        compiler_params=pltpu.CompilerParams(dimension_semantics=("arbitrary",)),
        name="moe_combine",
    )(dest_flat, x1, gw_t, mod_l, y)


def _grouped_moe(h2, eidx, gw, x1, mod_l, w1, w3, w2, layer, bsz, seq):
    n = h2.shape[0]
    p_rows = (n * 2 // MOE_BLK + N_EXP) * MOE_BLK
    nb = p_rows // MOE_BLK
    rank, cnt = _ranks(eidx)
    counts = cnt[:, 0].astype(i32)
    padded = (counts + MOE_BLK - 1) // MOE_BLK * MOE_BLK
    pad_end = jnp.cumsum(padded)
    pad_start = pad_end - padded
    start_of = jnp.sum(jnp.where(eidx[..., None] == jnp.arange(N_EXP, dtype=i32), pad_start, 0), axis=-1)
    dest_flat = (start_of + rank).reshape(-1)
    nb_used = (pad_end[-1] // MOE_BLK).astype(i32)
    spare = nb_used + jnp.arange(N_EXP, dtype=i32)
    tails = jnp.concatenate([jnp.where(padded > 0, pad_end - MOE_BLK, -1),
                             jnp.where(spare < nb, spare * MOE_BLK, -1)]).astype(i32)
    blk = jnp.minimum(jnp.arange(nb, dtype=i32), nb_used - 1)
    seg_done = (pad_end[None, :] <= (blk * MOE_BLK)[:, None]).astype(i32)
    blk_e = jnp.minimum(jnp.sum(seg_done, axis=1), N_EXP - 1).astype(i32)
    xb = _dispatch(h2, dest_flat, tails, p_rows)
    y = _experts(xb, blk_e, nb_used.reshape(1), w1, w3, w2, layer)
    return _combine(y, dest_flat, x1, gw.T, mod_l, bsz, seq)


def kernel(x, c, w_ada, b_ada, attn_norm, ffn_norm, w_in, gla_gate_w, gla_gate_b, gla_out_norm,
           na_q_norm, na_k_norm, na_rpb, w_out, router_w, router_b, w1, w3, w2):
    bsz, seq, _ = x.shape
    depth = w_ada.shape[0]
    mod = _adaln_mod(c, w_ada, b_ada)
    xc = x.reshape(bsz * seq, D)
    for l in range(depth):
        (qg, kg, vg, sg, la_f, la_b, qn, kn, vn), bounded_f, bounded_b = _inproj(
            xc, mod[l], attn_norm[l], w_in[l], gla_gate_w[l], gla_gate_b[l], bsz, seq)
        o_f = _gla_dir(False, qg, kg, vg, la_f, bounded_f, bsz, seq)
        o_gla = _gla_dir(True, qg, kg, vg, la_b, bounded_b, bsz, seq, extra=(o_f, sg, gla_out_norm[l]))
        o_na = _natten(qn, kn, vn, na_q_norm[l], na_k_norm[l], na_rpb[l], bsz, seq)
        x1, h2, eidx, gw = _outproj(o_gla, o_na, w_out[l], xc, mod[l], ffn_norm[l],
                                    router_w, router_b, bsz, seq)
        xc = _grouped_moe(h2, eidx, gw, x1, mod[l], w1, w3, w2, l, bsz, seq)
    return xc.reshape(bsz, seq, D)
```

```python
import functools

import numpy as np
import jax
import jax.numpy as jnp
from jax import lax
from jax.experimental import pallas as pl
from jax.experimental.pallas import tpu as pltpu

f32 = jnp.float32
bf16 = jnp.bfloat16
i32 = jnp.int32

D = 1024
GRID_W = 64
GLA_H, GLA_DK, GLA_DV = 4, 64, 128
GLA_RANK = 16
GLA_TAU = 16.0
CHUNK = 64
NA_H, NA_DH = 8, 64
WIN_H, WIN_W = 8, 16
N_EXP, N_GRP, EPG = 32, 4, 8
D_EXP = 512
QK_W = GLA_H * GLA_DK
V_W = GLA_H * GLA_DV
NA_W = NA_H * NA_DH
EPS = 1e-6
NEG = -1e30

TM_PROJ = 512
TB_GLA = 256
NA_QROWS = 4
NA_KROWS = 12
NA_SUB = 4
TM_RANK = 512
TM_DISP = 256
TM_COMB = 256
MOE_BLK = 512
N_LEVELS = 6
GLA_SAFE_RANGE = 40.0


def _nt(a, b):
    return lax.dot_general(a, b, (((1,), (1,)), ((), ())), preferred_element_type=f32)


def _tn(a, b):
    return lax.dot_general(a, b, (((0,), (0,)), ((), ())), preferred_element_type=f32)


def _split3(x):
    hi = x.astype(bf16)
    r = x - hi.astype(f32)
    mid = r.astype(bf16)
    lo = (r - mid.astype(f32)).astype(bf16)
    return hi, mid, lo


def _sel_l(m01, x):
    hi, mid, lo = _split3(x)
    d = lambda p: jnp.dot(m01, p, preferred_element_type=f32)
    return d(hi) + d(mid) + d(lo)


def _sel_r(x, m01):
    hi, mid, lo = _split3(x)
    d = lambda p: jnp.dot(p, m01, preferred_element_type=f32)
    return d(hi) + d(mid) + d(lo)


def _mod_kernel(c_ref, w_ref, b_ref, o_ref):
    c = c_ref[...]
    ca = c * jax.nn.sigmoid(c)
    w = w_ref[0]
    c_hi = ca.astype(bf16)
    c_lo = (ca - c_hi.astype(f32)).astype(bf16)
    w_hi = w.astype(bf16)
    w_lo = (w - w_hi.astype(f32)).astype(bf16)
    acc = (jnp.dot(c_hi, w_hi, preferred_element_type=f32) + jnp.dot(c_lo, w_hi, preferred_element_type=f32)
           + jnp.dot(c_hi, w_lo, preferred_element_type=f32))
    o_ref[0, 0] = acc + b_ref[0, 0]


def _adaln_mod(c, w_ada, b_ada):
    depth = w_ada.shape[0]
    bsz = c.shape[0]
    cp = jnp.zeros((8, D), f32).at[:bsz].set(c)
    out = pl.pallas_call(
        _mod_kernel,
        grid=(depth, 6),
        in_specs=[pl.BlockSpec((8, D), lambda l, j: (0, 0)),
                  pl.BlockSpec((1, D, D), lambda l, j: (l, 0, j)),
                  pl.BlockSpec((1, 1, 1, D), lambda l, j: (l, j, 0, 0))],
        out_specs=pl.BlockSpec((1, 1, 8, D), lambda l, j: (l, j, 0, 0)),
        out_shape=jax.ShapeDtypeStruct((depth, 6, 8, D), f32),
        name="adaln_mod",
    )(cp, w_ada, b_ada.reshape(depth, 6, 1, D))
    return out[:, :, :bsz].reshape(depth, 6, bsz, 1, D)


def _log_sigmoid(z):
    return jnp.minimum(z, 0.0) - jnp.log1p(jnp.exp(-jnp.abs(z)))


def _inproj_kernel(x_ref, mod_ref, nrm_ref, wm_ref, wlr_ref, gf_ref, bf_ref,
                   qg_ref, kg_ref, vg_ref, sg_ref, laf_ref, lab_ref, qn_ref, kn_ref, vn_ref, lamin_ref):
    x = x_ref[...]
    y = x * lax.rsqrt(jnp.mean(x * x, axis=-1, keepdims=True) + EPS) * nrm_ref[...]
    h = y * (1.0 + mod_ref[1]) + mod_ref[0]
    hb = h.astype(bf16)
    proj = lambda lo, hi: jnp.dot(hb, wm_ref[:, lo:hi], preferred_element_type=f32)
    qg_ref[...] = (proj(0, 256) * (GLA_DK ** -0.5)).astype(bf16)
    kg_ref[...] = proj(256, 512).astype(bf16)
    vg_ref[...] = proj(512, 1024).astype(bf16)
    gg = proj(1024, 1536)
    sg_ref[...] = (gg * jax.nn.sigmoid(gg)).astype(bf16)
    qn_ref[...] = proj(1536, 2048).astype(bf16)
    kn_ref[...] = proj(2048, 2560).astype(bf16)
    vn_ref[...] = proj(2560, 3072).astype(bf16)
    lr = jnp.dot(hb, wlr_ref[...], preferred_element_type=f32)
    z = jnp.dot(lr, gf_ref[...], precision=lax.Precision.HIGHEST, preferred_element_type=f32) + bf_ref[...]
    la = _log_sigmoid(z) * (1.0 / GLA_TAU)
    la_f = la[:, 0:QK_W]
    la_b = la[:, QK_W:2 * QK_W]
    laf_ref[...] = la_f
    lab_ref[...] = la_b
    rows = []
    for la in (la_f, la_b):
        for blk in range(la.shape[0] // TB_GLA):
            tot = jnp.sum(la[blk * TB_GLA:(blk + 1) * TB_GLA], axis=0, keepdims=True)
            rows.append(jnp.broadcast_to(jnp.min(tot, axis=1, keepdims=True), (1, 128)))
    rows.append(jnp.zeros((8 - len(rows), 128), f32))
    lamin_ref[...] = jnp.concatenate(rows, axis=0)


def _inproj(x2d, mod_l, nrm, w_in, gate_w, gate_b, bsz, seq):
    n = x2d.shape[0]
    tm = TM_PROJ
    nt = seq // tm
    lo, hi = 2 * QK_W + 2 * V_W, 2 * QK_W + 2 * V_W + 2 * GLA_RANK
    wm = jnp.concatenate([w_in[:, :lo], w_in[:, hi:]], axis=1).astype(bf16)
    wlr = w_in[:, lo:hi].astype(bf16)
    zpad = jnp.zeros((GLA_RANK, QK_W), f32)
    gmat = jnp.concatenate([jnp.concatenate([gate_w[0], zpad], axis=1),
                            jnp.concatenate([zpad, gate_w[1]], axis=1)], axis=0)
    gbias = jnp.concatenate([gate_b[0], gate_b[1]]).reshape(1, 2 * QK_W)
    row = lambda b, i: (b * nt + i, 0)
    const = lambda b, i: (0, 0)
    widths = [(QK_W, bf16), (QK_W, bf16), (V_W, bf16), (V_W, bf16), (QK_W, f32), (QK_W, f32),
              (NA_W, bf16), (NA_W, bf16), (NA_W, bf16)]
    outs = pl.pallas_call(
        _inproj_kernel,
        grid=(bsz, nt),
        in_specs=[pl.BlockSpec((tm, D), row),
                  pl.BlockSpec((6, None, 1, D), lambda b, i: (0, b, 0, 0)),
                  pl.BlockSpec((1, D), const),
                  pl.BlockSpec(wm.shape, const),
                  pl.BlockSpec(wlr.shape, const),
                  pl.BlockSpec(gmat.shape, const),
                  pl.BlockSpec(gbias.shape, const)],
        out_specs=[pl.BlockSpec((tm, w), row) for w, _ in widths]
        + [pl.BlockSpec((None, 8, 128), lambda b, i: (b * nt + i, 0, 0))],
        out_shape=[jax.ShapeDtypeStruct((n, w), dt) for w, dt in widths]
        + [jax.ShapeDtypeStruct((n // tm, 8, 128), f32)],
        compiler_params=pltpu.CompilerParams(dimension_semantics=("arbitrary", "arbitrary")),
        name="inproj",
    )(x2d, mod_l, nrm.reshape(1, D), wm, wlr, gmat, gbias)
    per = tm // TB_GLA
    lamin = outs[-1][:, :2 * per, 0]
    bounded_f = (lamin[:, :per].reshape(-1) > -GLA_SAFE_RANGE).astype(i32)
    bounded_b = (lamin[:, per:].reshape(-1) > -GLA_SAFE_RANGE).astype(i32)
    return outs[:-1], bounded_f, bounded_b


def _gla_consts(rev, tb):
    c = CHUNK
    idx = np.arange(c)
    if not rev:
        cum = (idx[None, :] <= idx[:, None]).astype(np.float32)
    else:
        cum = (idx[None, :] >= idx[:, None]).astype(np.float32)
    mats = [cum]
    masks = []
    for lvl in range(N_LEVELS):
        s = c >> (lvl + 1)
        blk = idx // (2 * s)
        second = (idx % (2 * s)) >= s
        ref_row = blk * 2 * s + (s if rev else s - 1)
        mats.append(cum[ref_row])
        same = blk[:, None] == blk[None, :]
        if not rev:
            m = same & second[:, None] & (~second[None, :])
        else:
            m = same & (~second[:, None]) & second[None, :]
        masks.append(m)
    masks.append(np.eye(c, dtype=bool) & (not rev))
    mst = np.concatenate(mats, axis=0)
    msk = np.stack([np.tile(m, (1, GLA_H)) for m in masks]).astype(np.float32)
    tix = np.arange(tb)
    tri = (tix[None, :] >= tix[:, None]) if rev else (tix[None, :] <= tix[:, None])
    cmask = (tix[:, None] < tix[None, :]) if rev else (tix[:, None] >= tix[None, :])
    nch = tb // c
    selb = np.zeros((tb, nch * 128), np.float32)
    for ch in range(nch):
        selb[ch * c:(ch + 1) * c, ch * 128:(ch + 1) * 128] = 1.0
    hk = np.arange(QK_W) // GLA_DK
    hv = np.arange(V_W) // GLA_DV
    kmask = (hk[:, None] == hk[None, :]).astype(np.float32)
    vmask = (hk[:, None] == hv[None, :]).astype(np.float32)
    return (jnp.asarray(mst, bf16), jnp.asarray(msk, f32), jnp.asarray(selb, bf16),
            jnp.asarray(kmask, bf16), jnp.asarray(vmask, bf16), jnp.asarray(vmask, f32),
            jnp.asarray(tri, bf16), jnp.asarray(cmask, f32), jnp.ones((tb, 128), bf16))


def _gla_kernel(rev, tb, *refs):
    if rev:
        (bounded_ref, q_ref, k_ref, v_ref, la_ref, mst_ref, msk_ref, selb_ref, kmask_ref, vmask_ref,
         smask_ref, tri_ref, cmask_ref, ones_ref, of_ref, sg_ref, gn_ref, out_ref, s_scr, o_scr) = refs
    else:
        (bounded_ref, q_ref, k_ref, v_ref, la_ref, mst_ref, msk_ref, selb_ref, kmask_ref, vmask_ref,
         smask_ref, tri_ref, cmask_ref, ones_ref, out_ref, s_scr) = refs
        o_scr = out_ref
    c = CHUNK
    nch = tb // c

    @pl.when(pl.program_id(1) == 0)
    def _():
        s_scr[...] = jnp.zeros_like(s_scr)

    def block_bounded():
        la = la_ref[...]
        tot_blk = _sel_r(la.T, ones_ref[...])
        la_hi = la.astype(bf16)
        la_lo = (la - la_hi.astype(f32)).astype(bf16)
        tri = tri_ref[...]
        bcum = (jnp.dot(tri, la_hi, preferred_element_type=f32)
                + jnp.dot(tri, la_lo, preferred_element_type=f32))
        q = q_ref[...].astype(f32)
        k = k_ref[...].astype(f32)
        v = v_ref[...]
        qe = (q * jnp.exp(bcum)).astype(bf16)
        kx = (k * jnp.exp(-bcum)).astype(bf16)
        s_prev = s_scr[...]
        o_inter = jnp.dot(qe, s_prev.astype(bf16), preferred_element_type=f32)
        lane_head = lax.broadcasted_iota(i32, qe.shape, 1) // GLA_DK
        cmask = cmask_ref[...]
        for hd in range(GLA_H):
            cols = slice(hd * GLA_DV, (hd + 1) * GLA_DV)
            qh = jnp.where(lane_head == hd, qe, jnp.zeros_like(qe))
            ph = jnp.where(cmask > 0.0, _nt(qh, kx), 0.0).astype(bf16)
            o_scr[:, cols] = o_inter[:, cols] + jnp.dot(ph, v[:, cols], preferred_element_type=f32)
        blast = bcum[0:1] if rev else bcum[tb - 1:tb]
        ke = (k * jnp.exp(blast - bcum)).astype(bf16)
        dec = jnp.exp(tot_blk)
        s_scr[...] = s_prev * jnp.concatenate([dec] * GLA_H, axis=1) + _tn(ke, v) * smask_ref[...]

    def chunks_robust():
        tot = _sel_r(la_ref[...].T, selb_ref[...])
        mst = mst_ref[...]
        kmask = kmask_ref[...]
        vmask = vmask_ref[...]
        for ch in (range(nch - 1, -1, -1) if rev else range(nch)):
            rows = slice(ch * c, (ch + 1) * c)
            q = q_ref[rows, :].astype(f32)
            k = k_ref[rows, :].astype(f32)
            v = v_ref[rows, :]
            r_all = _sel_l(mst, la_ref[rows, :])
            bcum = r_all[0:c]
            qe = (q * jnp.exp(bcum)).astype(bf16)
            o = jnp.dot(qe, s_scr[...].astype(bf16), preferred_element_type=f32)
            sall = jnp.zeros((c, QK_W), f32)
            for lvl in range(N_LEVELS):
                ref = r_all[(lvl + 1) * c:(lvl + 2) * c]
                a = (q * jnp.exp(jnp.minimum(bcum - ref, 0.0))).astype(bf16)
                bm = (k * jnp.exp(jnp.minimum(ref - bcum, 0.0))).astype(bf16)
                bbd = jnp.concatenate([bm] * GLA_H, axis=0) * kmask
                sall = sall + jnp.where(msk_ref[lvl] > 0.0, _nt(a, bbd), 0.0)
            if not rev:
                bbd = jnp.concatenate([k_ref[rows, :]] * GLA_H, axis=0) * kmask
                sall = sall + jnp.where(msk_ref[N_LEVELS] > 0.0, _nt(q_ref[rows, :], bbd), 0.0)
            vbd = jnp.concatenate([v] * GLA_H, axis=0) * vmask
            o_scr[rows, :] = o + jnp.dot(sall.astype(bf16), vbd, preferred_element_type=f32)
            blast = bcum[0:1] if rev else bcum[c - 1:c]
            ke = (k * jnp.exp(blast - bcum)).astype(bf16)
            dec = jnp.exp(tot[:, ch * 128:(ch + 1) * 128])
            s_scr[...] = (s_scr[...] * jnp.concatenate([dec] * GLA_H, axis=1)
                          + _tn(ke, v) * smask_ref[...])

    nb = pl.num_programs(1)
    blk = pl.program_id(0) * nb + (nb - 1 - pl.program_id(1) if rev else pl.program_id(1))
    bounded = bounded_ref[blk] > 0
    pl.when(bounded)(block_bounded)
    pl.when(jnp.logical_not(bounded))(chunks_robust)

    if rev:
        gn = gn_ref[...]
        for hd in range(GLA_H):
            cols = slice(hd * GLA_DV, (hd + 1) * GLA_DV)
            t = of_ref[:, cols] + o_scr[:, cols]
            y = t * lax.rsqrt(jnp.mean(t * t, axis=-1, keepdims=True) + EPS) * gn
            out_ref[:, cols] = (y * sg_ref[:, cols].astype(f32)).astype(bf16)


def _gla_dir(rev, q, k, v, la, bounded, bsz, seq, extra=None):
    n = q.shape[0]
    tb = TB_GLA
    nb = seq // tb
    consts = _gla_consts(rev, tb)
    if rev:
        row = lambda b, i, flags: (b * nb + nb - 1 - i, 0)
    else:
        row = lambda b, i, flags: (b * nb + i, 0)
    const2 = lambda b, i, flags: (0, 0)
    const3 = lambda b, i, flags: (0, 0, 0)
    in_specs = [pl.BlockSpec((tb, QK_W), row), pl.BlockSpec((tb, QK_W), row),
                pl.BlockSpec((tb, V_W), row), pl.BlockSpec((tb, QK_W), row),
                pl.BlockSpec(consts[0].shape, const2), pl.BlockSpec(consts[1].shape, const3),
                pl.BlockSpec(consts[2].shape, const2), pl.BlockSpec(consts[3].shape, const2),
                pl.BlockSpec(consts[4].shape, const2), pl.BlockSpec(consts[5].shape, const2),
                pl.BlockSpec(consts[6].shape, const2), pl.BlockSpec(consts[7].shape, const2),
                pl.BlockSpec(consts[8].shape, const2)]
    args = [q, k, v, la, *consts]
    scratch = [pltpu.VMEM((QK_W, V_W), f32)]
    if rev:
        o_f, sg, gn = extra
        in_specs += [pl.BlockSpec((tb, V_W), row), pl.BlockSpec((tb, V_W), row),
                     pl.BlockSpec((1, GLA_DV), const2)]
        args += [o_f, sg, gn.reshape(1, GLA_DV)]
        scratch.append(pltpu.VMEM((tb, V_W), f32))
        out_dtype = bf16
    else:
        out_dtype = f32
    return pl.pallas_call(
        functools.partial(_gla_kernel, rev, tb),
        grid_spec=pltpu.PrefetchScalarGridSpec(
            num_scalar_prefetch=1,
            grid=(bsz, nb),
            in_specs=in_specs,
            out_specs=pl.BlockSpec((tb, V_W), row),
            scratch_shapes=scratch,
        ),
        out_shape=jax.ShapeDtypeStruct((n, V_W), out_dtype),
        compiler_params=pltpu.CompilerParams(dimension_semantics=("arbitrary", "arbitrary")),
        name="gla_bwd" if rev else "gla_fwd",
    )(bounded, *args)


def _rpb_expand_kernel(rpb_ref, sel_ref, valid_ref, o_ref):
    e = _sel_r(rpb_ref[...], sel_ref[...])
    o_ref[...] = jnp.where(valid_ref[...] > 0.0, e, NEG)


def _na_bias_table(rpb):
    nri, nci = 2 * WIN_H - 1, 2 * WIN_W - 1
    qc = np.arange(GRID_W)
    col_start = np.clip(qc - WIN_W // 2, 0, GRID_W - WIN_W)
    kc = np.arange(GRID_W)
    valid = (kc[None, :] >= col_start[:, None]) & (kc[None, :] < col_start[:, None] + WIN_W)
    ci = kc[None, :] - qc[:, None] + WIN_W - 1
    sel = np.zeros((32, GRID_W * GRID_W), np.float32)
    flat_ci = ci.reshape(-1)
    ok = (flat_ci >= 0) & (flat_ci < nci)
    sel[flat_ci[ok], np.nonzero(ok)[0]] = 1.0
    rpb2 = jnp.zeros((NA_H * nri + 8 - (NA_H * nri) % 8, 32), f32).at[:NA_H * nri, :nci].set(
        rpb.reshape(NA_H * nri, nci))
    nrow = rpb2.shape[0]
    e = pl.pallas_call(
        _rpb_expand_kernel,
        out_shape=jax.ShapeDtypeStruct((nrow, GRID_W * GRID_W), f32),
        name="rpb_expand",
    )(rpb2, jnp.asarray(sel, bf16), jnp.asarray(valid.reshape(1, -1), f32))
    return e[:NA_H * nri].reshape(NA_H // 2, 2, nri, GRID_W, GRID_W)


def _na_row_classes():
    rows = GRID_W
    ri = -np.ones((3, NA_QROWS, NA_KROWS), np.int64)
    for cls, rb in enumerate((0, 1, rows // NA_QROWS - 1)):
        r0 = rb * NA_QROWS
        kb = int(np.clip(r0 - WIN_H // 2, 0, rows - NA_KROWS))
        for rq in range(NA_QROWS):
            r = r0 + rq
            rs = int(np.clip(r - WIN_H // 2, 0, rows - WIN_H))
            for m in range(NA_KROWS):
                kr = kb + m
                if rs <= kr < rs + WIN_H:
                    ri[cls, rq, m] = kr - r + WIN_H - 1
    return ri


def _natten_kernel(q_ref, k_ref, v_ref, qn_ref, kn_ref, gsum_ref, e_ref, o_ref, kn_scr, tbl_scr):
    b = pl.program_id(1)
    rb = pl.program_id(2)
    gsum = gsum_ref[...]
    log2e = 1.4426950408889634

    def headnorm(x, g):
        ms = _sel_r(x * x, gsum) * (1.0 / NA_DH)
        return x * lax.rsqrt(ms + EPS) * g

    @pl.when((b == 0) & (rb == 0))
    def _():
        ri = _na_row_classes()
        neg = jnp.full((GRID_W, GRID_W), NEG, f32)
        for hh in range(2):
            for cls in range(3):
                for rq in range(NA_QROWS):
                    for mp in range(NA_KROWS // 2):
                        parts = []
                        for m in (2 * mp, 2 * mp + 1):
                            r = int(ri[cls, rq, m])
                            parts.append(neg if r < 0 else e_ref[hh, r] * log2e)
                        r0 = (hh * NA_QROWS + rq) * GRID_W
                        tbl_scr[cls, r0:r0 + GRID_W, mp * 128:(mp + 1) * 128] = (
                            jnp.concatenate(parts, axis=1))

    @pl.when(rb == 0)
    def _():
        kn_scr[...] = headnorm(k_ref[...].astype(f32), kn_ref[...]).astype(bf16)

    nq = NA_QROWS * GRID_W
    nk = NA_KROWS * GRID_W
    nblk = GRID_W // NA_QROWS
    first = lax.broadcasted_iota(i32, (nq, 128), 1) < NA_DH
    qs, wins, tbls = [], [], []
    for sub in range(NA_SUB):
        qb = rb * NA_SUB + sub
        q = headnorm(q_ref[sub * nq:(sub + 1) * nq, :].astype(f32), qn_ref[...]) * (NA_DH ** -0.5 * log2e)
        kb = jnp.clip(qb * NA_QROWS - WIN_H // 2, 0, GRID_W - NA_KROWS)
        start = pl.multiple_of(kb * GRID_W, GRID_W)
        cls = jnp.where(qb == 0, 0, jnp.where(qb == nblk - 1, 2, 1))
        qs.append([jnp.where(first, q, 0.0).astype(bf16), jnp.where(first, 0.0, q).astype(bf16)])
        wins.append((kn_scr[pl.ds(start, nk), :], v_ref[pl.ds(start, nk), :]))
        tbls.append(cls)
    scores = [[_nt(qs[sub][hh], wins[sub][0]) + tbl_scr[tbls[sub], hh * nq:(hh + 1) * nq, :]
               for hh in range(2)] for sub in range(NA_SUB)]
    for sub in range(NA_SUB):
        outs = []
        for s in scores[sub]:
            m = jnp.max(s, axis=-1, keepdims=True)
            p = jnp.exp2(s - m)
            l = jnp.sum(p, axis=-1, keepdims=True)
            outs.append(jnp.dot(p.astype(bf16), wins[sub][1], preferred_element_type=f32) / l)
        o_ref[sub * nq:(sub + 1) * nq, :] = jnp.where(first, outs[0], outs[1]).astype(bf16)


def _natten(qn, kn, vn, q_norm, k_norm, rpb, bsz, seq):
    n = qn.shape[0]
    e5 = _na_bias_table(rpb)
    nq = NA_QROWS * GRID_W
    nqs = NA_SUB * nq
    nrb = seq // nqs
    lane_h = np.arange(128) // NA_DH
    gsum = jnp.asarray(lane_h[:, None] == lane_h[None, :], bf16)
    qn2 = jnp.tile(q_norm.reshape(1, NA_DH), (1, 2))
    kn2 = jnp.tile(k_norm.reshape(1, NA_DH), (1, 2))
    const2 = lambda p, b, r: (0, 0)
    return pl.pallas_call(
        _natten_kernel,
        grid=(NA_H // 2, bsz, nrb),
        in_specs=[pl.BlockSpec((nqs, 128), lambda p, b, r: (b * nrb + r, p)),
                  pl.BlockSpec((seq, 128), lambda p, b, r: (b, p)),
                  pl.BlockSpec((seq, 128), lambda p, b, r: (b, p)),
                  pl.BlockSpec((1, 128), const2),
                  pl.BlockSpec((1, 128), const2),
                  pl.BlockSpec((128, 128), const2),
                  pl.BlockSpec((None,) + e5.shape[1:], lambda p, b, r: (p, 0, 0, 0, 0))],
        out_specs=pl.BlockSpec((nqs, 128), lambda p, b, r: (b * nrb + r, p)),
        out_shape=jax.ShapeDtypeStruct((n, NA_W), bf16),
        scratch_shapes=[pltpu.VMEM((seq, 128), bf16),
                        pltpu.VMEM((3, 2 * nq, NA_KROWS * GRID_W), f32)],
        compiler_params=pltpu.CompilerParams(dimension_semantics=("arbitrary", "arbitrary", "arbitrary")),
        name="natten",
    )(qn, kn, vn, qn2, kn2, gsum, e5)


def _top2(vals):
    io = lax.broadcasted_iota(i32, vals.shape, 0)
    m1 = jnp.max(vals, axis=0, keepdims=True)
    i1 = jnp.min(jnp.where(vals == m1, io, EPG), axis=0, keepdims=True)
    v2 = jnp.where(io == i1, -jnp.inf, vals)
    m2 = jnp.max(v2, axis=0, keepdims=True)
    i2 = jnp.min(jnp.where(v2 == m2, io, EPG), axis=0, keepdims=True)
    return m1, i1, m2, i2


def _outproj_kernel(og_ref, on_ref, wo_ref, x_ref, mod_ref, nrm_ref, rw_ref, rb_ref,
                    x1_ref, h2_ref, eidx_ref, gw_ref):
    mix = (jnp.dot(og_ref[...], wo_ref[0:V_W, :], preferred_element_type=f32)
           + jnp.dot(on_ref[...], wo_ref[V_W:V_W + NA_W, :], preferred_element_type=f32))
    x1 = x_ref[...] + mod_ref[2] * mix
    x1_ref[...] = x1
    y = x1 * lax.rsqrt(jnp.mean(x1 * x1, axis=-1, keepdims=True) + EPS) * nrm_ref[...]
    h2 = y * (1.0 + mod_ref[4]) + mod_ref[3]
    h2_ref[...] = h2
    rw = rw_ref[...]
    rw_hi = rw.astype(bf16)
    rw_lo = (rw - rw_hi.astype(f32)).astype(bf16)
    h_hi = h2.astype(bf16)
    h_lo = (h2 - h_hi.astype(f32)).astype(bf16)
    logits = _nt(rw_hi, h_hi) + _nt(rw_hi, h_lo) + _nt(rw_lo, h_hi)
    scores = jax.nn.sigmoid(logits)
    sel = scores + rb_ref[...]
    tops = [_top2(sel[g * EPG:(g + 1) * EPG]) for g in range(N_GRP)]
    best = jnp.zeros_like(tops[0][1])
    bs = tops[0][0] + tops[0][2]
    for g in range(1, N_GRP):
        gs = tops[g][0] + tops[g][2]
        take = gs > bs
        best = jnp.where(take, g, best)
        bs = jnp.where(take, gs, bs)
    io = lax.broadcasted_iota(i32, (EPG, sel.shape[1]), 0)
    i1 = jnp.zeros_like(best)
    i2 = jnp.zeros_like(best)
    s1 = jnp.zeros(best.shape, f32)
    s2 = jnp.zeros(best.shape, f32)
    for g in range(N_GRP):
        sc = scores[g * EPG:(g + 1) * EPG]
        pick = best == g
        i1 = jnp.where(pick, tops[g][1], i1)
        i2 = jnp.where(pick, tops[g][3], i2)
        s1 = jnp.where(pick, jnp.sum(jnp.where(io == tops[g][1], sc, 0.0), axis=0, keepdims=True), s1)
        s2 = jnp.where(pick, jnp.sum(jnp.where(io == tops[g][3], sc, 0.0), axis=0, keepdims=True), s2)
    eidx_ref[0:1, :] = best * EPG + i1
    eidx_ref[1:2, :] = best * EPG + i2
    tot = s1 + s2
    gw_ref[0:1, :] = s1 / tot
    gw_ref[1:2, :] = s2 / tot


def _outproj(o_gla, o_na, w_out, x2d, mod_l, nrm, router_w, router_b, bsz, seq):
    n = x2d.shape[0]
    tm = TM_PROJ
    nt = seq // tm
    row = lambda b, i: (b * nt + i, 0)
    col = lambda b, i: (0, b * nt + i)
    const = lambda b, i: (0, 0)
    return pl.pallas_call(
        _outproj_kernel,
        grid=(bsz, nt),
        in_specs=[pl.BlockSpec((tm, V_W), row),
                  pl.BlockSpec((tm, NA_W), row),
                  pl.BlockSpec((V_W + NA_W, D), const),
                  pl.BlockSpec((tm, D), row),
                  pl.BlockSpec((6, None, 1, D), lambda b, i: (0, b, 0, 0)),
                  pl.BlockSpec((1, D), const),
                  pl.BlockSpec((N_EXP, D), const),
                  pl.BlockSpec((N_EXP, 1), const)],
        out_specs=[pl.BlockSpec((tm, D), row), pl.BlockSpec((tm, D), row),
                   pl.BlockSpec((2, tm), col), pl.BlockSpec((2, tm), col)],
        out_shape=[jax.ShapeDtypeStruct((n, D), f32), jax.ShapeDtypeStruct((n, D), f32),
                   jax.ShapeDtypeStruct((2, n), i32), jax.ShapeDtypeStruct((2, n), f32)],
        compiler_params=pltpu.CompilerParams(dimension_semantics=("arbitrary", "arbitrary")),
        name="outproj_router",
    )(o_gla, o_na, w_out.astype(bf16), x2d, mod_l, nrm.reshape(1, D),
      router_w.T, router_b.reshape(N_EXP, 1))


def _rank_kernel(eidx_ref, tri_ref, rank_ref, cnt_ref, carry):
    @pl.when(pl.program_id(0) == 0)
    def _():
        carry[...] = jnp.zeros_like(carry)

    e = eidx_ref[...]
    tm = e.shape[1]
    io = lax.broadcasted_iota(i32, (N_EXP, tm), 0)
    run = carry[...]
    for k in range(2):
        oh = io == e[k:k + 1, :]
        ohf = oh.astype(f32)
        pre = jnp.dot(ohf.astype(bf16), tri_ref[...], preferred_element_type=f32) + run[:, 0:1]
        rank_ref[k:k + 1, :] = jnp.sum(jnp.where(oh, pre, 0.0), axis=0, keepdims=True).astype(i32)
        run = run + jnp.sum(ohf, axis=1, keepdims=True)
    carry[...] = run
    cnt_ref[...] = run


def _ranks(eidx):
    n = eidx.shape[1]
    tm = TM_RANK
    t = np.arange(tm)
    tri = jnp.asarray(t[:, None] < t[None, :], bf16)
    return pl.pallas_call(
        _rank_kernel,
        grid=(n // tm,),
        in_specs=[pl.BlockSpec((2, tm), lambda i: (0, i)), pl.BlockSpec((tm, tm), lambda i: (0, 0))],
        out_specs=[pl.BlockSpec((2, tm), lambda i: (0, i)), pl.BlockSpec((N_EXP, 128), lambda i: (0, 0))],
        out_shape=[jax.ShapeDtypeStruct((2, n), i32), jax.ShapeDtypeStruct((N_EXP, 128), f32)],
        scratch_shapes=[pltpu.VMEM((N_EXP, 128), f32)],
        compiler_params=pltpu.CompilerParams(dimension_semantics=("arbitrary",)),
        name="moe_rank",
    )(eidx, tri)


def _dispatch_kernel(dest_ref, tails_ref, h_ref, xb_ref, zbuf, hbuf, lsem, rsem, zsem):
    i = pl.program_id(0)
    nsteps = pl.num_programs(0)
    tm = TM_DISP
    n = nsteps * tm
    nbuf = hbuf.shape[0]

    def load(j):
        start = pl.multiple_of(j * tm, tm)
        return pltpu.make_async_copy(h_ref.at[pl.ds(start, tm)], hbuf.at[j % nbuf], lsem.at[j % nbuf])

    def drain_rows(j):
        for _ in range(2):
            pltpu.make_async_copy(hbuf.at[j % nbuf], xb_ref.at[pl.ds(0, tm)], rsem.at[j % nbuf]).wait()

    def tail_copy(e):
        start = pl.multiple_of(tails_ref[e], MOE_BLK)
        return pltpu.make_async_copy(zbuf, xb_ref.at[pl.ds(start, MOE_BLK)], zsem)

    @pl.when(i == 0)
    def _():
        zbuf[...] = jnp.zeros_like(zbuf)
        for e in range(2 * N_EXP):
            @pl.when(tails_ref[e] >= 0)
            def _():
                tail_copy(e).start()
        for e in range(2 * N_EXP):
            @pl.when(tails_ref[e] >= 0)
            def _():
                tail_copy(e).wait()
        load(i).start()

    @pl.when(i >= nbuf - 1)
    def _():
        drain_rows(i + 1 - nbuf)

    @pl.when(i + 1 < nsteps)
    def _():
        load(i + 1).start()

    load(i).wait()
    slot = i % nbuf

    def issue(t, carry):
        src = hbuf.at[slot, pl.ds(t, 1)]
        for k in range(2):
            d = dest_ref[k * n + i * tm + t]
            pltpu.make_async_copy(src, xb_ref.at[pl.ds(d, 1)], rsem.at[slot]).start()
        return carry

    lax.fori_loop(0, tm, issue, 0, unroll=True)

    @pl.when(i == nsteps - 1)
    def _():
        for back in range(nbuf - 2, -1, -1):
            drain_rows(i - back)


def _dispatch(h2, dest_flat, tails, p_rows):
    n = h2.shape[0]
    tm = TM_DISP
    return pl.pallas_call(
        _dispatch_kernel,
        grid_spec=pltpu.PrefetchScalarGridSpec(
            num_scalar_prefetch=2,
            grid=(n // tm,),
            in_specs=[pl.BlockSpec(memory_space=pl.ANY)],
            out_specs=pl.BlockSpec(memory_space=pl.ANY),
            scratch_shapes=[pltpu.VMEM((MOE_BLK, D), f32), pltpu.VMEM((3, tm, D), f32),
                            pltpu.SemaphoreType.DMA((3,)), pltpu.SemaphoreType.DMA((3,)),
                            pltpu.SemaphoreType.DMA(())],
        ),
        out_shape=jax.ShapeDtypeStruct((p_rows, D), f32),
        compiler_params=pltpu.CompilerParams(dimension_semantics=("arbitrary",)),
        name="moe_dispatch",
    )(dest_flat, tails, h2)


def _expert_kernel(blk_e_ref, nb_ref, xb_ref, w1_ref, w3_ref, w2_ref, y_ref, w1b, w3b, w2b):
    i = pl.program_id(0)

    @pl.when((i < nb_ref[0]) & ((i == 0) | (blk_e_ref[i] != blk_e_ref[jnp.maximum(i - 1, 0)])))
    def _():
        w1b[...] = w1_ref[...].astype(bf16)
        w3b[...] = w3_ref[...].astype(bf16)
        w2b[...] = w2_ref[...].astype(bf16)

    @pl.when(i < nb_ref[0])
    def _():
        x = xb_ref[...].astype(bf16)
        a = jnp.dot(x, w1b[...], preferred_element_type=f32)
        b = jnp.dot(x, w3b[...], preferred_element_type=f32)
        h = (a * jax.nn.sigmoid(a) * b).astype(bf16)
        y_ref[...] = jnp.dot(h, w2b[...], preferred_element_type=f32)

    @pl.when(i >= nb_ref[0])
    def _():
        y_ref[...] = jnp.zeros_like(y_ref)


def _experts(xb, blk_e, nb_used, w1, w3, w2, layer):
    p_rows = xb.shape[0]
    nb = p_rows // MOE_BLK

    def xmap(i, be, nbu):
        return (jnp.minimum(i, nbu[0] - 1), 0)

    def wmap(i, be, nbu):
        return (layer, be[i], 0, 0)

    return pl.pallas_call(
        _expert_kernel,
        grid_spec=pltpu.PrefetchScalarGridSpec(
            num_scalar_prefetch=2,
            grid=(nb,),
            in_specs=[pl.BlockSpec((MOE_BLK, D), xmap),
                      pl.BlockSpec((None, None, D, D_EXP), wmap),
                      pl.BlockSpec((None, None, D, D_EXP), wmap),
                      pl.BlockSpec((None, None, D_EXP, D), wmap)],
            out_specs=pl.BlockSpec((MOE_BLK, D), lambda i, be, nbu: (i, 0)),
            scratch_shapes=[pltpu.VMEM((D, D_EXP), bf16), pltpu.VMEM((D, D_EXP), bf16),
                            pltpu.VMEM((D_EXP, D), bf16)],
        ),
        out_shape=jax.ShapeDtypeStruct((p_rows, D), f32),
        compiler_params=pltpu.CompilerParams(dimension_semantics=("arbitrary",),
                                             vmem_limit_bytes=48 * 1024 * 1024),
        name="moe_experts",
    )(blk_e, nb_used, xb, w1, w3, w2)


def _combine_kernel(dest_ref, x1_ref, gw_ref, mod_ref, y_ref, o_ref, ybuf, sems):
    i = pl.program_id(0)
    nsteps = pl.num_programs(0)
    tm = x1_ref.shape[0]
    n = nsteps * tm

    def issue_tile(j):
        slot = j % 2

        def issue(t, carry):
            for k in range(2):
                d = dest_ref[k * n + j * tm + t]
                pltpu.make_async_copy(y_ref.at[pl.ds(d, 1)], ybuf.at[slot, k, pl.ds(t, 1)],
                                      sems.at[slot]).start()
            return carry

        lax.fori_loop(0, tm, issue, 0, unroll=True)

    @pl.when(i == 0)
    def _():
        issue_tile(i)

    @pl.when(i + 1 < nsteps)
    def _():
        issue_tile(i + 1)

    slot = i % 2
    for k in range(2):
        pltpu.make_async_copy(y_ref.at[pl.ds(0, tm)], ybuf.at[slot, k], sems.at[slot]).wait()
    gw = gw_ref[...]
    y = ybuf[slot, 0] * gw[:, 0:1] + ybuf[slot, 1] * gw[:, 1:2]
    o_ref[...] = x1_ref[...] + mod_ref[5] * y


def _combine(y, dest_flat, x1, gw_t, mod_l, bsz, seq):
    n = x1.shape[0]
    tm = TM_COMB
    nt = seq // tm
    return pl.pallas_call(
        _combine_kernel,
        grid_spec=pltpu.PrefetchScalarGridSpec(
            num_scalar_prefetch=1,
            grid=(n // tm,),
            in_specs=[pl.BlockSpec((tm, D), lambda i, *_: (i, 0)),
                      pl.BlockSpec((tm, 2), lambda i, *_: (i, 0)),
                      pl.BlockSpec((6, None, 1, D), lambda i, *_: (0, i // nt, 0, 0)),
                      pl.BlockSpec(memory_space=pl.ANY)],
            out_specs=pl.BlockSpec((tm, D), lambda i, *_: (i, 0)),
            scratch_shapes=[pltpu.VMEM((2, 2, tm, D), f32), pltpu.SemaphoreType.DMA((2,))],
        ),
        out_shape=jax.ShapeDtypeStruct((n, D), f32),
        compiler_params=pltpu.CompilerParams(dimension_semantics=("arbitrary",)),
        name="moe_combine",
    )(dest_flat, x1, gw_t, mod_l, y)


def _grouped_moe(h2, eidx, gw, x1, mod_l, w1, w3, w2, layer, bsz, seq):
    n = h2.shape[0]
    p_rows = (n * 2 // MOE_BLK + N_EXP) * MOE_BLK
    nb = p_rows // MOE_BLK
    rank, cnt = _ranks(eidx)
    counts = cnt[:, 0].astype(i32)
    padded = (counts + MOE_BLK - 1) // MOE_BLK * MOE_BLK
    pad_end = jnp.cumsum(padded)
    pad_start = pad_end - padded
    start_of = jnp.sum(jnp.where(eidx[..., None] == jnp.arange(N_EXP, dtype=i32), pad_start, 0), axis=-1)
    dest_flat = (start_of + rank).reshape(-1)
    nb_used = (pad_end[-1] // MOE_BLK).astype(i32)
    spare = nb_used + jnp.arange(N_EXP, dtype=i32)
    tails = jnp.concatenate([jnp.where(padded > 0, pad_end - MOE_BLK, -1),
                             jnp.where(spare < nb, spare * MOE_BLK, -1)]).astype(i32)
    blk = jnp.minimum(jnp.arange(nb, dtype=i32), nb_used - 1)
    seg_done = (pad_end[None, :] <= (blk * MOE_BLK)[:, None]).astype(i32)
    blk_e = jnp.minimum(jnp.sum(seg_done, axis=1), N_EXP - 1).astype(i32)
    xb = _dispatch(h2, dest_flat, tails, p_rows)
    y = _experts(xb, blk_e, nb_used.reshape(1), w1, w3, w2, layer)
    return _combine(y, dest_flat, x1, gw.T, mod_l, bsz, seq)


def kernel(x, c, w_ada, b_ada, attn_norm, ffn_norm, w_in, gla_gate_w, gla_gate_b, gla_out_norm,
           na_q_norm, na_k_norm, na_rpb, w_out, router_w, router_b, w1, w3, w2):
    bsz, seq, _ = x.shape
    depth = w_ada.shape[0]
    mod = _adaln_mod(c, w_ada, b_ada)
    xc = x.reshape(bsz * seq, D)
    for l in range(depth):
        (qg, kg, vg, sg, la_f, la_b, qn, kn, vn), bounded_f, bounded_b = _inproj(
            xc, mod[l], attn_norm[l], w_in[l], gla_gate_w[l], gla_gate_b[l], bsz, seq)
        o_f = _gla_dir(False, qg, kg, vg, la_f, bounded_f, bsz, seq)
        o_gla = _gla_dir(True, qg, kg, vg, la_b, bounded_b, bsz, seq, extra=(o_f, sg, gla_out_norm[l]))
        o_na = _natten(qn, kn, vn, na_q_norm[l], na_k_norm[l], na_rpb[l], bsz, seq)
        x1, h2, eidx, gw = _outproj(o_gla, o_na, w_out[l], xc, mod[l], ffn_norm[l],
                                    router_w, router_b, bsz, seq)
        xc = _grouped_moe(h2, eidx, gw, x1, mod[l], w1, w3, w2, l, bsz, seq)
    return xc.reshape(bsz, seq, D)
```

```python
import functools

import numpy as np
import jax
import jax.numpy as jnp
from jax import lax
from jax.experimental import pallas as pl
from jax.experimental.pallas import tpu as pltpu

f32 = jnp.float32
bf16 = jnp.bfloat16
i32 = jnp.int32

D = 1024
GRID_W = 64
GLA_H, GLA_DK, GLA_DV = 4, 64, 128
GLA_RANK = 16
GLA_TAU = 16.0
CHUNK = 64
NA_H, NA_DH = 8, 64
WIN_H, WIN_W = 8, 16
N_EXP, N_GRP, EPG = 32, 4, 8
D_EXP = 512
QK_W = GLA_H * GLA_DK
V_W = GLA_H * GLA_DV
NA_W = NA_H * NA_DH
EPS = 1e-6
NEG = -1e30

TM_PROJ = 1024
TB_GLA = 256
NA_QROWS = 4
NA_KROWS = 12
NA_SUB = 4
TM_RANK = 1024
TM_DISP = 256
TM_COMB = 256
MOE_BLK = 512
N_LEVELS = 6
GLA_SAFE_RANGE = 40.0


def _nt(a, b):
    return lax.dot_general(a, b, (((1,), (1,)), ((), ())), preferred_element_type=f32)


def _tn(a, b):
    return lax.dot_general(a, b, (((0,), (0,)), ((), ())), preferred_element_type=f32)


def _split3(x):
    hi = x.astype(bf16)
    r = x - hi.astype(f32)
    mid = r.astype(bf16)
    lo = (r - mid.astype(f32)).astype(bf16)
    return hi, mid, lo


def _sel_l(m01, x):
    hi, mid, lo = _split3(x)
    d = lambda p: jnp.dot(m01, p, preferred_element_type=f32)
    return d(hi) + d(mid) + d(lo)


def _sel_r(x, m01):
    hi, mid, lo = _split3(x)
    d = lambda p: jnp.dot(p, m01, preferred_element_type=f32)
    return d(hi) + d(mid) + d(lo)


def _mod_kernel(c_ref, w_ref, b_ref, o_ref):
    c = c_ref[...]
    ca = c * jax.nn.sigmoid(c)
    w = w_ref[0]
    c_hi = ca.astype(bf16)
    c_lo = (ca - c_hi.astype(f32)).astype(bf16)
    w_hi = w.astype(bf16)
    w_lo = (w - w_hi.astype(f32)).astype(bf16)
    acc = (jnp.dot(c_hi, w_hi, preferred_element_type=f32) + jnp.dot(c_lo, w_hi, preferred_element_type=f32)
           + jnp.dot(c_hi, w_lo, preferred_element_type=f32))
    o_ref[0, 0] = acc + b_ref[0, 0]


def _adaln_mod(c, w_ada, b_ada):
    depth = w_ada.shape[0]
    bsz = c.shape[0]
    cp = jnp.zeros((8, D), f32).at[:bsz].set(c)
    out = pl.pallas_call(
        _mod_kernel,
        grid=(depth, 6),
        in_specs=[pl.BlockSpec((8, D), lambda l, j: (0, 0)),
                  pl.BlockSpec((1, D, D), lambda l, j: (l, 0, j)),
                  pl.BlockSpec((1, 1, 1, D), lambda l, j: (l, j, 0, 0))],
        out_specs=pl.BlockSpec((1, 1, 8, D), lambda l, j: (l, j, 0, 0)),
        out_shape=jax.ShapeDtypeStruct((depth, 6, 8, D), f32),
        name="adaln_mod",
    )(cp, w_ada, b_ada.reshape(depth, 6, 1, D))
    return out[:, :, :bsz].reshape(depth, 6, bsz, 1, D)


def _log_sigmoid(z):
    return jnp.minimum(z, 0.0) - jnp.log1p(jnp.exp(-jnp.abs(z)))


def _inproj_kernel(x_ref, mod_ref, nrm_ref, wm_ref, wlr_ref, gf_ref, bf_ref,
                   qg_ref, kg_ref, vg_ref, sg_ref, laf_ref, lab_ref, qn_ref, kn_ref, vn_ref, lamin_ref):
    x = x_ref[...]
    y = x * lax.rsqrt(jnp.mean(x * x, axis=-1, keepdims=True) + EPS) * nrm_ref[...]
    h = y * (1.0 + mod_ref[1]) + mod_ref[0]
    hb = h.astype(bf16)
    proj = lambda lo, hi: jnp.dot(hb, wm_ref[:, lo:hi], preferred_element_type=f32)
    qg_ref[...] = (proj(0, 256) * (GLA_DK ** -0.5)).astype(bf16)
    kg_ref[...] = proj(256, 512).astype(bf16)
    vg_ref[...] = proj(512, 1024).astype(bf16)
    gg = proj(1024, 1536)
    sg_ref[...] = (gg * jax.nn.sigmoid(gg)).astype(bf16)
    qn_ref[...] = proj(1536, 2048).astype(bf16)
    kn_ref[...] = proj(2048, 2560).astype(bf16)
    vn_ref[...] = proj(2560, 3072).astype(bf16)
    lr = jnp.dot(hb, wlr_ref[...], preferred_element_type=f32)
    z = jnp.dot(lr, gf_ref[...], precision=lax.Precision.HIGHEST, preferred_element_type=f32) + bf_ref[...]
    la = _log_sigmoid(z) * (1.0 / GLA_TAU)
    la_f = la[:, 0:QK_W]
    la_b = la[:, QK_W:2 * QK_W]
    laf_ref[...] = la_f
    lab_ref[...] = la_b
    rows = []
    for la in (la_f, la_b):
        for blk in range(la.shape[0] // TB_GLA):
            tot = jnp.sum(la[blk * TB_GLA:(blk + 1) * TB_GLA], axis=0, keepdims=True)
            rows.append(jnp.broadcast_to(jnp.min(tot, axis=1, keepdims=True), (1, 128)))
    if len(rows) < 8:
        rows.append(jnp.zeros((8 - len(rows), 128), f32))
    lamin_ref[...] = jnp.concatenate(rows, axis=0)


def _inproj(x2d, mod_l, nrm, w_in, gate_w, gate_b, bsz, seq):
    n = x2d.shape[0]
    tm = TM_PROJ
    nt = seq // tm
    lo, hi = 2 * QK_W + 2 * V_W, 2 * QK_W + 2 * V_W + 2 * GLA_RANK
    wm = jnp.concatenate([w_in[:, :lo], w_in[:, hi:]], axis=1).astype(bf16)
    wlr = w_in[:, lo:hi].astype(bf16)
    zpad = jnp.zeros((GLA_RANK, QK_W), f32)
    gmat = jnp.concatenate([jnp.concatenate([gate_w[0], zpad], axis=1),
                            jnp.concatenate([zpad, gate_w[1]], axis=1)], axis=0)
    gbias = jnp.concatenate([gate_b[0], gate_b[1]]).reshape(1, 2 * QK_W)
    row = lambda b, i: (b * nt + i, 0)
    const = lambda b, i: (0, 0)
    widths = [(QK_W, bf16), (QK_W, bf16), (V_W, bf16), (V_W, bf16), (QK_W, f32), (QK_W, f32),
              (NA_W, bf16), (NA_W, bf16), (NA_W, bf16)]
    outs = pl.pallas_call(
        _inproj_kernel,
        grid=(bsz, nt),
        in_specs=[pl.BlockSpec((tm, D), row),
                  pl.BlockSpec((6, None, 1, D), lambda b, i: (0, b, 0, 0)),
                  pl.BlockSpec((1, D), const),
                  pl.BlockSpec(wm.shape, const),
                  pl.BlockSpec(wlr.shape, const),
                  pl.BlockSpec(gmat.shape, const),
                  pl.BlockSpec(gbias.shape, const)],
        out_specs=[pl.BlockSpec((tm, w), row) for w, _ in widths]
        + [pl.BlockSpec((None, 8, 128), lambda b, i: (b * nt + i, 0, 0))],
        out_shape=[jax.ShapeDtypeStruct((n, w), dt) for w, dt in widths]
        + [jax.ShapeDtypeStruct((n // tm, 8, 128), f32)],
        compiler_params=pltpu.CompilerParams(dimension_semantics=("arbitrary", "arbitrary")),
        name="inproj",
    )(x2d, mod_l, nrm.reshape(1, D), wm, wlr, gmat, gbias)
    per = tm // TB_GLA
    lamin = outs[-1][:, :2 * per, 0]
    bounded_f = (lamin[:, :per].reshape(-1) > -GLA_SAFE_RANGE).astype(i32)
    bounded_b = (lamin[:, per:].reshape(-1) > -GLA_SAFE_RANGE).astype(i32)
    return outs[:-1], bounded_f, bounded_b


def _gla_consts(rev, tb):
    c = CHUNK
    idx = np.arange(c)
    if not rev:
        cum = (idx[None, :] <= idx[:, None]).astype(np.float32)
    else:
        cum = (idx[None, :] >= idx[:, None]).astype(np.float32)
    mats = [cum]
    masks = []
    for lvl in range(N_LEVELS):
        s = c >> (lvl + 1)
        blk = idx // (2 * s)
        second = (idx % (2 * s)) >= s
        ref_row = blk * 2 * s + (s if rev else s - 1)
        mats.append(cum[ref_row])
        same = blk[:, None] == blk[None, :]
        if not rev:
            m = same & second[:, None] & (~second[None, :])
        else:
            m = same & (~second[:, None]) & second[None, :]
        masks.append(m)
    masks.append(np.eye(c, dtype=bool) & (not rev))
    mst = np.concatenate(mats, axis=0)
    msk = np.stack([np.tile(m, (1, GLA_H)) for m in masks]).astype(np.float32)
    tix = np.arange(tb)
    tri = (tix[None, :] >= tix[:, None]) if rev else (tix[None, :] <= tix[:, None])
    cmask = (tix[:, None] < tix[None, :]) if rev else (tix[:, None] >= tix[None, :])
    nch = tb // c
    selb = np.zeros((tb, nch * 128), np.float32)
    for ch in range(nch):
        selb[ch * c:(ch + 1) * c, ch * 128:(ch + 1) * 128] = 1.0
    hk = np.arange(QK_W) // GLA_DK
    hv = np.arange(V_W) // GLA_DV
    kmask = (hk[:, None] == hk[None, :]).astype(np.float32)
    vmask = (hk[:, None] == hv[None, :]).astype(np.float32)
    return (jnp.asarray(mst, bf16), jnp.asarray(msk, f32), jnp.asarray(selb, bf16),
            jnp.asarray(kmask, bf16), jnp.asarray(vmask, bf16), jnp.asarray(vmask, f32),
            jnp.asarray(tri, bf16), jnp.asarray(cmask, f32), jnp.ones((tb, 128), bf16))


def _gla_kernel(rev, tb, *refs):
    if rev:
        (bounded_ref, q_ref, k_ref, v_ref, la_ref, mst_ref, msk_ref, selb_ref, kmask_ref, vmask_ref,
         smask_ref, tri_ref, cmask_ref, ones_ref, of_ref, sg_ref, gn_ref, out_ref, s_scr, o_scr) = refs
    else:
        (bounded_ref, q_ref, k_ref, v_ref, la_ref, mst_ref, msk_ref, selb_ref, kmask_ref, vmask_ref,
         smask_ref, tri_ref, cmask_ref, ones_ref, out_ref, s_scr) = refs
        o_scr = out_ref
    c = CHUNK
    nch = tb // c

    @pl.when(pl.program_id(1) == 0)
    def _():
        s_scr[...] = jnp.zeros_like(s_scr)

    def block_bounded():
        la = la_ref[...]
        tot_blk = _sel_r(la.T, ones_ref[...])
        la_hi = la.astype(bf16)
        la_lo = (la - la_hi.astype(f32)).astype(bf16)
        tri = tri_ref[...]
        bcum = (jnp.dot(tri, la_hi, preferred_element_type=f32)
                + jnp.dot(tri, la_lo, preferred_element_type=f32))
        q = q_ref[...].astype(f32)
        k = k_ref[...].astype(f32)
        v = v_ref[...]
        qe = (q * jnp.exp(bcum)).astype(bf16)
        kx = (k * jnp.exp(-bcum)).astype(bf16)
        s_prev = s_scr[...]
        o_inter = jnp.dot(qe, s_prev.astype(bf16), preferred_element_type=f32)
        lane_head = lax.broadcasted_iota(i32, qe.shape, 1) // GLA_DK
        cmask = cmask_ref[...]
        for hd in range(GLA_H):
            cols = slice(hd * GLA_DV, (hd + 1) * GLA_DV)
            qh = jnp.where(lane_head == hd, qe, jnp.zeros_like(qe))
            ph = jnp.where(cmask > 0.0, _nt(qh, kx), 0.0).astype(bf16)
            o_scr[:, cols] = o_inter[:, cols] + jnp.dot(ph, v[:, cols], preferred_element_type=f32)
        blast = bcum[0:1] if rev else bcum[tb - 1:tb]
        ke = (k * jnp.exp(blast - bcum)).astype(bf16)
        dec = jnp.exp(tot_blk)
        s_scr[...] = s_prev * jnp.concatenate([dec] * GLA_H, axis=1) + _tn(ke, v) * smask_ref[...]

    def chunks_robust():
        tot = _sel_r(la_ref[...].T, selb_ref[...])
        mst = mst_ref[...]
        kmask = kmask_ref[...]
        vmask = vmask_ref[...]
        for ch in (range(nch - 1, -1, -1) if rev else range(nch)):
            rows = slice(ch * c, (ch + 1) * c)
            q = q_ref[rows, :].astype(f32)
            k = k_ref[rows, :].astype(f32)
            v = v_ref[rows, :]
            r_all = _sel_l(mst, la_ref[rows, :])
            bcum = r_all[0:c]
            qe = (q * jnp.exp(bcum)).astype(bf16)
            o = jnp.dot(qe, s_scr[...].astype(bf16), preferred_element_type=f32)
            sall = jnp.zeros((c, QK_W), f32)
            for lvl in range(N_LEVELS):
                ref = r_all[(lvl + 1) * c:(lvl + 2) * c]
                a = (q * jnp.exp(jnp.minimum(bcum - ref, 0.0))).astype(bf16)
                bm = (k * jnp.exp(jnp.minimum(ref - bcum, 0.0))).astype(bf16)
                bbd = jnp.concatenate([bm] * GLA_H, axis=0) * kmask
                sall = sall + jnp.where(msk_ref[lvl] > 0.0, _nt(a, bbd), 0.0)
            if not rev:
                bbd = jnp.concatenate([k_ref[rows, :]] * GLA_H, axis=0) * kmask
                sall = sall + jnp.where(msk_ref[N_LEVELS] > 0.0, _nt(q_ref[rows, :], bbd), 0.0)
            vbd = jnp.concatenate([v] * GLA_H, axis=0) * vmask
            o_scr[rows, :] = o + jnp.dot(sall.astype(bf16), vbd, preferred_element_type=f32)
            blast = bcum[0:1] if rev else bcum[c - 1:c]
            ke = (k * jnp.exp(blast - bcum)).astype(bf16)
            dec = jnp.exp(tot[:, ch * 128:(ch + 1) * 128])
            s_scr[...] = (s_scr[...] * jnp.concatenate([dec] * GLA_H, axis=1)
                          + _tn(ke, v) * smask_ref[...])

    nb = pl.num_programs(1)
    blk = pl.program_id(0) * nb + (nb - 1 - pl.program_id(1) if rev else pl.program_id(1))
    bounded = bounded_ref[blk] > 0
    pl.when(bounded)(block_bounded)
    pl.when(jnp.logical_not(bounded))(chunks_robust)

    if rev:
        gn = gn_ref[...]
        for hd in range(GLA_H):
            cols = slice(hd * GLA_DV, (hd + 1) * GLA_DV)
            t = of_ref[:, cols] + o_scr[:, cols]
            y = t * lax.rsqrt(jnp.mean(t * t, axis=-1, keepdims=True) + EPS) * gn
            out_ref[:, cols] = (y * sg_ref[:, cols].astype(f32)).astype(bf16)


def _gla_dir(rev, q, k, v, la, bounded, bsz, seq, extra=None):
    n = q.shape[0]
    tb = TB_GLA
    nb = seq // tb
    consts = _gla_consts(rev, tb)
    if rev:
        row = lambda b, i, flags: (b * nb + nb - 1 - i, 0)
    else:
        row = lambda b, i, flags: (b * nb + i, 0)
    const2 = lambda b, i, flags: (0, 0)
    const3 = lambda b, i, flags: (0, 0, 0)
    in_specs = [pl.BlockSpec((tb, QK_W), row), pl.BlockSpec((tb, QK_W), row),
                pl.BlockSpec((tb, V_W), row), pl.BlockSpec((tb, QK_W), row),
                pl.BlockSpec(consts[0].shape, const2), pl.BlockSpec(consts[1].shape, const3),
                pl.BlockSpec(consts[2].shape, const2), pl.BlockSpec(consts[3].shape, const2),
                pl.BlockSpec(consts[4].shape, const2), pl.BlockSpec(consts[5].shape, const2),
                pl.BlockSpec(consts[6].shape, const2), pl.BlockSpec(consts[7].shape, const2),
                pl.BlockSpec(consts[8].shape, const2)]
    args = [q, k, v, la, *consts]
    scratch = [pltpu.VMEM((QK_W, V_W), f32)]
    if rev:
        o_f, sg, gn = extra
        in_specs += [pl.BlockSpec((tb, V_W), row), pl.BlockSpec((tb, V_W), row),
                     pl.BlockSpec((1, GLA_DV), const2)]
        args += [o_f, sg, gn.reshape(1, GLA_DV)]
        scratch.append(pltpu.VMEM((tb, V_W), f32))
        out_dtype = bf16
    else:
        out_dtype = f32
    return pl.pallas_call(
        functools.partial(_gla_kernel, rev, tb),
        grid_spec=pltpu.PrefetchScalarGridSpec(
            num_scalar_prefetch=1,
            grid=(bsz, nb),
            in_specs=in_specs,
            out_specs=pl.BlockSpec((tb, V_W), row),
            scratch_shapes=scratch,
        ),
        out_shape=jax.ShapeDtypeStruct((n, V_W), out_dtype),
        compiler_params=pltpu.CompilerParams(dimension_semantics=("arbitrary", "arbitrary")),
        name="gla_bwd" if rev else "gla_fwd",
    )(bounded, *args)


def _rpb_expand_kernel(rpb_ref, sel_ref, valid_ref, o_ref):
    e = _sel_r(rpb_ref[...], sel_ref[...])
    o_ref[...] = jnp.where(valid_ref[...] > 0.0, e, NEG)


def _na_bias_table(rpb):
    nri, nci = 2 * WIN_H - 1, 2 * WIN_W - 1
    qc = np.arange(GRID_W)
    col_start = np.clip(qc - WIN_W // 2, 0, GRID_W - WIN_W)
    kc = np.arange(GRID_W)
    valid = (kc[None, :] >= col_start[:, None]) & (kc[None, :] < col_start[:, None] + WIN_W)
    ci = kc[None, :] - qc[:, None] + WIN_W - 1
    sel = np.zeros((32, GRID_W * GRID_W), np.float32)
    flat_ci = ci.reshape(-1)
    ok = (flat_ci >= 0) & (flat_ci < nci)
    sel[flat_ci[ok], np.nonzero(ok)[0]] = 1.0
    rpb2 = jnp.zeros((NA_H * nri + 8 - (NA_H * nri) % 8, 32), f32).at[:NA_H * nri, :nci].set(
        rpb.reshape(NA_H * nri, nci))
    nrow = rpb2.shape[0]
    e = pl.pallas_call(
        _rpb_expand_kernel,
        out_shape=jax.ShapeDtypeStruct((nrow, GRID_W * GRID_W), f32),
        name="rpb_expand",
    )(rpb2, jnp.asarray(sel, bf16), jnp.asarray(valid.reshape(1, -1), f32))
    return e[:NA_H * nri].reshape(NA_H // 2, 2, nri, GRID_W, GRID_W)


def _na_row_classes():
    rows = GRID_W
    ri = -np.ones((3, NA_QROWS, NA_KROWS), np.int64)
    for cls, rb in enumerate((0, 1, rows // NA_QROWS - 1)):
        r0 = rb * NA_QROWS
        kb = int(np.clip(r0 - WIN_H // 2, 0, rows - NA_KROWS))
        for rq in range(NA_QROWS):
            r = r0 + rq
            rs = int(np.clip(r - WIN_H // 2, 0, rows - WIN_H))
            for m in range(NA_KROWS):
                kr = kb + m
                if rs <= kr < rs + WIN_H:
                    ri[cls, rq, m] = kr - r + WIN_H - 1
    return ri


def _natten_kernel(q_ref, k_ref, v_ref, qn_ref, kn_ref, gsum_ref, e_ref, o_ref, kn_scr, tbl_scr):
    b = pl.program_id(1)
    rb = pl.program_id(2)
    gsum = gsum_ref[...]
    log2e = 1.4426950408889634

    def headnorm(x, g):
        ms = _sel_r(x * x, gsum) * (1.0 / NA_DH)
        return x * lax.rsqrt(ms + EPS) * g

    @pl.when((b == 0) & (rb == 0))
    def _():
        ri = _na_row_classes()
        neg = jnp.full((GRID_W, GRID_W), NEG, f32)
        for hh in range(2):
            for cls in range(3):
                for rq in range(NA_QROWS):
                    for mp in range(NA_KROWS // 2):
                        parts = []
                        for m in (2 * mp, 2 * mp + 1):
                            r = int(ri[cls, rq, m])
                            parts.append(neg if r < 0 else e_ref[hh, r] * log2e)
                        r0 = (hh * NA_QROWS + rq) * GRID_W
                        tbl_scr[cls, r0:r0 + GRID_W, mp * 128:(mp + 1) * 128] = (
                            jnp.concatenate(parts, axis=1))

    @pl.when(rb == 0)
    def _():
        kn_scr[...] = headnorm(k_ref[...].astype(f32), kn_ref[...]).astype(bf16)

    nq = NA_QROWS * GRID_W
    nk = NA_KROWS * GRID_W
    nblk = GRID_W // NA_QROWS
    first = lax.broadcasted_iota(i32, (nq, 128), 1) < NA_DH
    qs, wins, tbls = [], [], []
    for sub in range(NA_SUB):
        qb = rb * NA_SUB + sub
        q = headnorm(q_ref[sub * nq:(sub + 1) * nq, :].astype(f32), qn_ref[...]) * (NA_DH ** -0.5 * log2e)
        kb = jnp.clip(qb * NA_QROWS - WIN_H // 2, 0, GRID_W - NA_KROWS)
        start = pl.multiple_of(kb * GRID_W, GRID_W)
        cls = jnp.where(qb == 0, 0, jnp.where(qb == nblk - 1, 2, 1))
        qs.append([jnp.where(first, q, 0.0).astype(bf16), jnp.where(first, 0.0, q).astype(bf16)])
        wins.append((kn_scr[pl.ds(start, nk), :], v_ref[pl.ds(start, nk), :]))
        tbls.append(cls)
    scores = [[_nt(qs[sub][hh], wins[sub][0]) + tbl_scr[tbls[sub], hh * nq:(hh + 1) * nq, :]
               for hh in range(2)] for sub in range(NA_SUB)]
    for sub in range(NA_SUB):
        outs = []
        for s in scores[sub]:
            m = jnp.max(s, axis=-1, keepdims=True)
            p = jnp.exp2(s - m)
            l = jnp.sum(p, axis=-1, keepdims=True)
            outs.append(jnp.dot(p.astype(bf16), wins[sub][1], preferred_element_type=f32) / l)
        o_ref[sub * nq:(sub + 1) * nq, :] = jnp.where(first, outs[0], outs[1]).astype(bf16)


def _natten(qn, kn, vn, q_norm, k_norm, rpb, bsz, seq):
    n = qn.shape[0]
    e5 = _na_bias_table(rpb)
    nq = NA_QROWS * GRID_W
    nqs = NA_SUB * nq
    nrb = seq // nqs
    lane_h = np.arange(128) // NA_DH
    gsum = jnp.asarray(lane_h[:, None] == lane_h[None, :], bf16)
    qn2 = jnp.tile(q_norm.reshape(1, NA_DH), (1, 2))
    kn2 = jnp.tile(k_norm.reshape(1, NA_DH), (1, 2))
    const2 = lambda p, b, r: (0, 0)
    return pl.pallas_call(
        _natten_kernel,
        grid=(NA_H // 2, bsz, nrb),
        in_specs=[pl.BlockSpec((nqs, 128), lambda p, b, r: (b * nrb + r, p)),
                  pl.BlockSpec((seq, 128), lambda p, b, r: (b, p)),
                  pl.BlockSpec((seq, 128), lambda p, b, r: (b, p)),
                  pl.BlockSpec((1, 128), const2),
                  pl.BlockSpec((1, 128), const2),
                  pl.BlockSpec((128, 128), const2),
                  pl.BlockSpec((None,) + e5.shape[1:], lambda p, b, r: (p, 0, 0, 0, 0))],
        out_specs=pl.BlockSpec((nqs, 128), lambda p, b, r: (b * nrb + r, p)),
        out_shape=jax.ShapeDtypeStruct((n, NA_W), bf16),
        scratch_shapes=[pltpu.VMEM((seq, 128), bf16),
                        pltpu.VMEM((3, 2 * nq, NA_KROWS * GRID_W), f32)],
        compiler_params=pltpu.CompilerParams(dimension_semantics=("arbitrary", "arbitrary", "arbitrary")),
        name="natten",
    )(qn, kn, vn, qn2, kn2, gsum, e5)


def _top2(vals):
    io = lax.broadcasted_iota(i32, vals.shape, 0)
    m1 = jnp.max(vals, axis=0, keepdims=True)
    i1 = jnp.min(jnp.where(vals == m1, io, EPG), axis=0, keepdims=True)
    v2 = jnp.where(io == i1, -jnp.inf, vals)
    m2 = jnp.max(v2, axis=0, keepdims=True)
    i2 = jnp.min(jnp.where(v2 == m2, io, EPG), axis=0, keepdims=True)
    return m1, i1, m2, i2


def _outproj_kernel(og_ref, on_ref, wo_ref, x_ref, mod_ref, nrm_ref, rw_ref, rb_ref,
                    x1_ref, h2_ref, eidx_ref, gw_ref):
    mix = (jnp.dot(og_ref[...], wo_ref[0:V_W, :], preferred_element_type=f32)
           + jnp.dot(on_ref[...], wo_ref[V_W:V_W + NA_W, :], preferred_element_type=f32))
    x1 = x_ref[...] + mod_ref[2] * mix
    x1_ref[...] = x1
    y = x1 * lax.rsqrt(jnp.mean(x1 * x1, axis=-1, keepdims=True) + EPS) * nrm_ref[...]
    h2 = y * (1.0 + mod_ref[4]) + mod_ref[3]
    h2_ref[...] = h2
    rw = rw_ref[...]
    rw_hi = rw.astype(bf16)
    rw_lo = (rw - rw_hi.astype(f32)).astype(bf16)
    h_hi = h2.astype(bf16)
    h_lo = (h2 - h_hi.astype(f32)).astype(bf16)
    logits = _nt(rw_hi, h_hi) + _nt(rw_hi, h_lo) + _nt(rw_lo, h_hi)
    scores = jax.nn.sigmoid(logits)
    sel = scores + rb_ref[...]
    tops = [_top2(sel[g * EPG:(g + 1) * EPG]) for g in range(N_GRP)]
    best = jnp.zeros_like(tops[0][1])
    bs = tops[0][0] + tops[0][2]
    for g in range(1, N_GRP):
        gs = tops[g][0] + tops[g][2]
        take = gs > bs
        best = jnp.where(take, g, best)
        bs = jnp.where(take, gs, bs)
    io = lax.broadcasted_iota(i32, (EPG, sel.shape[1]), 0)
    i1 = jnp.zeros_like(best)
    i2 = jnp.zeros_like(best)
    s1 = jnp.zeros(best.shape, f32)
    s2 = jnp.zeros(best.shape, f32)
    for g in range(N_GRP):
        sc = scores[g * EPG:(g + 1) * EPG]
        pick = best == g
        i1 = jnp.where(pick, tops[g][1], i1)
        i2 = jnp.where(pick, tops[g][3], i2)
        s1 = jnp.where(pick, jnp.sum(jnp.where(io == tops[g][1], sc, 0.0), axis=0, keepdims=True), s1)
        s2 = jnp.where(pick, jnp.sum(jnp.where(io == tops[g][3], sc, 0.0), axis=0, keepdims=True), s2)
    eidx_ref[0:1, :] = best * EPG + i1
    eidx_ref[1:2, :] = best * EPG + i2
    tot = s1 + s2
    gw_ref[0:1, :] = s1 / tot
    gw_ref[1:2, :] = s2 / tot


def _outproj(o_gla, o_na, w_out, x2d, mod_l, nrm, router_w, router_b, bsz, seq):
    n = x2d.shape[0]
    tm = TM_PROJ
    nt = seq // tm
    row = lambda b, i: (b * nt + i, 0)
    col = lambda b, i: (0, b * nt + i)
    const = lambda b, i: (0, 0)
    return pl.pallas_call(
        _outproj_kernel,
        grid=(bsz, nt),
        in_specs=[pl.BlockSpec((tm, V_W), row),
                  pl.BlockSpec((tm, NA_W), row),
                  pl.BlockSpec((V_W + NA_W, D), const),
                  pl.BlockSpec((tm, D), row),
                  pl.BlockSpec((6, None, 1, D), lambda b, i: (0, b, 0, 0)),
                  pl.BlockSpec((1, D), const),
                  pl.BlockSpec((N_EXP, D), const),
                  pl.BlockSpec((N_EXP, 1), const)],
        out_specs=[pl.BlockSpec((tm, D), row), pl.BlockSpec((tm, D), row),
                   pl.BlockSpec((2, tm), col), pl.BlockSpec((2, tm), col)],
        out_shape=[jax.ShapeDtypeStruct((n, D), f32), jax.ShapeDtypeStruct((n, D), f32),
                   jax.ShapeDtypeStruct((2, n), i32), jax.ShapeDtypeStruct((2, n), f32)],
        compiler_params=pltpu.CompilerParams(dimension_semantics=("arbitrary", "arbitrary")),
        name="outproj_router",
    )(o_gla, o_na, w_out.astype(bf16), x2d, mod_l, nrm.reshape(1, D),
      router_w.T, router_b.reshape(N_EXP, 1))


def _rank_kernel(eidx_ref, tri_ref, rank_ref, cnt_ref, carry):
    @pl.when(pl.program_id(0) == 0)
    def _():
        carry[...] = jnp.zeros_like(carry)

    e = eidx_ref[...]
    tm = e.shape[1]
    io = lax.broadcasted_iota(i32, (N_EXP, tm), 0)
    run = carry[...]
    for k in range(2):
        oh = io == e[k:k + 1, :]
        ohf = oh.astype(f32)
        pre = jnp.dot(ohf.astype(bf16), tri_ref[...], preferred_element_type=f32) + run[:, 0:1]
        rank_ref[k:k + 1, :] = jnp.sum(jnp.where(oh, pre, 0.0), axis=0, keepdims=True).astype(i32)
        run = run + jnp.sum(ohf, axis=1, keepdims=True)
    carry[...] = run
    cnt_ref[...] = run


def _ranks(eidx):
    n = eidx.shape[1]
    tm = TM_RANK
    t = np.arange(tm)
    tri = jnp.asarray(t[:, None] < t[None, :], bf16)
    return pl.pallas_call(
        _rank_kernel,
        grid=(n // tm,),
        in_specs=[pl.BlockSpec((2, tm), lambda i: (0, i)), pl.BlockSpec((tm, tm), lambda i: (0, 0))],
        out_specs=[pl.BlockSpec((2, tm), lambda i: (0, i)), pl.BlockSpec((N_EXP, 128), lambda i: (0, 0))],
        out_shape=[jax.ShapeDtypeStruct((2, n), i32), jax.ShapeDtypeStruct((N_EXP, 128), f32)],
        scratch_shapes=[pltpu.VMEM((N_EXP, 128), f32)],
        compiler_params=pltpu.CompilerParams(dimension_semantics=("arbitrary",)),
        name="moe_rank",
    )(eidx, tri)


def _dispatch_kernel(dest_ref, tails_ref, h_ref, xb_ref, zbuf, hbuf, lsem, rsem, zsem):
    i = pl.program_id(0)
    nsteps = pl.num_programs(0)
    tm = TM_DISP
    n = nsteps * tm
    nbuf = hbuf.shape[0]

    def load(j):
        start = pl.multiple_of(j * tm, tm)
        return pltpu.make_async_copy(h_ref.at[pl.ds(start, tm)], hbuf.at[j % nbuf], lsem.at[j % nbuf])

    def drain_rows(j):
        for _ in range(2):
            pltpu.make_async_copy(hbuf.at[j % nbuf], xb_ref.at[pl.ds(0, tm)], rsem.at[j % nbuf]).wait()

    def tail_copy(e):
        start = pl.multiple_of(tails_ref[e], MOE_BLK)
        return pltpu.make_async_copy(zbuf, xb_ref.at[pl.ds(start, MOE_BLK)], zsem)

    @pl.when(i == 0)
    def _():
        zbuf[...] = jnp.zeros_like(zbuf)
        for e in range(2 * N_EXP):
            @pl.when(tails_ref[e] >= 0)
            def _():
                tail_copy(e).start()
        for e in range(2 * N_EXP):
            @pl.when(tails_ref[e] >= 0)
            def _():
                tail_copy(e).wait()
        load(i).start()

    @pl.when(i >= nbuf - 1)
    def _():
        drain_rows(i + 1 - nbuf)

    @pl.when(i + 1 < nsteps)
    def _():
        load(i + 1).start()

    load(i).wait()
    slot = i % nbuf

    def issue(t, carry):
        src = hbuf.at[slot, pl.ds(t, 1)]
        for k in range(2):
            d = dest_ref[k * n + i * tm + t]
            pltpu.make_async_copy(src, xb_ref.at[pl.ds(d, 1)], rsem.at[slot]).start()
        return carry

    lax.fori_loop(0, tm, issue, 0, unroll=True)

    @pl.when(i == nsteps - 1)
    def _():
        for back in range(nbuf - 2, -1, -1):
            drain_rows(i - back)


def _dispatch(h2, dest_flat, tails, p_rows):
    n = h2.shape[0]
    tm = TM_DISP
    return pl.pallas_call(
        _dispatch_kernel,
        grid_spec=pltpu.PrefetchScalarGridSpec(
            num_scalar_prefetch=2,
            grid=(n // tm,),
            in_specs=[pl.BlockSpec(memory_space=pl.ANY)],
            out_specs=pl.BlockSpec(memory_space=pl.ANY),
            scratch_shapes=[pltpu.VMEM((MOE_BLK, D), f32), pltpu.VMEM((3, tm, D), f32),
                            pltpu.SemaphoreType.DMA((3,)), pltpu.SemaphoreType.DMA((3,)),
                            pltpu.SemaphoreType.DMA(())],
        ),
        out_shape=jax.ShapeDtypeStruct((p_rows, D), f32),
        compiler_params=pltpu.CompilerParams(dimension_semantics=("arbitrary",)),
        name="moe_dispatch",
    )(dest_flat, tails, h2)


def _expert_kernel(blk_e_ref, nb_ref, xb_ref, w1_ref, w3_ref, w2_ref, y_ref, w1b, w3b, w2b):
    i = pl.program_id(0)

    @pl.when((i < nb_ref[0]) & ((i == 0) | (blk_e_ref[i] != blk_e_ref[jnp.maximum(i - 1, 0)])))
    def _():
        w1b[...] = w1_ref[...].astype(bf16)
        w3b[...] = w3_ref[...].astype(bf16)
        w2b[...] = w2_ref[...].astype(bf16)

    @pl.when(i < nb_ref[0])
    def _():
        x = xb_ref[...].astype(bf16)
        a = jnp.dot(x, w1b[...], preferred_element_type=f32)
        b = jnp.dot(x, w3b[...], preferred_element_type=f32)
        h = (a * jax.nn.sigmoid(a) * b).astype(bf16)
        y_ref[...] = jnp.dot(h, w2b[...], preferred_element_type=f32)

    @pl.when(i >= nb_ref[0])
    def _():
        y_ref[...] = jnp.zeros_like(y_ref)


def _experts(xb, blk_e, nb_used, w1, w3, w2, layer):
    p_rows = xb.shape[0]
    nb = p_rows // MOE_BLK

    def xmap(i, be, nbu):
        return (jnp.minimum(i, nbu[0] - 1), 0)

    def wmap(i, be, nbu):
        return (layer, be[i], 0, 0)

    return pl.pallas_call(
        _expert_kernel,
        grid_spec=pltpu.PrefetchScalarGridSpec(
            num_scalar_prefetch=2,
            grid=(nb,),
            in_specs=[pl.BlockSpec((MOE_BLK, D), xmap),
                      pl.BlockSpec((None, None, D, D_EXP), wmap),
                      pl.BlockSpec((None, None, D, D_EXP), wmap),
                      pl.BlockSpec((None, None, D_EXP, D), wmap)],
            out_specs=pl.BlockSpec((MOE_BLK, D), lambda i, be, nbu: (i, 0)),
            scratch_shapes=[pltpu.VMEM((D, D_EXP), bf16), pltpu.VMEM((D, D_EXP), bf16),
                            pltpu.VMEM((D_EXP, D), bf16)],
        ),
        out_shape=jax.ShapeDtypeStruct((p_rows, D), f32),
        compiler_params=pltpu.CompilerParams(dimension_semantics=("arbitrary",),
                                             vmem_limit_bytes=48 * 1024 * 1024),
        name="moe_experts",
    )(blk_e, nb_used, xb, w1, w3, w2)


def _combine_kernel(dest_ref, x1_ref, gw_ref, mod_ref, y_ref, o_ref, ybuf, sems):
    i = pl.program_id(0)
    nsteps = pl.num_programs(0)
    tm = x1_ref.shape[0]
    n = nsteps * tm

    def issue_tile(j):
        slot = j % 2

        def issue(t, carry):
            for k in range(2):
                d = dest_ref[k * n + j * tm + t]
                pltpu.make_async_copy(y_ref.at[pl.ds(d, 1)], ybuf.at[slot, k, pl.ds(t, 1)],
                                      sems.at[slot]).start()
            return carry

        lax.fori_loop(0, tm, issue, 0, unroll=True)

    @pl.when(i == 0)
    def _():
        issue_tile(i)

    @pl.when(i + 1 < nsteps)
    def _():
        issue_tile(i + 1)

    slot = i % 2
    for k in range(2):
        pltpu.make_async_copy(y_ref.at[pl.ds(0, tm)], ybuf.at[slot, k], sems.at[slot]).wait()
    gw = gw_ref[...]
    y = ybuf[slot, 0] * gw[:, 0:1] + ybuf[slot, 1] * gw[:, 1:2]
    o_ref[...] = x1_ref[...] + mod_ref[5] * y


def _combine(y, dest_flat, x1, gw_t, mod_l, bsz, seq):
    n = x1.shape[0]
    tm = TM_COMB
    nt = seq // tm
    return pl.pallas_call(
        _combine_kernel,
        grid_spec=pltpu.PrefetchScalarGridSpec(
            num_scalar_prefetch=1,
            grid=(n // tm,),
            in_specs=[pl.BlockSpec((tm, D), lambda i, *_: (i, 0)),
                      pl.BlockSpec((tm, 2), lambda i, *_: (i, 0)),
                      pl.BlockSpec((6, None, 1, D), lambda i, *_: (0, i // nt, 0, 0)),
                      pl.BlockSpec(memory_space=pl.ANY)],
            out_specs=pl.BlockSpec((tm, D), lambda i, *_: (i, 0)),
            scratch_shapes=[pltpu.VMEM((2, 2, tm, D), f32), pltpu.SemaphoreType.DMA((2,))],
        ),
        out_shape=jax.ShapeDtypeStruct((n, D), f32),
        compiler_params=pltpu.CompilerParams(dimension_semantics=("arbitrary",)),
        name="moe_combine",
    )(dest_flat, x1, gw_t, mod_l, y)


def _grouped_moe(h2, eidx, gw, x1, mod_l, w1, w3, w2, layer, bsz, seq):
    n = h2.shape[0]
    p_rows = (n * 2 // MOE_BLK + N_EXP) * MOE_BLK
    nb = p_rows // MOE_BLK
    rank, cnt = _ranks(eidx)
    counts = cnt[:, 0].astype(i32)
    padded = (counts + MOE_BLK - 1) // MOE_BLK * MOE_BLK
    pad_end = jnp.cumsum(padded)
    pad_start = pad_end - padded
    start_of = jnp.sum(jnp.where(eidx[..., None] == jnp.arange(N_EXP, dtype=i32), pad_start, 0), axis=-1)
    dest_flat = (start_of + rank).reshape(-1)
    nb_used = (pad_end[-1] // MOE_BLK).astype(i32)
    spare = nb_used + jnp.arange(N_EXP, dtype=i32)
    tails = jnp.concatenate([jnp.where(padded > 0, pad_end - MOE_BLK, -1),
                             jnp.where(spare < nb, spare * MOE_BLK, -1)]).astype(i32)
    blk = jnp.minimum(jnp.arange(nb, dtype=i32), nb_used - 1)
    seg_done = (pad_end[None, :] <= (blk * MOE_BLK)[:, None]).astype(i32)
    blk_e = jnp.minimum(jnp.sum(seg_done, axis=1), N_EXP - 1).astype(i32)
    xb = _dispatch(h2, dest_flat, tails, p_rows)
    y = _experts(xb, blk_e, nb_used.reshape(1), w1, w3, w2, layer)
    return _combine(y, dest_flat, x1, gw.T, mod_l, bsz, seq)


def kernel(x, c, w_ada, b_ada, attn_norm, ffn_norm, w_in, gla_gate_w, gla_gate_b, gla_out_norm,
           na_q_norm, na_k_norm, na_rpb, w_out, router_w, router_b, w1, w3, w2):
    bsz, seq, _ = x.shape
    depth = w_ada.shape[0]
    mod = _adaln_mod(c, w_ada, b_ada)
    xc = x.reshape(bsz * seq, D)
    for l in range(depth):
        (qg, kg, vg, sg, la_f, la_b, qn, kn, vn), bounded_f, bounded_b = _inproj(
            xc, mod[l], attn_norm[l], w_in[l], gla_gate_w[l], gla_gate_b[l], bsz, seq)
        o_f = _gla_dir(False, qg, kg, vg, la_f, bounded_f, bsz, seq)
        o_gla = _gla_dir(True, qg, kg, vg, la_b, bounded_b, bsz, seq, extra=(o_f, sg, gla_out_norm[l]))
        o_na = _natten(qn, kn, vn, na_q_norm[l], na_k_norm[l], na_rpb[l], bsz, seq)
        x1, h2, eidx, gw = _outproj(o_gla, o_na, w_out[l], xc, mod[l], ffn_norm[l],
                                    router_w, router_b, bsz, seq)
        xc = _grouped_moe(h2, eidx, gw, x1, mod[l], w1, w3, w2, l, bsz, seq)
    return xc.reshape(bsz, seq, D)
```

```python
import functools

import numpy as np
import jax
import jax.numpy as jnp
from jax import lax
from jax.experimental import pallas as pl
from jax.experimental.pallas import tpu as pltpu

f32 = jnp.float32
bf16 = jnp.bfloat16
i32 = jnp.int32

D = 1024
GRID_W = 64
GLA_H, GLA_DK, GLA_DV = 4, 64, 128
GLA_RANK = 16
GLA_TAU = 16.0
CHUNK = 64
NA_H, NA_DH = 8, 64
WIN_H, WIN_W = 8, 16
N_EXP, N_GRP, EPG = 32, 4, 8
D_EXP = 512
QK_W = GLA_H * GLA_DK
V_W = GLA_H * GLA_DV
NA_W = NA_H * NA_DH
EPS = 1e-6
NEG = -1e30

TM_PROJ = 1024
TB_GLA = 256
GLA_NB = 4
NA_QROWS = 4
NA_KROWS = 12
NA_SUB = 8
TM_RANK = 1024
TM_DISP = 512
TM_COMB = 512
MOE_BLK = 512
N_LEVELS = 6
GLA_SAFE_RANGE = 40.0


def _nt(a, b):
    return lax.dot_general(a, b, (((1,), (1,)), ((), ())), preferred_element_type=f32)


def _tn(a, b):
    return lax.dot_general(a, b, (((0,), (0,)), ((), ())), preferred_element_type=f32)


def _split3(x):
    hi = x.astype(bf16)
    r = x - hi.astype(f32)
    mid = r.astype(bf16)
    lo = (r - mid.astype(f32)).astype(bf16)
    return hi, mid, lo


def _sel_l(m01, x):
    hi, mid, lo = _split3(x)
    d = lambda p: jnp.dot(m01, p, preferred_element_type=f32)
    return d(hi) + d(mid) + d(lo)


def _sel_r(x, m01):
    hi, mid, lo = _split3(x)
    d = lambda p: jnp.dot(p, m01, preferred_element_type=f32)
    return d(hi) + d(mid) + d(lo)


def _mod_kernel(c_ref, w_ref, b_ref, o_ref):
    c = c_ref[...]
    ca = c * jax.nn.sigmoid(c)
    w = w_ref[0]
    c_hi = ca.astype(bf16)
    c_lo = (ca - c_hi.astype(f32)).astype(bf16)
    w_hi = w.astype(bf16)
    acc = jnp.dot(c_hi, w_hi, preferred_element_type=f32) + jnp.dot(c_lo, w_hi, preferred_element_type=f32)
    o_ref[0, 0] = acc + b_ref[0, 0]


def _adaln_mod(c, w_ada, b_ada):
    depth = w_ada.shape[0]
    bsz = c.shape[0]
    cp = jnp.zeros((8, D), f32).at[:bsz].set(c)
    out = pl.pallas_call(
        _mod_kernel,
        grid=(depth, 6),
        in_specs=[pl.BlockSpec((8, D), lambda l, j: (0, 0)),
                  pl.BlockSpec((1, D, D), lambda l, j: (l, 0, j)),
                  pl.BlockSpec((1, 1, 1, D), lambda l, j: (l, j, 0, 0))],
        out_specs=pl.BlockSpec((1, 1, 8, D), lambda l, j: (l, j, 0, 0)),
        out_shape=jax.ShapeDtypeStruct((depth, 6, 8, D), f32),
        name="adaln_mod",
    )(cp, w_ada, b_ada.reshape(depth, 6, 1, D))
    return out[:, :, :bsz].reshape(depth, 6, bsz, 1, D)


def _log_sigmoid(z):
    return jnp.minimum(z, 0.0) - jnp.log1p(jnp.exp(-jnp.abs(z)))


def _inproj_kernel(x_ref, mod_ref, nrm_ref, wm_ref, wlr_ref, gf_ref, bf_ref,
                   qg_ref, kg_ref, vg_ref, sg_ref, laf_ref, lab_ref, qn_ref, kn_ref, vn_ref, lamin_ref):
    x = x_ref[...]
    y = x * lax.rsqrt(jnp.mean(x * x, axis=-1, keepdims=True) + EPS) * nrm_ref[...]
    h = y * (1.0 + mod_ref[1]) + mod_ref[0]
    hb = h.astype(bf16)
    proj = lambda lo, hi: jnp.dot(hb, wm_ref[:, lo:hi], preferred_element_type=f32)
    qg_ref[...] = (proj(0, 256) * (GLA_DK ** -0.5)).astype(bf16)
    kg_ref[...] = proj(256, 512).astype(bf16)
    vg_ref[...] = proj(512, 1024).astype(bf16)
    gg = proj(1024, 1536)
    sg_ref[...] = (gg * jax.nn.sigmoid(gg)).astype(bf16)
    qn_ref[...] = proj(1536, 2048).astype(bf16)
    kn_ref[...] = proj(2048, 2560).astype(bf16)
    vn_ref[...] = proj(2560, 3072).astype(bf16)
    lr = jnp.dot(hb, wlr_ref[...], preferred_element_type=f32)
    z = jnp.dot(lr, gf_ref[...], precision=lax.Precision.HIGHEST, preferred_element_type=f32) + bf_ref[...]
    la = _log_sigmoid(z) * (1.0 / GLA_TAU)
    la_f = la[:, 0:QK_W]
    la_b = la[:, QK_W:2 * QK_W]
    laf_ref[...] = la_f
    lab_ref[...] = la_b
    rows = []
    for la in (la_f, la_b):
        for blk in range(la.shape[0] // TB_GLA):
            tot = jnp.sum(la[blk * TB_GLA:(blk + 1) * TB_GLA], axis=0, keepdims=True)
            rows.append(jnp.broadcast_to(jnp.min(tot, axis=1, keepdims=True), (1, 128)))
    if len(rows) < 8:
        rows.append(jnp.zeros((8 - len(rows), 128), f32))
    lamin_ref[...] = jnp.concatenate(rows, axis=0)


def _inproj(x2d, mod_l, nrm, w_in, gate_w, gate_b, bsz, seq):
    n = x2d.shape[0]
    tm = TM_PROJ
    nt = seq // tm
    lo, hi = 2 * QK_W + 2 * V_W, 2 * QK_W + 2 * V_W + 2 * GLA_RANK
    wm = jnp.concatenate([w_in[:, :lo], w_in[:, hi:]], axis=1).astype(bf16)
    wlr = w_in[:, lo:hi].astype(bf16)
    zpad = jnp.zeros((GLA_RANK, QK_W), f32)
    gmat = jnp.concatenate([jnp.concatenate([gate_w[0], zpad], axis=1),
                            jnp.concatenate([zpad, gate_w[1]], axis=1)], axis=0)
    gbias = jnp.concatenate([gate_b[0], gate_b[1]]).reshape(1, 2 * QK_W)
    row = lambda b, i: (b * nt + i, 0)
    const = lambda b, i: (0, 0)
    widths = [(QK_W, bf16), (QK_W, bf16), (V_W, bf16), (V_W, bf16), (QK_W, f32), (QK_W, f32),
              (NA_W, bf16), (NA_W, bf16), (NA_W, bf16)]
    outs = pl.pallas_call(
        _inproj_kernel,
        grid=(bsz, nt),
        in_specs=[pl.BlockSpec((tm, D), row),
                  pl.BlockSpec((6, None, 1, D), lambda b, i: (0, b, 0, 0)),
                  pl.BlockSpec((1, D), const),
                  pl.BlockSpec(wm.shape, const),
                  pl.BlockSpec(wlr.shape, const),
                  pl.BlockSpec(gmat.shape, const),
                  pl.BlockSpec(gbias.shape, const)],
        out_specs=[pl.BlockSpec((tm, w), row) for w, _ in widths]
        + [pl.BlockSpec((None, 8, 128), lambda b, i: (b * nt + i, 0, 0))],
        out_shape=[jax.ShapeDtypeStruct((n, w), dt) for w, dt in widths]
        + [jax.ShapeDtypeStruct((n // tm, 8, 128), f32)],
        compiler_params=pltpu.CompilerParams(dimension_semantics=("arbitrary", "arbitrary")),
        name="inproj",
    )(x2d, mod_l, nrm.reshape(1, D), wm, wlr, gmat, gbias)
    per = tm // TB_GLA
    lamin = outs[-1][:, :2 * per, 0]
    bounded_f = (lamin[:, :per].reshape(-1) > -GLA_SAFE_RANGE).astype(i32)
    bounded_b = (lamin[:, per:].reshape(-1) > -GLA_SAFE_RANGE).astype(i32)
    return outs[:-1], bounded_f, bounded_b


def _gla_consts(rev, tb):
    c = CHUNK
    idx = np.arange(c)
    if not rev:
        cum = (idx[None, :] <= idx[:, None]).astype(np.float32)
    else:
        cum = (idx[None, :] >= idx[:, None]).astype(np.float32)
    mats = [cum]
    masks = []
    for lvl in range(N_LEVELS):
        s = c >> (lvl + 1)
        blk = idx // (2 * s)
        second = (idx % (2 * s)) >= s
        ref_row = blk * 2 * s + (s if rev else s - 1)
        mats.append(cum[ref_row])
        same = blk[:, None] == blk[None, :]
        if not rev:
            m = same & second[:, None] & (~second[None, :])
        else:
            m = same & (~second[:, None]) & second[None, :]
        masks.append(m)
    masks.append(np.eye(c, dtype=bool) & (not rev))
    mst = np.concatenate(mats, axis=0)
    msk = np.stack([np.tile(m, (1, GLA_H)) for m in masks]).astype(np.float32)
    tix = np.arange(tb)
    tri = (tix[None, :] >= tix[:, None]) if rev else (tix[None, :] <= tix[:, None])
    cmask = (tix[:, None] < tix[None, :]) if rev else (tix[:, None] >= tix[None, :])
    nch = tb // c
    selb = np.zeros((tb, nch * 128), np.float32)
    for ch in range(nch):
        selb[ch * c:(ch + 1) * c, ch * 128:(ch + 1) * 128] = 1.0
    hk = np.arange(QK_W) // GLA_DK
    hv = np.arange(V_W) // GLA_DV
    kmask = (hk[:, None] == hk[None, :]).astype(np.float32)
    vmask = (hk[:, None] == hv[None, :]).astype(np.float32)
    return (jnp.asarray(mst, bf16), jnp.asarray(msk, f32), jnp.asarray(selb, bf16),
            jnp.asarray(kmask, bf16), jnp.asarray(vmask, bf16), jnp.asarray(vmask, f32),
            jnp.asarray(tri, bf16), jnp.asarray(cmask, f32), jnp.ones((tb, 128), bf16))


def _gla_kernel(rev, tb, *refs):
    if rev:
        (bounded_ref, q_ref, k_ref, v_ref, la_ref, mst_ref, msk_ref, selb_ref, kmask_ref, vmask_ref,
         smask_ref, tri_ref, cmask_ref, ones_ref, of_ref, sg_ref, gn_ref, out_ref, s_scr, o_scr) = refs
    else:
        (bounded_ref, q_ref, k_ref, v_ref, la_ref, mst_ref, msk_ref, selb_ref, kmask_ref, vmask_ref,
         smask_ref, tri_ref, cmask_ref, ones_ref, out_ref, s_scr) = refs
        o_scr = out_ref
    c = CHUNK
    nch = tb // c
    elems = range(GLA_NB)

    @pl.when(pl.program_id(1) == 0)
    def _():
        s_scr[...] = jnp.zeros_like(s_scr)

    def blocks_bounded():
        tri = tri_ref[...]
        cmask = cmask_ref[...]
        las = [la_ref[e] for e in elems]
        tots = [_sel_r(la.T, ones_ref[...]) for la in las]
        bcums = []
        for la in las:
            la_hi = la.astype(bf16)
            la_lo = (la - la_hi.astype(f32)).astype(bf16)
            bcums.append(jnp.dot(tri, la_hi, preferred_element_type=f32)
                         + jnp.dot(tri, la_lo, preferred_element_type=f32))
        ks = [k_ref[e].astype(f32) for e in elems]
        qes = [(q_ref[e].astype(f32) * jnp.exp(bcums[e])).astype(bf16) for e in elems]
        kxs = [(ks[e] * jnp.exp(-bcums[e])).astype(bf16) for e in elems]
        s_prevs = [s_scr[e] for e in elems]
        o_inters = [jnp.dot(qes[e], s_prevs[e].astype(bf16), preferred_element_type=f32) for e in elems]
        lane_head = lax.broadcasted_iota(i32, (tb, QK_W), 1) // GLA_DK
        for hd in range(GLA_H):
            cols = slice(hd * GLA_DV, (hd + 1) * GLA_DV)
            phs = []
            for e in elems:
                qh = jnp.where(lane_head == hd, qes[e], jnp.zeros_like(qes[e]))
                phs.append(jnp.where(cmask > 0.0, _nt(qh, kxs[e]), 0.0).astype(bf16))
            for e in elems:
                o_blk = o_inters[e][:, cols] + jnp.dot(phs[e], v_ref[e, :, cols], preferred_element_type=f32)
                o_scr[e, :, cols] = o_blk.astype(o_scr.dtype)
        for e in elems:
            blast = bcums[e][0:1] if rev else bcums[e][tb - 1:tb]
            ke = (ks[e] * jnp.exp(blast - bcums[e])).astype(bf16)
            dec = jnp.exp(tots[e])
            s_scr[e] = (s_prevs[e] * jnp.concatenate([dec] * GLA_H, axis=1)
                        + _tn(ke, v_ref[e]) * smask_ref[...])

    def chunks_robust():
        mst = mst_ref[...]
        kmask = kmask_ref[...]
        vmask = vmask_ref[...]
        for e in elems:
            tot = _sel_r(la_ref[e].T, selb_ref[...])
            for ch in (range(nch - 1, -1, -1) if rev else range(nch)):
                rows = slice(ch * c, (ch + 1) * c)
                q = q_ref[e, rows, :].astype(f32)
                k = k_ref[e, rows, :].astype(f32)
                v = v_ref[e, rows, :]
                r_all = _sel_l(mst, la_ref[e, rows, :])
                bcum = r_all[0:c]
                qe = (q * jnp.exp(bcum)).astype(bf16)
                o = jnp.dot(qe, s_scr[e].astype(bf16), preferred_element_type=f32)
                sall = jnp.zeros((c, QK_W), f32)
                for lvl in range(N_LEVELS):
                    ref = r_all[(lvl + 1) * c:(lvl + 2) * c]
                    a = (q * jnp.exp(jnp.minimum(bcum - ref, 0.0))).astype(bf16)
                    bm = (k * jnp.exp(jnp.minimum(ref - bcum, 0.0))).astype(bf16)
                    bbd = jnp.concatenate([bm] * GLA_H, axis=0) * kmask
                    sall = sall + jnp.where(msk_ref[lvl] > 0.0, _nt(a, bbd), 0.0)
                if not rev:
                    bbd = jnp.concatenate([k_ref[e, rows, :]] * GLA_H, axis=0) * kmask
                    sall = sall + jnp.where(msk_ref[N_LEVELS] > 0.0, _nt(q_ref[e, rows, :], bbd), 0.0)
                vbd = jnp.concatenate([v] * GLA_H, axis=0) * vmask
                o_blk = o + jnp.dot(sall.astype(bf16), vbd, preferred_element_type=f32)
                o_scr[e, rows, :] = o_blk.astype(o_scr.dtype)
                blast = bcum[0:1] if rev else bcum[c - 1:c]
                ke = (k * jnp.exp(blast - bcum)).astype(bf16)
                dec = jnp.exp(tot[:, ch * 128:(ch + 1) * 128])
                s_scr[e] = (s_scr[e] * jnp.concatenate([dec] * GLA_H, axis=1)
                            + _tn(ke, v) * smask_ref[...])

    nb = pl.num_programs(1)
    pos = nb - 1 - pl.program_id(1) if rev else pl.program_id(1)
    bounded = bounded_ref[(pl.program_id(0) * GLA_NB) * nb + pos] > 0
    for e in range(1, GLA_NB):
        bounded = bounded & (bounded_ref[(pl.program_id(0) * GLA_NB + e) * nb + pos] > 0)
    pl.when(bounded)(blocks_bounded)
    pl.when(jnp.logical_not(bounded))(chunks_robust)

    if rev:
        gn = gn_ref[...]
        for e in elems:
            for hd in range(GLA_H):
                cols = slice(hd * GLA_DV, (hd + 1) * GLA_DV)
                t = of_ref[e, :, cols].astype(f32) + o_scr[e, :, cols]
                y = t * lax.rsqrt(jnp.mean(t * t, axis=-1, keepdims=True) + EPS) * gn
                out_ref[e, :, cols] = (y * sg_ref[e, :, cols].astype(f32)).astype(bf16)


def _gla_dir(rev, q, k, v, la, bounded, bsz, seq, extra=None):
    n = q.shape[0]
    tb = TB_GLA
    nb = seq // tb
    consts = _gla_consts(rev, tb)
    if rev:
        row = lambda b, i, flags: (b, nb - 1 - i, 0)
    else:
        row = lambda b, i, flags: (b, i, 0)
    const2 = lambda b, i, flags: (0, 0)
    const3 = lambda b, i, flags: (0, 0, 0)
    per_batch = lambda a: a.reshape(bsz, seq, a.shape[-1])
    q, k, v, la = per_batch(q), per_batch(k), per_batch(v), per_batch(la)
    in_specs = [pl.BlockSpec((GLA_NB, tb, QK_W), row), pl.BlockSpec((GLA_NB, tb, QK_W), row),
                pl.BlockSpec((GLA_NB, tb, V_W), row), pl.BlockSpec((GLA_NB, tb, QK_W), row),
                pl.BlockSpec(consts[0].shape, const2), pl.BlockSpec(consts[1].shape, const3),
                pl.BlockSpec(consts[2].shape, const2), pl.BlockSpec(consts[3].shape, const2),
                pl.BlockSpec(consts[4].shape, const2), pl.BlockSpec(consts[5].shape, const2),
                pl.BlockSpec(consts[6].shape, const2), pl.BlockSpec(consts[7].shape, const2),
                pl.BlockSpec(consts[8].shape, const2)]
    args = [q, k, v, la, *consts]
    scratch = [pltpu.VMEM((GLA_NB, QK_W, V_W), f32)]
    if rev:
        o_f, sg, gn = extra
        in_specs += [pl.BlockSpec((GLA_NB, tb, V_W), row), pl.BlockSpec((GLA_NB, tb, V_W), row),
                     pl.BlockSpec((1, GLA_DV), const2)]
        args += [per_batch(o_f), per_batch(sg), gn.reshape(1, GLA_DV)]
        scratch.append(pltpu.VMEM((GLA_NB, tb, V_W), f32))
    out = pl.pallas_call(
        functools.partial(_gla_kernel, rev, tb),
        grid_spec=pltpu.PrefetchScalarGridSpec(
            num_scalar_prefetch=1,
            grid=(bsz // GLA_NB, nb),
            in_specs=in_specs,
            out_specs=pl.BlockSpec((GLA_NB, tb, V_W), row),
            scratch_shapes=scratch,
        ),
        out_shape=jax.ShapeDtypeStruct((bsz, seq, V_W), bf16),
        compiler_params=pltpu.CompilerParams(dimension_semantics=("arbitrary", "arbitrary")),
        name="gla_bwd" if rev else "gla_fwd",
    )(bounded, *args)
    return out.reshape(n, V_W)


def _rpb_expand_kernel(rpb_ref, sel_ref, valid_ref, o_ref):
    e = _sel_r(rpb_ref[...], sel_ref[...])
    o_ref[...] = jnp.where(valid_ref[...] > 0.0, e, NEG)


def _na_bias_table(rpb):
    nri, nci = 2 * WIN_H - 1, 2 * WIN_W - 1
    qc = np.arange(GRID_W)
    col_start = np.clip(qc - WIN_W // 2, 0, GRID_W - WIN_W)
    kc = np.arange(GRID_W)
    valid = (kc[None, :] >= col_start[:, None]) & (kc[None, :] < col_start[:, None] + WIN_W)
    ci = kc[None, :] - qc[:, None] + WIN_W - 1
    sel = np.zeros((32, GRID_W * GRID_W), np.float32)
    flat_ci = ci.reshape(-1)
    ok = (flat_ci >= 0) & (flat_ci < nci)
    sel[flat_ci[ok], np.nonzero(ok)[0]] = 1.0
    rpb2 = jnp.zeros((NA_H * nri + 8 - (NA_H * nri) % 8, 32), f32).at[:NA_H * nri, :nci].set(
        rpb.reshape(NA_H * nri, nci))
    nrow = rpb2.shape[0]
    e = pl.pallas_call(
        _rpb_expand_kernel,
        out_shape=jax.ShapeDtypeStruct((nrow, GRID_W * GRID_W), f32),
        name="rpb_expand",
    )(rpb2, jnp.asarray(sel, bf16), jnp.asarray(valid.reshape(1, -1), f32))
    return e[:NA_H * nri].reshape(NA_H // 2, 2, nri, GRID_W, GRID_W)


def _na_row_classes():
    rows = GRID_W
    ri = -np.ones((3, NA_QROWS, NA_KROWS), np.int64)
    for cls, rb in enumerate((0, 1, rows // NA_QROWS - 1)):
        r0 = rb * NA_QROWS
        kb = int(np.clip(r0 - WIN_H // 2, 0, rows - NA_KROWS))
        for rq in range(NA_QROWS):
            r = r0 + rq
            rs = int(np.clip(r - WIN_H // 2, 0, rows - WIN_H))
            for m in range(NA_KROWS):
                kr = kb + m
                if rs <= kr < rs + WIN_H:
                    ri[cls, rq, m] = kr - r + WIN_H - 1
    return ri


def _natten_kernel(q_ref, k_ref, v_ref, qn_ref, kn_ref, gsum_ref, e_ref, o_ref, kn_scr, tbl_scr):
    b = pl.program_id(1)
    rb = pl.program_id(2)
    gsum = gsum_ref[...]
    log2e = 1.4426950408889634

    def headnorm(x, g):
        sq = x * x
        hi = sq.astype(bf16)
        lo = (sq - hi.astype(f32)).astype(bf16)
        ms = (jnp.dot(hi, gsum, preferred_element_type=f32)
              + jnp.dot(lo, gsum, preferred_element_type=f32)) * (1.0 / NA_DH)
        return x * lax.rsqrt(ms + EPS) * g

    @pl.when((b == 0) & (rb == 0))
    def _():
        ri = _na_row_classes()
        neg = jnp.full((GRID_W, GRID_W), NEG, f32)
        for hh in range(2):
            for cls in range(3):
                for rq in range(NA_QROWS):
                    for mp in range(NA_KROWS // 2):
                        parts = []
                        for m in (2 * mp, 2 * mp + 1):
                            r = int(ri[cls, rq, m])
                            parts.append(neg if r < 0 else e_ref[hh, r] * log2e)
                        r0 = (hh * NA_QROWS + rq) * GRID_W
                        tbl_scr[cls, r0:r0 + GRID_W, mp * 128:(mp + 1) * 128] = (
                            jnp.concatenate(parts, axis=1))

    @pl.when(rb == 0)
    def _():
        kn_scr[...] = headnorm(k_ref[...].astype(f32), kn_ref[...]).astype(bf16)

    nq = NA_QROWS * GRID_W
    nk = NA_KROWS * GRID_W
    nblk = GRID_W // NA_QROWS
    first = lax.broadcasted_iota(i32, (nq, 128), 1) < NA_DH
    qs, wins, tbls = [], [], []
    for sub in range(NA_SUB):
        qb = rb * NA_SUB + sub
        q = headnorm(q_ref[sub * nq:(sub + 1) * nq, :].astype(f32), qn_ref[...]) * (NA_DH ** -0.5 * log2e)
        kb = jnp.clip(qb * NA_QROWS - WIN_H // 2, 0, GRID_W - NA_KROWS)
        start = pl.multiple_of(kb * GRID_W, GRID_W)
        cls = jnp.where(qb == 0, 0, jnp.where(qb == nblk - 1, 2, 1))
        qs.append([jnp.where(first, q, 0.0).astype(bf16), jnp.where(first, 0.0, q).astype(bf16)])
        wins.append((kn_scr[pl.ds(start, nk), :], v_ref[pl.ds(start, nk), :]))
        tbls.append(cls)
    scores = [[_nt(qs[sub][hh], wins[sub][0]) + tbl_scr[tbls[sub], hh * nq:(hh + 1) * nq, :]
               for hh in range(2)] for sub in range(NA_SUB)]
    for sub in range(NA_SUB):
        outs = []
        for s in scores[sub]:
            m = jnp.max(s, axis=-1, keepdims=True)
            p = jnp.exp2(s - m)
            l = jnp.sum(p, axis=-1, keepdims=True)
            outs.append(jnp.dot(p.astype(bf16), wins[sub][1], preferred_element_type=f32) / l)
        o_ref[sub * nq:(sub + 1) * nq, :] = jnp.where(first, outs[0], outs[1]).astype(bf16)


def _natten(qn, kn, vn, q_norm, k_norm, rpb, bsz, seq):
    n = qn.shape[0]
    e5 = _na_bias_table(rpb)
    nq = NA_QROWS * GRID_W
    nqs = NA_SUB * nq
    nrb = seq // nqs
    lane_h = np.arange(128) // NA_DH
    gsum = jnp.asarray(lane_h[:, None] == lane_h[None, :], bf16)
    qn2 = jnp.tile(q_norm.reshape(1, NA_DH), (1, 2))
    kn2 = jnp.tile(k_norm.reshape(1, NA_DH), (1, 2))
    const2 = lambda p, b, r: (0, 0)
    return pl.pallas_call(
        _natten_kernel,
        grid=(NA_H // 2, bsz, nrb),
        in_specs=[pl.BlockSpec((nqs, 128), lambda p, b, r: (b * nrb + r, p)),
                  pl.BlockSpec((seq, 128), lambda p, b, r: (b, p)),
                  pl.BlockSpec((seq, 128), lambda p, b, r: (b, p)),
                  pl.BlockSpec((1, 128), const2),
                  pl.BlockSpec((1, 128), const2),
                  pl.BlockSpec((128, 128), const2),
                  pl.BlockSpec((None,) + e5.shape[1:], lambda p, b, r: (p, 0, 0, 0, 0))],
        out_specs=pl.BlockSpec((nqs, 128), lambda p, b, r: (b * nrb + r, p)),
        out_shape=jax.ShapeDtypeStruct((n, NA_W), bf16),
        scratch_shapes=[pltpu.VMEM((seq, 128), bf16),
                        pltpu.VMEM((3, 2 * nq, NA_KROWS * GRID_W), f32)],
        compiler_params=pltpu.CompilerParams(dimension_semantics=("arbitrary", "arbitrary", "arbitrary")),
        name="natten",
    )(qn, kn, vn, qn2, kn2, gsum, e5)


def _top2(vals):
    io = lax.broadcasted_iota(i32, vals.shape, 0)
    m1 = jnp.max(vals, axis=0, keepdims=True)
    i1 = jnp.min(jnp.where(vals == m1, io, EPG), axis=0, keepdims=True)
    v2 = jnp.where(io == i1, -jnp.inf, vals)
    m2 = jnp.max(v2, axis=0, keepdims=True)
    i2 = jnp.min(jnp.where(v2 == m2, io, EPG), axis=0, keepdims=True)
    return m1, i1, m2, i2


def _outproj_kernel(og_ref, on_ref, wo_ref, x_ref, mod_ref, nrm_ref, rw_ref, rb_ref,
                    x1_ref, h2_ref, eidx_ref, gw_ref):
    mix = (jnp.dot(og_ref[...], wo_ref[0:V_W, :], preferred_element_type=f32)
           + jnp.dot(on_ref[...], wo_ref[V_W:V_W + NA_W, :], preferred_element_type=f32))
    x1 = x_ref[...] + mod_ref[2] * mix
    x1_ref[...] = x1
    y = x1 * lax.rsqrt(jnp.mean(x1 * x1, axis=-1, keepdims=True) + EPS) * nrm_ref[...]
    h2 = y * (1.0 + mod_ref[4]) + mod_ref[3]
    h2_ref[...] = h2
    rw = rw_ref[...]
    rw_hi = rw.astype(bf16)
    rw_lo = (rw - rw_hi.astype(f32)).astype(bf16)
    h_hi = h2.astype(bf16)
    h_lo = (h2 - h_hi.astype(f32)).astype(bf16)
    logits = _nt(rw_hi, h_hi) + _nt(rw_hi, h_lo) + _nt(rw_lo, h_hi)
    scores = jax.nn.sigmoid(logits)
    sel = scores + rb_ref[...]
    tops = [_top2(sel[g * EPG:(g + 1) * EPG]) for g in range(N_GRP)]
    best = jnp.zeros_like(tops[0][1])
    bs = tops[0][0] + tops[0][2]
    for g in range(1, N_GRP):
        gs = tops[g][0] + tops[g][2]
        take = gs > bs
        best = jnp.where(take, g, best)
        bs = jnp.where(take, gs, bs)
    io = lax.broadcasted_iota(i32, (EPG, sel.shape[1]), 0)
    i1 = jnp.zeros_like(best)
    i2 = jnp.zeros_like(best)
    s1 = jnp.zeros(best.shape, f32)
    s2 = jnp.zeros(best.shape, f32)
    for g in range(N_GRP):
        sc = scores[g * EPG:(g + 1) * EPG]
        pick = best == g
        i1 = jnp.where(pick, tops[g][1], i1)
        i2 = jnp.where(pick, tops[g][3], i2)
        s1 = jnp.where(pick, jnp.sum(jnp.where(io == tops[g][1], sc, 0.0), axis=0, keepdims=True), s1)
        s2 = jnp.where(pick, jnp.sum(jnp.where(io == tops[g][3], sc, 0.0), axis=0, keepdims=True), s2)
    eidx_ref[0:1, :] = best * EPG + i1
    eidx_ref[1:2, :] = best * EPG + i2
    tot = s1 + s2
    gw_ref[0:1, :] = s1 / tot
    gw_ref[1:2, :] = s2 / tot


def _outproj(o_gla, o_na, w_out, x2d, mod_l, nrm, router_w, router_b, bsz, seq):
    n = x2d.shape[0]
    tm = TM_PROJ
    nt = seq // tm
    row = lambda b, i: (b * nt + i, 0)
    col = lambda b, i: (0, b * nt + i)
    const = lambda b, i: (0, 0)
    return pl.pallas_call(
        _outproj_kernel,
        grid=(bsz, nt),
        in_specs=[pl.BlockSpec((tm, V_W), row),
                  pl.BlockSpec((tm, NA_W), row),
                  pl.BlockSpec((V_W + NA_W, D), const),
                  pl.BlockSpec((tm, D), row),
                  pl.BlockSpec((6, None, 1, D), lambda b, i: (0, b, 0, 0)),
                  pl.BlockSpec((1, D), const),
                  pl.BlockSpec((N_EXP, D), const),
                  pl.BlockSpec((N_EXP, 1), const)],
        out_specs=[pl.BlockSpec((tm, D), row), pl.BlockSpec((tm, D), row),
                   pl.BlockSpec((2, tm), col), pl.BlockSpec((2, tm), col)],
        out_shape=[jax.ShapeDtypeStruct((n, D), f32), jax.ShapeDtypeStruct((n, D), f32),
                   jax.ShapeDtypeStruct((2, n), i32), jax.ShapeDtypeStruct((2, n), f32)],
        compiler_params=pltpu.CompilerParams(dimension_semantics=("arbitrary", "arbitrary")),
        name="outproj_router",
    )(o_gla, o_na, w_out.astype(bf16), x2d, mod_l, nrm.reshape(1, D),
      router_w.T, router_b.reshape(N_EXP, 1))


def _rank_kernel(eidx_ref, tri_ref, rank_ref, cnt_ref, carry):
    @pl.when(pl.program_id(0) == 0)
    def _():
        carry[...] = jnp.zeros_like(carry)

    e = eidx_ref[...]
    tm = e.shape[1]
    io = lax.broadcasted_iota(i32, (N_EXP, tm), 0)
    run = carry[...]
    for k in range(2):
        oh = io == e[k:k + 1, :]
        ohf = oh.astype(f32)
        pre = jnp.dot(ohf.astype(bf16), tri_ref[...], preferred_element_type=f32) + run[:, 0:1]
        rank_ref[k:k + 1, :] = jnp.sum(jnp.where(oh, pre, 0.0), axis=0, keepdims=True).astype(i32)
        run = run + jnp.sum(ohf, axis=1, keepdims=True)
    carry[...] = run
    cnt_ref[...] = run


def _ranks(eidx):
    n = eidx.shape[1]
    tm = TM_RANK
    t = np.arange(tm)
    tri = jnp.asarray(t[:, None] < t[None, :], bf16)
    return pl.pallas_call(
        _rank_kernel,
        grid=(n // tm,),
        in_specs=[pl.BlockSpec((2, tm), lambda i: (0, i)), pl.BlockSpec((tm, tm), lambda i: (0, 0))],
        out_specs=[pl.BlockSpec((2, tm), lambda i: (0, i)), pl.BlockSpec((N_EXP, 128), lambda i: (0, 0))],
        out_shape=[jax.ShapeDtypeStruct((2, n), i32), jax.ShapeDtypeStruct((N_EXP, 128), f32)],
        scratch_shapes=[pltpu.VMEM((N_EXP, 128), f32)],
        compiler_params=pltpu.CompilerParams(dimension_semantics=("arbitrary",)),
        name="moe_rank",
    )(eidx, tri)


def _dispatch_kernel(dest_ref, tails_ref, h_ref, xb_ref, zbuf, hbuf, lsem, rsem, zsem):
    i = pl.program_id(0)
    nsteps = pl.num_programs(0)
    tm = TM_DISP
    n = nsteps * tm
    nbuf = hbuf.shape[0]

    def load(j):
        start = pl.multiple_of(j * tm, tm)
        return pltpu.make_async_copy(h_ref.at[pl.ds(start, tm)], hbuf.at[j % nbuf], lsem.at[j % nbuf])

    def drain_rows(j):
        for _ in range(2):
            pltpu.make_async_copy(hbuf.at[j % nbuf], xb_ref.at[pl.ds(0, tm)], rsem.at[j % nbuf]).wait()

    def tail_copy(e):
        start = pl.multiple_of(tails_ref[e], MOE_BLK)
        return pltpu.make_async_copy(zbuf, xb_ref.at[pl.ds(start, MOE_BLK)], zsem)

    @pl.when(i == 0)
    def _():
        zbuf[...] = jnp.zeros_like(zbuf)
        for e in range(2 * N_EXP):
            @pl.when(tails_ref[e] >= 0)
            def _():
                tail_copy(e).start()
        for e in range(2 * N_EXP):
            @pl.when(tails_ref[e] >= 0)
            def _():
                tail_copy(e).wait()
        load(i).start()

    @pl.when(i >= nbuf - 1)
    def _():
        drain_rows(i + 1 - nbuf)

    @pl.when(i + 1 < nsteps)
    def _():
        load(i + 1).start()

    load(i).wait()
    slot = i % nbuf

    def issue(t, carry):
        src = hbuf.at[slot, pl.ds(t, 1)]
        for k in range(2):
            d = dest_ref[k * n + i * tm + t]
            pltpu.make_async_copy(src, xb_ref.at[pl.ds(d, 1)], rsem.at[slot]).start()
        return carry

    lax.fori_loop(0, tm, issue, 0, unroll=True)

    @pl.when(i == nsteps - 1)
    def _():
        for back in range(nbuf - 2, -1, -1):
            drain_rows(i - back)


def _dispatch(h2, dest_flat, tails, p_rows):
    n = h2.shape[0]
    tm = TM_DISP
    return pl.pallas_call(
        _dispatch_kernel,
        grid_spec=pltpu.PrefetchScalarGridSpec(
            num_scalar_prefetch=2,
            grid=(n // tm,),
            in_specs=[pl.BlockSpec(memory_space=pl.ANY)],
            out_specs=pl.BlockSpec(memory_space=pl.ANY),
            scratch_shapes=[pltpu.VMEM((MOE_BLK, D), f32), pltpu.VMEM((3, tm, D), f32),
                            pltpu.SemaphoreType.DMA((3,)), pltpu.SemaphoreType.DMA((3,)),
                            pltpu.SemaphoreType.DMA(())],
        ),
        out_shape=jax.ShapeDtypeStruct((p_rows, D), f32),
        compiler_params=pltpu.CompilerParams(dimension_semantics=("arbitrary",)),
        name="moe_dispatch",
    )(dest_flat, tails, h2)


def _expert_kernel(blk_e_ref, nb_ref, xb_ref, w1_ref, w3_ref, w2_ref, y_ref, w1b, w3b, w2b):
    i = pl.program_id(0)

    @pl.when((i < nb_ref[0]) & ((i == 0) | (blk_e_ref[i] != blk_e_ref[jnp.maximum(i - 1, 0)])))
    def _():
        w1b[...] = w1_ref[...].astype(bf16)
        w3b[...] = w3_ref[...].astype(bf16)
        w2b[...] = w2_ref[...].astype(bf16)

    @pl.when(i < nb_ref[0])
    def _():
        x = xb_ref[...].astype(bf16)
        a = jnp.dot(x, w1b[...], preferred_element_type=f32)
        b = jnp.dot(x, w3b[...], preferred_element_type=f32)
        h = (a * jax.nn.sigmoid(a) * b).astype(bf16)
        y_ref[...] = jnp.dot(h, w2b[...], preferred_element_type=f32)

    @pl.when(i >= nb_ref[0])
    def _():
        y_ref[...] = jnp.zeros_like(y_ref)


def _experts(xb, blk_e, nb_used, w1, w3, w2, layer):
    p_rows = xb.shape[0]
    nb = p_rows // MOE_BLK

    def xmap(i, be, nbu):
        return (jnp.minimum(i, nbu[0] - 1), 0)

    def wmap(i, be, nbu):
        return (layer, be[i], 0, 0)

    return pl.pallas_call(
        _expert_kernel,
        grid_spec=pltpu.PrefetchScalarGridSpec(
            num_scalar_prefetch=2,
            grid=(nb,),
            in_specs=[pl.BlockSpec((MOE_BLK, D), xmap),
                      pl.BlockSpec((None, None, D, D_EXP), wmap),
                      pl.BlockSpec((None, None, D, D_EXP), wmap),
                      pl.BlockSpec((None, None, D_EXP, D), wmap)],
            out_specs=pl.BlockSpec((MOE_BLK, D), lambda i, be, nbu: (i, 0)),
            scratch_shapes=[pltpu.VMEM((D, D_EXP), bf16), pltpu.VMEM((D, D_EXP), bf16),
                            pltpu.VMEM((D_EXP, D), bf16)],
        ),
        out_shape=jax.ShapeDtypeStruct((p_rows, D), f32),
        compiler_params=pltpu.CompilerParams(dimension_semantics=("arbitrary",),
                                             vmem_limit_bytes=48 * 1024 * 1024),
        name="moe_experts",
    )(blk_e, nb_used, xb, w1, w3, w2)


def _combine_kernel(dest_ref, x1_ref, gw_ref, mod_ref, y_ref, o_ref, ybuf, sems):
    i = pl.program_id(0)
    nsteps = pl.num_programs(0)
    tm = x1_ref.shape[0]
    n = nsteps * tm

    def issue_tile(j):
        slot = j % 2

        def issue(t, carry):
            for k in range(2):
                d = dest_ref[k * n + j * tm + t]
                pltpu.make_async_copy(y_ref.at[pl.ds(d, 1)], ybuf.at[slot, k, pl.ds(t, 1)],
                                      sems.at[slot]).start()
            return carry

        lax.fori_loop(0, tm, issue, 0, unroll=True)

    @pl.when(i == 0)
    def _():
        issue_tile(i)

    @pl.when(i + 1 < nsteps)
    def _():
        issue_tile(i + 1)

    slot = i % 2
    for k in range(2):
        pltpu.make_async_copy(y_ref.at[pl.ds(0, tm)], ybuf.at[slot, k], sems.at[slot]).wait()
    gw = gw_ref[...]
    y = ybuf[slot, 0] * gw[:, 0:1] + ybuf[slot, 1] * gw[:, 1:2]
    o_ref[...] = x1_ref[...] + mod_ref[5] * y


def _combine(y, dest_flat, x1, gw_t, mod_l, bsz, seq):
    n = x1.shape[0]
    tm = TM_COMB
    nt = seq // tm
    return pl.pallas_call(
        _combine_kernel,
        grid_spec=pltpu.PrefetchScalarGridSpec(
            num_scalar_prefetch=1,
            grid=(n // tm,),
            in_specs=[pl.BlockSpec((tm, D), lambda i, *_: (i, 0)),
                      pl.BlockSpec((tm, 2), lambda i, *_: (i, 0)),
                      pl.BlockSpec((6, None, 1, D), lambda i, *_: (0, i // nt, 0, 0)),
                      pl.BlockSpec(memory_space=pl.ANY)],
            out_specs=pl.BlockSpec((tm, D), lambda i, *_: (i, 0)),
            scratch_shapes=[pltpu.VMEM((2, 2, tm, D), f32), pltpu.SemaphoreType.DMA((2,))],
        ),
        out_shape=jax.ShapeDtypeStruct((n, D), f32),
        compiler_params=pltpu.CompilerParams(dimension_semantics=("arbitrary",)),
        name="moe_combine",
    )(dest_flat, x1, gw_t, mod_l, y)


def _grouped_moe(h2, eidx, gw, x1, mod_l, w1, w3, w2, layer, bsz, seq):
    n = h2.shape[0]
    p_rows = (n * 2 // MOE_BLK + N_EXP) * MOE_BLK
    nb = p_rows // MOE_BLK
    rank, cnt = _ranks(eidx)
    counts = cnt[:, 0].astype(i32)
    padded = (counts + MOE_BLK - 1) // MOE_BLK * MOE_BLK
    pad_end = jnp.cumsum(padded)
    pad_start = pad_end - padded
    start_of = jnp.sum(jnp.where(eidx[..., None] == jnp.arange(N_EXP, dtype=i32), pad_start, 0), axis=-1)
    dest_flat = (start_of + rank).reshape(-1)
    nb_used = (pad_end[-1] // MOE_BLK).astype(i32)
    spare = nb_used + jnp.arange(N_EXP, dtype=i32)
    tails = jnp.concatenate([jnp.where(padded > 0, pad_end - MOE_BLK, -1),
                             jnp.where(spare < nb, spare * MOE_BLK, -1)]).astype(i32)
    blk = jnp.minimum(jnp.arange(nb, dtype=i32), nb_used - 1)
    seg_done = (pad_end[None, :] <= (blk * MOE_BLK)[:, None]).astype(i32)
    blk_e = jnp.minimum(jnp.sum(seg_done, axis=1), N_EXP - 1).astype(i32)
    xb = _dispatch(h2, dest_flat, tails, p_rows)
    y = _experts(xb, blk_e, nb_used.reshape(1), w1, w3, w2, layer)
    return _combine(y, dest_flat, x1, gw.T, mod_l, bsz, seq)


def kernel(x, c, w_ada, b_ada, attn_norm, ffn_norm, w_in, gla_gate_w, gla_gate_b, gla_out_norm,
           na_q_norm, na_k_norm, na_rpb, w_out, router_w, router_b, w1, w3, w2):
    bsz, seq, _ = x.shape
    depth = w_ada.shape[0]
    mod = _adaln_mod(c, w_ada, b_ada)
    xc = x.reshape(bsz * seq, D)
    for l in range(depth):
        (qg, kg, vg, sg, la_f, la_b, qn, kn, vn), bounded_f, bounded_b = _inproj(
            xc, mod[l], attn_norm[l], w_in[l], gla_gate_w[l], gla_gate_b[l], bsz, seq)
        o_f = _gla_dir(False, qg, kg, vg, la_f, bounded_f, bsz, seq)
        o_gla = _gla_dir(True, qg, kg, vg, la_b, bounded_b, bsz, seq, extra=(o_f, sg, gla_out_norm[l]))
        o_na = _natten(qn, kn, vn, na_q_norm[l], na_k_norm[l], na_rpb[l], bsz, seq)
        x1, h2, eidx, gw = _outproj(o_gla, o_na, w_out[l], xc, mod[l], ffn_norm[l],
                                    router_w, router_b, bsz, seq)
        xc = _grouped_moe(h2, eidx, gw, x1, mod[l], w1, w3, w2, l, bsz, seq)
    return xc.reshape(bsz, seq, D)
```

```python
import functools

import numpy as np
import jax
import jax.numpy as jnp
from jax import lax
from jax.experimental import pallas as pl
from jax.experimental.pallas import tpu as pltpu

f32 = jnp.float32
bf16 = jnp.bfloat16
i32 = jnp.int32

D = 1024
GRID_W = 64
GLA_H, GLA_DK, GLA_DV = 4, 64, 128
GLA_RANK = 16
GLA_TAU = 16.0
CHUNK = 64
NA_H, NA_DH = 8, 64
WIN_H, WIN_W = 8, 16
N_EXP, N_GRP, EPG = 32, 4, 8
D_EXP = 512
QK_W = GLA_H * GLA_DK
V_W = GLA_H * GLA_DV
NA_W = NA_H * NA_DH
EPS = 1e-6
NEG = -1e30

TM_PROJ = 1024
TB_GLA = 256
GLA_NB = 4
NA_QROWS = 4
NA_KROWS = 12
NA_SUB = 8
TM_RANK = 1024
TM_DISP = 512
TM_COMB = 512
TM_COMB_PROJ = 512
MOE_BLK = 512
N_LEVELS = 6
GLA_SAFE_RANGE = 40.0


def _nt(a, b):
    return lax.dot_general(a, b, (((1,), (1,)), ((), ())), preferred_element_type=f32)


def _tn(a, b):
    return lax.dot_general(a, b, (((0,), (0,)), ((), ())), preferred_element_type=f32)


def _split3(x):
    hi = x.astype(bf16)
    r = x - hi.astype(f32)
    mid = r.astype(bf16)
    lo = (r - mid.astype(f32)).astype(bf16)
    return hi, mid, lo


def _sel_l(m01, x):
    hi, mid, lo = _split3(x)
    d = lambda p: jnp.dot(m01, p, preferred_element_type=f32)
    return d(hi) + d(mid) + d(lo)


def _sel_r(x, m01):
    hi, mid, lo = _split3(x)
    d = lambda p: jnp.dot(p, m01, preferred_element_type=f32)
    return d(hi) + d(mid) + d(lo)


def _mod_kernel(c_ref, w_ref, b_ref, o_ref):
    c = c_ref[...]
    ca = c * jax.nn.sigmoid(c)
    w = w_ref[0]
    c_hi = ca.astype(bf16)
    c_lo = (ca - c_hi.astype(f32)).astype(bf16)
    w_hi = w.astype(bf16)
    acc = jnp.dot(c_hi, w_hi, preferred_element_type=f32) + jnp.dot(c_lo, w_hi, preferred_element_type=f32)
    o_ref[0, 0] = acc + b_ref[0, 0]


def _adaln_mod(c, w_ada, b_ada):
    depth = w_ada.shape[0]
    bsz = c.shape[0]
    cp = jnp.zeros((8, D), f32).at[:bsz].set(c)
    out = pl.pallas_call(
        _mod_kernel,
        grid=(depth, 6),
        in_specs=[pl.BlockSpec((8, D), lambda l, j: (0, 0)),
                  pl.BlockSpec((1, D, D), lambda l, j: (l, 0, j)),
                  pl.BlockSpec((1, 1, 1, D), lambda l, j: (l, j, 0, 0))],
        out_specs=pl.BlockSpec((1, 1, 8, D), lambda l, j: (l, j, 0, 0)),
        out_shape=jax.ShapeDtypeStruct((depth, 6, 8, D), f32),
        name="adaln_mod",
    )(cp, w_ada, b_ada.reshape(depth, 6, 1, D))
    return out[:, :, :bsz].reshape(depth, 6, bsz, 1, D)


def _log_sigmoid(z):
    return jnp.minimum(z, 0.0) - jnp.log1p(jnp.exp(-jnp.abs(z)))


def _inproj_body(x, mod_ref, nrm_ref, wm_ref, wlr_ref, gf_ref, bf_ref,
                 qg_ref, kg_ref, vg_ref, sg_ref, laf_ref, lab_ref, qn_ref, kn_ref, vn_ref, lamin_ref):
    y = x * lax.rsqrt(jnp.mean(x * x, axis=-1, keepdims=True) + EPS) * nrm_ref[...]
    h = y * (1.0 + mod_ref[1]) + mod_ref[0]
    hb = h.astype(bf16)
    proj = lambda lo, hi: jnp.dot(hb, wm_ref[:, lo:hi], preferred_element_type=f32)
    qg_ref[...] = (proj(0, 256) * (GLA_DK ** -0.5)).astype(bf16)
    kg_ref[...] = proj(256, 512).astype(bf16)
    vg_ref[...] = proj(512, 1024).astype(bf16)
    gg = proj(1024, 1536)
    sg_ref[...] = (gg * jax.nn.sigmoid(gg)).astype(bf16)
    qn_ref[...] = proj(1536, 2048).astype(bf16)
    kn_ref[...] = proj(2048, 2560).astype(bf16)
    vn_ref[...] = proj(2560, 3072).astype(bf16)
    lr =jnp.dot(hb, wlr_ref[...], preferred_element_type=f32)
    z = jnp.dot(lr, gf_ref[...], precision=lax.Precision.HIGHEST, preferred_element_type=f32) + bf_ref[...]
    la = _log_sigmoid(z) * (1.0 / GLA_TAU)
    la_f = la[:, 0:QK_W]
    la_b = la[:, QK_W:2 * QK_W]
    laf_ref[...] = la_f
    lab_ref[...] = la_b
    rows = []
    for la in (la_f, la_b):
        for blk in range(la.shape[0] // TB_GLA):
            tot = jnp.sum(la[blk * TB_GLA:(blk + 1) * TB_GLA], axis=0, keepdims=True)
            rows.append(jnp.broadcast_to(jnp.min(tot, axis=1, keepdims=True), (1, 128)))
    if len(rows) < 8:
        rows.append(jnp.zeros((8 - len(rows), 128), f32))
    lamin_ref[...] = jnp.concatenate(rows, axis=0)


def _inproj_kernel(x_ref, *refs):
    _inproj_body(x_ref[...], *refs)


def _combine_inproj_kernel(dest_ref, x1_ref, gw_ref, modp_ref, y_ref, *refs):
    x2_ref = refs[6]
    ybuf, sems = refs[-2], refs[-1]
    proj_refs = refs[:6] + refs[7:-2]
    i = pl.program_id(0) * pl.num_programs(1) + pl.program_id(1)
    nsteps = pl.num_programs(0) * pl.num_programs(1)
    tm = x1_ref.shape[0]
    n = nsteps * tm

    def issue_tile(j, slot):
        for t in range(tm):
            for k in range(2):
                d = dest_ref[k * n + j * tm + t]
                pltpu.make_async_copy(y_ref.at[pl.ds(d, 1)], ybuf.at[slot, k, pl.ds(t, 1)],
                                      sems.at[slot]).start()

    def wait_tile(slot):
        for k in range(2):
            pltpu.make_async_copy(y_ref.at[pl.ds(0, tm)], ybuf.at[slot, k], sems.at[slot]).wait()

    slot = i % 2

    @pl.when(i == 0)
    def _():
        issue_tile(i, slot)

    wait_tile(slot)
    gw = gw_ref[...]
    y = ybuf[slot, 0] * gw[:, 0:1] + ybuf[slot, 1] * gw[:, 1:2]
    x2 = x1_ref[...] + modp_ref[5] * y
    x2_ref[...] = x2
    issue_tile(jnp.where(i + 1 < nsteps, i + 1, 0), 1 - slot)
    _inproj_body(x2, *proj_refs)

    @pl.when(i == nsteps - 1)
    def _():
        wait_tile(1 - slot)


_PROJ_WIDTHS = [(QK_W, bf16), (QK_W, bf16), (V_W, bf16), (V_W, bf16), (QK_W, f32), (QK_W, f32),
                (NA_W, bf16), (NA_W, bf16), (NA_W, bf16)]


def _inproj_operands(nrm, w_in, gate_w, gate_b, imap):
    lo, hi = 2 * QK_W + 2 * V_W, 2 * QK_W + 2 * V_W + 2 * GLA_RANK
    wm = jnp.concatenate([w_in[:, :lo], w_in[:, hi:]], axis=1).astype(bf16)
    wlr = w_in[:, lo:hi].astype(bf16)
    zpad = jnp.zeros((GLA_RANK, QK_W), f32)
    gmat = jnp.concatenate([jnp.concatenate([gate_w[0], zpad], axis=1),
                            jnp.concatenate([zpad, gate_w[1]], axis=1)], axis=0)
    gbias = jnp.concatenate([gate_b[0], gate_b[1]]).reshape(1, 2 * QK_W)
    args = [nrm.reshape(1, D), wm, wlr, gmat, gbias]
    return args, [pl.BlockSpec(a.shape, imap) for a in args]


def _bounded_flags(lamin, tm):
    per = tm // TB_GLA
    lamin = lamin[:, :2 * per, 0]
    bounded_f = (lamin[:, :per].reshape(-1) > -GLA_SAFE_RANGE).astype(i32)
    bounded_b = (lamin[:, per:].reshape(-1) > -GLA_SAFE_RANGE).astype(i32)
    return bounded_f, bounded_b


def _inproj(x2d, mod_l, nrm, w_in, gate_w, gate_b, bsz, seq):
    n = x2d.shape[0]
    tm = TM_PROJ
    nt = seq // tm
    row = lambda b, i: (b * nt + i, 0)
    w_args, w_specs = _inproj_operands(nrm, w_in, gate_w, gate_b, lambda b, i: (0, 0))
    outs = pl.pallas_call(
        _inproj_kernel,
        grid=(bsz, nt),
        in_specs=[pl.BlockSpec((tm, D), row),
                  pl.BlockSpec((6, None, 1, D), lambda b, i: (0, b, 0, 0))] + w_specs,
        out_specs=[pl.BlockSpec((tm, w), row) for w, _ in _PROJ_WIDTHS]
        + [pl.BlockSpec((None, 8, 128), lambda b, i: (b * nt + i, 0, 0))],
        out_shape=[jax.ShapeDtypeStruct((n, w), dt) for w, dt in _PROJ_WIDTHS]
        + [jax.ShapeDtypeStruct((n // tm, 8, 128), f32)],
        compiler_params=pltpu.CompilerParams(dimension_semantics=("arbitrary", "arbitrary")),
        name="inproj",
    )(x2d, mod_l, *w_args)
    return outs[:-1], *_bounded_flags(outs[-1], tm)


def _combine_inproj(y, dest_flat, x1, gw_t, mod_prev, mod_l, nrm, w_in, gate_w, gate_b, bsz, seq):
    n = x1.shape[0]
    tm = TM_COMB_PROJ
    nt = seq // tm
    row = lambda b, i, *_: (b * nt + i, 0)
    mods = lambda b, i, *_: (0, b, 0, 0)
    w_args, w_specs = _inproj_operands(nrm, w_in, gate_w, gate_b, lambda b, i, *_: (0, 0))
    outs = pl.pallas_call(
        _combine_inproj_kernel,
        grid_spec=pltpu.PrefetchScalarGridSpec(
            num_scalar_prefetch=1,
            grid=(bsz, nt),
            in_specs=[pl.BlockSpec((tm, D), row),
                      pl.BlockSpec((tm, 2), row),
                      pl.BlockSpec((6, None, 1, D), mods),
                      pl.BlockSpec(memory_space=pl.ANY),
                      pl.BlockSpec((6, None, 1, D), mods)] + w_specs,
            out_specs=[pl.BlockSpec((tm, D), row)]
            + [pl.BlockSpec((tm, w), row) for w, _ in _PROJ_WIDTHS]
            + [pl.BlockSpec((None, 8, 128), lambda b, i, *_: (b * nt + i, 0, 0))],
            scratch_shapes=[pltpu.VMEM((2, 2, tm, D), f32), pltpu.SemaphoreType.DMA((2,))],
        ),
        out_shape=[jax.ShapeDtypeStruct((n, D), f32)]
        + [jax.ShapeDtypeStruct((n, w), dt) for w, dt in _PROJ_WIDTHS]
        + [jax.ShapeDtypeStruct((n // tm, 8, 128), f32)],
        compiler_params=pltpu.CompilerParams(dimension_semantics=("arbitrary", "arbitrary"),
                                             vmem_limit_bytes=56 * 1024 * 1024),
        name="combine_inproj",
    )(dest_flat, x1, gw_t, mod_prev, y, mod_l, *w_args)
    return outs[0], outs[1:-1], *_bounded_flags(outs[-1], tm)


def _gla_consts(rev, tb):
    c = CHUNK
    idx = np.arange(c)
    if not rev:
        cum = (idx[None, :] <= idx[:, None]).astype(np.float32)
    else:
        cum = (idx[None, :] >= idx[:, None]).astype(np.float32)
    mats = [cum]
    masks = []
    for lvl in range(N_LEVELS):
        s = c >> (lvl + 1)
        blk = idx // (2 * s)
        second = (idx % (2 * s)) >= s
        ref_row = blk * 2 * s + (s if rev else s - 1)
        mats.append(cum[ref_row])
        same = blk[:, None] == blk[None, :]
        if not rev:
            m = same & second[:, None] & (~second[None, :])
        else:
            m = same & (~second[:, None]) & second[None, :]
        masks.append(m)
    masks.append(np.eye(c, dtype=bool) & (not rev))
    mst = np.concatenate(mats, axis=0)
    msk = np.stack([np.tile(m, (1, GLA_H)) for m in masks]).astype(np.float32)
    tix = np.arange(tb)
    tri = (tix[None, :] >= tix[:, None]) if rev else (tix[None, :] <= tix[:, None])
    cmask = (tix[:, None] < tix[None, :]) if rev else (tix[:, None] >= tix[None, :])
    nch = tb // c
    selb = np.zeros((tb, nch * 128), np.float32)
    for ch in range(nch):
        selb[ch * c:(ch + 1) * c, ch * 128:(ch + 1) * 128] = 1.0
    hk = np.arange(QK_W) // GLA_DK
    hv = np.arange(V_W) // GLA_DV
    kmask = (hk[:, None] == hk[None, :]).astype(np.float32)
    vmask = (hk[:, None] == hv[None, :]).astype(np.float32)
    return (jnp.asarray(mst, bf16), jnp.asarray(msk, f32), jnp.asarray(selb, bf16),
            jnp.asarray(kmask, bf16), jnp.asarray(vmask, bf16), jnp.asarray(vmask, f32),
            jnp.asarray(tri, bf16), jnp.asarray(cmask, f32), jnp.ones((tb, 128), bf16))


def _gla_kernel(rev, tb, *refs):
    if rev:
        (bounded_ref, q_ref, k_ref, v_ref, la_ref, mst_ref, msk_ref, selb_ref, kmask_ref, vmask_ref,
         smask_ref, tri_ref, cmask_ref, ones_ref, of_ref, sg_ref, gn_ref, out_ref, s_scr, o_scr) = refs
    else:
        (bounded_ref, q_ref, k_ref, v_ref, la_ref, mst_ref, msk_ref, selb_ref, kmask_ref, vmask_ref,
         smask_ref, tri_ref, cmask_ref, ones_ref, out_ref, s_scr) = refs
        o_scr = out_ref
    c = CHUNK
    nch = tb // c
    elems = range(GLA_NB)

    @pl.when(pl.program_id(1) == 0)
    def _():
        s_scr[...] = jnp.zeros_like(s_scr)

    def blocks_bounded():
        tri = tri_ref[...]
        cmask = cmask_ref[...]
        las = [la_ref[e] for e in elems]
        tots = [_sel_r(la.T, ones_ref[...]) for la in las]
        bcums = []
        for la in las:
            la_hi = la.astype(bf16)
            la_lo = (la - la_hi.astype(f32)).astype(bf16)
            bcums.append(jnp.dot(tri, la_hi, preferred_element_type=f32)
                         + jnp.dot(tri, la_lo, preferred_element_type=f32))
        ks = [k_ref[e].astype(f32) for e in elems]
        qes = [(q_ref[e].astype(f32) * jnp.exp(bcums[e])).astype(bf16) for e in elems]
        kxs = [(ks[e] * jnp.exp(-bcums[e])).astype(bf16) for e in elems]
        s_prevs = [s_scr[e] for e in elems]
        o_inters = [jnp.dot(qes[e], s_prevs[e].astype(bf16), preferred_element_type=f32) for e in elems]
        lane_head = lax.broadcasted_iota(i32, (tb, QK_W), 1) // GLA_DK
        for hd in range(GLA_H):
            cols = slice(hd * GLA_DV, (hd + 1) * GLA_DV)
            phs = []
            for e in elems:
                qh = jnp.where(lane_head == hd, qes[e], jnp.zeros_like(qes[e]))
                phs.append(jnp.where(cmask > 0.0, _nt(qh, kxs[e]), 0.0).astype(bf16))
            for e in elems:
                o_blk = o_inters[e][:, cols] + jnp.dot(phs[e], v_ref[e, :, cols], preferred_element_type=f32)
                o_scr[e, :, cols] = o_blk.astype(o_scr.dtype)
        for e in elems:
            blast = bcums[e][0:1] if rev else bcums[e][tb - 1:tb]
            ke = (ks[e] * jnp.exp(blast - bcums[e])).astype(bf16)
            dec = jnp.exp(tots[e])
            s_scr[e] = (s_prevs[e] * jnp.concatenate([dec] * GLA_H, axis=1)
                        + _tn(ke, v_ref[e]) * smask_ref[...])

    def chunks_robust():
        mst = mst_ref[...]
        kmask = kmask_ref[...]
        vmask = vmask_ref[...]
        for e in elems:
            tot = _sel_r(la_ref[e].T, selb_ref[...])
            for ch in (range(nch - 1, -1, -1) if rev else range(nch)):
                rows = slice(ch * c, (ch + 1) * c)
                q = q_ref[e, rows, :].astype(f32)
                k = k_ref[e, rows, :].astype(f32)
                v = v_ref[e, rows, :]
                r_all = _sel_l(mst, la_ref[e, rows, :])
                bcum = r_all[0:c]
                qe = (q * jnp.exp(bcum)).astype(bf16)
                o = jnp.dot(qe, s_scr[e].astype(bf16), preferred_element_type=f32)
                sall = jnp.zeros((c, QK_W), f32)
                for lvl in range(N_LEVELS):
                    ref = r_all[(lvl + 1) * c:(lvl + 2) * c]
                    a = (q * jnp.exp(jnp.minimum(bcum - ref, 0.0))).astype(bf16)
                    bm = (k * jnp.exp(jnp.minimum(ref - bcum, 0.0))).astype(bf16)
                    bbd = jnp.concatenate([bm] * GLA_H, axis=0) * kmask
                    sall = sall + jnp.where(msk_ref[lvl] > 0.0, _nt(a, bbd), 0.0)
                if not rev:
                    bbd = jnp.concatenate([k_ref[e, rows, :]] * GLA_H, axis=0) * kmask
                    sall = sall + jnp.where(msk_ref[N_LEVELS] > 0.0, _nt(q_ref[e, rows, :], bbd), 0.0)
                vbd = jnp.concatenate([v] * GLA_H, axis=0) * vmask
                o_blk = o + jnp.dot(sall.astype(bf16), vbd, preferred_element_type=f32)
                o_scr[e, rows, :] = o_blk.astype(o_scr.dtype)
                blast = bcum[0:1] if rev else bcum[c - 1:c]
                ke = (k * jnp.exp(blast - bcum)).astype(bf16)
                dec = jnp.exp(tot[:, ch * 128:(ch + 1) * 128])
                s_scr[e] = (s_scr[e] * jnp.concatenate([dec] * GLA_H, axis=1)
                            + _tn(ke, v) * smask_ref[...])

    nb = pl.num_programs(1)
    pos = nb - 1 - pl.program_id(1) if rev else pl.program_id(1)
    bounded = bounded_ref[(pl.program_id(0) * GLA_NB) * nb + pos] > 0
    for e in range(1, GLA_NB):
        bounded = bounded & (bounded_ref[(pl.program_id(0) * GLA_NB + e) * nb + pos] > 0)
    pl.when(bounded)(blocks_bounded)
    pl.when(jnp.logical_not(bounded))(chunks_robust)

    if rev:
        gn = gn_ref[...]
        for e in elems:
            for hd in range(GLA_H):
                cols = slice(hd * GLA_DV, (hd + 1) * GLA_DV)
                t = of_ref[e, :, cols].astype(f32) + o_scr[e, :, cols]
                y = t * lax.rsqrt(jnp.mean(t * t, axis=-1, keepdims=True) + EPS) * gn
                out_ref[e, :, cols] = (y * sg_ref[e, :, cols].astype(f32)).astype(bf16)


def _gla_dir(rev, q, k, v, la, bounded, bsz, seq, extra=None):
    n = q.shape[0]
    tb = TB_GLA
    nb = seq // tb
    consts = _gla_consts(rev, tb)
    if rev:
        row = lambda b, i, flags: (b, nb - 1 - i, 0)
    else:
        row = lambda b, i, flags: (b, i, 0)
    const2 = lambda b, i, flags: (0, 0)
    const3 = lambda b, i, flags: (0, 0, 0)
    per_batch = lambda a: a.reshape(bsz, seq, a.shape[-1])
    q, k, v, la = per_batch(q), per_batch(k), per_batch(v), per_batch(la)
    in_specs = [pl.BlockSpec((GLA_NB, tb, QK_W), row), pl.BlockSpec((GLA_NB, tb, QK_W), row),
                pl.BlockSpec((GLA_NB, tb, V_W), row), pl.BlockSpec((GLA_NB, tb, QK_W), row),
                pl.BlockSpec(consts[0].shape, const2), pl.BlockSpec(consts[1].shape, const3),
                pl.BlockSpec(consts[2].shape, const2), pl.BlockSpec(consts[3].shape, const2),
                pl.BlockSpec(consts[4].shape, const2), pl.BlockSpec(consts[5].shape, const2),
                pl.BlockSpec(consts[6].shape, const2), pl.BlockSpec(consts[7].shape, const2),
                pl.BlockSpec(consts[8].shape, const2)]
    args = [q, k, v, la, *consts]
    scratch = [pltpu.VMEM((GLA_NB, QK_W, V_W), f32)]
    if rev:
        o_f, sg, gn = extra
        in_specs += [pl.BlockSpec((GLA_NB, tb, V_W), row), pl.BlockSpec((GLA_NB, tb, V_W), row),
                     pl.BlockSpec((1, GLA_DV), const2)]
        args += [per_batch(o_f), per_batch(sg), gn.reshape(1, GLA_DV)]
        scratch.append(pltpu.VMEM((GLA_NB, tb, V_W), f32))
    out = pl.pallas_call(
        functools.partial(_gla_kernel, rev, tb),
        grid_spec=pltpu.PrefetchScalarGridSpec(
            num_scalar_prefetch=1,
            grid=(bsz // GLA_NB, nb),
            in_specs=in_specs,
            out_specs=pl.BlockSpec((GLA_NB, tb, V_W), row),
            scratch_shapes=scratch,
        ),
        out_shape=jax.ShapeDtypeStruct((bsz, seq, V_W), bf16),
        compiler_params=pltpu.CompilerParams(dimension_semantics=("arbitrary", "arbitrary")),
        name="gla_bwd" if rev else "gla_fwd",
    )(bounded, *args)
    return out.reshape(n, V_W)


def _rpb_expand_kernel(rpb_ref, sel_ref, valid_ref, o_ref):
    e = _sel_r(rpb_ref[...], sel_ref[...])
    o_ref[...] = jnp.where(valid_ref[...] > 0.0, e, NEG)


def _na_bias_table(rpb):
    nri, nci = 2 * WIN_H - 1, 2 * WIN_W - 1
    qc = np.arange(GRID_W)
    col_start = np.clip(qc - WIN_W // 2, 0, GRID_W - WIN_W)
    kc = np.arange(GRID_W)
    valid = (kc[None, :] >= col_start[:, None]) & (kc[None, :] < col_start[:, None] + WIN_W)
    ci = kc[None, :] - qc[:, None] + WIN_W - 1
    sel = np.zeros((32, GRID_W * GRID_W), np.float32)
    flat_ci = ci.reshape(-1)
    ok = (flat_ci >= 0) & (flat_ci < nci)
    sel[flat_ci[ok], np.nonzero(ok)[0]] = 1.0
    rpb2 = jnp.zeros((NA_H * nri + 8 - (NA_H * nri) % 8, 32), f32).at[:NA_H * nri, :nci].set(
        rpb.reshape(NA_H * nri, nci))
    nrow = rpb2.shape[0]
    e = pl.pallas_call(
        _rpb_expand_kernel,
        out_shape=jax.ShapeDtypeStruct((nrow, GRID_W * GRID_W), f32),
        name="rpb_expand",
    )(rpb2, jnp.asarray(sel, bf16), jnp.asarray(valid.reshape(1, -1), f32))
    return e[:NA_H * nri].reshape(NA_H // 2, 2, nri, GRID_W, GRID_W)


def _na_row_classes():
    rows = GRID_W
    ri = -np.ones((3, NA_QROWS, NA_KROWS), np.int64)
    for cls, rb in enumerate((0, 1, rows // NA_QROWS - 1)):
        r0 = rb * NA_QROWS
        kb = int(np.clip(r0 - WIN_H // 2, 0, rows - NA_KROWS))
        for rq in range(NA_QROWS):
            r = r0 + rq
            rs = int(np.clip(r - WIN_H // 2, 0, rows - WIN_H))
            for m in range(NA_KROWS):
                kr = kb + m
                if rs <= kr < rs + WIN_H:
                    ri[cls, rq, m] = kr - r + WIN_H - 1
    return ri


def _natten_kernel(q_ref, k_ref, v_ref, qn_ref, kn_ref, gsum_ref, e_ref, o_ref, kn_scr, tbl_scr):
    b = pl.program_id(1)
    rb = pl.program_id(2)
    gsum = gsum_ref[...]
    log2e = 1.4426950408889634

    def headnorm(x, g):
        sq = x * x
        hi = sq.astype(bf16)
        lo = (sq - hi.astype(f32)).astype(bf16)
        ms = (jnp.dot(hi, gsum, preferred_element_type=f32)
              + jnp.dot(lo, gsum, preferred_element_type=f32)) * (1.0 / NA_DH)
        return x * lax.rsqrt(ms + EPS) * g

    @pl.when((b == 0) & (rb == 0))
    def _():
        ri = _na_row_classes()
        neg = jnp.full((GRID_W, GRID_W), NEG, f32)
        for hh in range(2):
            for cls in range(3):
                for rq in range(NA_QROWS):
                    for mp in range(NA_KROWS // 2):
                        parts = []
                        for m in (2 * mp, 2 * mp + 1):
                            r = int(ri[cls, rq, m])
                            parts.append(neg if r < 0 else e_ref[hh, r] * log2e)
                        r0 = (hh * NA_QROWS + rq) * GRID_W
                        tbl_scr[cls, r0:r0 + GRID_W, mp * 128:(mp + 1) * 128] = (
                            jnp.concatenate(parts, axis=1))

    @pl.when(rb == 0)
    def _():
        kn_scr[...] = headnorm(k_ref[...].astype(f32), kn_ref[...]).astype(bf16)

    nq = NA_QROWS * GRID_W
    nk = NA_KROWS * GRID_W
    nblk = GRID_W // NA_QROWS
    first = lax.broadcasted_iota(i32, (nq, 128), 1) < NA_DH
    qs, wins, tbls = [], [], []
    for sub in range(NA_SUB):
        qb = rb * NA_SUB + sub
        q = headnorm(q_ref[sub * nq:(sub + 1) * nq, :].astype(f32), qn_ref[...]) * (NA_DH ** -0.5 * log2e)
        kb = jnp.clip(qb * NA_QROWS - WIN_H // 2, 0, GRID_W - NA_KROWS)
        start = pl.multiple_of(kb * GRID_W, GRID_W)
        cls = jnp.where(qb == 0, 0, jnp.where(qb == nblk - 1, 2, 1))
        qs.append([jnp.where(first, q, 0.0).astype(bf16), jnp.where(first, 0.0, q).astype(bf16)])
        wins.append((kn_scr[pl.ds(start, nk), :], v_ref[pl.ds(start, nk), :]))
        tbls.append(cls)
    scores = [[_nt(qs[sub][hh], wins[sub][0]) + tbl_scr[tbls[sub], hh * nq:(hh + 1) * nq, :]
               for hh in range(2)] for sub in range(NA_SUB)]
    for sub in range(NA_SUB):
        outs = []
        for s in scores[sub]:
            m = jnp.max(s, axis=-1, keepdims=True)
            p = jnp.exp2(s - m)
            l = jnp.sum(p, axis=-1, keepdims=True)
            outs.append(jnp.dot(p.astype(bf16), wins[sub][1], preferred_element_type=f32) / l)
        o_ref[sub * nq:(sub + 1) * nq, :] = jnp.where(first, outs[0], outs[1]).astype(bf16)


def _natten(qn, kn, vn, q_norm, k_norm, rpb, bsz, seq):
    n = qn.shape[0]
    e5 = _na_bias_table(rpb)
    nq = NA_QROWS * GRID_W
    nqs = NA_SUB * nq
    nrb = seq // nqs
    lane_h = np.arange(128) // NA_DH
    gsum = jnp.asarray(lane_h[:, None] == lane_h[None, :], bf16)
    qn2 = jnp.tile(q_norm.reshape(1, NA_DH), (1, 2))
    kn2 = jnp.tile(k_norm.reshape(1, NA_DH), (1, 2))
    const2 = lambda p, b, r: (0, 0)
    return pl.pallas_call(
        _natten_kernel,
        grid=(NA_H // 2, bsz, nrb),
        in_specs=[pl.BlockSpec((nqs, 128), lambda p, b, r: (b * nrb + r, p)),
                  pl.BlockSpec((seq, 128), lambda p, b, r: (b, p)),
                  pl.BlockSpec((seq, 128), lambda p, b, r: (b, p)),
                  pl.BlockSpec((1, 128), const2),
                  pl.BlockSpec((1, 128), const2),
                  pl.BlockSpec((128, 128), const2),
                  pl.BlockSpec((None,) + e5.shape[1:], lambda p, b, r: (p, 0, 0, 0, 0))],
        out_specs=pl.BlockSpec((nqs, 128), lambda p, b, r: (b * nrb + r, p)),
        out_shape=jax.ShapeDtypeStruct((n, NA_W), bf16),
        scratch_shapes=[pltpu.VMEM((seq, 128), bf16),
                        pltpu.VMEM((3, 2 * nq, NA_KROWS * GRID_W), f32)],
        compiler_params=pltpu.CompilerParams(dimension_semantics=("arbitrary", "arbitrary", "arbitrary")),
        name="natten",
    )(qn, kn, vn, qn2, kn2, gsum, e5)


def _top2(vals):
    io = lax.broadcasted_iota(i32, vals.shape, 0)
    m1 = jnp.max(vals, axis=0, keepdims=True)
    i1 = jnp.min(jnp.where(vals == m1, io, EPG), axis=0, keepdims=True)
    v2 = jnp.where(io == i1, -jnp.inf, vals)
    m2 = jnp.max(v2, axis=0, keepdims=True)
    i2 = jnp.min(jnp.where(v2 == m2, io, EPG), axis=0, keepdims=True)
    return m1, i1, m2, i2


def _outproj_kernel(og_ref, on_ref, wo_ref, x_ref, mod_ref, nrm_ref, rw_ref, rb_ref,
                    x1_ref, h2_ref, eidx_ref, gw_ref):
    mix = (jnp.dot(og_ref[...], wo_ref[0:V_W, :], preferred_element_type=f32)
           + jnp.dot(on_ref[...], wo_ref[V_W:V_W + NA_W, :], preferred_element_type=f32))
    x1 = x_ref[...] + mod_ref[2] * mix
    x1_ref[...] = x1
    y = x1 * lax.rsqrt(jnp.mean(x1 * x1, axis=-1, keepdims=True) + EPS) * nrm_ref[...]
    h2 = y * (1.0 + mod_ref[4]) + mod_ref[3]
    h2_ref[...] = h2
    rw = rw_ref[...]
    rw_hi = rw.astype(bf16)
    rw_lo = (rw - rw_hi.astype(f32)).astype(bf16)
    h_hi = h2.astype(bf16)
    h_lo = (h2 - h_hi.astype(f32)).astype(bf16)
    logits = _nt(rw_hi, h_hi) + _nt(rw_hi, h_lo) + _nt(rw_lo, h_hi)
    scores = jax.nn.sigmoid(logits)
    sel = scores + rb_ref[...]
    tops = [_top2(sel[g * EPG:(g + 1) * EPG]) for g in range(N_GRP)]
    best = jnp.zeros_like(tops[0][1])
    bs = tops[0][0] + tops[0][2]
    for g in range(1, N_GRP):
        gs = tops[g][0] + tops[g][2]
        take = gs > bs
        best = jnp.where(take, g, best)
        bs = jnp.where(take, gs, bs)
    io = lax.broadcasted_iota(i32, (EPG, sel.shape[1]), 0)
    i1 = jnp.zeros_like(best)
    i2 = jnp.zeros_like(best)
    s1 = jnp.zeros(best.shape, f32)
    s2 = jnp.zeros(best.shape, f32)
    for g in range(N_GRP):
        sc = scores[g * EPG:(g + 1) * EPG]
        pick = best == g
        i1 = jnp.where(pick, tops[g][1], i1)
        i2 = jnp.where(pick, tops[g][3], i2)
        s1 = jnp.where(pick, jnp.sum(jnp.where(io == tops[g][1], sc, 0.0), axis=0, keepdims=True), s1)
        s2 = jnp.where(pick, jnp.sum(jnp.where(io == tops[g][3], sc, 0.0), axis=0, keepdims=True), s2)
    eidx_ref[0:1, :] = best * EPG + i1
    eidx_ref[1:2, :] = best * EPG + i2
    tot = s1 + s2
    gw_ref[0:1, :] = s1 / tot
    gw_ref[1:2, :] = s2 / tot


def _outproj(o_gla, o_na, w_out, x2d, mod_l, nrm, router_w, router_b, bsz, seq):
    n = x2d.shape[0]
    tm = TM_PROJ
    nt = seq // tm
    row = lambda b, i: (b * nt + i, 0)
    col = lambda b, i: (0, b * nt + i)
    const = lambda b, i: (0, 0)
    return pl.pallas_call(
        _outproj_kernel,
        grid=(bsz, nt),
        in_specs=[pl.BlockSpec((tm, V_W), row),
                  pl.BlockSpec((tm, NA_W), row),
                  pl.BlockSpec((V_W + NA_W, D), const),
                  pl.BlockSpec((tm, D), row),
                  pl.BlockSpec((6, None, 1, D), lambda b, i: (0, b, 0, 0)),
                  pl.BlockSpec((1, D), const),
                  pl.BlockSpec((N_EXP, D), const),
                  pl.BlockSpec((N_EXP, 1), const)],
        out_specs=[pl.BlockSpec((tm, D), row), pl.BlockSpec((tm, D), row),
                   pl.BlockSpec((2, tm), col), pl.BlockSpec((2, tm), col)],
        out_shape=[jax.ShapeDtypeStruct((n, D), f32), jax.ShapeDtypeStruct((n, D), f32),
                   jax.ShapeDtypeStruct((2, n), i32), jax.ShapeDtypeStruct((2, n), f32)],
        compiler_params=pltpu.CompilerParams(dimension_semantics=("arbitrary", "arbitrary")),
        name="outproj_router",
    )(o_gla, o_na, w_out.astype(bf16), x2d, mod_l, nrm.reshape(1, D),
      router_w.T, router_b.reshape(N_EXP, 1))


def _rank_kernel(eidx_ref, tri_ref, rank_ref, cnt_ref, carry):
    @pl.when(pl.program_id(0) == 0)
    def _():
        carry[...] = jnp.zeros_like(carry)

    e = eidx_ref[...]
    tm = e.shape[1]
    io = lax.broadcasted_iota(i32, (N_EXP, tm), 0)
    run = carry[...]
    for k in range(2):
        oh = io == e[k:k + 1, :]
        ohf = oh.astype(f32)
        pre = jnp.dot(ohf.astype(bf16), tri_ref[...], preferred_element_type=f32) + run[:, 0:1]
        rank_ref[k:k + 1, :] = jnp.sum(jnp.where(oh, pre, 0.0), axis=0, keepdims=True).astype(i32)
        run = run + jnp.sum(ohf, axis=1, keepdims=True)
    carry[...] = run
    cnt_ref[...] = run


def _ranks(eidx):
    n = eidx.shape[1]
    tm = TM_RANK
    t = np.arange(tm)
    tri = jnp.asarray(t[:, None] < t[None, :], bf16)
    return pl.pallas_call(
        _rank_kernel,
        grid=(n // tm,),
        in_specs=[pl.BlockSpec((2, tm), lambda i: (0, i)), pl.BlockSpec((tm, tm), lambda i: (0, 0))],
        out_specs=[pl.BlockSpec((2, tm), lambda i: (0, i)), pl.BlockSpec((N_EXP, 128), lambda i: (0, 0))],
        out_shape=[jax.ShapeDtypeStruct((2, n), i32), jax.ShapeDtypeStruct((N_EXP, 128), f32)],
        scratch_shapes=[pltpu.VMEM((N_EXP, 128), f32)],
        compiler_params=pltpu.CompilerParams(dimension_semantics=("arbitrary",)),
        name="moe_rank",
    )(eidx, tri)


def _dispatch_kernel(dest_ref, tails_ref, h_ref, xb_ref, zbuf, hbuf, lsem, rsem, zsem):
    i = pl.program_id(0)
    nsteps = pl.num_programs(0)
    tm = TM_DISP
    n = nsteps * tm
    nbuf = hbuf.shape[0]

    def load(j):
        start = pl.multiple_of(j * tm, tm)
        return pltpu.make_async_copy(h_ref.at[pl.ds(start, tm)], hbuf.at[j % nbuf], lsem.at[j % nbuf])

    def drain_rows(j):
        for _ in range(2):
            pltpu.make_async_copy(hbuf.at[j % nbuf], xb_ref.at[pl.ds(0, tm)], rsem.at[j % nbuf]).wait()

    def tail_copy(e):
        start = pl.multiple_of(tails_ref[e], MOE_BLK)
        return pltpu.make_async_copy(zbuf, xb_ref.at[pl.ds(start, MOE_BLK)], zsem)

    @pl.when(i == 0)
    def _():
        zbuf[...] = jnp.zeros_like(zbuf)
        for e in range(2 * N_EXP):
            @pl.when(tails_ref[e] >= 0)
            def _():
                tail_copy(e).start()
        for e in range(2 * N_EXP):
            @pl.when(tails_ref[e] >= 0)
            def _():
                tail_copy(e).wait()
        load(i).start()

    @pl.when(i >= nbuf - 1)
    def _():
        drain_rows(i + 1 - nbuf)

    @pl.when(i + 1 < nsteps)
    def _():
        load(i + 1).start()

    load(i).wait()
    slot = i % nbuf

    def issue(t, carry):
        src = hbuf.at[slot, pl.ds(t, 1)]
        for k in range(2):
            d = dest_ref[k * n + i * tm + t]
            pltpu.make_async_copy(src, xb_ref.at[pl.ds(d, 1)], rsem.at[slot]).start()
        return carry

    lax.fori_loop(0, tm, issue, 0, unroll=True)

    @pl.when(i == nsteps - 1)
    def _():
        for back in range(nbuf - 2, -1, -1):
            drain_rows(i - back)


def _dispatch(h2, dest_flat, tails, p_rows):
    n = h2.shape[0]
    tm = TM_DISP
    return pl.pallas_call(
        _dispatch_kernel,
        grid_spec=pltpu.PrefetchScalarGridSpec(
            num_scalar_prefetch=2,
            grid=(n // tm,),
            in_specs=[pl.BlockSpec(memory_space=pl.ANY)],
            out_specs=pl.BlockSpec(memory_space=pl.ANY),
            scratch_shapes=[pltpu.VMEM((MOE_BLK, D), f32), pltpu.VMEM((3, tm, D), f32),
                            pltpu.SemaphoreType.DMA((3,)), pltpu.SemaphoreType.DMA((3,)),
                            pltpu.SemaphoreType.DMA(())],
        ),
        out_shape=jax.ShapeDtypeStruct((p_rows, D), f32),
        compiler_params=pltpu.CompilerParams(dimension_semantics=("arbitrary",)),
        name="moe_dispatch",
    )(dest_flat, tails, h2)


def _expert_kernel(blk_e_ref, nb_ref, xb_ref, w1_ref, w3_ref, w2_ref, y_ref, w1b, w3b, w2b):
    i = pl.program_id(0)

    @pl.when((i < nb_ref[0]) & ((i == 0) | (blk_e_ref[i] != blk_e_ref[jnp.maximum(i - 1, 0)])))
    def _():
        w1b[...] = w1_ref[...].astype(bf16)
        w3b[...] = w3_ref[...].astype(bf16)
        w2b[...] = w2_ref[...].astype(bf16)

    @pl.when(i < nb_ref[0])
    def _():
        x = xb_ref[...].astype(bf16)
        a = jnp.dot(x, w1b[...], preferred_element_type=f32)
        b = jnp.dot(x, w3b[...], preferred_element_type=f32)
        h = (a * jax.nn.sigmoid(a) * b).astype(bf16)
        y_ref[...] = jnp.dot(h, w2b[...], preferred_element_type=f32)

    @pl.when(i >= nb_ref[0])
    def _():
        y_ref[...] = jnp.zeros_like(y_ref)


def _experts(xb, blk_e, nb_used, w1, w3, w2, layer):
    p_rows = xb.shape[0]
    nb = p_rows // MOE_BLK

    def xmap(i, be, nbu):
        return (jnp.minimum(i, nbu[0] - 1), 0)

    def wmap(i, be, nbu):
        return (layer, be[i], 0, 0)

    return pl.pallas_call(
        _expert_kernel,
        grid_spec=pltpu.PrefetchScalarGridSpec(
            num_scalar_prefetch=2,
            grid=(nb,),
            in_specs=[pl.BlockSpec((MOE_BLK, D), xmap),
                      pl.BlockSpec((None, None, D, D_EXP), wmap),
                      pl.BlockSpec((None, None, D, D_EXP), wmap),
                      pl.BlockSpec((None, None, D_EXP, D), wmap)],
            out_specs=pl.BlockSpec((MOE_BLK, D), lambda i, be, nbu: (i, 0)),
            scratch_shapes=[pltpu.VMEM((D, D_EXP), bf16), pltpu.VMEM((D, D_EXP), bf16),
                            pltpu.VMEM((D_EXP, D), bf16)],
        ),
        out_shape=jax.ShapeDtypeStruct((p_rows, D), f32),
        compiler_params=pltpu.CompilerParams(dimension_semantics=("arbitrary",),
                                             vmem_limit_bytes=48 * 1024 * 1024),
        name="moe_experts",
    )(blk_e, nb_used, xb, w1, w3, w2)


def _combine_kernel(dest_ref, x1_ref, gw_ref, mod_ref, y_ref, o_ref, ybuf, sems):
    i = pl.program_id(0)
    nsteps = pl.num_programs(0)
    tm = x1_ref.shape[0]
    n = nsteps * tm

    def issue_tile(j):
        slot = j % 2

        def issue(t, carry):
            for k in range(2):
                d = dest_ref[k * n + j * tm + t]
                pltpu.make_async_copy(y_ref.at[pl.ds(d, 1)], ybuf.at[slot, k, pl.ds(t, 1)],
                                      sems.at[slot]).start()
            return carry

        lax.fori_loop(0, tm, issue, 0, unroll=True)

    @pl.when(i == 0)
    def _():
        issue_tile(i)

    @pl.when(i + 1 < nsteps)
    def _():
        issue_tile(i + 1)

    slot = i % 2
    for k in range(2):
        pltpu.make_async_copy(y_ref.at[pl.ds(0, tm)], ybuf.at[slot, k], sems.at[slot]).wait()
    gw = gw_ref[...]
    y = ybuf[slot, 0] * gw[:, 0:1] + ybuf[slot, 1] * gw[:, 1:2]
    o_ref[...] = x1_ref[...] + mod_ref[5] * y


def _combine(y, dest_flat, x1, gw_t, mod_l, bsz, seq):
    n = x1.shape[0]
    tm = TM_COMB
    nt = seq // tm
    return pl.pallas_call(
        _combine_kernel,
        grid_spec=pltpu.PrefetchScalarGridSpec(
            num_scalar_prefetch=1,
            grid=(n // tm,),
            in_specs=[pl.BlockSpec((tm, D), lambda i, *_: (i, 0)),
                      pl.BlockSpec((tm, 2), lambda i, *_: (i, 0)),
                      pl.BlockSpec((6, None, 1, D), lambda i, *_: (0, i // nt, 0, 0)),
                      pl.BlockSpec(memory_space=pl.ANY)],
            out_specs=pl.BlockSpec((tm, D), lambda i, *_: (i, 0)),
            scratch_shapes=[pltpu.VMEM((2, 2, tm, D), f32), pltpu.SemaphoreType.DMA((2,))],
        ),
        out_shape=jax.ShapeDtypeStruct((n, D), f32),
        compiler_params=pltpu.CompilerParams(dimension_semantics=("arbitrary",)),
        name="moe_combine",
    )(dest_flat, x1, gw_t, mod_l, y)


def _moe_experts_sorted(h2, eidx, w1, w3, w2, layer):
    n = h2.shape[0]
    p_rows = (n * 2 // MOE_BLK + N_EXP) * MOE_BLK
    nb = p_rows // MOE_BLK
    rank, cnt = _ranks(eidx)
    counts = cnt[:, 0].astype(i32)
    padded = (counts + MOE_BLK - 1) // MOE_BLK * MOE_BLK
    pad_end = jnp.cumsum(padded)
    pad_start = pad_end - padded
    start_of = jnp.sum(jnp.where(eidx[..., None] == jnp.arange(N_EXP, dtype=i32), pad_start, 0), axis=-1)
    dest_flat = (start_of + rank).reshape(-1)
    nb_used = (pad_end[-1] // MOE_BLK).astype(i32)
    spare = nb_used + jnp.arange(N_EXP, dtype=i32)
    tails = jnp.concatenate([jnp.where(padded > 0, pad_end - MOE_BLK, -1),
                             jnp.where(spare < nb, spare * MOE_BLK, -1)]).astype(i32)
    blk = jnp.minimum(jnp.arange(nb, dtype=i32), nb_used - 1)
    seg_done = (pad_end[None, :] <= (blk * MOE_BLK)[:, None]).astype(i32)
    blk_e = jnp.minimum(jnp.sum(seg_done, axis=1), N_EXP - 1).astype(i32)
    xb = _dispatch(h2, dest_flat, tails, p_rows)
    y = _experts(xb, blk_e, nb_used.reshape(1), w1, w3, w2, layer)
    return y, dest_flat


def kernel(x, c, w_ada, b_ada, attn_norm, ffn_norm, w_in, gla_gate_w, gla_gate_b, gla_out_norm,
           na_q_norm, na_k_norm, na_rpb, w_out, router_w, router_b, w1, w3, w2):
    bsz, seq, _ = x.shape
    depth = w_ada.shape[0]
    mod = _adaln_mod(c, w_ada, b_ada)
    xc = x.reshape(bsz * seq, D)
    pending = None
    for l in range(depth):
        if pending is None:
            (qg, kg, vg, sg, la_f, la_b, qn, kn, vn), bounded_f, bounded_b = _inproj(
                xc, mod[l], attn_norm[l], w_in[l], gla_gate_w[l], gla_gate_b[l], bsz, seq)
        else:
            y, dest_flat, x1, gw_t = pending
            xc, (qg, kg, vg, sg, la_f, la_b, qn, kn, vn), bounded_f, bounded_b = _combine_inproj(
                y, dest_flat, x1, gw_t, mod[l - 1], mod[l], attn_norm[l], w_in[l], gla_gate_w[l],
                gla_gate_b[l], bsz, seq)
        o_f = _gla_dir(False, qg, kg, vg, la_f, bounded_f, bsz, seq)
        o_gla = _gla_dir(True, qg, kg, vg, la_b, bounded_b, bsz, seq, extra=(o_f, sg, gla_out_norm[l]))
        o_na = _natten(qn, kn, vn, na_q_norm[l], na_k_norm[l], na_rpb[l], bsz, seq)
        x1, h2, eidx, gw = _outproj(o_gla, o_na, w_out[l], xc, mod[l], ffn_norm[l],
                                    router_w, router_b, bsz, seq)
        y, dest_flat = _moe_experts_sorted(h2, eidx, w1, w3, w2, l)
        pending = (y, dest_flat, x1, gw.T)
    y, dest_flat, x1, gw_t = pending
    xc = _combine(y, dest_flat, x1, gw_t, mod[depth - 1], bsz, seq)
    return xc.reshape(bsz, seq, D)
```

```python
import functools

import numpy as np
import jax
import jax.numpy as jnp
from jax import lax
from jax.experimental import pallas as pl
from jax.experimental.pallas import tpu as pltpu

f32 = jnp.float32
bf16 = jnp.bfloat16
i32 = jnp.int32

D = 1024
GRID_W = 64
GLA_H, GLA_DK, GLA_DV = 4, 64, 128
GLA_RANK = 16
GLA_TAU = 16.0
CHUNK = 64
NA_H, NA_DH = 8, 64
WIN_H, WIN_W = 8, 16
N_EXP, N_GRP, EPG = 32, 4, 8
D_EXP = 512
QK_W = GLA_H * GLA_DK
V_W = GLA_H * GLA_DV
NA_W = NA_H * NA_DH
EPS = 1e-6
NEG = -1e30

TM_PROJ = 1024
TB_GLA = 256
GLA_NB = 4
NA_QROWS = 4
NA_KROWS = 12
NA_SUB = 8
TM_RANK = 1024
TM_DISP = 512
TM_COMB = 512
TM_COMB_PROJ = 512
MOE_BLK = 512
N_LEVELS = 6
GLA_SAFE_RANGE = 40.0


def _nt(a, b):
    return lax.dot_general(a, b, (((1,), (1,)), ((), ())), preferred_element_type=f32)


def _tn(a, b):
    return lax.dot_general(a, b, (((0,), (0,)), ((), ())), preferred_element_type=f32)


def _split3(x):
    hi = x.astype(bf16)
    r = x - hi.astype(f32)
    mid = r.astype(bf16)
    lo = (r - mid.astype(f32)).astype(bf16)
    return hi, mid, lo


def _sel_l(m01, x):
    hi, mid, lo = _split3(x)
    d = lambda p: jnp.dot(m01, p, preferred_element_type=f32)
    return d(hi) + d(mid) + d(lo)


def _sel_r(x, m01):
    hi, mid, lo = _split3(x)
    d = lambda p: jnp.dot(p, m01, preferred_element_type=f32)
    return d(hi) + d(mid) + d(lo)


def _mod_kernel(c_ref, w_ref, b_ref, o_ref):
    c = c_ref[...]
    ca = c * jax.nn.sigmoid(c)
    w = w_ref[0]
    c_hi = ca.astype(bf16)
    c_lo = (ca - c_hi.astype(f32)).astype(bf16)
    w_hi = w.astype(bf16)
    acc = jnp.dot(c_hi, w_hi, preferred_element_type=f32) + jnp.dot(c_lo, w_hi, preferred_element_type=f32)
    o_ref[0, 0] = acc + b_ref[0, 0]


def _adaln_mod(c, w_ada, b_ada):
    depth = w_ada.shape[0]
    bsz = c.shape[0]
    cp = jnp.zeros((8, D), f32).at[:bsz].set(c)
    out = pl.pallas_call(
        _mod_kernel,
        grid=(depth, 6),
        in_specs=[pl.BlockSpec((8, D), lambda l, j: (0, 0)),
                  pl.BlockSpec((1, D, D), lambda l, j: (l, 0, j)),
                  pl.BlockSpec((1, 1, 1, D), lambda l, j: (l, j, 0, 0))],
        out_specs=pl.BlockSpec((1, 1, 8, D), lambda l, j: (l, j, 0, 0)),
        out_shape=jax.ShapeDtypeStruct((depth, 6, 8, D), f32),
        name="adaln_mod",
    )(cp, w_ada, b_ada.reshape(depth, 6, 1, D))
    return out[:, :, :bsz].reshape(depth, 6, bsz, 1, D)


def _log_sigmoid(z):
    return jnp.minimum(z, 0.0) - jnp.log1p(jnp.exp(-jnp.abs(z)))


def _inproj_body(x, mod_ref, nrm_ref, wm_ref, wlr_ref, gf_ref, bf_ref,
                 qg_ref, kg_ref, vg_ref, sg_ref, laf_ref, lab_ref, qn_ref, kn_ref, vn_ref, lamin_ref):
    y = x * lax.rsqrt(jnp.mean(x * x, axis=-1, keepdims=True) + EPS) * nrm_ref[...]
    h = y * (1.0 + mod_ref[1]) + mod_ref[0]
    hb = h.astype(bf16)
    proj = lambda lo, hi: jnp.dot(hb, wm_ref[:, lo:hi], preferred_element_type=f32)
    qg_ref[...] = (proj(0, 256) * (GLA_DK ** -0.5)).astype(bf16)
    kg_ref[...] = proj(256, 512).astype(bf16)
    vg_ref[...] = proj(512, 1024).astype(bf16)
    gg = proj(1024, 1536)
    sg_ref[...] = (gg * jax.nn.sigmoid(gg)).astype(bf16)
    qn_ref[...] = proj(1536, 2048).astype(bf16)
    kn_ref[...] = proj(2048, 2560).astype(bf16)
    vn_ref[...] = proj(2560, 3072).astype(bf16)
    lr =jnp.dot(hb, wlr_ref[...], preferred_element_type=f32)
    z = jnp.dot(lr, gf_ref[...], precision=lax.Precision.HIGHEST, preferred_element_type=f32) + bf_ref[...]
    la = _log_sigmoid(z) * (1.0 / GLA_TAU)
    la_f = la[:, 0:QK_W]
    la_b = la[:, QK_W:2 * QK_W]
    laf_ref[...] = la_f
    lab_ref[...] = la_b
    rows = []
    for la in (la_f, la_b):
        for blk in range(la.shape[0] // TB_GLA):
            tot = jnp.sum(la[blk * TB_GLA:(blk + 1) * TB_GLA], axis=0, keepdims=True)
            rows.append(jnp.broadcast_to(jnp.min(tot, axis=1, keepdims=True), (1, 128)))
    if len(rows) < 8:
        rows.append(jnp.zeros((8 - len(rows), 128), f32))
    lamin_ref[...] = jnp.concatenate(rows, axis=0)


def _inproj_kernel(x_ref, *refs):
    _inproj_body(x_ref[...], *refs)


def _combine_inproj_kernel(dest_ref, x1_ref, gw_ref, modp_ref, y_ref, *refs):
    x2_ref = refs[6]
    ybuf, sems = refs[-2], refs[-1]
    proj_refs = refs[:6] + refs[7:-2]
    i = pl.program_id(0) * pl.num_programs(1) + pl.program_id(1)
    nsteps = pl.num_programs(0) * pl.num_programs(1)
    tm = x1_ref.shape[0]
    n = nsteps * tm

    nslot = ybuf.shape[0]

    def row_copies(j, slot, t):
        for k in range(2):
            d = dest_ref[k * n + j * tm + t]
            pltpu.make_async_copy(y_ref.at[pl.ds(d, 1)], ybuf.at[slot, k, pl.ds(t, 1)], sems.at[slot]).start()

    def issue_tile(j, slot):
        for t in range(tm):
            row_copies(j, slot, t)

    def wait_tile(slot):
        for k in range(2):
            pltpu.make_async_copy(y_ref.at[pl.ds(0, tm)], ybuf.at[slot, k], sems.at[slot]).wait()

    slot = i % nslot

    @pl.when(i == 0)
    def _():
        for j in range(nslot - 1):
            lax.fori_loop(0, tm, lambda t, c, j=j: (row_copies(j, j, t), c)[1], 0)

    wait_tile(slot)
    gw = gw_ref[...]
    y = ybuf[slot, 0] * gw[:, 0:1] + ybuf[slot, 1] * gw[:, 1:2]
    x2 = x1_ref[...] + modp_ref[5] * y
    x2_ref[...] = x2
    ahead = i + nslot - 1
    issue_tile(jnp.where(ahead < nsteps, ahead, ahead - nsteps), ahead % nslot)
    _inproj_body(x2, *proj_refs)

    @pl.when(i == nsteps - 1)
    def _():
        for back in range(1, nslot):
            wait_tile((i + back) % nslot)


_PROJ_WIDTHS = [(QK_W, bf16), (QK_W, bf16), (V_W, bf16), (V_W, bf16), (QK_W, f32), (QK_W, f32),
                (NA_W, bf16), (NA_W, bf16), (NA_W, bf16)]


def _inproj_operands(nrm, w_in, gate_w, gate_b, imap):
    lo, hi = 2 * QK_W + 2 * V_W, 2 * QK_W + 2 * V_W + 2 * GLA_RANK
    wm = jnp.concatenate([w_in[:, :lo], w_in[:, hi:]], axis=1).astype(bf16)
    wlr = w_in[:, lo:hi].astype(bf16)
    zpad = jnp.zeros((GLA_RANK, QK_W), f32)
    gmat = jnp.concatenate([jnp.concatenate([gate_w[0], zpad], axis=1),
                            jnp.concatenate([zpad, gate_w[1]], axis=1)], axis=0)
    gbias = jnp.concatenate([gate_b[0], gate_b[1]]).reshape(1, 2 * QK_W)
    args = [nrm.reshape(1, D), wm, wlr, gmat, gbias]
    return args, [pl.BlockSpec(a.shape, imap) for a in args]


def _bounded_flags(lamin, tm):
    per = tm // TB_GLA
    lamin = lamin[:, :2 * per, 0]
    bounded_f = (lamin[:, :per].reshape(-1) > -GLA_SAFE_RANGE).astype(i32)
    bounded_b = (lamin[:, per:].reshape(-1) > -GLA_SAFE_RANGE).astype(i32)
    return bounded_f, bounded_b


def _inproj(x2d, mod_l, nrm, w_in, gate_w, gate_b, bsz, seq):
    n = x2d.shape[0]
    tm = TM_PROJ
    nt = seq // tm
    row = lambda b, i: (b * nt + i, 0)
    w_args, w_specs = _inproj_operands(nrm, w_in, gate_w, gate_b, lambda b, i: (0, 0))
    outs = pl.pallas_call(
        _inproj_kernel,
        grid=(bsz, nt),
        in_specs=[pl.BlockSpec((tm, D), row),
                  pl.BlockSpec((6, None, 1, D), lambda b, i: (0, b, 0, 0))] + w_specs,
        out_specs=[pl.BlockSpec((tm, w), row) for w, _ in _PROJ_WIDTHS]
        + [pl.BlockSpec((None, 8, 128), lambda b, i: (b * nt + i, 0, 0))],
        out_shape=[jax.ShapeDtypeStruct((n, w), dt) for w, dt in _PROJ_WIDTHS]
        + [jax.ShapeDtypeStruct((n // tm, 8, 128), f32)],
        compiler_params=pltpu.CompilerParams(dimension_semantics=("arbitrary", "arbitrary")),
        name="inproj",
    )(x2d, mod_l, *w_args)
    return outs[:-1], *_bounded_flags(outs[-1], tm)


def _combine_inproj(y, dest_flat, x1, gw_t, mod_prev, mod_l, nrm, w_in, gate_w, gate_b, bsz, seq):
    n = x1.shape[0]
    tm = TM_COMB_PROJ
    nt = seq // tm
    row = lambda b, i, *_: (b * nt + i, 0)
    mods = lambda b, i, *_: (0, b, 0, 0)
    w_args, w_specs = _inproj_operands(nrm, w_in, gate_w, gate_b, lambda b, i, *_: (0, 0))
    outs = pl.pallas_call(
        _combine_inproj_kernel,
        grid_spec=pltpu.PrefetchScalarGridSpec(
            num_scalar_prefetch=1,
            grid=(bsz, nt),
            in_specs=[pl.BlockSpec((tm, D), row),
                      pl.BlockSpec((tm, 2), row),
                      pl.BlockSpec((6, None, 1, D), mods),
                      pl.BlockSpec(memory_space=pl.ANY),
                      pl.BlockSpec((6, None, 1, D), mods)] + w_specs,
            out_specs=[pl.BlockSpec((tm, D), row)]
            + [pl.BlockSpec((tm, w), row) for w, _ in _PROJ_WIDTHS]
            + [pl.BlockSpec((None, 8, 128), lambda b, i, *_: (b * nt + i, 0, 0))],
            scratch_shapes=[pltpu.VMEM((3, 2, tm, D), f32), pltpu.SemaphoreType.DMA((3,))],
        ),
        out_shape=[jax.ShapeDtypeStruct((n, D), f32)]
        + [jax.ShapeDtypeStruct((n, w), dt) for w, dt in _PROJ_WIDTHS]
        + [jax.ShapeDtypeStruct((n // tm, 8, 128), f32)],
        compiler_params=pltpu.CompilerParams(dimension_semantics=("arbitrary", "arbitrary"),
                                             vmem_limit_bytes=56 * 1024 * 1024),
        name="combine_inproj",
    )(dest_flat, x1, gw_t, mod_prev, y, mod_l, *w_args)
    return outs[0], outs[1:-1], *_bounded_flags(outs[-1], tm)


def _gla_consts(rev, tb):
    c = CHUNK
    idx = np.arange(c)
    if not rev:
        cum = (idx[None, :] <= idx[:, None]).astype(np.float32)
    else:
        cum = (idx[None, :] >= idx[:, None]).astype(np.float32)
    mats = [cum]
    masks = []
    for lvl in range(N_LEVELS):
        s = c >> (lvl + 1)
        blk = idx // (2 * s)
        second = (idx % (2 * s)) >= s
        ref_row = blk * 2 * s + (s if rev else s - 1)
        mats.append(cum[ref_row])
        same = blk[:, None] == blk[None, :]
        if not rev:
            m = same & second[:, None] & (~second[None, :])
        else:
            m = same & (~second[:, None]) & second[None, :]
        masks.append(m)
    masks.append(np.eye(c, dtype=bool) & (not rev))
    mst = np.concatenate(mats, axis=0)
    msk = np.stack([np.tile(m, (1, GLA_H)) for m in masks]).astype(np.float32)
    tix = np.arange(tb)
    tri = (tix[None, :] >= tix[:, None]) if rev else (tix[None, :] <= tix[:, None])
    cmask = (tix[:, None] < tix[None, :]) if rev else (tix[:, None] >= tix[None, :])
    nch = tb // c
    selb = np.zeros((tb, nch * 128), np.float32)
    for ch in range(nch):
        selb[ch * c:(ch + 1) * c, ch * 128:(ch + 1) * 128] = 1.0
    hk = np.arange(QK_W) // GLA_DK
    hv = np.arange(V_W) // GLA_DV
    kmask = (hk[:, None] == hk[None, :]).astype(np.float32)
    vmask = (hk[:, None] == hv[None, :]).astype(np.float32)
    return (jnp.asarray(mst, bf16), jnp.asarray(msk, f32), jnp.asarray(selb, bf16),
            jnp.asarray(kmask, bf16), jnp.asarray(vmask, bf16), jnp.asarray(vmask, f32),
            jnp.asarray(tri, bf16), jnp.asarray(cmask, f32), jnp.ones((tb, 128), bf16))


def _gla_kernel(rev, tb, *refs):
    if rev:
        (bounded_ref, q_ref, k_ref, v_ref, la_ref, mst_ref, msk_ref, selb_ref, kmask_ref, vmask_ref,
         smask_ref, tri_ref, cmask_ref, ones_ref, of_ref, sg_ref, gn_ref, out_ref, s_scr, o_scr) = refs
    else:
        (bounded_ref, q_ref, k_ref, v_ref, la_ref, mst_ref, msk_ref, selb_ref, kmask_ref, vmask_ref,
         smask_ref, tri_ref, cmask_ref, ones_ref, out_ref, s_scr) = refs
        o_scr = out_ref
    c = CHUNK
    nch = tb // c
    elems = range(GLA_NB)

    @pl.when(pl.program_id(1) == 0)
    def _():
        s_scr[...] = jnp.zeros_like(s_scr)

    def blocks_bounded():
        tri = tri_ref[...]
        cmask = cmask_ref[...]
        las = [la_ref[e] for e in elems]
        tots = [_sel_r(la.T, ones_ref[...]) for la in las]
        bcums = []
        for la in las:
            la_hi = la.astype(bf16)
            la_lo = (la - la_hi.astype(f32)).astype(bf16)
            bcums.append(jnp.dot(tri, la_hi, preferred_element_type=f32)
                         + jnp.dot(tri, la_lo, preferred_element_type=f32))
        ks = [k_ref[e].astype(f32) for e in elems]
        qes = [(q_ref[e].astype(f32) * jnp.exp(bcums[e])).astype(bf16) for e in elems]
        kxs = [(ks[e] * jnp.exp(-bcums[e])).astype(bf16) for e in elems]
        s_prevs = [s_scr[e] for e in elems]
        o_inters = [jnp.dot(qes[e], s_prevs[e].astype(bf16), preferred_element_type=f32) for e in elems]
        lane_head = lax.broadcasted_iota(i32, (tb, QK_W), 1) // GLA_DK
        for hd in range(GLA_H):
            cols = slice(hd * GLA_DV, (hd + 1) * GLA_DV)
            phs = []
            for e in elems:
                qh = jnp.where(lane_head == hd, qes[e], jnp.zeros_like(qes[e]))
                phs.append(jnp.where(cmask > 0.0, _nt(qh, kxs[e]), 0.0).astype(bf16))
            for e in elems:
                o_blk = o_inters[e][:, cols] + jnp.dot(phs[e], v_ref[e, :, cols], preferred_element_type=f32)
                o_scr[e, :, cols] = o_blk.astype(o_scr.dtype)
        for e in elems:
            blast = bcums[e][0:1] if rev else bcums[e][tb - 1:tb]
            ke = (ks[e] * jnp.exp(blast - bcums[e])).astype(bf16)
            dec = jnp.exp(tots[e])
            s_scr[e] = (s_prevs[e] * jnp.concatenate([dec] * GLA_H, axis=1)
                        + _tn(ke, v_ref[e]) * smask_ref[...])

    def chunks_robust():
        mst = mst_ref[...]
        kmask = kmask_ref[...]
        vmask = vmask_ref[...]
        for e in elems:
            tot = _sel_r(la_ref[e].T, selb_ref[...])
            for ch in (range(nch - 1, -1, -1) if rev else range(nch)):
                rows = slice(ch * c, (ch + 1) * c)
                q = q_ref[e, rows, :].astype(f32)
                k = k_ref[e, rows, :].astype(f32)
                v = v_ref[e, rows, :]
                r_all = _sel_l(mst, la_ref[e, rows, :])
                bcum = r_all[0:c]
                qe = (q * jnp.exp(bcum)).astype(bf16)
                o = jnp.dot(qe, s_scr[e].astype(bf16), preferred_element_type=f32)
                sall = jnp.zeros((c, QK_W), f32)
                for lvl in range(N_LEVELS):
                    ref = r_all[(lvl + 1) * c:(lvl + 2) * c]
                    a = (q * jnp.exp(jnp.minimum(bcum - ref, 0.0))).astype(bf16)
                    bm = (k * jnp.exp(jnp.minimum(ref - bcum, 0.0))).astype(bf16)
                    bbd = jnp.concatenate([bm] * GLA_H, axis=0) * kmask
                    sall = sall + jnp.where(msk_ref[lvl] > 0.0, _nt(a, bbd), 0.0)
                if not rev:
                    bbd = jnp.concatenate([k_ref[e, rows, :]] * GLA_H, axis=0) * kmask
                    sall = sall + jnp.where(msk_ref[N_LEVELS] > 0.0, _nt(q_ref[e, rows, :], bbd), 0.0)
                vbd = jnp.concatenate([v] * GLA_H, axis=0) * vmask
                o_blk = o + jnp.dot(sall.astype(bf16), vbd, preferred_element_type=f32)
                o_scr[e, rows, :] = o_blk.astype(o_scr.dtype)
                blast = bcum[0:1] if rev else bcum[c - 1:c]
                ke = (k * jnp.exp(blast - bcum)).astype(bf16)
                dec = jnp.exp(tot[:, ch * 128:(ch + 1) * 128])
                s_scr[e] = (s_scr[e] * jnp.concatenate([dec] * GLA_H, axis=1)
                            + _tn(ke, v) * smask_ref[...])

    nb = pl.num_programs(1)
    pos = nb - 1 - pl.program_id(1) if rev else pl.program_id(1)
    bounded = bounded_ref[(pl.program_id(0) * GLA_NB) * nb + pos] > 0
    for e in range(1, GLA_NB):
        bounded = bounded & (bounded_ref[(pl.program_id(0) * GLA_NB + e) * nb + pos] > 0)
    pl.when(bounded)(blocks_bounded)
    pl.when(jnp.logical_not(bounded))(chunks_robust)

    if rev:
        gn = gn_ref[...]
        for e in elems:
            for hd in range(GLA_H):
                cols = slice(hd * GLA_DV, (hd + 1) * GLA_DV)
                t = of_ref[e, :, cols].astype(f32) + o_scr[e, :, cols]
                y = t * lax.rsqrt(jnp.mean(t * t, axis=-1, keepdims=True) + EPS) * gn
                out_ref[e, :, cols] = (y * sg_ref[e, :, cols].astype(f32)).astype(bf16)


def _gla_dir(rev, q, k, v, la, bounded, bsz, seq, extra=None):
    n = q.shape[0]
    tb = TB_GLA
    nb = seq // tb
    consts = _gla_consts(rev, tb)
    if rev:
        row = lambda b, i, flags: (b, nb - 1 - i, 0)
    else:
        row = lambda b, i, flags: (b, i, 0)
    const2 = lambda b, i, flags: (0, 0)
    const3 = lambda b, i, flags: (0, 0, 0)
    per_batch = lambda a: a.reshape(bsz, seq, a.shape[-1])
    q, k, v, la = per_batch(q), per_batch(k), per_batch(v), per_batch(la)
    in_specs = [pl.BlockSpec((GLA_NB, tb, QK_W), row), pl.BlockSpec((GLA_NB, tb, QK_W), row),
                pl.BlockSpec((GLA_NB, tb, V_W), row), pl.BlockSpec((GLA_NB, tb, QK_W), row),
                pl.BlockSpec(consts[0].shape, const2), pl.BlockSpec(consts[1].shape, const3),
                pl.BlockSpec(consts[2].shape, const2), pl.BlockSpec(consts[3].shape, const2),
                pl.BlockSpec(consts[4].shape, const2), pl.BlockSpec(consts[5].shape, const2),
                pl.BlockSpec(consts[6].shape, const2), pl.BlockSpec(consts[7].shape, const2),
                pl.BlockSpec(consts[8].shape, const2)]
    args = [q, k, v, la, *consts]
    scratch = [pltpu.VMEM((GLA_NB, QK_W, V_W), f32)]
    if rev:
        o_f, sg, gn = extra
        in_specs += [pl.BlockSpec((GLA_NB, tb, V_W), row), pl.BlockSpec((GLA_NB, tb, V_W), row),
                     pl.BlockSpec((1, GLA_DV), const2)]
        args += [per_batch(o_f), per_batch(sg), gn.reshape(1, GLA_DV)]
        scratch.append(pltpu.VMEM((GLA_NB, tb, V_W), f32))
    out = pl.pallas_call(
        functools.partial(_gla_kernel, rev, tb),
        grid_spec=pltpu.PrefetchScalarGridSpec(
            num_scalar_prefetch=1,
            grid=(bsz // GLA_NB, nb),
            in_specs=in_specs,
            out_specs=pl.BlockSpec((GLA_NB, tb, V_W), row),
            scratch_shapes=scratch,
        ),
        out_shape=jax.ShapeDtypeStruct((bsz, seq, V_W), bf16),
        compiler_params=pltpu.CompilerParams(dimension_semantics=("arbitrary", "arbitrary")),
        name="gla_bwd" if rev else "gla_fwd",
    )(bounded, *args)
    return out.reshape(n, V_W)


def _rpb_expand_kernel(rpb_ref, sel_ref, valid_ref, o_ref):
    e = _sel_r(rpb_ref[...], sel_ref[...])
    o_ref[...] = jnp.where(valid_ref[...] > 0.0, e, NEG)


def _na_bias_table(rpb):
    nri, nci = 2 * WIN_H - 1, 2 * WIN_W - 1
    qc = np.arange(GRID_W)
    col_start = np.clip(qc - WIN_W // 2, 0, GRID_W - WIN_W)
    kc = np.arange(GRID_W)
    valid = (kc[None, :] >= col_start[:, None]) & (kc[None, :] < col_start[:, None] + WIN_W)
    ci = kc[None, :] - qc[:, None] + WIN_W - 1
    sel = np.zeros((32, GRID_W * GRID_W), np.float32)
    flat_ci = ci.reshape(-1)
    ok = (flat_ci >= 0) & (flat_ci < nci)
    sel[flat_ci[ok], np.nonzero(ok)[0]] = 1.0
    rpb2 = jnp.zeros((NA_H * nri + 8 - (NA_H * nri) % 8, 32), f32).at[:NA_H * nri, :nci].set(
        rpb.reshape(NA_H * nri, nci))
    nrow = rpb2.shape[0]
    e = pl.pallas_call(
        _rpb_expand_kernel,
        out_shape=jax.ShapeDtypeStruct((nrow, GRID_W * GRID_W), f32),
        name="rpb_expand",
    )(rpb2, jnp.asarray(sel, bf16), jnp.asarray(valid.reshape(1, -1), f32))
    return e[:NA_H * nri].reshape(NA_H // 2, 2, nri, GRID_W, GRID_W)


def _na_row_classes():
    rows = GRID_W
    ri = -np.ones((3, NA_QROWS, NA_KROWS), np.int64)
    for cls, rb in enumerate((0, 1, rows // NA_QROWS - 1)):
        r0 = rb * NA_QROWS
        kb = int(np.clip(r0 - WIN_H // 2, 0, rows - NA_KROWS))
        for rq in range(NA_QROWS):
            r = r0 + rq
            rs = int(np.clip(r - WIN_H // 2, 0, rows - WIN_H))
            for m in range(NA_KROWS):
                kr = kb + m
                if rs <= kr < rs + WIN_H:
                    ri[cls, rq, m] = kr - r + WIN_H - 1
    return ri


def _natten_kernel(q_ref, k_ref, v_ref, qn_ref, kn_ref, gsum_ref, e_ref, o_ref, kn_scr, tbl_scr):
    b = pl.program_id(1)
    rb = pl.program_id(2)
    gsum = gsum_ref[...]
    log2e = 1.4426950408889634

    def headnorm(x, g):
        sq = x * x
        hi = sq.astype(bf16)
        lo = (sq - hi.astype(f32)).astype(bf16)
        ms = (jnp.dot(hi, gsum, preferred_element_type=f32)
              + jnp.dot(lo, gsum, preferred_element_type=f32)) * (1.0 / NA_DH)
        return x * lax.rsqrt(ms + EPS) * g

    @pl.when((b == 0) & (rb == 0))
    def _():
        ri = _na_row_classes()
        neg = jnp.full((GRID_W, GRID_W), NEG, f32)
        for hh in range(2):
            for cls in range(3):
                for rq in range(NA_QROWS):
                    for mp in range(NA_KROWS // 2):
                        parts = []
                        for m in (2 * mp, 2 * mp + 1):
                            r = int(ri[cls, rq, m])
                            parts.append(neg if r < 0 else e_ref[hh, r] * log2e)
                        r0 = (hh * NA_QROWS + rq) * GRID_W
                        tbl_scr[cls, r0:r0 + GRID_W, mp * 128:(mp + 1) * 128] = (
                            jnp.concatenate(parts, axis=1))

    @pl.when(rb == 0)
    def _():
        kn_scr[...] = headnorm(k_ref[...].astype(f32), kn_ref[...]).astype(bf16)

    nq = NA_QROWS * GRID_W
    nk = NA_KROWS * GRID_W
    nblk = GRID_W // NA_QROWS
    first = lax.broadcasted_iota(i32, (nq, 128), 1) < NA_DH
    qs, wins, tbls = [], [], []
    for sub in range(NA_SUB):
        qb = rb * NA_SUB + sub
        q = headnorm(q_ref[sub * nq:(sub + 1) * nq, :].astype(f32), qn_ref[...]) * (NA_DH ** -0.5 * log2e)
        kb = jnp.clip(qb * NA_QROWS - WIN_H // 2, 0, GRID_W - NA_KROWS)
        start = pl.multiple_of(kb * GRID_W, GRID_W)
        cls = jnp.where(qb == 0, 0, jnp.where(qb == nblk - 1, 2, 1))
        qs.append([jnp.where(first, q, 0.0).astype(bf16), jnp.where(first, 0.0, q).astype(bf16)])
        wins.append((kn_scr[pl.ds(start, nk), :], v_ref[pl.ds(start, nk), :]))
        tbls.append(cls)
    scores = [[_nt(qs[sub][hh], wins[sub][0]) + tbl_scr[tbls[sub], hh * nq:(hh + 1) * nq, :]
               for hh in range(2)] for sub in range(NA_SUB)]
    for sub in range(NA_SUB):
        outs = []
        for s in scores[sub]:
            m = jnp.max(s, axis=-1, keepdims=True)
            p = jnp.exp2(s - m)
            l = jnp.sum(p, axis=-1, keepdims=True)
            outs.append(jnp.dot(p.astype(bf16), wins[sub][1], preferred_element_type=f32) / l)
        o_ref[sub * nq:(sub + 1) * nq, :] = jnp.where(first, outs[0], outs[1]).astype(bf16)


def _natten(qn, kn, vn, q_norm, k_norm, rpb, bsz, seq):
    n = qn.shape[0]
    e5 = _na_bias_table(rpb)
    nq = NA_QROWS * GRID_W
    nqs = NA_SUB * nq
    nrb = seq // nqs
    lane_h = np.arange(128) // NA_DH
    gsum = jnp.asarray(lane_h[:, None] == lane_h[None, :], bf16)
    qn2 = jnp.tile(q_norm.reshape(1, NA_DH), (1, 2))
    kn2 = jnp.tile(k_norm.reshape(1, NA_DH), (1, 2))
    const2 = lambda p, b, r: (0, 0)
    return pl.pallas_call(
        _natten_kernel,
        grid=(NA_H // 2, bsz, nrb),
        in_specs=[pl.BlockSpec((nqs, 128), lambda p, b, r: (b * nrb + r, p)),
                  pl.BlockSpec((seq, 128), lambda p, b, r: (b, p)),
                  pl.BlockSpec((seq, 128), lambda p, b, r: (b, p)),
                  pl.BlockSpec((1, 128), const2),
                  pl.BlockSpec((1, 128), const2),
                  pl.BlockSpec((128, 128), const2),
                  pl.BlockSpec((None,) + e5.shape[1:], lambda p, b, r: (p, 0, 0, 0, 0))],
        out_specs=pl.BlockSpec((nqs, 128), lambda p, b, r: (b * nrb + r, p)),
        out_shape=jax.ShapeDtypeStruct((n, NA_W), bf16),
        scratch_shapes=[pltpu.VMEM((seq, 128), bf16),
                        pltpu.VMEM((3, 2 * nq, NA_KROWS * GRID_W), f32)],
        compiler_params=pltpu.CompilerParams(dimension_semantics=("arbitrary", "arbitrary", "arbitrary")),
        name="natten",
    )(qn, kn, vn, qn2, kn2, gsum, e5)


def _top2(vals):
    io = lax.broadcasted_iota(i32, vals.shape, 0)
    m1 = jnp.max(vals, axis=0, keepdims=True)
    i1 = jnp.min(jnp.where(vals == m1, io, EPG), axis=0, keepdims=True)
    v2 = jnp.where(io == i1, -jnp.inf, vals)
    m2 = jnp.max(v2, axis=0, keepdims=True)
    i2 = jnp.min(jnp.where(v2 == m2, io, EPG), axis=0, keepdims=True)
    return m1, i1, m2, i2


def _outproj_kernel(og_ref, on_ref, wo_ref, x_ref, mod_ref, nrm_ref, rw_ref, rb_ref,
                    x1_ref, h2_ref, eidx_ref, gw_ref):
    mix = (jnp.dot(og_ref[...], wo_ref[0:V_W, :], preferred_element_type=f32)
           + jnp.dot(on_ref[...], wo_ref[V_W:V_W + NA_W, :], preferred_element_type=f32))
    x1 = x_ref[...] + mod_ref[2] * mix
    x1_ref[...] = x1
    y = x1 * lax.rsqrt(jnp.mean(x1 * x1, axis=-1, keepdims=True) + EPS) * nrm_ref[...]
    h2 = y * (1.0 + mod_ref[4]) + mod_ref[3]
    h2_ref[...] = h2
    rw = rw_ref[...]
    rw_hi = rw.astype(bf16)
    rw_lo = (rw - rw_hi.astype(f32)).astype(bf16)
    h_hi = h2.astype(bf16)
    h_lo = (h2 - h_hi.astype(f32)).astype(bf16)
    logits = _nt(rw_hi, h_hi) + _nt(rw_hi, h_lo) + _nt(rw_lo, h_hi)
    scores = jax.nn.sigmoid(logits)
    sel = scores + rb_ref[...]
    tops = [_top2(sel[g * EPG:(g + 1) * EPG]) for g in range(N_GRP)]
    best = jnp.zeros_like(tops[0][1])
    bs = tops[0][0] + tops[0][2]
    for g in range(1, N_GRP):
        gs = tops[g][0] + tops[g][2]
        take = gs > bs
        best = jnp.where(take, g, best)
        bs = jnp.where(take, gs, bs)
    io = lax.broadcasted_iota(i32, (EPG, sel.shape[1]), 0)
    i1 = jnp.zeros_like(best)
    i2 = jnp.zeros_like(best)
    s1 = jnp.zeros(best.shape, f32)
    s2 = jnp.zeros(best.shape, f32)
    for g in range(N_GRP):
        sc = scores[g * EPG:(g + 1) * EPG]
        pick = best == g
        i1 = jnp.where(pick, tops[g][1], i1)
        i2 = jnp.where(pick, tops[g][3], i2)
        s1 = jnp.where(pick, jnp.sum(jnp.where(io == tops[g][1], sc, 0.0), axis=0, keepdims=True), s1)
        s2 = jnp.where(pick, jnp.sum(jnp.where(io == tops[g][3], sc, 0.0), axis=0, keepdims=True), s2)
    eidx_ref[0:1, :] = best * EPG + i1
    eidx_ref[1:2, :] = best * EPG + i2
    tot = s1 + s2
    gw_ref[0:1, :] = s1 / tot
    gw_ref[1:2, :] = s2 / tot


def _outproj(o_gla, o_na, w_out, x2d, mod_l, nrm, router_w, router_b, bsz, seq):
    n = x2d.shape[0]
    tm = TM_PROJ
    nt = seq // tm
    row = lambda b, i: (b * nt + i, 0)
    col = lambda b, i: (0, b * nt + i)
    const = lambda b, i: (0, 0)
    return pl.pallas_call(
        _outproj_kernel,
        grid=(bsz, nt),
        in_specs=[pl.BlockSpec((tm, V_W), row),
                  pl.BlockSpec((tm, NA_W), row),
                  pl.BlockSpec((V_W + NA_W, D), const),
                  pl.BlockSpec((tm, D), row),
                  pl.BlockSpec((6, None, 1, D), lambda b, i: (0, b, 0, 0)),
                  pl.BlockSpec((1, D), const),
                  pl.BlockSpec((N_EXP, D), const),
                  pl.BlockSpec((N_EXP, 1), const)],
        out_specs=[pl.BlockSpec((tm, D), row), pl.BlockSpec((tm, D), row),
                   pl.BlockSpec((2, tm), col), pl.BlockSpec((2, tm), col)],
        out_shape=[jax.ShapeDtypeStruct((n, D), f32), jax.ShapeDtypeStruct((n, D), f32),
                   jax.ShapeDtypeStruct((2, n), i32), jax.ShapeDtypeStruct((2, n), f32)],
        compiler_params=pltpu.CompilerParams(dimension_semantics=("arbitrary", "arbitrary")),
        name="outproj_router",
    )(o_gla, o_na, w_out.astype(bf16), x2d, mod_l, nrm.reshape(1, D),
      router_w.T, router_b.reshape(N_EXP, 1))


def _rank_kernel(eidx_ref, tri_ref, rank_ref, cnt_ref, carry):
    @pl.when(pl.program_id(0) == 0)
    def _():
        carry[...] = jnp.zeros_like(carry)

    e = eidx_ref[...]
    tm = e.shape[1]
    io = lax.broadcasted_iota(i32, (N_EXP, tm), 0)
    run = carry[...]
    for k in range(2):
        oh = io == e[k:k + 1, :]
        ohf = oh.astype(f32)
        pre = jnp.dot(ohf.astype(bf16), tri_ref[...], preferred_element_type=f32) + run[:, 0:1]
        rank_ref[k:k + 1, :] = jnp.sum(jnp.where(oh, pre, 0.0), axis=0, keepdims=True).astype(i32)
        run = run + jnp.sum(ohf, axis=1, keepdims=True)
    carry[...] = run
    cnt_ref[...] = run


def _ranks(eidx):
    n = eidx.shape[1]
    tm = TM_RANK
    t = np.arange(tm)
    tri = jnp.asarray(t[:, None] < t[None, :], bf16)
    return pl.pallas_call(
        _rank_kernel,
        grid=(n // tm,),
        in_specs=[pl.BlockSpec((2, tm), lambda i: (0, i)), pl.BlockSpec((tm, tm), lambda i: (0, 0))],
        out_specs=[pl.BlockSpec((2, tm), lambda i: (0, i)), pl.BlockSpec((N_EXP, 128), lambda i: (0, 0))],
        out_shape=[jax.ShapeDtypeStruct((2, n), i32), jax.ShapeDtypeStruct((N_EXP, 128), f32)],
        scratch_shapes=[pltpu.VMEM((N_EXP, 128), f32)],
        compiler_params=pltpu.CompilerParams(dimension_semantics=("arbitrary",)),
        name="moe_rank",
    )(eidx, tri)


def _dispatch_kernel(dest_ref, tails_ref, h_ref, xb_ref, zbuf, hbuf, lsem, rsem, zsem):
    i = pl.program_id(0)
    nsteps = pl.num_programs(0)
    tm = TM_DISP
    n = nsteps * tm
    nbuf = hbuf.shape[0]

    def load(j):
        start = pl.multiple_of(j * tm, tm)
        return pltpu.make_async_copy(h_ref.at[pl.ds(start, tm)], hbuf.at[j % nbuf], lsem.at[j % nbuf])

    def drain_rows(j):
        for _ in range(2):
            pltpu.make_async_copy(hbuf.at[j % nbuf], xb_ref.at[pl.ds(0, tm)], rsem.at[j % nbuf]).wait()

    def tail_copy(e):
        start = pl.multiple_of(tails_ref[e], MOE_BLK)
        return pltpu.make_async_copy(zbuf, xb_ref.at[pl.ds(start, MOE_BLK)], zsem)

    @pl.when(i == 0)
    def _():
        zbuf[...] = jnp.zeros_like(zbuf)
        for e in range(2 * N_EXP):
            @pl.when(tails_ref[e] >= 0)
            def _():
                tail_copy(e).start()
        for e in range(2 * N_EXP):
            @pl.when(tails_ref[e] >= 0)
            def _():
                tail_copy(e).wait()
        load(i).start()

    @pl.when(i >= nbuf - 1)
    def _():
        drain_rows(i + 1 - nbuf)

    @pl.when(i + 1 < nsteps)
    def _():
        load(i + 1).start()

    load(i).wait()
    slot = i % nbuf

    def issue(t, carry):
        src = hbuf.at[slot, pl.ds(t, 1)]
        for k in range(2):
            d = dest_ref[k * n + i * tm + t]
            pltpu.make_async_copy(src, xb_ref.at[pl.ds(d, 1)], rsem.at[slot]).start()
        return carry

    lax.fori_loop(0, tm, issue, 0, unroll=True)

    @pl.when(i == nsteps - 1)
    def _():
        for back in range(nbuf - 2, -1, -1):
            drain_rows(i - back)


def _dispatch(h2, dest_flat, tails, p_rows):
    n = h2.shape[0]
    tm = TM_DISP
    return pl.pallas_call(
        _dispatch_kernel,
        grid_spec=pltpu.PrefetchScalarGridSpec(
            num_scalar_prefetch=2,
            grid=(n // tm,),
            in_specs=[pl.BlockSpec(memory_space=pl.ANY)],
            out_specs=pl.BlockSpec(memory_space=pl.ANY),
            scratch_shapes=[pltpu.VMEM((MOE_BLK, D), f32), pltpu.VMEM((3, tm, D), f32),
                            pltpu.SemaphoreType.DMA((3,)), pltpu.SemaphoreType.DMA((3,)),
                            pltpu.SemaphoreType.DMA(())],
        ),
        out_shape=jax.ShapeDtypeStruct((p_rows, D), f32),
        compiler_params=pltpu.CompilerParams(dimension_semantics=("arbitrary",)),
        name="moe_dispatch",
    )(dest_flat, tails, h2)


def _expert_kernel(blk_e_ref, nb_ref, xb_ref, w1_ref, w3_ref, w2_ref, y_ref, w1b, w3b, w2b):
    i = pl.program_id(0)

    @pl.when((i < nb_ref[0]) & ((i == 0) | (blk_e_ref[i] != blk_e_ref[jnp.maximum(i - 1, 0)])))
    def _():
        w1b[...] = w1_ref[...].astype(bf16)
        w3b[...] = w3_ref[...].astype(bf16)
        w2b[...] = w2_ref[...].astype(bf16)

    @pl.when(i < nb_ref[0])
    def _():
        x = xb_ref[...].astype(bf16)
        a = jnp.dot(x, w1b[...], preferred_element_type=f32)
        b = jnp.dot(x, w3b[...], preferred_element_type=f32)
        h = (a * jax.nn.sigmoid(a) * b).astype(bf16)
        y_ref[...] = jnp.dot(h, w2b[...], preferred_element_type=f32)

    @pl.when(i >= nb_ref[0])
    def _():
        y_ref[...] = jnp.zeros_like(y_ref)


def _experts(xb, blk_e, nb_used, w1, w3, w2, layer):
    p_rows = xb.shape[0]
    nb = p_rows // MOE_BLK

    def xmap(i, be, nbu):
        return (jnp.minimum(i, nbu[0] - 1), 0)

    def wmap(i, be, nbu):
        return (layer, be[i], 0, 0)

    return pl.pallas_call(
        _expert_kernel,
        grid_spec=pltpu.PrefetchScalarGridSpec(
            num_scalar_prefetch=2,
            grid=(nb,),
            in_specs=[pl.BlockSpec((MOE_BLK, D), xmap),
                      pl.BlockSpec((None, None, D, D_EXP), wmap),
                      pl.BlockSpec((None, None, D, D_EXP), wmap),
                      pl.BlockSpec((None, None, D_EXP, D), wmap)],
            out_specs=pl.BlockSpec((MOE_BLK, D), lambda i, be, nbu: (i, 0)),
            scratch_shapes=[pltpu.VMEM((D, D_EXP), bf16), pltpu.VMEM((D, D_EXP), bf16),
                            pltpu.VMEM((D_EXP, D), bf16)],
        ),
        out_shape=jax.ShapeDtypeStruct((p_rows, D), f32),
        compiler_params=pltpu.CompilerParams(dimension_semantics=("arbitrary",),
                                             vmem_limit_bytes=48 * 1024 * 1024),
        name="moe_experts",
    )(blk_e, nb_used, xb, w1, w3, w2)


def _combine_kernel(dest_ref, x1_ref, gw_ref, mod_ref, y_ref, o_ref, ybuf, sems):
    i = pl.program_id(0)
    nsteps = pl.num_programs(0)
    tm = x1_ref.shape[0]
    n = nsteps * tm

    def issue_tile(j):
        slot = j % 2

        def issue(t, carry):
            for k in range(2):
                d = dest_ref[k * n + j * tm + t]
                pltpu.make_async_copy(y_ref.at[pl.ds(d, 1)], ybuf.at[slot, k, pl.ds(t, 1)],
                                      sems.at[slot]).start()
            return carry

        lax.fori_loop(0, tm, issue, 0, unroll=True)

    @pl.when(i == 0)
    def _():
        issue_tile(i)

    @pl.when(i + 1 < nsteps)
    def _():
        issue_tile(i + 1)

    slot = i % 2
    for k in range(2):
        pltpu.make_async_copy(y_ref.at[pl.ds(0, tm)], ybuf.at[slot, k], sems.at[slot]).wait()
    gw = gw_ref[...]
    y = ybuf[slot, 0] * gw[:, 0:1] + ybuf[slot, 1] * gw[:, 1:2]
    o_ref[...] = x1_ref[...] + mod_ref[5] * y


def _combine(y, dest_flat, x1, gw_t, mod_l, bsz, seq):
    n = x1.shape[0]
    tm = TM_COMB
    nt = seq // tm
    return pl.pallas_call(
        _combine_kernel,
        grid_spec=pltpu.PrefetchScalarGridSpec(
            num_scalar_prefetch=1,
            grid=(n // tm,),
            in_specs=[pl.BlockSpec((tm, D), lambda i, *_: (i, 0)),
                      pl.BlockSpec((tm, 2), lambda i, *_: (i, 0)),
                      pl.BlockSpec((6, None, 1, D), lambda i, *_: (0, i // nt, 0, 0)),
                      pl.BlockSpec(memory_space=pl.ANY)],
            out_specs=pl.BlockSpec((tm, D), lambda i, *_: (i, 0)),
            scratch_shapes=[pltpu.VMEM((2, 2, tm, D), f32), pltpu.SemaphoreType.DMA((2,))],
        ),
        out_shape=jax.ShapeDtypeStruct((n, D), f32),
        compiler_params=pltpu.CompilerParams(dimension_semantics=("arbitrary",)),
        name="moe_combine",
    )(dest_flat, x1, gw_t, mod_l, y)


def _moe_experts_sorted(h2, eidx, w1, w3, w2, layer):
    n = h2.shape[0]
    p_rows = (n * 2 // MOE_BLK + N_EXP) * MOE_BLK
    nb = p_rows // MOE_BLK
    rank, cnt = _ranks(eidx)
    counts = cnt[:, 0].astype(i32)
    padded = (counts + MOE_BLK - 1) // MOE_BLK * MOE_BLK
    pad_end = jnp.cumsum(padded)
    pad_start = pad_end - padded
    start_of = jnp.sum(jnp.where(eidx[..., None] == jnp.arange(N_EXP, dtype=i32), pad_start, 0), axis=-1)
    dest_flat = (start_of + rank).reshape(-1)
    nb_used = (pad_end[-1] // MOE_BLK).astype(i32)
    spare = nb_used + jnp.arange(N_EXP, dtype=i32)
    tails = jnp.concatenate([jnp.where(padded > 0, pad_end - MOE_BLK, -1),
                             jnp.where(spare < nb, spare * MOE_BLK, -1)]).astype(i32)
    blk = jnp.minimum(jnp.arange(nb, dtype=i32), nb_used - 1)
    seg_done = (pad_end[None, :] <= (blk * MOE_BLK)[:, None]).astype(i32)
    blk_e = jnp.minimum(jnp.sum(seg_done, axis=1), N_EXP - 1).astype(i32)
    xb = _dispatch(h2, dest_flat, tails, p_rows)
    y = _experts(xb, blk_e, nb_used.reshape(1), w1, w3, w2, layer)
    return y, dest_flat


def kernel(x, c, w_ada, b_ada, attn_norm, ffn_norm, w_in, gla_gate_w, gla_gate_b, gla_out_norm,
           na_q_norm, na_k_norm, na_rpb, w_out, router_w, router_b, w1, w3, w2):
    bsz, seq, _ = x.shape
    depth = w_ada.shape[0]
    mod = _adaln_mod(c, w_ada, b_ada)
    xc = x.reshape(bsz * seq, D)
    pending = None
    for l in range(depth):
        if pending is None:
            (qg, kg, vg, sg, la_f, la_b, qn, kn, vn), bounded_f, bounded_b = _inproj(
                xc, mod[l], attn_norm[l], w_in[l], gla_gate_w[l], gla_gate_b[l], bsz, seq)
        else:
            y, dest_flat, x1, gw_t = pending
            xc, (qg, kg, vg, sg, la_f, la_b, qn, kn, vn), bounded_f, bounded_b = _combine_inproj(
                y, dest_flat, x1, gw_t, mod[l - 1], mod[l], attn_norm[l], w_in[l], gla_gate_w[l],
                gla_gate_b[l], bsz, seq)
        o_f = _gla_dir(False, qg, kg, vg, la_f, bounded_f, bsz, seq)
        o_gla = _gla_dir(True, qg, kg, vg, la_b, bounded_b, bsz, seq, extra=(o_f, sg, gla_out_norm[l]))
        o_na = _natten(qn, kn, vn, na_q_norm[l], na_k_norm[l], na_rpb[l], bsz, seq)
        x1, h2, eidx, gw = _outproj(o_gla, o_na, w_out[l], xc, mod[l], ffn_norm[l],
                                    router_w, router_b, bsz, seq)
        y, dest_flat = _moe_experts_sorted(h2, eidx, w1, w3, w2, l)
        pending = (y, dest_flat, x1, gw.T)
    y, dest_flat, x1, gw_t = pending
    xc = _combine(y, dest_flat, x1, gw_t, mod[depth - 1], bsz, seq)
    return xc.reshape(bsz, seq, D)
```

```python
import functools

import numpy as np
import jax
import jax.numpy as jnp
from jax import lax
from jax.experimental import pallas as pl
from jax.experimental.pallas import tpu as pltpu

f32 = jnp.float32
bf16 = jnp.bfloat16
i32 = jnp.int32

D = 1024
GRID_W = 64
GLA_H, GLA_DK, GLA_DV = 4, 64, 128
GLA_RANK = 16
GLA_TAU = 16.0
CHUNK = 64
NA_H, NA_DH = 8, 64
WIN_H, WIN_W = 8, 16
N_EXP, N_GRP, EPG = 32, 4, 8
D_EXP = 512
QK_W = GLA_H * GLA_DK
V_W = GLA_H * GLA_DV
NA_W = NA_H * NA_DH
EPS = 1e-6
NEG = -1e30

TM_PROJ = 1024
TB_GLA = 256
GLA_NB = 4
NA_QROWS = 4
NA_KROWS = 12
NA_SUB = 8
TM_RANK = 1024
TM_DISP = 512
TM_COMB = 512
TM_COMB_PROJ = 512
MOE_BLK = 512
ZERO_ROWS = 64
N_LEVELS = 6
GLA_SAFE_RANGE = 40.0


def _nt(a, b):
    return lax.dot_general(a, b, (((1,), (1,)), ((), ())), preferred_element_type=f32)


def _tn(a, b):
    return lax.dot_general(a, b, (((0,), (0,)), ((), ())), preferred_element_type=f32)


def _split3(x):
    hi = x.astype(bf16)
    r = x - hi.astype(f32)
    mid = r.astype(bf16)
    lo = (r - mid.astype(f32)).astype(bf16)
    return hi, mid, lo


def _sel_l(m01, x):
    hi, mid, lo = _split3(x)
    d = lambda p: jnp.dot(m01, p, preferred_element_type=f32)
    return d(hi) + d(mid) + d(lo)


def _sel_r(x, m01):
    hi, mid, lo = _split3(x)
    d = lambda p: jnp.dot(p, m01, preferred_element_type=f32)
    return d(hi) + d(mid) + d(lo)


def _mod_kernel(c_ref, w_ref, b_ref, o_ref):
    c = c_ref[...]
    ca = c * jax.nn.sigmoid(c)
    w = w_ref[0]
    c_hi = ca.astype(bf16)
    c_lo = (ca - c_hi.astype(f32)).astype(bf16)
    w_hi = w.astype(bf16)
    acc = jnp.dot(c_hi, w_hi, preferred_element_type=f32) + jnp.dot(c_lo, w_hi, preferred_element_type=f32)
    o_ref[0, 0] = acc + b_ref[0, 0]


def _adaln_mod(c, w_ada, b_ada):
    depth = w_ada.shape[0]
    bsz = c.shape[0]
    cp = jnp.zeros((8, D), f32).at[:bsz].set(c)
    out = pl.pallas_call(
        _mod_kernel,
        grid=(depth, 6),
        in_specs=[pl.BlockSpec((8, D), lambda l, j: (0, 0)),
                  pl.BlockSpec((1, D, D), lambda l, j: (l, 0, j)),
                  pl.BlockSpec((1, 1, 1, D), lambda l, j: (l, j, 0, 0))],
        out_specs=pl.BlockSpec((1, 1, 8, D), lambda l, j: (l, j, 0, 0)),
        out_shape=jax.ShapeDtypeStruct((depth, 6, 8, D), f32),
        name="adaln_mod",
    )(cp, w_ada, b_ada.reshape(depth, 6, 1, D))
    return out[:, :, :bsz].reshape(depth, 6, bsz, 1, D)


def _log_sigmoid(z):
    return jnp.minimum(z, 0.0) - jnp.log1p(jnp.exp(-jnp.abs(z)))


def _inproj_body(x, mod_ref, nrm_ref, wm_ref, wlr_ref, gf_ref, bf_ref,
                 qg_ref, kg_ref, vg_ref, sg_ref, laf_ref, lab_ref, qn_ref, kn_ref, vn_ref, lamin_ref):
    y = x * lax.rsqrt(jnp.mean(x * x, axis=-1, keepdims=True) + EPS) * nrm_ref[...]
    h = y * (1.0 + mod_ref[1]) + mod_ref[0]
    hb = h.astype(bf16)
    proj = lambda lo, hi: jnp.dot(hb, wm_ref[:, lo:hi], preferred_element_type=f32)
    qg_ref[...] = (proj(0, 256) * (GLA_DK ** -0.5)).astype(bf16)
    kg_ref[...] = proj(256, 512).astype(bf16)
    vg_ref[...] = proj(512, 1024).astype(bf16)
    gg = proj(1024, 1536)
    sg_ref[...] = (gg * jax.nn.sigmoid(gg)).astype(bf16)
    qn_ref[...] = proj(1536, 2048).astype(bf16)
    kn_ref[...] = proj(2048, 2560).astype(bf16)
    vn_ref[...] = proj(2560, 3072).astype(bf16)
    lr =jnp.dot(hb, wlr_ref[...], preferred_element_type=f32)
    z = jnp.dot(lr, gf_ref[...], precision=lax.Precision.HIGHEST, preferred_element_type=f32) + bf_ref[...]
    la = _log_sigmoid(z) * (1.0 / GLA_TAU)
    la_f = la[:, 0:QK_W]
    la_b = la[:, QK_W:2 * QK_W]
    laf_ref[...] = la_f
    lab_ref[...] = la_b
    rows = []
    for la in (la_f, la_b):
        for blk in range(la.shape[0] // TB_GLA):
            tot = jnp.sum(la[blk * TB_GLA:(blk + 1) * TB_GLA], axis=0, keepdims=True)
            rows.append(jnp.broadcast_to(jnp.min(tot, axis=1, keepdims=True), (1, 128)))
    if len(rows) < 8:
        rows.append(jnp.zeros((8 - len(rows), 128), f32))
    lamin_ref[...] = jnp.concatenate(rows, axis=0)


def _inproj_kernel(x_ref, *refs):
    _inproj_body(x_ref[...], *refs)


def _combine_inproj_kernel(dest_ref, x1_ref, gw_ref, modp_ref, y_ref, *refs):
    x2_ref = refs[6]
    ybuf, sems = refs[-2], refs[-1]
    proj_refs = refs[:6] + refs[7:-2]
    i = pl.program_id(0) * pl.num_programs(1) + pl.program_id(1)
    nsteps = pl.num_programs(0) * pl.num_programs(1)
    tm = x1_ref.shape[0]
    n = nsteps * tm

    nslot = ybuf.shape[0]

    def row_copies(j, slot, t):
        for k in range(2):
            d = dest_ref[k * n + j * tm + t]
            pltpu.make_async_copy(y_ref.at[pl.ds(d, 1)], ybuf.at[slot, k, pl.ds(t, 1)], sems.at[slot]).start()

    def issue_tile(j, slot):
        for t in range(tm):
            row_copies(j, slot, t)

    def wait_tile(slot):
        for k in range(2):
            pltpu.make_async_copy(y_ref.at[pl.ds(0, tm)], ybuf.at[slot, k], sems.at[slot]).wait()

    slot = i % nslot

    @pl.when(i == 0)
    def _():
        for j in range(nslot - 1):
            lax.fori_loop(0, tm, lambda t, c, j=j: (row_copies(j, j, t), c)[1], 0)

    wait_tile(slot)
    gw = gw_ref[...]
    y = ybuf[slot, 0] * gw[:, 0:1] + ybuf[slot, 1] * gw[:, 1:2]
    x2 = x1_ref[...] + modp_ref[5] * y
    x2_ref[...] = x2
    ahead = i + nslot - 1
    issue_tile(jnp.where(ahead < nsteps, ahead, ahead - nsteps), ahead % nslot)
    _inproj_body(x2, *proj_refs)

    @pl.when(i == nsteps - 1)
    def _():
        for back in range(1, nslot):
            wait_tile((i + back) % nslot)


_PROJ_WIDTHS = [(QK_W, bf16), (QK_W, bf16), (V_W, bf16), (V_W, bf16), (QK_W, f32), (QK_W, f32),
                (NA_W, bf16), (NA_W, bf16), (NA_W, bf16)]


def _inproj_operands(nrm, w_in, gate_w, gate_b, imap):
    lo, hi = 2 * QK_W + 2 * V_W, 2 * QK_W + 2 * V_W + 2 * GLA_RANK
    wm = jnp.concatenate([w_in[:, :lo], w_in[:, hi:]], axis=1).astype(bf16)
    wlr = w_in[:, lo:hi].astype(bf16)
    zpad = jnp.zeros((GLA_RANK, QK_W), f32)
    gmat = jnp.concatenate([jnp.concatenate([gate_w[0], zpad], axis=1),
                            jnp.concatenate([zpad, gate_w[1]], axis=1)], axis=0)
    gbias = jnp.concatenate([gate_b[0], gate_b[1]]).reshape(1, 2 * QK_W)
    args = [nrm.reshape(1, D), wm, wlr, gmat, gbias]
    return args, [pl.BlockSpec(a.shape, imap) for a in args]


def _bounded_flags(lamin, tm):
    per = tm // TB_GLA
    lamin = lamin[:, :2 * per, 0]
    bounded_f = (lamin[:, :per].reshape(-1) > -GLA_SAFE_RANGE).astype(i32)
    bounded_b = (lamin[:, per:].reshape(-1) > -GLA_SAFE_RANGE).astype(i32)
    return bounded_f, bounded_b


def _inproj(x2d, mod_l, nrm, w_in, gate_w, gate_b, bsz, seq):
    n = x2d.shape[0]
    tm = TM_PROJ
    nt = seq // tm
    row = lambda b, i: (b * nt + i, 0)
    w_args, w_specs = _inproj_operands(nrm, w_in, gate_w, gate_b, lambda b, i: (0, 0))
    outs = pl.pallas_call(
        _inproj_kernel,
        grid=(bsz, nt),
        in_specs=[pl.BlockSpec((tm, D), row),
                  pl.BlockSpec((6, None, 1, D), lambda b, i: (0, b, 0, 0))] + w_specs,
        out_specs=[pl.BlockSpec((tm, w), row) for w, _ in _PROJ_WIDTHS]
        + [pl.BlockSpec((None, 8, 128), lambda b, i: (b * nt + i, 0, 0))],
        out_shape=[jax.ShapeDtypeStruct((n, w), dt) for w, dt in _PROJ_WIDTHS]
        + [jax.ShapeDtypeStruct((n // tm, 8, 128), f32)],
        compiler_params=pltpu.CompilerParams(dimension_semantics=("arbitrary", "arbitrary")),
        name="inproj",
    )(x2d, mod_l, *w_args)
    return outs[:-1], *_bounded_flags(outs[-1], tm)


def _combine_inproj(y, dest_flat, x1, gw_t, mod_prev, mod_l, nrm, w_in, gate_w, gate_b, bsz, seq):
    n = x1.shape[0]
    tm = TM_COMB_PROJ
    nt = seq // tm
    row = lambda b, i, *_: (b * nt + i, 0)
    mods = lambda b, i, *_: (0, b, 0, 0)
    w_args, w_specs = _inproj_operands(nrm, w_in, gate_w, gate_b, lambda b, i, *_: (0, 0))
    outs = pl.pallas_call(
        _combine_inproj_kernel,
        grid_spec=pltpu.PrefetchScalarGridSpec(
            num_scalar_prefetch=1,
            grid=(bsz, nt),
            in_specs=[pl.BlockSpec((tm, D), row),
                      pl.BlockSpec((tm, 2), row),
                      pl.BlockSpec((6, None, 1, D), mods),
                      pl.BlockSpec(memory_space=pl.ANY),
                      pl.BlockSpec((6, None, 1, D), mods)] + w_specs,
            out_specs=[pl.BlockSpec((tm, D), row)]
            + [pl.BlockSpec((tm, w), row) for w, _ in _PROJ_WIDTHS]
            + [pl.BlockSpec((None, 8, 128), lambda b, i, *_: (b * nt + i, 0, 0))],
            scratch_shapes=[pltpu.VMEM((3, 2, tm, D), f32), pltpu.SemaphoreType.DMA((3,))],
        ),
        out_shape=[jax.ShapeDtypeStruct((n, D), f32)]
        + [jax.ShapeDtypeStruct((n, w), dt) for w, dt in _PROJ_WIDTHS]
        + [jax.ShapeDtypeStruct((n // tm, 8, 128), f32)],
        compiler_params=pltpu.CompilerParams(dimension_semantics=("arbitrary", "arbitrary"),
                                             vmem_limit_bytes=56 * 1024 * 1024),
        name="combine_inproj",
    )(dest_flat, x1, gw_t, mod_prev, y, mod_l, *w_args)
    return outs[0], outs[1:-1], *_bounded_flags(outs[-1], tm)


def _gla_consts(rev, tb):
    c = CHUNK
    idx = np.arange(c)
    if not rev:
        cum = (idx[None, :] <= idx[:, None]).astype(np.float32)
    else:
        cum = (idx[None, :] >= idx[:, None]).astype(np.float32)
    mats = [cum]
    masks = []
    for lvl in range(N_LEVELS):
        s = c >> (lvl + 1)
        blk = idx // (2 * s)
        second = (idx % (2 * s)) >= s
        ref_row = blk * 2 * s + (s if rev else s - 1)
        mats.append(cum[ref_row])
        same = blk[:, None] == blk[None, :]
        if not rev:
            m = same & second[:, None] & (~second[None, :])
        else:
            m = same & (~second[:, None]) & second[None, :]
        masks.append(m)
    masks.append(np.eye(c, dtype=bool) & (not rev))
    mst = np.concatenate(mats, axis=0)
    msk = np.stack([np.tile(m, (1, GLA_H)) for m in masks]).astype(np.float32)
    tix = np.arange(tb)
    tri = (tix[None, :] >= tix[:, None]) if rev else (tix[None, :] <= tix[:, None])
    cmask = (tix[:, None] < tix[None, :]) if rev else (tix[:, None] >= tix[None, :])
    nch = tb // c
    selb = np.zeros((tb, nch * 128), np.float32)
    for ch in range(nch):
        selb[ch * c:(ch + 1) * c, ch * 128:(ch + 1) * 128] = 1.0
    hk = np.arange(QK_W) // GLA_DK
    hv = np.arange(V_W) // GLA_DV
    kmask = (hk[:, None] == hk[None, :]).astype(np.float32)
    vmask = (hk[:, None] == hv[None, :]).astype(np.float32)
    return (jnp.asarray(mst, bf16), jnp.asarray(msk, f32), jnp.asarray(selb, bf16),
            jnp.asarray(kmask, bf16), jnp.asarray(vmask, bf16), jnp.asarray(vmask, f32),
            jnp.asarray(tri, bf16), jnp.asarray(cmask, f32), jnp.ones((tb, 128), bf16))


def _gla_kernel(rev, tb, *refs):
    if rev:
        (bounded_ref, q_ref, k_ref, v_ref, la_ref, mst_ref, msk_ref, selb_ref, kmask_ref, vmask_ref,
         smask_ref, tri_ref, cmask_ref, ones_ref, of_ref, sg_ref, gn_ref, out_ref, s_scr, o_scr) = refs
    else:
        (bounded_ref, q_ref, k_ref, v_ref, la_ref, mst_ref, msk_ref, selb_ref, kmask_ref, vmask_ref,
         smask_ref, tri_ref, cmask_ref, ones_ref, out_ref, s_scr) = refs
        o_scr = out_ref
    c = CHUNK
    nch = tb // c
    elems = range(GLA_NB)

    @pl.when(pl.program_id(1) == 0)
    def _():
        s_scr[...] = jnp.zeros_like(s_scr)

    def blocks_bounded():
        tri = tri_ref[...]
        cmask = cmask_ref[...]
        las = [la_ref[e] for e in elems]
        tots = [_sel_r(la.T, ones_ref[...]) for la in las]
        bcums = []
        for la in las:
            la_hi = la.astype(bf16)
            la_lo = (la - la_hi.astype(f32)).astype(bf16)
            bcums.append(jnp.dot(tri, la_hi, preferred_element_type=f32)
                         + jnp.dot(tri, la_lo, preferred_element_type=f32))
        ks = [k_ref[e].astype(f32) for e in elems]
        qes = [(q_ref[e].astype(f32) * jnp.exp(bcums[e])).astype(bf16) for e in elems]
        kxs = [(ks[e] * jnp.exp(-bcums[e])).astype(bf16) for e in elems]
        s_prevs = [s_scr[e] for e in elems]
        o_inters = [jnp.dot(qes[e], s_prevs[e].astype(bf16), preferred_element_type=f32) for e in elems]
        lane_head = lax.broadcasted_iota(i32, (tb, QK_W), 1) // GLA_DK
        for hd in range(GLA_H):
            cols = slice(hd * GLA_DV, (hd + 1) * GLA_DV)
            phs = []
            for e in elems:
                qh = jnp.where(lane_head == hd, qes[e], jnp.zeros_like(qes[e]))
                phs.append(jnp.where(cmask > 0.0, _nt(qh, kxs[e]), 0.0).astype(bf16))
            for e in elems:
                o_blk = o_inters[e][:, cols] + jnp.dot(phs[e], v_ref[e, :, cols], preferred_element_type=f32)
                o_scr[e, :, cols] = o_blk.astype(o_scr.dtype)
        for e in elems:
            blast = bcums[e][0:1] if rev else bcums[e][tb - 1:tb]
            ke = (ks[e] * jnp.exp(blast - bcums[e])).astype(bf16)
            dec = jnp.exp(tots[e])
            s_scr[e] = (s_prevs[e] * jnp.concatenate([dec] * GLA_H, axis=1)
                        + _tn(ke, v_ref[e]) * smask_ref[...])

    def chunks_robust():
        mst = mst_ref[...]
        kmask = kmask_ref[...]
        vmask = vmask_ref[...]
        for e in elems:
            tot = _sel_r(la_ref[e].T, selb_ref[...])
            for ch in (range(nch - 1, -1, -1) if rev else range(nch)):
                rows = slice(ch * c, (ch + 1) * c)
                q = q_ref[e, rows, :].astype(f32)
                k = k_ref[e, rows, :].astype(f32)
                v = v_ref[e, rows, :]
                r_all = _sel_l(mst, la_ref[e, rows, :])
                bcum = r_all[0:c]
                qe = (q * jnp.exp(bcum)).astype(bf16)
                o = jnp.dot(qe, s_scr[e].astype(bf16), preferred_element_type=f32)
                sall = jnp.zeros((c, QK_W), f32)
                for lvl in range(N_LEVELS):
                    ref = r_all[(lvl + 1) * c:(lvl + 2) * c]
                    a = (q * jnp.exp(jnp.minimum(bcum - ref, 0.0))).astype(bf16)
                    bm = (k * jnp.exp(jnp.minimum(ref - bcum, 0.0))).astype(bf16)
                    bbd = jnp.concatenate([bm] * GLA_H, axis=0) * kmask
                    sall = sall + jnp.where(msk_ref[lvl] > 0.0, _nt(a, bbd), 0.0)
                if not rev:
                    bbd = jnp.concatenate([k_ref[e, rows, :]] * GLA_H, axis=0) * kmask
                    sall = sall + jnp.where(msk_ref[N_LEVELS] > 0.0, _nt(q_ref[e, rows, :], bbd), 0.0)
                vbd = jnp.concatenate([v] * GLA_H, axis=0) * vmask
                o_blk = o + jnp.dot(sall.astype(bf16), vbd, preferred_element_type=f32)
                o_scr[e, rows, :] = o_blk.astype(o_scr.dtype)
                blast = bcum[0:1] if rev else bcum[c - 1:c]
                ke = (k * jnp.exp(blast - bcum)).astype(bf16)
                dec = jnp.exp(tot[:, ch * 128:(ch + 1) * 128])
                s_scr[e] = (s_scr[e] * jnp.concatenate([dec] * GLA_H, axis=1)
                            + _tn(ke, v) * smask_ref[...])

    nb = pl.num_programs(1)
    pos = nb - 1 - pl.program_id(1) if rev else pl.program_id(1)
    bounded = bounded_ref[(pl.program_id(0) * GLA_NB) * nb + pos] > 0
    for e in range(1, GLA_NB):
        bounded = bounded & (bounded_ref[(pl.program_id(0) * GLA_NB + e) * nb + pos] > 0)
    pl.when(bounded)(blocks_bounded)
    pl.when(jnp.logical_not(bounded))(chunks_robust)

    if rev:
        gn = gn_ref[...]
        for e in elems:
            for hd in range(GLA_H):
                cols = slice(hd * GLA_DV, (hd + 1) * GLA_DV)
                t = of_ref[e, :, cols].astype(f32) + o_scr[e, :, cols]
                y = t * lax.rsqrt(jnp.mean(t * t, axis=-1, keepdims=True) + EPS) * gn
                out_ref[e, :, cols] = (y * sg_ref[e, :, cols].astype(f32)).astype(bf16)


def _gla_dir(rev, q, k, v, la, bounded, bsz, seq, extra=None):
    n = q.shape[0]
    tb = TB_GLA
    nb = seq // tb
    consts = _gla_consts(rev, tb)
    if rev:
        row = lambda b, i, flags: (b, nb - 1 - i, 0)
    else:
        row = lambda b, i, flags: (b, i, 0)
    const2 = lambda b, i, flags: (0, 0)
    const3 = lambda b, i, flags: (0, 0, 0)
    per_batch = lambda a: a.reshape(bsz, seq, a.shape[-1])
    q, k, v, la = per_batch(q), per_batch(k), per_batch(v), per_batch(la)
    in_specs = [pl.BlockSpec((GLA_NB, tb, QK_W), row), pl.BlockSpec((GLA_NB, tb, QK_W), row),
                pl.BlockSpec((GLA_NB, tb, V_W), row), pl.BlockSpec((GLA_NB, tb, QK_W), row),
                pl.BlockSpec(consts[0].shape, const2), pl.BlockSpec(consts[1].shape, const3),
                pl.BlockSpec(consts[2].shape, const2), pl.BlockSpec(consts[3].shape, const2),
                pl.BlockSpec(consts[4].shape, const2), pl.BlockSpec(consts[5].shape, const2),
                pl.BlockSpec(consts[6].shape, const2), pl.BlockSpec(consts[7].shape, const2),
                pl.BlockSpec(consts[8].shape, const2)]
    args = [q, k, v, la, *consts]
    scratch = [pltpu.VMEM((GLA_NB, QK_W, V_W), f32)]
    if rev:
        o_f, sg, gn = extra
        in_specs += [pl.BlockSpec((GLA_NB, tb, V_W), row), pl.BlockSpec((GLA_NB, tb, V_W), row),
                     pl.BlockSpec((1, GLA_DV), const2)]
        args += [per_batch(o_f), per_batch(sg), gn.reshape(1, GLA_DV)]
        scratch.append(pltpu.VMEM((GLA_NB, tb, V_W), f32))
    out = pl.pallas_call(
        functools.partial(_gla_kernel, rev, tb),
        grid_spec=pltpu.PrefetchScalarGridSpec(
            num_scalar_prefetch=1,
            grid=(bsz // GLA_NB, nb),
            in_specs=in_specs,
            out_specs=pl.BlockSpec((GLA_NB, tb, V_W), row),
            scratch_shapes=scratch,
        ),
        out_shape=jax.ShapeDtypeStruct((bsz, seq, V_W), bf16),
        compiler_params=pltpu.CompilerParams(dimension_semantics=("arbitrary", "arbitrary")),
        name="gla_bwd" if rev else "gla_fwd",
    )(bounded, *args)
    return out.reshape(n, V_W)


def _rpb_expand_kernel(rpb_ref, sel_ref, valid_ref, o_ref):
    e = _sel_r(rpb_ref[...], sel_ref[...])
    o_ref[...] = jnp.where(valid_ref[...] > 0.0, e, NEG)


def _na_bias_table(rpb):
    nri, nci = 2 * WIN_H - 1, 2 * WIN_W - 1
    qc = np.arange(GRID_W)
    col_start = np.clip(qc - WIN_W // 2, 0, GRID_W - WIN_W)
    kc = np.arange(GRID_W)
    valid = (kc[None, :] >= col_start[:, None]) & (kc[None, :] < col_start[:, None] + WIN_W)
    ci = kc[None, :] - qc[:, None] + WIN_W - 1
    sel = np.zeros((32, GRID_W * GRID_W), np.float32)
    flat_ci = ci.reshape(-1)
    ok = (flat_ci >= 0) & (flat_ci < nci)
    sel[flat_ci[ok], np.nonzero(ok)[0]] = 1.0
    rpb2 = jnp.zeros((NA_H * nri + 8 - (NA_H * nri) % 8, 32), f32).at[:NA_H * nri, :nci].set(
        rpb.reshape(NA_H * nri, nci))
    nrow = rpb2.shape[0]
    e = pl.pallas_call(
        _rpb_expand_kernel,
        out_shape=jax.ShapeDtypeStruct((nrow, GRID_W * GRID_W), f32),
        name="rpb_expand",
    )(rpb2, jnp.asarray(sel, bf16), jnp.asarray(valid.reshape(1, -1), f32))
    return e[:NA_H * nri].reshape(NA_H // 2, 2, nri, GRID_W, GRID_W)


def _na_row_classes():
    rows = GRID_W
    ri = -np.ones((3, NA_QROWS, NA_KROWS), np.int64)
    for cls, rb in enumerate((0, 1, rows // NA_QROWS - 1)):
        r0 = rb * NA_QROWS
        kb = int(np.clip(r0 - WIN_H // 2, 0, rows - NA_KROWS))
        for rq in range(NA_QROWS):
            r = r0 + rq
            rs = int(np.clip(r - WIN_H // 2, 0, rows - WIN_H))
            for m in range(NA_KROWS):
                kr = kb + m
                if rs <= kr < rs + WIN_H:
                    ri[cls, rq, m] = kr - r + WIN_H - 1
    return ri


def _natten_kernel(q_ref, k_ref, v_ref, qn_ref, kn_ref, gsum_ref, e_ref, o_ref, kn_scr, tbl_scr):
    b = pl.program_id(1)
    rb = pl.program_id(2)
    gsum = gsum_ref[...]
    log2e = 1.4426950408889634

    def headnorm(x, g):
        sq = x * x
        hi = sq.astype(bf16)
        lo = (sq - hi.astype(f32)).astype(bf16)
        ms = (jnp.dot(hi, gsum, preferred_element_type=f32)
              + jnp.dot(lo, gsum, preferred_element_type=f32)) * (1.0 / NA_DH)
        return x * lax.rsqrt(ms + EPS) * g

    @pl.when((b == 0) & (rb == 0))
    def _():
        ri = _na_row_classes()
        neg = jnp.full((GRID_W, GRID_W), NEG, f32)
        for hh in range(2):
            for cls in range(3):
                for rq in range(NA_QROWS):
                    for mp in range(NA_KROWS // 2):
                        parts = []
                        for m in (2 * mp, 2 * mp + 1):
                            r = int(ri[cls, rq, m])
                            parts.append(neg if r < 0 else e_ref[hh, r] * log2e)
                        r0 = (hh * NA_QROWS + rq) * GRID_W
                        tbl_scr[cls, r0:r0 + GRID_W, mp * 128:(mp + 1) * 128] = (
                            jnp.concatenate(parts, axis=1))

    @pl.when(rb == 0)
    def _():
        kn_scr[...] = headnorm(k_ref[...].astype(f32), kn_ref[...]).astype(bf16)

    nq = NA_QROWS * GRID_W
    nk = NA_KROWS * GRID_W
    nblk = GRID_W // NA_QROWS
    first = lax.broadcasted_iota(i32, (nq, 128), 1) < NA_DH
    qs, wins, tbls = [], [], []
    for sub in range(NA_SUB):
        qb = rb * NA_SUB + sub
        q = headnorm(q_ref[sub * nq:(sub + 1) * nq, :].astype(f32), qn_ref[...]) * (NA_DH ** -0.5 * log2e)
        kb = jnp.clip(qb * NA_QROWS - WIN_H // 2, 0, GRID_W - NA_KROWS)
        start = pl.multiple_of(kb * GRID_W, GRID_W)
        cls = jnp.where(qb == 0, 0, jnp.where(qb == nblk - 1, 2, 1))
        qs.append([jnp.where(first, q, 0.0).astype(bf16), jnp.where(first, 0.0, q).astype(bf16)])
        wins.append((kn_scr[pl.ds(start, nk), :], v_ref[pl.ds(start, nk), :]))
        tbls.append(cls)
    scores = [[_nt(qs[sub][hh], wins[sub][0]) + tbl_scr[tbls[sub], hh * nq:(hh + 1) * nq, :]
               for hh in range(2)] for sub in range(NA_SUB)]
    for sub in range(NA_SUB):
        outs = []
        for s in scores[sub]:
            m = jnp.max(s, axis=-1, keepdims=True)
            p = jnp.exp2(s - m)
            l = jnp.sum(p, axis=-1, keepdims=True)
            outs.append(jnp.dot(p.astype(bf16), wins[sub][1], preferred_element_type=f32) / l)
        o_ref[sub * nq:(sub + 1) * nq, :] = jnp.where(first, outs[0], outs[1]).astype(bf16)


def _natten(qn, kn, vn, q_norm, k_norm, rpb, bsz, seq):
    n = qn.shape[0]
    e5 = _na_bias_table(rpb)
    nq = NA_QROWS * GRID_W
    nqs = NA_SUB * nq
    nrb = seq // nqs
    lane_h = np.arange(128) // NA_DH
    gsum = jnp.asarray(lane_h[:, None] == lane_h[None, :], bf16)
    qn2 = jnp.tile(q_norm.reshape(1, NA_DH), (1, 2))
    kn2 = jnp.tile(k_norm.reshape(1, NA_DH), (1, 2))
    const2 = lambda p, b, r: (0, 0)
    return pl.pallas_call(
        _natten_kernel,
        grid=(NA_H // 2, bsz, nrb),
        in_specs=[pl.BlockSpec((nqs, 128), lambda p, b, r: (b * nrb + r, p)),
                  pl.BlockSpec((seq, 128), lambda p, b, r: (b, p)),
                  pl.BlockSpec((seq, 128), lambda p, b, r: (b, p)),
                  pl.BlockSpec((1, 128), const2),
                  pl.BlockSpec((1, 128), const2),
                  pl.BlockSpec((128, 128), const2),
                  pl.BlockSpec((None,) + e5.shape[1:], lambda p, b, r: (p, 0, 0, 0, 0))],
        out_specs=pl.BlockSpec((nqs, 128), lambda p, b, r: (b * nrb + r, p)),
        out_shape=jax.ShapeDtypeStruct((n, NA_W), bf16),
        scratch_shapes=[pltpu.VMEM((seq, 128), bf16),
                        pltpu.VMEM((3, 2 * nq, NA_KROWS * GRID_W), f32)],
        compiler_params=pltpu.CompilerParams(dimension_semantics=("arbitrary", "arbitrary", "arbitrary")),
        name="natten",
    )(qn, kn, vn, qn2, kn2, gsum, e5)


def _top2(vals):
    io = lax.broadcasted_iota(i32, vals.shape, 0)
    m1 = jnp.max(vals, axis=0, keepdims=True)
    i1 = jnp.min(jnp.where(vals == m1, io, EPG), axis=0, keepdims=True)
    v2 = jnp.where(io == i1, -jnp.inf, vals)
    m2 = jnp.max(v2, axis=0, keepdims=True)
    i2 = jnp.min(jnp.where(v2 == m2, io, EPG), axis=0, keepdims=True)
    return m1, i1, m2, i2


def _outproj_kernel(og_ref, on_ref, wo_ref, x_ref, mod_ref, nrm_ref, rw_ref, rb_ref,
                    x1_ref, h2_ref, eidx_ref, gw_ref):
    mix = (jnp.dot(og_ref[...], wo_ref[0:V_W, :], preferred_element_type=f32)
           + jnp.dot(on_ref[...], wo_ref[V_W:V_W + NA_W, :], preferred_element_type=f32))
    x1 = x_ref[...] + mod_ref[2] * mix
    x1_ref[...] = x1
    y = x1 * lax.rsqrt(jnp.mean(x1 * x1, axis=-1, keepdims=True) + EPS) * nrm_ref[...]
    h2 = y * (1.0 + mod_ref[4]) + mod_ref[3]
    h2_ref[...] = h2
    rw = rw_ref[...]
    rw_hi = rw.astype(bf16)
    rw_lo = (rw - rw_hi.astype(f32)).astype(bf16)
    h_hi = h2.astype(bf16)
    h_lo = (h2 - h_hi.astype(f32)).astype(bf16)
    logits = _nt(rw_hi, h_hi) + _nt(rw_hi, h_lo) + _nt(rw_lo, h_hi)
    scores = jax.nn.sigmoid(logits)
    sel = scores + rb_ref[...]
    tops = [_top2(sel[g * EPG:(g + 1) * EPG]) for g in range(N_GRP)]
    best = jnp.zeros_like(tops[0][1])
    bs = tops[0][0] + tops[0][2]
    for g in range(1, N_GRP):
        gs = tops[g][0] + tops[g][2]
        take = gs > bs
        best = jnp.where(take, g, best)
        bs = jnp.where(take, gs, bs)
    io = lax.broadcasted_iota(i32, (EPG, sel.shape[1]), 0)
    i1 = jnp.zeros_like(best)
    i2 = jnp.zeros_like(best)
    s1 = jnp.zeros(best.shape, f32)
    s2 = jnp.zeros(best.shape, f32)
    for g in range(N_GRP):
        sc = scores[g * EPG:(g + 1) * EPG]
        pick = best == g
        i1 = jnp.where(pick, tops[g][1], i1)
        i2 = jnp.where(pick, tops[g][3], i2)
        s1 = jnp.where(pick, jnp.sum(jnp.where(io == tops[g][1], sc, 0.0), axis=0, keepdims=True), s1)
        s2 = jnp.where(pick, jnp.sum(jnp.where(io == tops[g][3], sc, 0.0), axis=0, keepdims=True), s2)
    eidx_ref[0:1, :] = best * EPG + i1
    eidx_ref[1:2, :] = best * EPG + i2
    tot = s1 + s2
    gw_ref[0:1, :] = s1 / tot
    gw_ref[1:2, :] = s2 / tot


def _outproj(o_gla, o_na, w_out, x2d, mod_l, nrm, router_w, router_b, bsz, seq):
    n = x2d.shape[0]
    tm = TM_PROJ
    nt = seq // tm
    row = lambda b, i: (b * nt + i, 0)
    col = lambda b, i: (0, b * nt + i)
    const = lambda b, i: (0, 0)
    return pl.pallas_call(
        _outproj_kernel,
        grid=(bsz, nt),
        in_specs=[pl.BlockSpec((tm, V_W), row),
                  pl.BlockSpec((tm, NA_W), row),
                  pl.BlockSpec((V_W + NA_W, D), const),
                  pl.BlockSpec((tm, D), row),
                  pl.BlockSpec((6, None, 1, D), lambda b, i: (0, b, 0, 0)),
                  pl.BlockSpec((1, D), const),
                  pl.BlockSpec((N_EXP, D), const),
                  pl.BlockSpec((N_EXP, 1), const)],
        out_specs=[pl.BlockSpec((tm, D), row), pl.BlockSpec((tm, D), row),
                   pl.BlockSpec((2, tm), col), pl.BlockSpec((2, tm), col)],
        out_shape=[jax.ShapeDtypeStruct((n, D), f32), jax.ShapeDtypeStruct((n, D), f32),
                   jax.ShapeDtypeStruct((2, n), i32), jax.ShapeDtypeStruct((2, n), f32)],
        compiler_params=pltpu.CompilerParams(dimension_semantics=("arbitrary", "arbitrary")),
        name="outproj_router",
    )(o_gla, o_na, w_out.astype(bf16), x2d, mod_l, nrm.reshape(1, D),
      router_w.T, router_b.reshape(N_EXP, 1))


def _rank_kernel(eidx_ref, tri_ref, rank_ref, cnt_ref, carry):
    @pl.when(pl.program_id(0) == 0)
    def _():
        carry[...] = jnp.zeros_like(carry)

    e = eidx_ref[...]
    tm = e.shape[1]
    io = lax.broadcasted_iota(i32, (N_EXP, tm), 0)
    run = carry[...]
    for k in range(2):
        oh = io == e[k:k + 1, :]
        ohf = oh.astype(f32)
        pre = jnp.dot(ohf.astype(bf16), tri_ref[...], preferred_element_type=f32) + run[:, 0:1]
        rank_ref[k:k + 1, :] = jnp.sum(jnp.where(oh, pre, 0.0), axis=0, keepdims=True).astype(i32)
        run = run + jnp.sum(ohf, axis=1, keepdims=True)
    carry[...] = run
    cnt_ref[...] = run


def _ranks(eidx):
    n = eidx.shape[1]
    tm = TM_RANK
    t = np.arange(tm)
    tri = jnp.asarray(t[:, None] < t[None, :], bf16)
    return pl.pallas_call(
        _rank_kernel,
        grid=(n // tm,),
        in_specs=[pl.BlockSpec((2, tm), lambda i: (0, i)), pl.BlockSpec((tm, tm), lambda i: (0, 0))],
        out_specs=[pl.BlockSpec((2, tm), lambda i: (0, i)), pl.BlockSpec((N_EXP, 128), lambda i: (0, 0))],
        out_shape=[jax.ShapeDtypeStruct((2, n), i32), jax.ShapeDtypeStruct((N_EXP, 128), f32)],
        scratch_shapes=[pltpu.VMEM((N_EXP, 128), f32)],
        compiler_params=pltpu.CompilerParams(dimension_semantics=("arbitrary",)),
        name="moe_rank",
    )(eidx, tri)


def _dispatch_kernel(dest_ref, tails_ref, h_ref, xb_ref, zbuf, hbuf, lsem, rsem, zsem, ssem):
    i = pl.program_id(0)
    nsteps = pl.num_programs(0)
    tm = TM_DISP
    n = nsteps * tm
    nbuf = hbuf.shape[0]

    def load(j):
        start = pl.multiple_of(j * tm, tm)
        return pltpu.make_async_copy(h_ref.at[pl.ds(start, tm)], hbuf.at[j % nbuf], lsem.at[j % nbuf])

    def drain_rows(j):
        for _ in range(2):
            pltpu.make_async_copy(hbuf.at[j % nbuf], xb_ref.at[pl.ds(0, tm)], rsem.at[j % nbuf]).wait()

    def pad_copy(chunk):
        start = pl.multiple_of(chunk * ZERO_ROWS, ZERO_ROWS)
        return pltpu.make_async_copy(zbuf.at[pl.ds(0, ZERO_ROWS)], xb_ref.at[pl.ds(start, ZERO_ROWS)], zsem)

    def spare_copy(j):
        start = pl.multiple_of(tails_ref[2 * N_EXP + j], MOE_BLK)
        return pltpu.make_async_copy(zbuf, xb_ref.at[pl.ds(start, MOE_BLK)], ssem)

    def for_pad_chunks(fn):
        for e in range(N_EXP):
            lax.fori_loop(tails_ref[e], tails_ref[N_EXP + e], lambda ch, c: (fn(ch), c)[1], 0)

    def for_spare_blocks(fn):
        for j in range(N_EXP):
            @pl.when(tails_ref[2 * N_EXP + j] >= 0)
            def _():
                fn(j)

    @pl.when(i == 0)
    def _():
        zbuf[...] = jnp.zeros_like(zbuf)
        for_pad_chunks(lambda ch: pad_copy(ch).start())
        for_spare_blocks(lambda j: spare_copy(j).start())
        for_pad_chunks(lambda ch: pad_copy(ch).wait())
        load(i).start()

    @pl.when(i >= nbuf - 1)
    def _():
        drain_rows(i + 1 - nbuf)

    @pl.when(i + 1 < nsteps)
    def _():
        load(i + 1).start()

    load(i).wait()
    slot = i % nbuf

    def issue(t, carry):
        src = hbuf.at[slot, pl.ds(t, 1)]
        for k in range(2):
            d = dest_ref[k * n + i * tm + t]
            pltpu.make_async_copy(src, xb_ref.at[pl.ds(d, 1)], rsem.at[slot]).start()
        return carry

    lax.fori_loop(0, tm, issue, 0, unroll=True)

    @pl.when(i == nsteps - 1)
    def _():
        for back in range(nbuf - 2, -1, -1):
            drain_rows(i - back)
        for_spare_blocks(lambda j: spare_copy(j).wait())


def _dispatch(h2, dest_flat, tails, p_rows):
    n = h2.shape[0]
    tm = TM_DISP
    return pl.pallas_call(
        _dispatch_kernel,
        grid_spec=pltpu.PrefetchScalarGridSpec(
            num_scalar_prefetch=2,
            grid=(n // tm,),
            in_specs=[pl.BlockSpec(memory_space=pl.ANY)],
            out_specs=pl.BlockSpec(memory_space=pl.ANY),
            scratch_shapes=[pltpu.VMEM((MOE_BLK, D), f32), pltpu.VMEM((3, tm, D), f32),
                            pltpu.SemaphoreType.DMA((3,)), pltpu.SemaphoreType.DMA((3,)),
                            pltpu.SemaphoreType.DMA(()), pltpu.SemaphoreType.DMA(())],
        ),
        out_shape=jax.ShapeDtypeStruct((p_rows, D), f32),
        compiler_params=pltpu.CompilerParams(dimension_semantics=("arbitrary",)),
        name="moe_dispatch",
    )(dest_flat, tails, h2)


def _expert_kernel(blk_e_ref, nb_ref, xb_ref, w1_ref, w3_ref, w2_ref, y_ref, w1b, w3b, w2b):
    i = pl.program_id(0)

    @pl.when((i < nb_ref[0]) & ((i == 0) | (blk_e_ref[i] != blk_e_ref[jnp.maximum(i - 1, 0)])))
    def _():
        w1b[...] = w1_ref[...].astype(bf16)
        w3b[...] = w3_ref[...].astype(bf16)
        w2b[...] = w2_ref[...].astype(bf16)

    @pl.when(i < nb_ref[0])
    def _():
        x = xb_ref[...].astype(bf16)
        a = jnp.dot(x, w1b[...], preferred_element_type=f32)
        b = jnp.dot(x, w3b[...], preferred_element_type=f32)
        h = (a * jax.nn.sigmoid(a) * b).astype(bf16)
        y_ref[...] = jnp.dot(h, w2b[...], preferred_element_type=f32)

    @pl.when(i >= nb_ref[0])
    def _():
        y_ref[...] = jnp.zeros_like(y_ref)


def _experts(xb, blk_e, nb_used, w1, w3, w2, layer):
    p_rows = xb.shape[0]
    nb = p_rows // MOE_BLK

    def xmap(i, be, nbu):
        return (jnp.minimum(i, nbu[0] - 1), 0)

    def wmap(i, be, nbu):
        return (layer, be[i], 0, 0)

    return pl.pallas_call(
        _expert_kernel,
        grid_spec=pltpu.PrefetchScalarGridSpec(
            num_scalar_prefetch=2,
            grid=(nb,),
            in_specs=[pl.BlockSpec((MOE_BLK, D), xmap),
                      pl.BlockSpec((None, None, D, D_EXP), wmap),
                      pl.BlockSpec((None, None, D, D_EXP), wmap),
                      pl.BlockSpec((None, None, D_EXP, D), wmap)],
            out_specs=pl.BlockSpec((MOE_BLK, D), lambda i, be, nbu: (i, 0)),
            scratch_shapes=[pltpu.VMEM((D, D_EXP), bf16), pltpu.VMEM((D, D_EXP), bf16),
                            pltpu.VMEM((D_EXP, D), bf16)],
        ),
        out_shape=jax.ShapeDtypeStruct((p_rows, D), f32),
        compiler_params=pltpu.CompilerParams(dimension_semantics=("arbitrary",),
                                             vmem_limit_bytes=48 * 1024 * 1024),
        name="moe_experts",
    )(blk_e, nb_used, xb, w1, w3, w2)


def _combine_kernel(dest_ref, x1_ref, gw_ref, mod_ref, y_ref, o_ref, ybuf, sems):
    i = pl.program_id(0)
    nsteps = pl.num_programs(0)
    tm = x1_ref.shape[0]
    n = nsteps * tm

    def issue_tile(j):
        slot = j % 2

        def issue(t, carry):
            for k in range(2):
                d = dest_ref[k * n + j * tm + t]
                pltpu.make_async_copy(y_ref.at[pl.ds(d, 1)], ybuf.at[slot, k, pl.ds(t, 1)],
                                      sems.at[slot]).start()
            return carry

        lax.fori_loop(0, tm, issue, 0, unroll=True)

    @pl.when(i == 0)
    def _():
        issue_tile(i)

    @pl.when(i + 1 < nsteps)
    def _():
        issue_tile(i + 1)

    slot = i % 2
    for k in range(2):
        pltpu.make_async_copy(y_ref.at[pl.ds(0, tm)], ybuf.at[slot, k], sems.at[slot]).wait()
    gw = gw_ref[...]
    y = ybuf[slot, 0] * gw[:, 0:1] + ybuf[slot, 1] * gw[:, 1:2]
    o_ref[...] = x1_ref[...] + mod_ref[5] * y


def _combine(y, dest_flat, x1, gw_t, mod_l, bsz, seq):
    n = x1.shape[0]
    tm = TM_COMB
    nt = seq // tm
    return pl.pallas_call(
        _combine_kernel,
        grid_spec=pltpu.PrefetchScalarGridSpec(
            num_scalar_prefetch=1,
            grid=(n // tm,),
            in_specs=[pl.BlockSpec((tm, D), lambda i, *_: (i, 0)),
                      pl.BlockSpec((tm, 2), lambda i, *_: (i, 0)),
                      pl.BlockSpec((6, None, 1, D), lambda i, *_: (0, i // nt, 0, 0)),
                      pl.BlockSpec(memory_space=pl.ANY)],
            out_specs=pl.BlockSpec((tm, D), lambda i, *_: (i, 0)),
            scratch_shapes=[pltpu.VMEM((2, 2, tm, D), f32), pltpu.SemaphoreType.DMA((2,))],
        ),
        out_shape=jax.ShapeDtypeStruct((n, D), f32),
        compiler_params=pltpu.CompilerParams(dimension_semantics=("arbitrary",)),
        name="moe_combine",
    )(dest_flat, x1, gw_t, mod_l, y)


def _moe_experts_sorted(h2, eidx, w1, w3, w2, layer):
    n = h2.shape[0]
    p_rows = (n * 2 // MOE_BLK + N_EXP) * MOE_BLK
    nb = p_rows // MOE_BLK
    rank, cnt = _ranks(eidx)
    counts = cnt[:, 0].astype(i32)
    padded = (counts + MOE_BLK - 1) // MOE_BLK * MOE_BLK
    pad_end = jnp.cumsum(padded)
    pad_start = pad_end - padded
    start_of = jnp.sum(jnp.where(eidx[..., None] == jnp.arange(N_EXP, dtype=i32), pad_start, 0), axis=-1)
    dest_flat = (start_of + rank).reshape(-1)
    nb_used = (pad_end[-1] // MOE_BLK).astype(i32)
    spare = nb_used + jnp.arange(N_EXP, dtype=i32)
    tails = jnp.concatenate([(pad_start + counts) // ZERO_ROWS, pad_end // ZERO_ROWS,
                             jnp.where(spare < nb, spare * MOE_BLK, -1)]).astype(i32)
    blk = jnp.minimum(jnp.arange(nb, dtype=i32), nb_used - 1)
    seg_done = (pad_end[None, :] <= (blk * MOE_BLK)[:, None]).astype(i32)
    blk_e = jnp.minimum(jnp.sum(seg_done, axis=1), N_EXP - 1).astype(i32)
    xb = _dispatch(h2, dest_flat, tails, p_rows)
    y = _experts(xb, blk_e, nb_used.reshape(1), w1, w3, w2, layer)
    return y, dest_flat


def kernel(x, c, w_ada, b_ada, attn_norm, ffn_norm, w_in, gla_gate_w, gla_gate_b, gla_out_norm,
           na_q_norm, na_k_norm, na_rpb, w_out, router_w, router_b, w1, w3, w2):
    bsz, seq, _ = x.shape
    depth = w_ada.shape[0]
    mod = _adaln_mod(c, w_ada, b_ada)
    xc = x.reshape(bsz * seq, D)
    pending = None
    for l in range(depth):
        if pending is None:
            (qg, kg, vg, sg, la_f, la_b, qn, kn, vn), bounded_f, bounded_b = _inproj(
                xc, mod[l], attn_norm[l], w_in[l], gla_gate_w[l], gla_gate_b[l], bsz, seq)
        else:
            y, dest_flat, x1, gw_t = pending
            xc, (qg, kg, vg, sg, la_f, la_b, qn, kn, vn), bounded_f, bounded_b = _combine_inproj(
                y, dest_flat, x1, gw_t, mod[l - 1], mod[l], attn_norm[l], w_in[l], gla_gate_w[l],
                gla_gate_b[l], bsz, seq)
        o_f = _gla_dir(False, qg, kg, vg, la_f, bounded_f, bsz, seq)
        o_gla = _gla_dir(True, qg, kg, vg, la_b, bounded_b, bsz, seq, extra=(o_f, sg, gla_out_norm[l]))
        o_na = _natten(qn, kn, vn, na_q_norm[l], na_k_norm[l], na_rpb[l], bsz, seq)
        x1, h2, eidx, gw = _outproj(o_gla, o_na, w_out[l], xc, mod[l], ffn_norm[l],
                                    router_w, router_b, bsz, seq)
        y, dest_flat = _moe_experts_sorted(h2, eidx, w1, w3, w2, l)
        pending = (y, dest_flat, x1, gw.T)
    y, dest_flat, x1, gw_t = pending
    xc = _combine(y, dest_flat, x1, gw_t, mod[depth - 1], bsz, seq)
    return xc.reshape(bsz, seq, D)
```

```python
import functools

import numpy as np
import jax
import jax.numpy as jnp
from jax import lax
from jax.experimental import pallas as pl
from jax.experimental.pallas import tpu as pltpu

f32 = jnp.float32
bf16 = jnp.bfloat16
i32 = jnp.int32

D = 1024
GRID_W = 64
GLA_H, GLA_DK, GLA_DV = 4, 64, 128
GLA_RANK = 16
GLA_TAU = 16.0
CHUNK = 64
NA_H, NA_DH = 8, 64
WIN_H, WIN_W = 8, 16
N_EXP, N_GRP, EPG = 32, 4, 8
D_EXP = 512
QK_W = GLA_H * GLA_DK
V_W = GLA_H * GLA_DV
NA_W = NA_H * NA_DH
EPS = 1e-6
NEG = -1e30

TM_PROJ = 1024
TB_GLA = 256
GLA_NB = 4
NA_QROWS = 4
NA_KROWS = 12
NA_SUB = 8
TM_RANK = 1024
TM_DISP = 512
TM_COMB = 512
TM_COMB_PROJ = 512
MOE_BLK = 512
ZERO_ROWS = 64
N_LEVELS = 6
GLA_SAFE_RANGE = 40.0


def _nt(a, b):
    return lax.dot_general(a, b, (((1,), (1,)), ((), ())), preferred_element_type=f32)


def _tn(a, b):
    return lax.dot_general(a, b, (((0,), (0,)), ((), ())), preferred_element_type=f32)


def _split3(x):
    hi = x.astype(bf16)
    r = x - hi.astype(f32)
    mid = r.astype(bf16)
    lo = (r - mid.astype(f32)).astype(bf16)
    return hi, mid, lo


def _sel_l(m01, x):
    hi, mid, lo = _split3(x)
    d = lambda p: jnp.dot(m01, p, preferred_element_type=f32)
    return d(hi) + d(mid) + d(lo)


def _sel_r(x, m01):
    hi, mid, lo = _split3(x)
    d = lambda p: jnp.dot(p, m01, preferred_element_type=f32)
    return d(hi) + d(mid) + d(lo)


def _mod_kernel(c_ref, w_ref, b_ref, o_ref):
    c = c_ref[...]
    ca = c * jax.nn.sigmoid(c)
    w = w_ref[0]
    c_hi = ca.astype(bf16)
    c_lo = (ca - c_hi.astype(f32)).astype(bf16)
    w_hi = w.astype(bf16)
    acc = jnp.dot(c_hi, w_hi, preferred_element_type=f32) + jnp.dot(c_lo, w_hi, preferred_element_type=f32)
    o_ref[0, 0] = acc + b_ref[0, 0]


def _adaln_mod(c, w_ada, b_ada):
    depth = w_ada.shape[0]
    bsz = c.shape[0]
    cp = jnp.zeros((8, D), f32).at[:bsz].set(c)
    out = pl.pallas_call(
        _mod_kernel,
        grid=(depth, 6),
        in_specs=[pl.BlockSpec((8, D), lambda l, j: (0, 0)),
                  pl.BlockSpec((1, D, D), lambda l, j: (l, 0, j)),
                  pl.BlockSpec((1, 1, 1, D), lambda l, j: (l, j, 0, 0))],
        out_specs=pl.BlockSpec((1, 1, 8, D), lambda l, j: (l, j, 0, 0)),
        out_shape=jax.ShapeDtypeStruct((depth, 6, 8, D), f32),
        name="adaln_mod",
    )(cp, w_ada, b_ada.reshape(depth, 6, 1, D))
    return out[:, :, :bsz].reshape(depth, 6, bsz, 1, D)


def _log_sigmoid(z):
    return jnp.minimum(z, 0.0) - jnp.log1p(jnp.exp(-jnp.abs(z)))


def _inproj_body(x, mod_ref, nrm_ref, wm_ref, wlr_ref, gf_ref, bf_ref,
                 qg_ref, kg_ref, vg_ref, sg_ref, laf_ref, lab_ref, qn_ref, kn_ref, vn_ref, lamin_ref):
    y = x * lax.rsqrt(jnp.mean(x * x, axis=-1, keepdims=True) + EPS) * nrm_ref[...]
    h = y * (1.0 + mod_ref[1]) + mod_ref[0]
    hb = h.astype(bf16)
    proj = lambda lo, hi: jnp.dot(hb, wm_ref[:, lo:hi], preferred_element_type=f32)
    qg_ref[...] = (proj(0, 256) * (GLA_DK ** -0.5)).astype(bf16)
    kg_ref[...] = proj(256, 512).astype(bf16)
    vg_ref[...] = proj(512, 1024).astype(bf16)
    gg = proj(1024, 1536)
    sg_ref[...] = (gg * jax.nn.sigmoid(gg)).astype(bf16)
    qn_ref[...] = proj(1536, 2048).astype(bf16)
    kn_ref[...] = proj(2048, 2560).astype(bf16)
    vn_ref[...] = proj(2560, 3072).astype(bf16)
    lr =jnp.dot(hb, wlr_ref[...], preferred_element_type=f32)
    z = jnp.dot(lr, gf_ref[...], precision=lax.Precision.HIGHEST, preferred_element_type=f32) + bf_ref[...]
    la = _log_sigmoid(z) * (1.0 / GLA_TAU)
    la_f = la[:, 0:QK_W]
    la_b = la[:, QK_W:2 * QK_W]
    laf_ref[...] = la_f
    lab_ref[...] = la_b
    rows = []
    for la in (la_f, la_b):
        for blk in range(la.shape[0] // TB_GLA):
            tot = jnp.sum(la[blk * TB_GLA:(blk + 1) * TB_GLA], axis=0, keepdims=True)
            rows.append(jnp.broadcast_to(jnp.min(tot, axis=1, keepdims=True), (1, 128)))
    if len(rows) < 8:
        rows.append(jnp.zeros((8 - len(rows), 128), f32))
    lamin_ref[...] = jnp.concatenate(rows, axis=0)


def _inproj_kernel(x_ref, *refs):
    _inproj_body(x_ref[...], *refs)


def _combine_inproj_kernel(dest_ref, x1_ref, gw_ref, modp_ref, y_ref, *refs):
    x2_ref = refs[6]
    ybuf, sems = refs[-2], refs[-1]
    proj_refs = refs[:6] + refs[7:-2]
    i = pl.program_id(0) * pl.num_programs(1) + pl.program_id(1)
    nsteps = pl.num_programs(0) * pl.num_programs(1)
    tm = x1_ref.shape[0]
    n = nsteps * tm

    nslot = ybuf.shape[0]

    def row_copies(j, slot, t):
        for k in range(2):
            d = dest_ref[k * n + j * tm + t]
            pltpu.make_async_copy(y_ref.at[pl.ds(d, 1)], ybuf.at[slot, k, pl.ds(t, 1)], sems.at[slot]).start()

    def issue_tile(j, slot):
        for t in range(tm):
            row_copies(j, slot, t)

    def wait_tile(slot):
        for k in range(2):
            pltpu.make_async_copy(y_ref.at[pl.ds(0, tm)], ybuf.at[slot, k], sems.at[slot]).wait()

    slot = i % nslot

    @pl.when(i == 0)
    def _():
        for j in range(nslot - 1):
            lax.fori_loop(0, tm, lambda t, c, j=j: (row_copies(j, j, t), c)[1], 0)

    wait_tile(slot)
    gw = gw_ref[...]
    y = ybuf[slot, 0] * gw[:, 0:1] + ybuf[slot, 1] * gw[:, 1:2]
    x2 = x1_ref[...] + modp_ref[5] * y
    x2_ref[...] = x2
    ahead = i + nslot - 1
    issue_tile(jnp.where(ahead < nsteps, ahead, ahead - nsteps), ahead % nslot)
    _inproj_body(x2, *proj_refs)

    @pl.when(i == nsteps - 1)
    def _():
        for back in range(1, nslot):
            wait_tile((i + back) % nslot)


_PROJ_WIDTHS = [(QK_W, bf16), (QK_W, bf16), (V_W, bf16), (V_W, bf16), (QK_W, f32), (QK_W, f32),
                (NA_W, bf16), (NA_W, bf16), (NA_W, bf16)]


def _inproj_operands(nrm, w_in, gate_w, gate_b, imap):
    lo, hi = 2 * QK_W + 2 * V_W, 2 * QK_W + 2 * V_W + 2 * GLA_RANK
    wm = jnp.concatenate([w_in[:, :lo], w_in[:, hi:]], axis=1).astype(bf16)
    wlr = w_in[:, lo:hi].astype(bf16)
    zpad = jnp.zeros((GLA_RANK, QK_W), f32)
    gmat = jnp.concatenate([jnp.concatenate([gate_w[0], zpad], axis=1),
                            jnp.concatenate([zpad, gate_w[1]], axis=1)], axis=0)
    gbias = jnp.concatenate([gate_b[0], gate_b[1]]).reshape(1, 2 * QK_W)
    args = [nrm.reshape(1, D), wm, wlr, gmat, gbias]
    return args, [pl.BlockSpec(a.shape, imap) for a in args]


def _bounded_flags(lamin, tm):
    per = tm // TB_GLA
    lamin = lamin[:, :2 * per, 0]
    bounded_f = (lamin[:, :per].reshape(-1) > -GLA_SAFE_RANGE).astype(i32)
    bounded_b = (lamin[:, per:].reshape(-1) > -GLA_SAFE_RANGE).astype(i32)
    return bounded_f, bounded_b


def _inproj(x2d, mod_l, nrm, w_in, gate_w, gate_b, bsz, seq):
    n = x2d.shape[0]
    tm = TM_PROJ
    nt = seq // tm
    row = lambda b, i: (b * nt + i, 0)
    w_args, w_specs = _inproj_operands(nrm, w_in, gate_w, gate_b, lambda b, i: (0, 0))
    outs = pl.pallas_call(
        _inproj_kernel,
        grid=(bsz, nt),
        in_specs=[pl.BlockSpec((tm, D), row),
                  pl.BlockSpec((6, None, 1, D), lambda b, i: (0, b, 0, 0))] + w_specs,
        out_specs=[pl.BlockSpec((tm, w), row) for w, _ in _PROJ_WIDTHS]
        + [pl.BlockSpec((None, 8, 128), lambda b, i: (b * nt + i, 0, 0))],
        out_shape=[jax.ShapeDtypeStruct((n, w), dt) for w, dt in _PROJ_WIDTHS]
        + [jax.ShapeDtypeStruct((n // tm, 8, 128), f32)],
        compiler_params=pltpu.CompilerParams(dimension_semantics=("arbitrary", "arbitrary")),
        name="inproj",
    )(x2d, mod_l, *w_args)
    return outs[:-1], *_bounded_flags(outs[-1], tm)


def _combine_inproj(y, dest_flat, x1, gw_t, mod_prev, mod_l, nrm, w_in, gate_w, gate_b, bsz, seq):
    n = x1.shape[0]
    tm = TM_COMB_PROJ
    nt = seq // tm
    row = lambda b, i, *_: (b * nt + i, 0)
    mods = lambda b, i, *_: (0, b, 0, 0)
    w_args, w_specs = _inproj_operands(nrm, w_in, gate_w, gate_b, lambda b, i, *_: (0, 0))
    outs = pl.pallas_call(
        _combine_inproj_kernel,
        grid_spec=pltpu.PrefetchScalarGridSpec(
            num_scalar_prefetch=1,
            grid=(bsz, nt),
            in_specs=[pl.BlockSpec((tm, D), row),
                      pl.BlockSpec((tm, 2), row),
                      pl.BlockSpec((6, None, 1, D), mods),
                      pl.BlockSpec(memory_space=pl.ANY),
                      pl.BlockSpec((6, None, 1, D), mods)] + w_specs,
            out_specs=[pl.BlockSpec((tm, D), row)]
            + [pl.BlockSpec((tm, w), row) for w, _ in _PROJ_WIDTHS]
            + [pl.BlockSpec((None, 8, 128), lambda b, i, *_: (b * nt + i, 0, 0))],
            scratch_shapes=[pltpu.VMEM((3, 2, tm, D), f32), pltpu.SemaphoreType.DMA((3,))],
        ),
        out_shape=[jax.ShapeDtypeStruct((n, D), f32)]
        + [jax.ShapeDtypeStruct((n, w), dt) for w, dt in _PROJ_WIDTHS]
        + [jax.ShapeDtypeStruct((n // tm, 8, 128), f32)],
        compiler_params=pltpu.CompilerParams(dimension_semantics=("arbitrary", "arbitrary"),
                                             vmem_limit_bytes=56 * 1024 * 1024),
        name="combine_inproj",
    )(dest_flat, x1, gw_t, mod_prev, y, mod_l, *w_args)
    return outs[0], outs[1:-1], *_bounded_flags(outs[-1], tm)


def _gla_consts(rev, tb):
    c = CHUNK
    idx = np.arange(c)
    if not rev:
        cum = (idx[None, :] <= idx[:, None]).astype(np.float32)
    else:
        cum = (idx[None, :] >= idx[:, None]).astype(np.float32)
    mats = [cum]
    masks = []
    for lvl in range(N_LEVELS):
        s = c >> (lvl + 1)
        blk = idx // (2 * s)
        second = (idx % (2 * s)) >= s
        ref_row = blk * 2 * s + (s if rev else s - 1)
        mats.append(cum[ref_row])
        same = blk[:, None] == blk[None, :]
        if not rev:
            m = same & second[:, None] & (~second[None, :])
        else:
            m = same & (~second[:, None]) & second[None, :]
        masks.append(m)
    masks.append(np.eye(c, dtype=bool) & (not rev))
    mst = np.concatenate(mats, axis=0)
    msk = np.stack([np.tile(m, (1, GLA_H)) for m in masks]).astype(np.float32)
    tix = np.arange(tb)
    tri = (tix[None, :] >= tix[:, None]) if rev else (tix[None, :] <= tix[:, None])
    cmask = (tix[:, None] < tix[None, :]) if rev else (tix[:, None] >= tix[None, :])
    nch = tb // c
    selb = np.zeros((tb, nch * 128), np.float32)
    for ch in range(nch):
        selb[ch * c:(ch + 1) * c, ch * 128:(ch + 1) * 128] = 1.0
    hk = np.arange(QK_W) // GLA_DK
    hv = np.arange(V_W) // GLA_DV
    kmask = (hk[:, None] == hk[None, :]).astype(np.float32)
    vmask = (hk[:, None] == hv[None, :]).astype(np.float32)
    return (jnp.asarray(mst, bf16), jnp.asarray(msk, f32), jnp.asarray(selb, bf16),
            jnp.asarray(kmask, bf16), jnp.asarray(vmask, bf16), jnp.asarray(vmask, f32),
            jnp.asarray(tri, bf16), jnp.asarray(cmask, f32), jnp.ones((tb, 128), bf16))


def _gla_kernel(rev, tb, *refs):
    if rev:
        (bounded_ref, q_ref, k_ref, v_ref, la_ref, mst_ref, msk_ref, selb_ref, kmask_ref, vmask_ref,
         smask_ref, tri_ref, cmask_ref, ones_ref, of_ref, sg_ref, gn_ref, out_ref, s_scr, o_scr) = refs
    else:
        (bounded_ref, q_ref, k_ref, v_ref, la_ref, mst_ref, msk_ref, selb_ref, kmask_ref, vmask_ref,
         smask_ref, tri_ref, cmask_ref, ones_ref, out_ref, s_scr) = refs
        o_scr = out_ref
    c = CHUNK
    nch = tb // c
    elems = range(GLA_NB)

    @pl.when(pl.program_id(1) == 0)
    def _():
        s_scr[...] = jnp.zeros_like(s_scr)

    def blocks_bounded():
        tri = tri_ref[...]
        cmask = cmask_ref[...]
        las = [la_ref[e] for e in elems]
        tots = [_sel_r(la.T, ones_ref[...]) for la in las]
        bcums = []
        for la in las:
            la_hi = la.astype(bf16)
            la_lo = (la - la_hi.astype(f32)).astype(bf16)
            bcums.append(jnp.dot(tri, la_hi, preferred_element_type=f32)
                         + jnp.dot(tri, la_lo, preferred_element_type=f32))
        ks = [k_ref[e].astype(f32) for e in elems]
        qes = [(q_ref[e].astype(f32) * jnp.exp(bcums[e])).astype(bf16) for e in elems]
        kxs = [(ks[e] * jnp.exp(-bcums[e])).astype(bf16) for e in elems]
        s_prevs = [s_scr[e] for e in elems]
        o_inters = [jnp.dot(qes[e], s_prevs[e].astype(bf16), preferred_element_type=f32) for e in elems]
        lane_head = lax.broadcasted_iota(i32, (tb, QK_W), 1) // GLA_DK
        for hd in range(GLA_H):
            cols = slice(hd * GLA_DV, (hd + 1) * GLA_DV)
            phs = []
            for e in elems:
                qh = jnp.where(lane_head == hd, qes[e], jnp.zeros_like(qes[e]))
                phs.append(jnp.where(cmask > 0.0, _nt(qh, kxs[e]), 0.0).astype(bf16))
            for e in elems:
                o_blk = o_inters[e][:, cols] + jnp.dot(phs[e], v_ref[e, :, cols], preferred_element_type=f32)
                o_scr[e, :, cols] = o_blk.astype(o_scr.dtype)
        for e in elems:
            blast = bcums[e][0:1] if rev else bcums[e][tb - 1:tb]
            ke = (ks[e] * jnp.exp(blast - bcums[e])).astype(bf16)
            dec = jnp.exp(tots[e])
            s_scr[e] = (s_prevs[e] * jnp.concatenate([dec] * GLA_H, axis=1)
                        + _tn(ke, v_ref[e]) * smask_ref[...])

    def chunks_robust():
        mst = mst_ref[...]
        kmask = kmask_ref[...]
        vmask = vmask_ref[...]
        for e in elems:
            tot = _sel_r(la_ref[e].T, selb_ref[...])
            for ch in (range(nch - 1, -1, -1) if rev else range(nch)):
                rows = slice(ch * c, (ch + 1) * c)
                q = q_ref[e, rows, :].astype(f32)
                k = k_ref[e, rows, :].astype(f32)
                v = v_ref[e, rows, :]
                r_all = _sel_l(mst, la_ref[e, rows, :])
                bcum = r_all[0:c]
                qe = (q * jnp.exp(bcum)).astype(bf16)
                o = jnp.dot(qe, s_scr[e].astype(bf16), preferred_element_type=f32)
                sall = jnp.zeros((c, QK_W), f32)
                for lvl in range(N_LEVELS):
                    ref = r_all[(lvl + 1) * c:(lvl + 2) * c]
                    a = (q * jnp.exp(jnp.minimum(bcum - ref, 0.0))).astype(bf16)
                    bm = (k * jnp.exp(jnp.minimum(ref - bcum, 0.0))).astype(bf16)
                    bbd = jnp.concatenate([bm] * GLA_H, axis=0) * kmask
                    sall = sall + jnp.where(msk_ref[lvl] > 0.0, _nt(a, bbd), 0.0)
                if not rev:
                    bbd = jnp.concatenate([k_ref[e, rows, :]] * GLA_H, axis=0) * kmask
                    sall = sall + jnp.where(msk_ref[N_LEVELS] > 0.0, _nt(q_ref[e, rows, :], bbd), 0.0)
                vbd = jnp.concatenate([v] * GLA_H, axis=0) * vmask
                o_blk = o + jnp.dot(sall.astype(bf16), vbd, preferred_element_type=f32)
                o_scr[e, rows, :] = o_blk.astype(o_scr.dtype)
                blast = bcum[0:1] if rev else bcum[c - 1:c]
                ke = (k * jnp.exp(blast - bcum)).astype(bf16)
                dec = jnp.exp(tot[:, ch * 128:(ch + 1) * 128])
                s_scr[e] = (s_scr[e] * jnp.concatenate([dec] * GLA_H, axis=1)
                            + _tn(ke, v) * smask_ref[...])

    nb = pl.num_programs(1)
    pos = nb - 1 - pl.program_id(1) if rev else pl.program_id(1)
    bounded = bounded_ref[(pl.program_id(0) * GLA_NB) * nb + pos] > 0
    for e in range(1, GLA_NB):
        bounded = bounded & (bounded_ref[(pl.program_id(0) * GLA_NB + e) * nb + pos] > 0)
    pl.when(bounded)(blocks_bounded)
    pl.when(jnp.logical_not(bounded))(chunks_robust)

    if rev:
        gn = gn_ref[...]
        for e in elems:
            for hd in range(GLA_H):
                cols = slice(hd * GLA_DV, (hd + 1) * GLA_DV)
                t = of_ref[e, :, cols].astype(f32) + o_scr[e, :, cols]
                y = t * lax.rsqrt(jnp.mean(t * t, axis=-1, keepdims=True) + EPS) * gn
                out_ref[e, :, cols] = (y * sg_ref[e, :, cols].astype(f32)).astype(bf16)


def _gla_dir(rev, q, k, v, la, bounded, bsz, seq, extra=None):
    n = q.shape[0]
    tb = TB_GLA
    nb = seq // tb
    consts = _gla_consts(rev, tb)
    if rev:
        row = lambda b, i, flags: (b, nb - 1 - i, 0)
    else:
        row = lambda b, i, flags: (b, i, 0)
    const2 = lambda b, i, flags: (0, 0)
    const3 = lambda b, i, flags: (0, 0, 0)
    per_batch = lambda a: a.reshape(bsz, seq, a.shape[-1])
    q, k, v, la = per_batch(q), per_batch(k), per_batch(v), per_batch(la)
    in_specs = [pl.BlockSpec((GLA_NB, tb, QK_W), row), pl.BlockSpec((GLA_NB, tb, QK_W), row),
                pl.BlockSpec((GLA_NB, tb, V_W), row), pl.BlockSpec((GLA_NB, tb, QK_W), row),
                pl.BlockSpec(consts[0].shape, const2), pl.BlockSpec(consts[1].shape, const3),
                pl.BlockSpec(consts[2].shape, const2), pl.BlockSpec(consts[3].shape, const2),
                pl.BlockSpec(consts[4].shape, const2), pl.BlockSpec(consts[5].shape, const2),
                pl.BlockSpec(consts[6].shape, const2), pl.BlockSpec(consts[7].shape, const2),
                pl.BlockSpec(consts[8].shape, const2)]
    args = [q, k, v, la, *consts]
    scratch = [pltpu.VMEM((GLA_NB, QK_W, V_W), f32)]
    if rev:
        o_f, sg, gn = extra
        in_specs += [pl.BlockSpec((GLA_NB, tb, V_W), row), pl.BlockSpec((GLA_NB, tb, V_W), row),
                     pl.BlockSpec((1, GLA_DV), const2)]
        args += [per_batch(o_f), per_batch(sg), gn.reshape(1, GLA_DV)]
        scratch.append(pltpu.VMEM((GLA_NB, tb, V_W), f32))
    out = pl.pallas_call(
        functools.partial(_gla_kernel, rev, tb),
        grid_spec=pltpu.PrefetchScalarGridSpec(
            num_scalar_prefetch=1,
            grid=(bsz // GLA_NB, nb),
            in_specs=in_specs,
            out_specs=pl.BlockSpec((GLA_NB, tb, V_W), row),
            scratch_shapes=scratch,
        ),
        out_shape=jax.ShapeDtypeStruct((bsz, seq, V_W), bf16),
        compiler_params=pltpu.CompilerParams(dimension_semantics=("arbitrary", "arbitrary")),
        name="gla_bwd" if rev else "gla_fwd",
    )(bounded, *args)
    return out.reshape(n, V_W)


def _rpb_expand_kernel(rpb_ref, sel_ref, valid_ref, o_ref):
    e = _sel_r(rpb_ref[...], sel_ref[...])
    o_ref[...] = jnp.where(valid_ref[...] > 0.0, e, NEG)


def _na_bias_table(rpb):
    nri, nci = 2 * WIN_H - 1, 2 * WIN_W - 1
    qc = np.arange(GRID_W)
    col_start = np.clip(qc - WIN_W // 2, 0, GRID_W - WIN_W)
    kc = np.arange(GRID_W)
    valid = (kc[None, :] >= col_start[:, None]) & (kc[None, :] < col_start[:, None] + WIN_W)
    ci = kc[None, :] - qc[:, None] + WIN_W - 1
    sel = np.zeros((32, GRID_W * GRID_W), np.float32)
    flat_ci = ci.reshape(-1)
    ok = (flat_ci >= 0) & (flat_ci < nci)
    sel[flat_ci[ok], np.nonzero(ok)[0]] = 1.0
    rpb2 = jnp.zeros((NA_H * nri + 8 - (NA_H * nri) % 8, 32), f32).at[:NA_H * nri, :nci].set(
        rpb.reshape(NA_H * nri, nci))
    nrow = rpb2.shape[0]
    e = pl.pallas_call(
        _rpb_expand_kernel,
        out_shape=jax.ShapeDtypeStruct((nrow, GRID_W * GRID_W), f32),
        name="rpb_expand",
    )(rpb2, jnp.asarray(sel, bf16), jnp.asarray(valid.reshape(1, -1), f32))
    return e[:NA_H * nri].reshape(NA_H // 2, 2, nri, GRID_W, GRID_W)


def _na_row_classes():
    rows = GRID_W
    ri = -np.ones((3, NA_QROWS, NA_KROWS), np.int64)
    for cls, rb in enumerate((0, 1, rows // NA_QROWS - 1)):
        r0 = rb * NA_QROWS
        kb = int(np.clip(r0 - WIN_H // 2, 0, rows - NA_KROWS))
        for rq in range(NA_QROWS):
            r = r0 + rq
            rs = int(np.clip(r - WIN_H // 2, 0, rows - WIN_H))
            for m in range(NA_KROWS):
                kr = kb + m
                if rs <= kr < rs + WIN_H:
                    ri[cls, rq, m] = kr - r + WIN_H - 1
    return ri


def _natten_kernel(q_ref, k_ref, v_ref, qn_ref, kn_ref, gsum_ref, e_ref, o_ref, kn_scr, tbl_scr):
    b = pl.program_id(1)
    rb = pl.program_id(2)
    gsum = gsum_ref[...]
    log2e = 1.4426950408889634

    def headnorm(x, g):
        sq = x * x
        hi = sq.astype(bf16)
        lo = (sq - hi.astype(f32)).astype(bf16)
        ms = (jnp.dot(hi, gsum, preferred_element_type=f32)
              + jnp.dot(lo, gsum, preferred_element_type=f32)) * (1.0 / NA_DH)
        return x * lax.rsqrt(ms + EPS) * g

    @pl.when((b == 0) & (rb == 0))
    def _():
        ri = _na_row_classes()
        neg = jnp.full((GRID_W, GRID_W), NEG, f32)
        for hh in range(2):
            for cls in range(3):
                for rq in range(NA_QROWS):
                    for mp in range(NA_KROWS // 2):
                        parts = []
                        for m in (2 * mp, 2 * mp + 1):
                            r = int(ri[cls, rq, m])
                            parts.append(neg if r < 0 else e_ref[hh, r] * log2e)
                        r0 = (hh * NA_QROWS + rq) * GRID_W
                        tbl_scr[cls, r0:r0 + GRID_W, mp * 128:(mp + 1) * 128] = (
                            jnp.concatenate(parts, axis=1))

    @pl.when(rb == 0)
    def _():
        kn_scr[...] = headnorm(k_ref[...].astype(f32), kn_ref[...]).astype(bf16)

    nq = NA_QROWS * GRID_W
    nk = NA_KROWS * GRID_W
    nblk = GRID_W // NA_QROWS
    first = lax.broadcasted_iota(i32, (nq, 128), 1) < NA_DH
    qs, wins, tbls = [], [], []
    for sub in range(NA_SUB):
        qb = rb * NA_SUB + sub
        q = headnorm(q_ref[sub * nq:(sub + 1) * nq, :].astype(f32), qn_ref[...]) * (NA_DH ** -0.5 * log2e)
        kb = jnp.clip(qb * NA_QROWS - WIN_H // 2, 0, GRID_W - NA_KROWS)
        start = pl.multiple_of(kb * GRID_W, GRID_W)
        cls = jnp.where(qb == 0, 0, jnp.where(qb == nblk - 1, 2, 1))
        qs.append([jnp.where(first, q, 0.0).astype(bf16), jnp.where(first, 0.0, q).astype(bf16)])
        wins.append((kn_scr[pl.ds(start, nk), :], v_ref[pl.ds(start, nk), :]))
        tbls.append(cls)
    scores = [[_nt(qs[sub][hh], wins[sub][0]) + tbl_scr[tbls[sub], hh * nq:(hh + 1) * nq, :]
               for hh in range(2)] for sub in range(NA_SUB)]
    for sub in range(NA_SUB):
        outs = []
        for s in scores[sub]:
            m = jnp.max(s, axis=-1, keepdims=True)
            p = jnp.exp2(s - m)
            l = jnp.sum(p, axis=-1, keepdims=True)
            outs.append(jnp.dot(p.astype(bf16), wins[sub][1], preferred_element_type=f32) / l)
        o_ref[sub * nq:(sub + 1) * nq, :] = jnp.where(first, outs[0], outs[1]).astype(bf16)


def _natten(qn, kn, vn, q_norm, k_norm, rpb, bsz, seq):
    n = qn.shape[0]
    e5 = _na_bias_table(rpb)
    nq = NA_QROWS * GRID_W
    nqs = NA_SUB * nq
    nrb = seq // nqs
    lane_h = np.arange(128) // NA_DH
    gsum = jnp.asarray(lane_h[:, None] == lane_h[None, :], bf16)
    qn2 = jnp.tile(q_norm.reshape(1, NA_DH), (1, 2))
    kn2 = jnp.tile(k_norm.reshape(1, NA_DH), (1, 2))
    const2 = lambda p, b, r: (0, 0)
    return pl.pallas_call(
        _natten_kernel,
        grid=(NA_H // 2, bsz, nrb),
        in_specs=[pl.BlockSpec((nqs, 128), lambda p, b, r: (b * nrb + r, p)),
                  pl.BlockSpec((seq, 128), lambda p, b, r: (b, p)),
                  pl.BlockSpec((seq, 128), lambda p, b, r: (b, p)),
                  pl.BlockSpec((1, 128), const2),
                  pl.BlockSpec((1, 128), const2),
                  pl.BlockSpec((128, 128), const2),
                  pl.BlockSpec((None,) + e5.shape[1:], lambda p, b, r: (p, 0, 0, 0, 0))],
        out_specs=pl.BlockSpec((nqs, 128), lambda p, b, r: (b * nrb + r, p)),
        out_shape=jax.ShapeDtypeStruct((n, NA_W), bf16),
        scratch_shapes=[pltpu.VMEM((seq, 128), bf16),
                        pltpu.VMEM((3, 2 * nq, NA_KROWS * GRID_W), f32)],
        compiler_params=pltpu.CompilerParams(dimension_semantics=("arbitrary", "arbitrary", "arbitrary")),
        name="natten",
    )(qn, kn, vn, qn2, kn2, gsum, e5)


def _top2(vals):
    io = lax.broadcasted_iota(i32, vals.shape, 0)
    m1 = jnp.max(vals, axis=0, keepdims=True)
    i1 = jnp.min(jnp.where(vals == m1, io, EPG), axis=0, keepdims=True)
    v2 = jnp.where(io == i1, -jnp.inf, vals)
    m2 = jnp.max(v2, axis=0, keepdims=True)
    i2 = jnp.min(jnp.where(v2 == m2, io, EPG), axis=0, keepdims=True)
    return m1, i1, m2, i2


def _outproj_kernel(og_ref, on_ref, wo_ref, x_ref, mod_ref, nrm_ref, rw_ref, rb_ref,
                    x1_ref, h2_ref, eidx_ref, gw_ref):
    mix = (jnp.dot(og_ref[...], wo_ref[0:V_W, :], preferred_element_type=f32)
           + jnp.dot(on_ref[...], wo_ref[V_W:V_W + NA_W, :], preferred_element_type=f32))
    x1 = x_ref[...] + mod_ref[2] * mix
    x1_ref[...] = x1
    y = x1 * lax.rsqrt(jnp.mean(x1 * x1, axis=-1, keepdims=True) + EPS) * nrm_ref[...]
    h2 = y * (1.0 + mod_ref[4]) + mod_ref[3]
    h2_ref[...] = h2
    rw = rw_ref[...]
    rw_hi = rw.astype(bf16)
    rw_lo = (rw - rw_hi.astype(f32)).astype(bf16)
    h_hi = h2.astype(bf16)
    h_lo = (h2 - h_hi.astype(f32)).astype(bf16)
    logits = _nt(rw_hi, h_hi) + _nt(rw_hi, h_lo) + _nt(rw_lo, h_hi)
    scores = jax.nn.sigmoid(logits)
    sel = scores + rb_ref[...]
    tops = [_top2(sel[g * EPG:(g + 1) * EPG]) for g in range(N_GRP)]
    best = jnp.zeros_like(tops[0][1])
    bs = tops[0][0] + tops[0][2]
    for g in range(1, N_GRP):
        gs = tops[g][0] + tops[g][2]
        take = gs > bs
        best = jnp.where(take, g, best)
        bs = jnp.where(take, gs, bs)
    io = lax.broadcasted_iota(i32, (EPG, sel.shape[1]), 0)
    i1 = jnp.zeros_like(best)
    i2 = jnp.zeros_like(best)
    s1 = jnp.zeros(best.shape, f32)
    s2 = jnp.zeros(best.shape, f32)
    for g in range(N_GRP):
        sc = scores[g * EPG:(g + 1) * EPG]
        pick = best == g
        i1 = jnp.where(pick, tops[g][1], i1)
        i2 = jnp.where(pick, tops[g][3], i2)
        s1 = jnp.where(pick, jnp.sum(jnp.where(io == tops[g][1], sc, 0.0), axis=0, keepdims=True), s1)
        s2 = jnp.where(pick, jnp.sum(jnp.where(io == tops[g][3], sc, 0.0), axis=0, keepdims=True), s2)
    eidx_ref[0:1, :] = best * EPG + i1
    eidx_ref[1:2, :] = best * EPG + i2
    tot = s1 + s2
    gw_ref[0:1, :] = s1 / tot
    gw_ref[1:2, :] = s2 / tot


def _outproj(o_gla, o_na, w_out, x2d, mod_l, nrm, router_w, router_b, bsz, seq):
    n = x2d.shape[0]
    tm = TM_PROJ
    nt = seq // tm
    row = lambda b, i: (b * nt + i, 0)
    col = lambda b, i: (0, b * nt + i)
    const = lambda b, i: (0, 0)
    return pl.pallas_call(
        _outproj_kernel,
        grid=(bsz, nt),
        in_specs=[pl.BlockSpec((tm, V_W), row),
                  pl.BlockSpec((tm, NA_W), row),
                  pl.BlockSpec((V_W + NA_W, D), const),
                  pl.BlockSpec((tm, D), row),
                  pl.BlockSpec((6, None, 1, D), lambda b, i: (0, b, 0, 0)),
                  pl.BlockSpec((1, D), const),
                  pl.BlockSpec((N_EXP, D), const),
                  pl.BlockSpec((N_EXP, 1), const)],
        out_specs=[pl.BlockSpec((tm, D), row), pl.BlockSpec((tm, D), row),
                   pl.BlockSpec((2, tm), col), pl.BlockSpec((2, tm), col)],
        out_shape=[jax.ShapeDtypeStruct((n, D), f32), jax.ShapeDtypeStruct((n, D), f32),
                   jax.ShapeDtypeStruct((2, n), i32), jax.ShapeDtypeStruct((2, n), f32)],
        compiler_params=pltpu.CompilerParams(dimension_semantics=("arbitrary", "arbitrary")),
        name="outproj_router",
    )(o_gla, o_na, w_out.astype(bf16), x2d, mod_l, nrm.reshape(1, D),
      router_w.T, router_b.reshape(N_EXP, 1))


def _rank_kernel(eidx_ref, tri_ref, rank_ref, cnt_ref, carry):
    @pl.when(pl.program_id(0) == 0)
    def _():
        carry[...] = jnp.zeros_like(carry)

    e = eidx_ref[...]
    tm = e.shape[1]
    io = lax.broadcasted_iota(i32, (N_EXP, tm), 0)
    run = carry[...]
    for k in range(2):
        oh = io == e[k:k + 1, :]
        ohf = oh.astype(f32)
        pre = jnp.dot(ohf.astype(bf16), tri_ref[...], preferred_element_type=f32) + run[:, 0:1]
        rank_ref[k:k + 1, :] = jnp.sum(jnp.where(oh, pre, 0.0), axis=0, keepdims=True).astype(i32)
        run = run + jnp.sum(ohf, axis=1, keepdims=True)
    carry[...] = run
    cnt_ref[...] = run


def _ranks(eidx):
    n = eidx.shape[1]
    tm = TM_RANK
    t = np.arange(tm)
    tri = jnp.asarray(t[:, None] < t[None, :], bf16)
    return pl.pallas_call(
        _rank_kernel,
        grid=(n // tm,),
        in_specs=[pl.BlockSpec((2, tm), lambda i: (0, i)), pl.BlockSpec((tm, tm), lambda i: (0, 0))],
        out_specs=[pl.BlockSpec((2, tm), lambda i: (0, i)), pl.BlockSpec((N_EXP, 128), lambda i: (0, 0))],
        out_shape=[jax.ShapeDtypeStruct((2, n), i32), jax.ShapeDtypeStruct((N_EXP, 128), f32)],
        scratch_shapes=[pltpu.VMEM((N_EXP, 128), f32)],
        compiler_params=pltpu.CompilerParams(dimension_semantics=("arbitrary",)),
        name="moe_rank",
    )(eidx, tri)


def _dispatch_kernel(dest_ref, tails_ref, h_ref, xb_ref, zbuf, hbuf, lsem, rsem, zsem, ssem):
    i = pl.program_id(0)
    nsteps = pl.num_programs(0)
    tm = TM_DISP
    n = nsteps * tm
    nbuf = hbuf.shape[0]

    def load(j):
        start = pl.multiple_of(j * tm, tm)
        return pltpu.make_async_copy(h_ref.at[pl.ds(start, tm)], hbuf.at[j % nbuf], lsem.at[j % nbuf])

    def drain_rows(j):
        for _ in range(2):
            pltpu.make_async_copy(hbuf.at[j % nbuf], xb_ref.at[pl.ds(0, tm)], rsem.at[j % nbuf]).wait()

    def pad_copy(chunk):
        start = pl.multiple_of(chunk * ZERO_ROWS, ZERO_ROWS)
        return pltpu.make_async_copy(zbuf.at[pl.ds(0, ZERO_ROWS)], xb_ref.at[pl.ds(start, ZERO_ROWS)], zsem)

    def spare_copy(j):
        start = pl.multiple_of(tails_ref[2 * N_EXP + j], MOE_BLK)
        return pltpu.make_async_copy(zbuf, xb_ref.at[pl.ds(start, MOE_BLK)], ssem)

    def for_pad_chunks(fn):
        for e in range(N_EXP):
            lax.fori_loop(tails_ref[e], tails_ref[N_EXP + e], lambda ch, c: (fn(ch), c)[1], 0)

    def for_spare_blocks(fn):
        for j in range(N_EXP):
            @pl.when(tails_ref[2 * N_EXP + j] >= 0)
            def _():
                fn(j)

    @pl.when(i == 0)
    def _():
        zbuf[...] = jnp.zeros_like(zbuf)
        for_pad_chunks(lambda ch: pad_copy(ch).start())
        for_spare_blocks(lambda j: spare_copy(j).start())
        for_pad_chunks(lambda ch: pad_copy(ch).wait())
        load(i).start()

    @pl.when(i >= nbuf - 1)
    def _():
        drain_rows(i + 1 - nbuf)

    @pl.when(i + 1 < nsteps)
    def _():
        load(i + 1).start()

    load(i).wait()
    slot = i % nbuf

    def issue(t, carry):
        src = hbuf.at[slot, pl.ds(t, 1)]
        for k in range(2):
            d = dest_ref[k * n + i * tm + t]
            pltpu.make_async_copy(src, xb_ref.at[pl.ds(d, 1)], rsem.at[slot]).start()
        return carry

    lax.fori_loop(0, tm, issue, 0, unroll=True)

    @pl.when(i == nsteps - 1)
    def _():
        for back in range(nbuf - 2, -1, -1):
            drain_rows(i - back)
        for_spare_blocks(lambda j: spare_copy(j).wait())


def _dispatch(h2, dest_flat, tails, p_rows):
    n = h2.shape[0]
    tm = TM_DISP
    return pl.pallas_call(
        _dispatch_kernel,
        grid_spec=pltpu.PrefetchScalarGridSpec(
            num_scalar_prefetch=2,
            grid=(n // tm,),
            in_specs=[pl.BlockSpec(memory_space=pl.ANY)],
            out_specs=pl.BlockSpec(memory_space=pl.ANY),
            scratch_shapes=[pltpu.VMEM((MOE_BLK, D), f32), pltpu.VMEM((3, tm, D), f32),
                            pltpu.SemaphoreType.DMA((3,)), pltpu.SemaphoreType.DMA((3,)),
                            pltpu.SemaphoreType.DMA(()), pltpu.SemaphoreType.DMA(())],
        ),
        out_shape=jax.ShapeDtypeStruct((p_rows, D), f32),
        compiler_params=pltpu.CompilerParams(dimension_semantics=("arbitrary",)),
        name="moe_dispatch",
    )(dest_flat, tails, h2)


def _expert_kernel(blk_e_ref, nb_ref, xb_ref, w1_ref, w3_ref, w2_ref, y_ref, w1b, w3b, w2b):
    i = pl.program_id(0)

    @pl.when((i < nb_ref[0]) & ((i == 0) | (blk_e_ref[i] != blk_e_ref[jnp.maximum(i - 1, 0)])))
    def _():
        w1b[...] = w1_ref[...].astype(bf16)
        w3b[...] = w3_ref[...].astype(bf16)
        w2b[...] = w2_ref[...].astype(bf16)

    @pl.when(i < nb_ref[0])
    def _():
        x = xb_ref[...].astype(bf16)
        a = jnp.dot(x, w1b[...], preferred_element_type=f32)
        b = jnp.dot(x, w3b[...], preferred_element_type=f32)
        h = (a * jax.nn.sigmoid(a) * b).astype(bf16)
        y_ref[...] = jnp.dot(h, w2b[...], preferred_element_type=f32)

    @pl.when(i >= nb_ref[0])
    def _():
        y_ref[...] = jnp.zeros_like(y_ref)


def _experts(xb, blk_e, nb_used, w1, w3, w2, layer):
    p_rows = xb.shape[0]
    nb = p_rows // MOE_BLK

    def xmap(i, be, nbu):
        return (jnp.minimum(i, nbu[0] - 1), 0)

    def wmap(i, be, nbu):
        return (layer, be[i], 0, 0)

    return pl.pallas_call(
        _expert_kernel,
        grid_spec=pltpu.PrefetchScalarGridSpec(
            num_scalar_prefetch=2,
            grid=(nb,),
            in_specs=[pl.BlockSpec((MOE_BLK, D), xmap),
                      pl.BlockSpec((None, None, D, D_EXP), wmap),
                      pl.BlockSpec((None, None, D, D_EXP), wmap),
                      pl.BlockSpec((None, None, D_EXP, D), wmap)],
            out_specs=pl.BlockSpec((MOE_BLK, D), lambda i, be, nbu: (i, 0)),
            scratch_shapes=[pltpu.VMEM((D, D_EXP), bf16), pltpu.VMEM((D, D_EXP), bf16),
                            pltpu.VMEM((D_EXP, D), bf16)],
        ),
        out_shape=jax.ShapeDtypeStruct((p_rows, D), f32),
        compiler_params=pltpu.CompilerParams(dimension_semantics=("arbitrary",),
                                             vmem_limit_bytes=48 * 1024 * 1024),
        name="moe_experts",
    )(blk_e, nb_used, xb, w1, w3, w2)


def _combine_kernel(dest_ref, x1_ref, gw_ref, mod_ref, y_ref, o_ref, ybuf, sems):
    i = pl.program_id(0)
    nsteps = pl.num_programs(0)
    tm = x1_ref.shape[0]
    n = nsteps * tm

    def issue_tile(j):
        slot = j % 2

        def issue(t, carry):
            for k in range(2):
                d = dest_ref[k * n + j * tm + t]
                pltpu.make_async_copy(y_ref.at[pl.ds(d, 1)], ybuf.at[slot, k, pl.ds(t, 1)],
                                      sems.at[slot]).start()
            return carry

        lax.fori_loop(0, tm, issue, 0, unroll=True)

    @pl.when(i == 0)
    def _():
        issue_tile(i)

    @pl.when(i + 1 < nsteps)
    def _():
        issue_tile(i + 1)

    slot = i % 2
    for k in range(2):
        pltpu.make_async_copy(y_ref.at[pl.ds(0, tm)], ybuf.at[slot, k], sems.at[slot]).wait()
    gw = gw_ref[...]
    y = ybuf[slot, 0] * gw[:, 0:1] + ybuf[slot, 1] * gw[:, 1:2]
    o_ref[...] = x1_ref[...] + mod_ref[5] * y


def _combine(y, dest_flat, x1, gw_t, mod_l, bsz, seq):
    n = x1.shape[0]
    tm = TM_COMB
    nt = seq // tm
    return pl.pallas_call(
        _combine_kernel,
        grid_spec=pltpu.PrefetchScalarGridSpec(
            num_scalar_prefetch=1,
            grid=(n // tm,),
            in_specs=[pl.BlockSpec((tm, D), lambda i, *_: (i, 0)),
                      pl.BlockSpec((tm, 2), lambda i, *_: (i, 0)),
                      pl.BlockSpec((6, None, 1, D), lambda i, *_: (0, i // nt, 0, 0)),
                      pl.BlockSpec(memory_space=pl.ANY)],
            out_specs=pl.BlockSpec((tm, D), lambda i, *_: (i, 0)),
            scratch_shapes=[pltpu.VMEM((2, 2, tm, D), f32), pltpu.SemaphoreType.DMA((2,))],
        ),
        out_shape=jax.ShapeDtypeStruct((n, D), f32),
        compiler_params=pltpu.CompilerParams(dimension_semantics=("arbitrary",)),
        name="moe_combine",
    )(dest_flat, x1, gw_t, mod_l, y)


def _moe_experts_sorted(h2, eidx, w1, w3, w2, layer):
    n = h2.shape[0]
    p_rows = (n * 2 // MOE_BLK + N_EXP) * MOE_BLK
    nb = p_rows // MOE_BLK
    rank, cnt = _ranks(eidx)
    counts = cnt[:, 0].astype(i32)
    padded = (counts + MOE_BLK - 1) // MOE_BLK * MOE_BLK
    pad_end = jnp.cumsum(padded)
    pad_start = pad_end - padded
    start_of = jnp.sum(jnp.where(eidx[..., None] == jnp.arange(N_EXP, dtype=i32), pad_start, 0), axis=-1)
    dest_flat = (start_of + rank).reshape(-1)
    nb_used = (pad_end[-1] // MOE_BLK).astype(i32)
    spare = nb_used + jnp.arange(N_EXP, dtype=i32)
    tails = jnp.concatenate([(pad_start + counts) // ZERO_ROWS, pad_end // ZERO_ROWS,
                             jnp.where(spare < nb, spare * MOE_BLK, -1)]).astype(i32)
    blk = jnp.minimum(jnp.arange(nb, dtype=i32), nb_used - 1)
    seg_done = (pad_end[None, :] <= (blk * MOE_BLK)[:, None]).astype(i32)
    blk_e = jnp.minimum(jnp.sum(seg_done, axis=1), N_EXP - 1).astype(i32)
    xb = _dispatch(h2, dest_flat, tails, p_rows)
    y = _experts(xb, blk_e, nb_used.reshape(1), w1, w3, w2, layer)
    return y, dest_flat


def kernel(x, c, w_ada, b_ada, attn_norm, ffn_norm, w_in, gla_gate_w, gla_gate_b, gla_out_norm,
           na_q_norm, na_k_norm, na_rpb, w_out, router_w, router_b, w1, w3, w2):
    bsz, seq, dim = x.shape
    depth = w_ada.shape[0]
    assert dim == D and seq == GRID_W * GRID_W, (dim, seq)
    assert bsz % GLA_NB == 0 and bsz <= 8, bsz
    assert seq % TM_PROJ == 0 and seq % TM_COMB_PROJ == 0 and seq % (NA_SUB * NA_QROWS * GRID_W) == 0
    assert (bsz * seq) % TM_DISP == 0 and (bsz * seq) % TM_COMB == 0 and (bsz * seq) % TM_RANK == 0
    assert MOE_BLK % ZERO_ROWS == 0 and (2 * bsz * seq) % MOE_BLK == 0
    assert w_in.shape == (depth, D, 2 * QK_W + 2 * V_W + 2 * GLA_RANK + 3 * NA_W), w_in.shape
    assert w1.shape == (depth, N_EXP, D, D_EXP) and router_w.shape == (D, N_EXP)
    mod = _adaln_mod(c, w_ada, b_ada)
    xc = x.reshape(bsz * seq, D)
    pending = None
    for l in range(depth):
        if pending is None:
            (qg, kg, vg, sg, la_f, la_b, qn, kn, vn), bounded_f, bounded_b = _inproj(
                xc, mod[l], attn_norm[l], w_in[l], gla_gate_w[l], gla_gate_b[l], bsz, seq)
        else:
            y, dest_flat, x1, gw_t = pending
            xc, (qg, kg, vg, sg, la_f, la_b, qn, kn, vn), bounded_f, bounded_b = _combine_inproj(
                y, dest_flat, x1, gw_t, mod[l - 1], mod[l], attn_norm[l], w_in[l], gla_gate_w[l],
                gla_gate_b[l], bsz, seq)
        o_f = _gla_dir(False, qg, kg, vg, la_f, bounded_f, bsz, seq)
        o_gla = _gla_dir(True, qg, kg, vg, la_b, bounded_b, bsz, seq, extra=(o_f, sg, gla_out_norm[l]))
        o_na = _natten(qn, kn, vn, na_q_norm[l], na_k_norm[l], na_rpb[l], bsz, seq)
        x1, h2, eidx, gw = _outproj(o_gla, o_na, w_out[l], xc, mod[l], ffn_norm[l],
                                    router_w, router_b, bsz, seq)
        y, dest_flat = _moe_experts_sorted(h2, eidx, w1, w3, w2, l)
        pending = (y, dest_flat, x1, gw.T)
    y, dest_flat, x1, gw_t = pending
    xc = _combine(y, dest_flat, x1, gw_t, mod[depth - 1], bsz, seq)
    return xc.reshape(bsz, seq, D)
```

```python
import functools

import numpy as np
import jax
import jax.numpy as jnp
from jax import lax
from jax.experimental import pallas as pl
from jax.experimental.pallas import tpu as pltpu

f32 = jnp.float32
bf16 = jnp.bfloat16
i32 = jnp.int32

D = 1024
GRID_W = 64
GLA_H, GLA_DK, GLA_DV = 4, 64, 128
GLA_RANK = 16
GLA_TAU = 16.0
CHUNK = 64
NA_H, NA_DH = 8, 64
WIN_H, WIN_W = 8, 16
N_EXP, N_GRP, EPG = 32, 4, 8
D_EXP = 512
QK_W = GLA_H * GLA_DK
V_W = GLA_H * GLA_DV
NA_W = NA_H * NA_DH
EPS = 1e-6
NEG = -1e30

TM_PROJ = 1024
TB_GLA = 256
GLA_NB = 4
NA_QROWS = 4
NA_KROWS = 12
NA_SUB = 8
TM_RANK = 1024
TM_DISP = 512
TM_COMB = 512
TM_COMB_PROJ = 512
MOE_BLK = 512
ZERO_ROWS = 64
N_LEVELS = 6
GLA_SAFE_RANGE = 40.0


def _nt(a, b):
    return lax.dot_general(a, b, (((1,), (1,)), ((), ())), preferred_element_type=f32)


def _tn(a, b):
    return lax.dot_general(a, b, (((0,), (0,)), ((), ())), preferred_element_type=f32)


def _split3(x):
    hi = x.astype(bf16)
    r = x - hi.astype(f32)
    mid = r.astype(bf16)
    lo = (r - mid.astype(f32)).astype(bf16)
    return hi, mid, lo


def _sel_l(m01, x):
    hi, mid, lo = _split3(x)
    d = lambda p: jnp.dot(m01, p, preferred_element_type=f32)
    return d(hi) + d(mid) + d(lo)


def _sel_r(x, m01):
    hi, mid, lo = _split3(x)
    d = lambda p: jnp.dot(p, m01, preferred_element_type=f32)
    return d(hi) + d(mid) + d(lo)


def _mod_kernel(c_ref, w_ref, b_ref, o_ref):
    c = c_ref[...]
    ca = c * jax.nn.sigmoid(c)
    w = w_ref[0]
    c_hi = ca.astype(bf16)
    c_lo = (ca - c_hi.astype(f32)).astype(bf16)
    w_hi = w.astype(bf16)
    acc = jnp.dot(c_hi, w_hi, preferred_element_type=f32) + jnp.dot(c_lo, w_hi, preferred_element_type=f32)
    o_ref[0, 0] = acc + b_ref[0, 0]


def _adaln_mod(c, w_ada, b_ada):
    depth = w_ada.shape[0]
    bsz = c.shape[0]
    cp = jnp.zeros((8, D), f32).at[:bsz].set(c)
    out = pl.pallas_call(
        _mod_kernel,
        grid=(depth, 6),
        in_specs=[pl.BlockSpec((8, D), lambda l, j: (0, 0)),
                  pl.BlockSpec((1, D, D), lambda l, j: (l, 0, j)),
                  pl.BlockSpec((1, 1, 1, D), lambda l, j: (l, j, 0, 0))],
        out_specs=pl.BlockSpec((1, 1, 8, D), lambda l, j: (l, j, 0, 0)),
        out_shape=jax.ShapeDtypeStruct((depth, 6, 8, D), f32),
        name="adaln_mod",
    )(cp, w_ada, b_ada.reshape(depth, 6, 1, D))
    return out[:, :, :bsz].reshape(depth, 6, bsz, 1, D)


def _log_sigmoid(z):
    return jnp.minimum(z, 0.0) - jnp.log1p(jnp.exp(-jnp.abs(z)))


def _inproj_body(x, mod_ref, nrm_ref, wm_ref, wlr_ref, gf_ref, bf_ref,
                 qg_ref, kg_ref, vg_ref, sg_ref, laf_ref, lab_ref, qn_ref, kn_ref, vn_ref, lamin_ref,
                 wg_scr):
    y = x * lax.rsqrt(jnp.mean(x * x, axis=-1, keepdims=True) + EPS) * nrm_ref[...]
    h = y * (1.0 + mod_ref[1]) + mod_ref[0]
    hb = h.astype(bf16)
    proj = lambda lo, hi: jnp.dot(hb, wm_ref[:, lo:hi], preferred_element_type=f32)
    qg_ref[...] = (proj(0, 256) * (GLA_DK ** -0.5)).astype(bf16)
    kg_ref[...] = proj(256, 512).astype(bf16)
    vg_ref[...] = proj(512, 1024).astype(bf16)
    gg = proj(1024, 1536)
    sg_ref[...] = (gg * jax.nn.sigmoid(gg)).astype(bf16)
    qn_ref[...] = proj(1536, 2048).astype(bf16)
    kn_ref[...] = proj(2048, 2560).astype(bf16)
    vn_ref[...] = proj(2560, 3072).astype(bf16)
    z = jnp.dot(hb, wg_scr[...], preferred_element_type=f32) + bf_ref[...]
    la = _log_sigmoid(z) * (1.0 / GLA_TAU)
    la_f = la[:, 0:QK_W]
    la_b = la[:, QK_W:2 * QK_W]
    laf_ref[...] = la_f
    lab_ref[...] = la_b
    rows = []
    for la in (la_f, la_b):
        for blk in range(la.shape[0] // TB_GLA):
            tot = jnp.sum(la[blk * TB_GLA:(blk + 1) * TB_GLA], axis=0, keepdims=True)
            rows.append(jnp.broadcast_to(jnp.min(tot, axis=1, keepdims=True), (1, 128)))
    if len(rows) < 8:
        rows.append(jnp.zeros((8 - len(rows), 128), f32))
    lamin_ref[...] = jnp.concatenate(rows, axis=0)


def _compose_gate_proj(first_step, wlr_ref, gf_ref, wg_scr):
    @pl.when(first_step)
    def _():
        wg_scr[...] = jnp.dot(wlr_ref[...], gf_ref[...], precision=lax.Precision.HIGHEST,
                              preferred_element_type=f32).astype(bf16)


def _inproj_kernel(x_ref, *refs):
    _compose_gate_proj((pl.program_id(0) == 0) & (pl.program_id(1) == 0), refs[3], refs[4], refs[-1])
    _inproj_body(x_ref[...], *refs)


def _combine_inproj_kernel(dest_ref, x1_ref, gw_ref, modp_ref, y_ref, *refs):
    x2_ref = refs[6]
    ybuf, sems, wg_scr = refs[-3], refs[-2], refs[-1]
    proj_refs = refs[:6] + refs[7:-3] + (wg_scr,)
    i = pl.program_id(0) * pl.num_programs(1) + pl.program_id(1)
    nsteps = pl.num_programs(0) * pl.num_programs(1)
    tm = x1_ref.shape[0]
    n = nsteps * tm

    nslot = ybuf.shape[0]

    def row_copies(j, slot, t):
        for k in range(2):
            d = dest_ref[k * n + j * tm + t]
            pltpu.make_async_copy(y_ref.at[pl.ds(d, 1)], ybuf.at[slot, k, pl.ds(t, 1)], sems.at[slot]).start()

    def issue_tile(j, slot):
        for t in range(tm):
            row_copies(j, slot, t)

    def wait_tile(slot):
        for k in range(2):
            pltpu.make_async_copy(y_ref.at[pl.ds(0, tm)], ybuf.at[slot, k], sems.at[slot]).wait()

    slot = i % nslot
    _compose_gate_proj(i == 0, refs[3], refs[4], wg_scr)

    @pl.when(i == 0)
    def _():
        for j in range(nslot - 1):
            lax.fori_loop(0, tm, lambda t, c, j=j: (row_copies(j, j, t), c)[1], 0)

    wait_tile(slot)
    gw = gw_ref[...]
    y = ybuf[slot, 0] * gw[:, 0:1] + ybuf[slot, 1] * gw[:, 1:2]
    x2 = x1_ref[...] + modp_ref[5] * y
    x2_ref[...] = x2
    ahead = i + nslot - 1
    issue_tile(jnp.where(ahead < nsteps, ahead, ahead - nsteps), ahead % nslot)
    _inproj_body(x2, *proj_refs)

    @pl.when(i == nsteps - 1)
    def _():
        for back in range(1, nslot):
            wait_tile((i + back) % nslot)


_PROJ_WIDTHS = [(QK_W, bf16), (QK_W, bf16), (V_W, bf16), (V_W, bf16), (QK_W, f32), (QK_W, f32),
                (NA_W, bf16), (NA_W, bf16), (NA_W, bf16)]


def _inproj_operands(nrm, w_in, gate_w, gate_b, imap):
    lo, hi = 2 * QK_W + 2 * V_W, 2 * QK_W + 2 * V_W + 2 * GLA_RANK
    wm = jnp.concatenate([w_in[:, :lo], w_in[:, hi:]], axis=1).astype(bf16)
    wlr = w_in[:, lo:hi]
    zpad = jnp.zeros((GLA_RANK, QK_W), f32)
    gmat = jnp.concatenate([jnp.concatenate([gate_w[0], zpad], axis=1),
                            jnp.concatenate([zpad, gate_w[1]], axis=1)], axis=0)
    gbias = jnp.concatenate([gate_b[0], gate_b[1]]).reshape(1, 2 * QK_W)
    args = [nrm.reshape(1, D), wm, wlr, gmat, gbias]
    return args, [pl.BlockSpec(a.shape, imap) for a in args]


def _bounded_flags(lamin, tm):
    per = tm // TB_GLA
    lamin = lamin[:, :2 * per, 0]
    bounded_f = (lamin[:, :per].reshape(-1) > -GLA_SAFE_RANGE).astype(i32)
    bounded_b = (lamin[:, per:].reshape(-1) > -GLA_SAFE_RANGE).astype(i32)
    return bounded_f, bounded_b


def _inproj(x2d, mod_l, nrm, w_in, gate_w, gate_b, bsz, seq):
    n = x2d.shape[0]
    tm = TM_PROJ
    nt = seq // tm
    row = lambda b, i: (b * nt + i, 0)
    w_args, w_specs = _inproj_operands(nrm, w_in, gate_w, gate_b, lambda b, i: (0, 0))
    outs = pl.pallas_call(
        _inproj_kernel,
        grid=(bsz, nt),
        in_specs=[pl.BlockSpec((tm, D), row),
                  pl.BlockSpec((6, None, 1, D), lambda b, i: (0, b, 0, 0))] + w_specs,
        out_specs=[pl.BlockSpec((tm, w), row) for w, _ in _PROJ_WIDTHS]
        + [pl.BlockSpec((None, 8, 128), lambda b, i: (b * nt + i, 0, 0))],
        out_shape=[jax.ShapeDtypeStruct((n, w), dt) for w, dt in _PROJ_WIDTHS]
        + [jax.ShapeDtypeStruct((n // tm, 8, 128), f32)],
        scratch_shapes=[pltpu.VMEM((D, 2 * QK_W), bf16)],
        compiler_params=pltpu.CompilerParams(dimension_semantics=("arbitrary", "arbitrary")),
        name="inproj",
    )(x2d, mod_l, *w_args)
    return outs[:-1], *_bounded_flags(outs[-1], tm)


def _combine_inproj(y, dest_flat, x1, gw_t, mod_prev, mod_l, nrm, w_in, gate_w, gate_b, bsz, seq):
    n = x1.shape[0]
    tm = TM_COMB_PROJ
    nt = seq // tm
    row = lambda b, i, *_: (b * nt + i, 0)
    mods = lambda b, i, *_: (0, b, 0, 0)
    w_args, w_specs = _inproj_operands(nrm, w_in, gate_w, gate_b, lambda b, i, *_: (0, 0))
    outs = pl.pallas_call(
        _combine_inproj_kernel,
        grid_spec=pltpu.PrefetchScalarGridSpec(
            num_scalar_prefetch=1,
            grid=(bsz, nt),
            in_specs=[pl.BlockSpec((tm, D), row),
                      pl.BlockSpec((tm, 2), row),
                      pl.BlockSpec((6, None, 1, D), mods),
                      pl.BlockSpec(memory_space=pl.ANY),
                      pl.BlockSpec((6, None, 1, D), mods)] + w_specs,
            out_specs=[pl.BlockSpec((tm, D), row)]
            + [pl.BlockSpec((tm, w), row) for w, _ in _PROJ_WIDTHS]
            + [pl.BlockSpec((None, 8, 128), lambda b, i, *_: (b * nt + i, 0, 0))],
            scratch_shapes=[pltpu.VMEM((3, 2, tm, D), f32), pltpu.SemaphoreType.DMA((3,)),
                            pltpu.VMEM((D, 2 * QK_W), bf16)],
        ),
        out_shape=[jax.ShapeDtypeStruct((n, D), f32)]
        + [jax.ShapeDtypeStruct((n, w), dt) for w, dt in _PROJ_WIDTHS]
        + [jax.ShapeDtypeStruct((n // tm, 8, 128), f32)],
        compiler_params=pltpu.CompilerParams(dimension_semantics=("arbitrary", "arbitrary"),
                                             vmem_limit_bytes=56 * 1024 * 1024),
        name="combine_inproj",
    )(dest_flat, x1, gw_t, mod_prev, y, mod_l, *w_args)
    return outs[0], outs[1:-1], *_bounded_flags(outs[-1], tm)


def _gla_consts(rev, tb):
    c = CHUNK
    idx = np.arange(c)
    if not rev:
        cum = (idx[None, :] <= idx[:, None]).astype(np.float32)
    else:
        cum = (idx[None, :] >= idx[:, None]).astype(np.float32)
    mats = [cum]
    masks = []
    for lvl in range(N_LEVELS):
        s = c >> (lvl + 1)
        blk = idx // (2 * s)
        second = (idx % (2 * s)) >= s
        ref_row = blk * 2 * s + (s if rev else s - 1)
        mats.append(cum[ref_row])
        same = blk[:, None] == blk[None, :]
        if not rev:
            m = same & second[:, None] & (~second[None, :])
        else:
            m = same & (~second[:, None]) & second[None, :]
        masks.append(m)
    masks.append(np.eye(c, dtype=bool) & (not rev))
    mst = np.concatenate(mats, axis=0)
    msk = np.stack([np.tile(m, (1, GLA_H)) for m in masks]).astype(np.float32)
    tix = np.arange(tb)
    tri = (tix[None, :] >= tix[:, None]) if rev else (tix[None, :] <= tix[:, None])
    cmask = (tix[:, None] < tix[None, :]) if rev else (tix[:, None] >= tix[None, :])
    nch = tb // c
    selb = np.zeros((tb, nch * 128), np.float32)
    for ch in range(nch):
        selb[ch * c:(ch + 1) * c, ch * 128:(ch + 1) * 128] = 1.0
    hk = np.arange(QK_W) // GLA_DK
    hv = np.arange(V_W) // GLA_DV
    kmask = (hk[:, None] == hk[None, :]).astype(np.float32)
    vmask = (hk[:, None] == hv[None, :]).astype(np.float32)
    return (jnp.asarray(mst, bf16), jnp.asarray(msk, f32), jnp.asarray(selb, bf16),
            jnp.asarray(kmask, bf16), jnp.asarray(vmask, bf16), jnp.asarray(vmask, f32),
            jnp.asarray(tri, bf16), jnp.asarray(cmask, f32), jnp.ones((tb, 128), bf16))


def _gla_kernel(rev, tb, *refs):
    if rev:
        (bounded_ref, q_ref, k_ref, v_ref, la_ref, mst_ref, msk_ref, selb_ref, kmask_ref, vmask_ref,
         smask_ref, tri_ref, cmask_ref, ones_ref, of_ref, sg_ref, gn_ref, out_ref, s_scr, o_scr) = refs
    else:
        (bounded_ref, q_ref, k_ref, v_ref, la_ref, mst_ref, msk_ref, selb_ref, kmask_ref, vmask_ref,
         smask_ref, tri_ref, cmask_ref, ones_ref, out_ref, s_scr) = refs
        o_scr = out_ref
    c = CHUNK
    nch = tb // c
    elems = range(GLA_NB)

    @pl.when(pl.program_id(1) == 0)
    def _():
        s_scr[...] = jnp.zeros_like(s_scr)

    def blocks_bounded():
        tri = tri_ref[...]
        cmask = cmask_ref[...]
        las = [la_ref[e] for e in elems]
        tots = [_sel_r(la.T, ones_ref[...]) for la in las]
        bcums = []
        for la in las:
            la_hi = la.astype(bf16)
            la_lo = (la - la_hi.astype(f32)).astype(bf16)
            bcums.append(jnp.dot(tri, la_hi, preferred_element_type=f32)
                         + jnp.dot(tri, la_lo, preferred_element_type=f32))
        ks = [k_ref[e].astype(f32) for e in elems]
        qes = [(q_ref[e].astype(f32) * jnp.exp(bcums[e])).astype(bf16) for e in elems]
        kxs = [(ks[e] * jnp.exp(-bcums[e])).astype(bf16) for e in elems]
        s_prevs = [s_scr[e] for e in elems]
        o_inters = [jnp.dot(qes[e], s_prevs[e].astype(bf16), preferred_element_type=f32) for e in elems]
        lane_head = lax.broadcasted_iota(i32, (tb, QK_W), 1) // GLA_DK
        for hd in range(GLA_H):
            cols = slice(hd * GLA_DV, (hd + 1) * GLA_DV)
            phs = []
            for e in elems:
                qh = jnp.where(lane_head == hd, qes[e], jnp.zeros_like(qes[e]))
                phs.append(jnp.where(cmask > 0.0, _nt(qh, kxs[e]), 0.0).astype(bf16))
            for e in elems:
                o_blk = o_inters[e][:, cols] + jnp.dot(phs[e], v_ref[e, :, cols], preferred_element_type=f32)
                o_scr[e, :, cols] = o_blk.astype(o_scr.dtype)
        for e in elems:
            blast = bcums[e][0:1] if rev else bcums[e][tb - 1:tb]
            ke = (ks[e] * jnp.exp(blast - bcums[e])).astype(bf16)
            dec = jnp.exp(tots[e])
            s_scr[e] = (s_prevs[e] * jnp.concatenate([dec] * GLA_H, axis=1)
                        + _tn(ke, v_ref[e]) * smask_ref[...])

    def chunks_robust():
        mst = mst_ref[...]
        kmask = kmask_ref[...]
        vmask = vmask_ref[...]
        for e in elems:
            tot = _sel_r(la_ref[e].T, selb_ref[...])
            for ch in (range(nch - 1, -1, -1) if rev else range(nch)):
                rows = slice(ch * c, (ch + 1) * c)
                q = q_ref[e, rows, :].astype(f32)
                k = k_ref[e, rows, :].astype(f32)
                v = v_ref[e, rows, :]
                r_all = _sel_l(mst, la_ref[e, rows, :])
                bcum = r_all[0:c]
                qe = (q * jnp.exp(bcum)).astype(bf16)
                o = jnp.dot(qe, s_scr[e].astype(bf16), preferred_element_type=f32)
                sall = jnp.zeros((c, QK_W), f32)
                for lvl in range(N_LEVELS):
                    ref = r_all[(lvl + 1) * c:(lvl + 2) * c]
                    a = (q * jnp.exp(jnp.minimum(bcum - ref, 0.0))).astype(bf16)
                    bm = (k * jnp.exp(jnp.minimum(ref - bcum, 0.0))).astype(bf16)
                    bbd = jnp.concatenate([bm] * GLA_H, axis=0) * kmask
                    sall = sall + jnp.where(msk_ref[lvl] > 0.0, _nt(a, bbd), 0.0)
                if not rev:
                    bbd = jnp.concatenate([k_ref[e, rows, :]] * GLA_H, axis=0) * kmask
                    sall = sall + jnp.where(msk_ref[N_LEVELS] > 0.0, _nt(q_ref[e, rows, :], bbd), 0.0)
                vbd = jnp.concatenate([v] * GLA_H, axis=0) * vmask
                o_blk = o + jnp.dot(sall.astype(bf16), vbd, preferred_element_type=f32)
                o_scr[e, rows, :] = o_blk.astype(o_scr.dtype)
                blast = bcum[0:1] if rev else bcum[c - 1:c]
                ke = (k * jnp.exp(blast - bcum)).astype(bf16)
                dec = jnp.exp(tot[:, ch * 128:(ch + 1) * 128])
                s_scr[e] = (s_scr[e] * jnp.concatenate([dec] * GLA_H, axis=1)
                            + _tn(ke, v) * smask_ref[...])

    nb = pl.num_programs(1)
    pos = nb - 1 - pl.program_id(1) if rev else pl.program_id(1)
    bounded = bounded_ref[(pl.program_id(0) * GLA_NB) * nb + pos] > 0
    for e in range(1, GLA_NB):
        bounded = bounded & (bounded_ref[(pl.program_id(0) * GLA_NB + e) * nb + pos] > 0)
    pl.when(bounded)(blocks_bounded)
    pl.when(jnp.logical_not(bounded))(chunks_robust)

    if rev:
        gn = gn_ref[...]
        for e in elems:
            for hd in range(GLA_H):
                cols = slice(hd * GLA_DV, (hd + 1) * GLA_DV)
                t = of_ref[e, :, cols].astype(f32) + o_scr[e, :, cols]
                y = t * lax.rsqrt(jnp.mean(t * t, axis=-1, keepdims=True) + EPS) * gn
                out_ref[e, :, cols] = (y * sg_ref[e, :, cols].astype(f32)).astype(bf16)


def _gla_dir(rev, q, k, v, la, bounded, bsz, seq, extra=None):
    n = q.shape[0]
    tb = TB_GLA
    nb = seq // tb
    consts = _gla_consts(rev, tb)
    if rev:
        row = lambda b, i, flags: (b, nb - 1 - i, 0)
    else:
        row = lambda b, i, flags: (b, i, 0)
    const2 = lambda b, i, flags: (0, 0)
    const3 = lambda b, i, flags: (0, 0, 0)
    per_batch = lambda a: a.reshape(bsz, seq, a.shape[-1])
    q, k, v, la = per_batch(q), per_batch(k), per_batch(v), per_batch(la)
    in_specs = [pl.BlockSpec((GLA_NB, tb, QK_W), row), pl.BlockSpec((GLA_NB, tb, QK_W), row),
                pl.BlockSpec((GLA_NB, tb, V_W), row), pl.BlockSpec((GLA_NB, tb, QK_W), row),
                pl.BlockSpec(consts[0].shape, const2), pl.BlockSpec(consts[1].shape, const3),
                pl.BlockSpec(consts[2].shape, const2), pl.BlockSpec(consts[3].shape, const2),
                pl.BlockSpec(consts[4].shape, const2), pl.BlockSpec(consts[5].shape, const2),
                pl.BlockSpec(consts[6].shape, const2), pl.BlockSpec(consts[7].shape, const2),
                pl.BlockSpec(consts[8].shape, const2)]
    args = [q, k, v, la, *consts]
    scratch = [pltpu.VMEM((GLA_NB, QK_W, V_W), f32)]
    if rev:
        o_f, sg, gn = extra
        in_specs += [pl.BlockSpec((GLA_NB, tb, V_W), row), pl.BlockSpec((GLA_NB, tb, V_W), row),
                     pl.BlockSpec((1, GLA_DV), const2)]
        args += [per_batch(o_f), per_batch(sg), gn.reshape(1, GLA_DV)]
        scratch.append(pltpu.VMEM((GLA_NB, tb, V_W), f32))
    out = pl.pallas_call(
        functools.partial(_gla_kernel, rev, tb),
        grid_spec=pltpu.PrefetchScalarGridSpec(
            num_scalar_prefetch=1,
            grid=(bsz // GLA_NB, nb),
            in_specs=in_specs,
            out_specs=pl.BlockSpec((GLA_NB, tb, V_W), row),
            scratch_shapes=scratch,
        ),
        out_shape=jax.ShapeDtypeStruct((bsz, seq, V_W), bf16),
        compiler_params=pltpu.CompilerParams(dimension_semantics=("arbitrary", "arbitrary")),
        name="gla_bwd" if rev else "gla_fwd",
    )(bounded, *args)
    return out.reshape(n, V_W)


def _rpb_expand_kernel(rpb_ref, sel_ref, valid_ref, o_ref):
    e = _sel_r(rpb_ref[...], sel_ref[...])
    o_ref[...] = jnp.where(valid_ref[...] > 0.0, e, NEG)


def _na_bias_table(rpb):
    nri, nci = 2 * WIN_H - 1, 2 * WIN_W - 1
    qc = np.arange(GRID_W)
    col_start = np.clip(qc - WIN_W // 2, 0, GRID_W - WIN_W)
    kc = np.arange(GRID_W)
    valid = (kc[None, :] >= col_start[:, None]) & (kc[None, :] < col_start[:, None] + WIN_W)
    ci = kc[None, :] - qc[:, None] + WIN_W - 1
    sel = np.zeros((32, GRID_W * GRID_W), np.float32)
    flat_ci = ci.reshape(-1)
    ok = (flat_ci >= 0) & (flat_ci < nci)
    sel[flat_ci[ok], np.nonzero(ok)[0]] = 1.0
    rpb2 = jnp.zeros((NA_H * nri + 8 - (NA_H * nri) % 8, 32), f32).at[:NA_H * nri, :nci].set(
        rpb.reshape(NA_H * nri, nci))
    nrow = rpb2.shape[0]
    e = pl.pallas_call(
        _rpb_expand_kernel,
        out_shape=jax.ShapeDtypeStruct((nrow, GRID_W * GRID_W), f32),
        name="rpb_expand",
    )(rpb2, jnp.asarray(sel, bf16), jnp.asarray(valid.reshape(1, -1), f32))
    return e[:NA_H * nri].reshape(NA_H // 2, 2, nri, GRID_W, GRID_W)


def _na_row_classes():
    rows = GRID_W
    ri = -np.ones((3, NA_QROWS, NA_KROWS), np.int64)
    for cls, rb in enumerate((0, 1, rows // NA_QROWS - 1)):
        r0 = rb * NA_QROWS
        kb = int(np.clip(r0 - WIN_H // 2, 0, rows - NA_KROWS))
        for rq in range(NA_QROWS):
            r = r0 + rq
            rs = int(np.clip(r - WIN_H // 2, 0, rows - WIN_H))
            for m in range(NA_KROWS):
                kr = kb + m
                if rs <= kr < rs + WIN_H:
                    ri[cls, rq, m] = kr - r + WIN_H - 1
    return ri


def _natten_kernel(q_ref, k_ref, v_ref, qn_ref, kn_ref, gsum_ref, e_ref, o_ref, kn_scr, tbl_scr):
    b = pl.program_id(1)
    rb = pl.program_id(2)
    gsum = gsum_ref[...]
    log2e = 1.4426950408889634

    def headnorm(x, g):
        sq = x * x
        hi = sq.astype(bf16)
        lo = (sq - hi.astype(f32)).astype(bf16)
        ms = (jnp.dot(hi, gsum, preferred_element_type=f32)
              + jnp.dot(lo, gsum, preferred_element_type=f32)) * (1.0 / NA_DH)
        return x * lax.rsqrt(ms + EPS) * g

    @pl.when((b == 0) & (rb == 0))
    def _():
        ri = _na_row_classes()
        neg = jnp.full((GRID_W, GRID_W), NEG, f32)
        for hh in range(2):
            for cls in range(3):
                for rq in range(NA_QROWS):
                    for mp in range(NA_KROWS // 2):
                        parts = []
                        for m in (2 * mp, 2 * mp + 1):
                            r = int(ri[cls, rq, m])
                            parts.append(neg if r < 0 else e_ref[hh, r] * log2e)
                        r0 = (hh * NA_QROWS + rq) * GRID_W
                        tbl_scr[cls, r0:r0 + GRID_W, mp * 128:(mp + 1) * 128] = (
                            jnp.concatenate(parts, axis=1))

    @pl.when(rb == 0)
    def _():
        kn_scr[...] = headnorm(k_ref[...].astype(f32), kn_ref[...]).astype(bf16)

    nq = NA_QROWS * GRID_W
    nk = NA_KROWS * GRID_W
    nblk = GRID_W // NA_QROWS
    first = lax.broadcasted_iota(i32, (nq, 128), 1) < NA_DH
    qs, wins, tbls = [], [], []
    for sub in range(NA_SUB):
        qb = rb * NA_SUB + sub
        q = headnorm(q_ref[sub * nq:(sub + 1) * nq, :].astype(f32), qn_ref[...]) * (NA_DH ** -0.5 * log2e)
        kb = jnp.clip(qb * NA_QROWS - WIN_H // 2, 0, GRID_W - NA_KROWS)
        start = pl.multiple_of(kb * GRID_W, GRID_W)
        cls = jnp.where(qb == 0, 0, jnp.where(qb == nblk - 1, 2, 1))
        qs.append([jnp.where(first, q, 0.0).astype(bf16), jnp.where(first, 0.0, q).astype(bf16)])
        wins.append((kn_scr[pl.ds(start, nk), :], v_ref[pl.ds(start, nk), :]))
        tbls.append(cls)
    scores = [[_nt(qs[sub][hh], wins[sub][0]) + tbl_scr[tbls[sub], hh * nq:(hh + 1) * nq, :]
               for hh in range(2)] for sub in range(NA_SUB)]
    for sub in range(NA_SUB):
        outs = []
        for s in scores[sub]:
            m = jnp.max(s, axis=-1, keepdims=True)
            p = jnp.exp2(s - m)
            l = jnp.sum(p, axis=-1, keepdims=True)
            outs.append(jnp.dot(p.astype(bf16), wins[sub][1], preferred_element_type=f32) / l)
        o_ref[sub * nq:(sub + 1) * nq, :] = jnp.where(first, outs[0], outs[1]).astype(bf16)


def _natten(qn, kn, vn, q_norm, k_norm, rpb, bsz, seq):
    n = qn.shape[0]
    e5 = _na_bias_table(rpb)
    nq = NA_QROWS * GRID_W
    nqs = NA_SUB * nq
    nrb = seq // nqs
    lane_h = np.arange(128) // NA_DH
    gsum = jnp.asarray(lane_h[:, None] == lane_h[None, :], bf16)
    qn2 = jnp.tile(q_norm.reshape(1, NA_DH), (1, 2))
    kn2 = jnp.tile(k_norm.reshape(1, NA_DH), (1, 2))
    const2 = lambda p, b, r: (0, 0)
    return pl.pallas_call(
        _natten_kernel,
        grid=(NA_H // 2, bsz, nrb),
        in_specs=[pl.BlockSpec((nqs, 128), lambda p, b, r: (b * nrb + r, p)),
                  pl.BlockSpec((seq, 128), lambda p, b, r: (b, p)),
                  pl.BlockSpec((seq, 128), lambda p, b, r: (b, p)),
                  pl.BlockSpec((1, 128), const2),
                  pl.BlockSpec((1, 128), const2),
                  pl.BlockSpec((128, 128), const2),
                  pl.BlockSpec((None,) + e5.shape[1:], lambda p, b, r: (p, 0, 0, 0, 0))],
        out_specs=pl.BlockSpec((nqs, 128), lambda p, b, r: (b * nrb + r, p)),
        out_shape=jax.ShapeDtypeStruct((n, NA_W), bf16),
        scratch_shapes=[pltpu.VMEM((seq, 128), bf16),
                        pltpu.VMEM((3, 2 * nq, NA_KROWS * GRID_W), f32)],
        compiler_params=pltpu.CompilerParams(dimension_semantics=("arbitrary", "arbitrary", "arbitrary")),
        name="natten",
    )(qn, kn, vn, qn2, kn2, gsum, e5)


def _top2(vals):
    io = lax.broadcasted_iota(i32, vals.shape, 0)
    m1 = jnp.max(vals, axis=0, keepdims=True)
    i1 = jnp.min(jnp.where(vals == m1, io, EPG), axis=0, keepdims=True)
    v2 = jnp.where(io == i1, -jnp.inf, vals)
    m2 = jnp.max(v2, axis=0, keepdims=True)
    i2 = jnp.min(jnp.where(v2 == m2, io, EPG), axis=0, keepdims=True)
    return m1, i1, m2, i2


def _outproj_kernel(og_ref, on_ref, wo_ref, x_ref, mod_ref, nrm_ref, rw_ref, rb_ref,
                    x1_ref, h2_ref, eidx_ref, gw_ref):
    mix = (jnp.dot(og_ref[...], wo_ref[0:V_W, :], preferred_element_type=f32)
           + jnp.dot(on_ref[...], wo_ref[V_W:V_W + NA_W, :], preferred_element_type=f32))
    x1 = x_ref[...] + mod_ref[2] * mix
    x1_ref[...] = x1
    y = x1 * lax.rsqrt(jnp.mean(x1 * x1, axis=-1, keepdims=True) + EPS) * nrm_ref[...]
    h2 = y * (1.0 + mod_ref[4]) + mod_ref[3]
    h2_ref[...] = h2
    rw = rw_ref[...]
    rw_hi = rw.astype(bf16)
    rw_lo = (rw - rw_hi.astype(f32)).astype(bf16)
    h_hi = h2.astype(bf16)
    h_lo = (h2 - h_hi.astype(f32)).astype(bf16)
    logits = _nt(rw_hi, h_hi) + _nt(rw_hi, h_lo) + _nt(rw_lo, h_hi)
    scores = jax.nn.sigmoid(logits)
    sel = scores + rb_ref[...]
    tops = [_top2(sel[g * EPG:(g + 1) * EPG]) for g in range(N_GRP)]
    best = jnp.zeros_like(tops[0][1])
    bs = tops[0][0] + tops[0][2]
    for g in range(1, N_GRP):
        gs = tops[g][0] + tops[g][2]
        take = gs > bs
        best = jnp.where(take, g, best)
        bs = jnp.where(take, gs, bs)
    io = lax.broadcasted_iota(i32, (EPG, sel.shape[1]), 0)
    i1 = jnp.zeros_like(best)
    i2 = jnp.zeros_like(best)
    s1 = jnp.zeros(best.shape, f32)
    s2 = jnp.zeros(best.shape, f32)
    for g in range(N_GRP):
        sc = scores[g * EPG:(g + 1) * EPG]
        pick = best == g
        i1 = jnp.where(pick, tops[g][1], i1)
        i2 = jnp.where(pick, tops[g][3], i2)
        s1 = jnp.where(pick, jnp.sum(jnp.where(io == tops[g][1], sc, 0.0), axis=0, keepdims=True), s1)
        s2 = jnp.where(pick, jnp.sum(jnp.where(io == tops[g][3], sc, 0.0), axis=0, keepdims=True), s2)
    eidx_ref[0:1, :] = best * EPG + i1
    eidx_ref[1:2, :] = best * EPG + i2
    tot = s1 + s2
    gw_ref[0:1, :] = s1 / tot
    gw_ref[1:2, :] = s2 / tot


def _outproj(o_gla, o_na, w_out, x2d, mod_l, nrm, router_w, router_b, bsz, seq):
    n = x2d.shape[0]
    tm = TM_PROJ
    nt = seq // tm
    row = lambda b, i: (b * nt + i, 0)
    col = lambda b, i: (0, b * nt + i)
    const = lambda b, i: (0, 0)
    return pl.pallas_call(
        _outproj_kernel,
        grid=(bsz, nt),
        in_specs=[pl.BlockSpec((tm, V_W), row),
                  pl.BlockSpec((tm, NA_W), row),
                  pl.BlockSpec((V_W + NA_W, D), const),
                  pl.BlockSpec((tm, D), row),
                  pl.BlockSpec((6, None, 1, D), lambda b, i: (0, b, 0, 0)),
                  pl.BlockSpec((1, D), const),
                  pl.BlockSpec((N_EXP, D), const),
                  pl.BlockSpec((N_EXP, 1), const)],
        out_specs=[pl.BlockSpec((tm, D), row), pl.BlockSpec((tm, D), row),
                   pl.BlockSpec((2, tm), col), pl.BlockSpec((2, tm), col)],
        out_shape=[jax.ShapeDtypeStruct((n, D), f32), jax.ShapeDtypeStruct((n, D), f32),
                   jax.ShapeDtypeStruct((2, n), i32), jax.ShapeDtypeStruct((2, n), f32)],
        compiler_params=pltpu.CompilerParams(dimension_semantics=("arbitrary", "arbitrary")),
        name="outproj_router",
    )(o_gla, o_na, w_out.astype(bf16), x2d, mod_l, nrm.reshape(1, D),
      router_w.T, router_b.reshape(N_EXP, 1))


def _rank_kernel(eidx_ref, tri_ref, rank_ref, cnt_ref, carry):
    @pl.when(pl.program_id(0) == 0)
    def _():
        carry[...] = jnp.zeros_like(carry)

    e = eidx_ref[...]
    tm = e.shape[1]
    io = lax.broadcasted_iota(i32, (N_EXP, tm), 0)
    run = carry[...]
    for k in range(2):
        oh = io == e[k:k + 1, :]
        ohf = oh.astype(f32)
        pre = jnp.dot(ohf.astype(bf16), tri_ref[...], preferred_element_type=f32) + run[:, 0:1]
        rank_ref[k:k + 1, :] = jnp.sum(jnp.where(oh, pre, 0.0), axis=0, keepdims=True).astype(i32)
        run = run + jnp.sum(ohf, axis=1, keepdims=True)
    carry[...] = run
    cnt_ref[...] = run


def _ranks(eidx):
    n = eidx.shape[1]
    tm = TM_RANK
    t = np.arange(tm)
    tri = jnp.asarray(t[:, None] < t[None, :], bf16)
    return pl.pallas_call(
        _rank_kernel,
        grid=(n // tm,),
        in_specs=[pl.BlockSpec((2, tm), lambda i: (0, i)), pl.BlockSpec((tm, tm), lambda i: (0, 0))],
        out_specs=[pl.BlockSpec((2, tm), lambda i: (0, i)), pl.BlockSpec((N_EXP, 128), lambda i: (0, 0))],
        out_shape=[jax.ShapeDtypeStruct((2, n), i32), jax.ShapeDtypeStruct((N_EXP, 128), f32)],
        scratch_shapes=[pltpu.VMEM((N_EXP, 128), f32)],
        compiler_params=pltpu.CompilerParams(dimension_semantics=("arbitrary",)),
        name="moe_rank",
    )(eidx, tri)


def _dispatch_kernel(dest_ref, tails_ref, h_ref, xb_ref, zbuf, hbuf, lsem, rsem, zsem, ssem):
    i = pl.program_id(0)
    nsteps = pl.num_programs(0)
    tm = TM_DISP
    n = nsteps * tm
    nbuf = hbuf.shape[0]

    def load(j):
        start = pl.multiple_of(j * tm, tm)
        return pltpu.make_async_copy(h_ref.at[pl.ds(start, tm)], hbuf.at[j % nbuf], lsem.at[j % nbuf])

    def drain_rows(j):
        for _ in range(2):
            pltpu.make_async_copy(hbuf.at[j % nbuf], xb_ref.at[pl.ds(0, tm)], rsem.at[j % nbuf]).wait()

    def pad_copy(chunk):
        start = pl.multiple_of(chunk * ZERO_ROWS, ZERO_ROWS)
        return pltpu.make_async_copy(zbuf.at[pl.ds(0, ZERO_ROWS)], xb_ref.at[pl.ds(start, ZERO_ROWS)], zsem)

    def spare_copy(j):
        start = pl.multiple_of(tails_ref[2 * N_EXP + j], MOE_BLK)
        return pltpu.make_async_copy(zbuf, xb_ref.at[pl.ds(start, MOE_BLK)], ssem)

    def for_pad_chunks(fn):
        for e in range(N_EXP):
            lax.fori_loop(tails_ref[e], tails_ref[N_EXP + e], lambda ch, c: (fn(ch), c)[1], 0)

    def for_spare_blocks(fn):
        for j in range(N_EXP):
            @pl.when(tails_ref[2 * N_EXP + j] >= 0)
            def _():
                fn(j)

    @pl.when(i == 0)
    def _():
        zbuf[...] = jnp.zeros_like(zbuf)
        for_pad_chunks(lambda ch: pad_copy(ch).start())
        for_spare_blocks(lambda j: spare_copy(j).start())
        for_pad_chunks(lambda ch: pad_copy(ch).wait())
        load(i).start()

    @pl.when(i >= nbuf - 1)
    def _():
        drain_rows(i + 1 - nbuf)

    @pl.when(i + 1 < nsteps)
    def _():
        load(i + 1).start()

    load(i).wait()
    slot = i % nbuf

    def issue(t, carry):
        src = hbuf.at[slot, pl.ds(t, 1)]
        for k in range(2):
            d = dest_ref[k * n + i * tm + t]
            pltpu.make_async_copy(src, xb_ref.at[pl.ds(d, 1)], rsem.at[slot]).start()
        return carry

    lax.fori_loop(0, tm, issue, 0, unroll=True)

    @pl.when(i == nsteps - 1)
    def _():
        for back in range(nbuf - 2, -1, -1):
            drain_rows(i - back)
        for_spare_blocks(lambda j: spare_copy(j).wait())


def _dispatch(h2, dest_flat, tails, p_rows):
    n = h2.shape[0]
    tm = TM_DISP
    return pl.pallas_call(
        _dispatch_kernel,
        grid_spec=pltpu.PrefetchScalarGridSpec(
            num_scalar_prefetch=2,
            grid=(n // tm,),
            in_specs=[pl.BlockSpec(memory_space=pl.ANY)],
            out_specs=pl.BlockSpec(memory_space=pl.ANY),
            scratch_shapes=[pltpu.VMEM((MOE_BLK, D), f32), pltpu.VMEM((3, tm, D), f32),
                            pltpu.SemaphoreType.DMA((3,)), pltpu.SemaphoreType.DMA((3,)),
                            pltpu.SemaphoreType.DMA(()), pltpu.SemaphoreType.DMA(())],
        ),
        out_shape=jax.ShapeDtypeStruct((p_rows, D), f32),
        compiler_params=pltpu.CompilerParams(dimension_semantics=("arbitrary",)),
        name="moe_dispatch",
    )(dest_flat, tails, h2)


def _expert_kernel(blk_e_ref, nb_ref, xb_ref, w1_ref, w3_ref, w2_ref, y_ref, w1b, w3b, w2b):
    i = pl.program_id(0)

    @pl.when((i < nb_ref[0]) & ((i == 0) | (blk_e_ref[i] != blk_e_ref[jnp.maximum(i - 1, 0)])))
    def _():
        w1b[...] = w1_ref[...].astype(bf16)
        w3b[...] = w3_ref[...].astype(bf16)
        w2b[...] = w2_ref[...].astype(bf16)

    @pl.when(i < nb_ref[0])
    def _():
        x = xb_ref[...].astype(bf16)
        a = jnp.dot(x, w1b[...], preferred_element_type=f32)
        b = jnp.dot(x, w3b[...], preferred_element_type=f32)
        h = (a * jax.nn.sigmoid(a) * b).astype(bf16)
        y_ref[...] = jnp.dot(h, w2b[...], preferred_element_type=f32)

    @pl.when(i >= nb_ref[0])
    def _():
        y_ref[...] = jnp.zeros_like(y_ref)


def _experts(xb, blk_e, nb_used, w1, w3, w2, layer):
    p_rows = xb.shape[0]
    nb = p_rows // MOE_BLK

    def xmap(i, be, nbu):
        return (jnp.minimum(i, nbu[0] - 1), 0)

    def wmap(i, be, nbu):
        return (layer, be[i], 0, 0)

    return pl.pallas_call(
        _expert_kernel,
        grid_spec=pltpu.PrefetchScalarGridSpec(
            num_scalar_prefetch=2,
            grid=(nb,),
            in_specs=[pl.BlockSpec((MOE_BLK, D), xmap),
                      pl.BlockSpec((None, None, D, D_EXP), wmap),
                      pl.BlockSpec((None, None, D, D_EXP), wmap),
                      pl.BlockSpec((None, None, D_EXP, D), wmap)],
            out_specs=pl.BlockSpec((MOE_BLK, D), lambda i, be, nbu: (i, 0)),
            scratch_shapes=[pltpu.VMEM((D, D_EXP), bf16), pltpu.VMEM((D, D_EXP), bf16),
                            pltpu.VMEM((D_EXP, D), bf16)],
        ),
        out_shape=jax.ShapeDtypeStruct((p_rows, D), f32),
        compiler_params=pltpu.CompilerParams(dimension_semantics=("arbitrary",),
                                             vmem_limit_bytes=48 * 1024 * 1024),
        name="moe_experts",
    )(blk_e, nb_used, xb, w1, w3, w2)


def _combine_kernel(dest_ref, x1_ref, gw_ref, mod_ref, y_ref, o_ref, ybuf, sems):
    i = pl.program_id(0)
    nsteps = pl.num_programs(0)
    tm = x1_ref.shape[0]
    n = nsteps * tm

    def issue_tile(j):
        slot = j % 2

        def issue(t, carry):
            for k in range(2):
                d = dest_ref[k * n + j * tm + t]
                pltpu.make_async_copy(y_ref.at[pl.ds(d, 1)], ybuf.at[slot, k, pl.ds(t, 1)],
                                      sems.at[slot]).start()
            return carry

        lax.fori_loop(0, tm, issue, 0, unroll=True)

    @pl.when(i == 0)
    def _():
        issue_tile(i)

    @pl.when(i + 1 < nsteps)
    def _():
        issue_tile(i + 1)

    slot = i % 2
    for k in range(2):
        pltpu.make_async_copy(y_ref.at[pl.ds(0, tm)], ybuf.at[slot, k], sems.at[slot]).wait()
    gw = gw_ref[...]
    y = ybuf[slot, 0] * gw[:, 0:1] + ybuf[slot, 1] * gw[:, 1:2]
    o_ref[...] = x1_ref[...] + mod_ref[5] * y


def _combine(y, dest_flat, x1, gw_t, mod_l, bsz, seq):
    n = x1.shape[0]
    tm = TM_COMB
    nt = seq // tm
    return pl.pallas_call(
        _combine_kernel,
        grid_spec=pltpu.PrefetchScalarGridSpec(
            num_scalar_prefetch=1,
            grid=(n // tm,),
            in_specs=[pl.BlockSpec((tm, D), lambda i, *_: (i, 0)),
                      pl.BlockSpec((tm, 2), lambda i, *_: (i, 0)),
                      pl.BlockSpec((6, None, 1, D), lambda i, *_: (0, i // nt, 0, 0)),
                      pl.BlockSpec(memory_space=pl.ANY)],
            out_specs=pl.BlockSpec((tm, D), lambda i, *_: (i, 0)),
            scratch_shapes=[pltpu.VMEM((2, 2, tm, D), f32), pltpu.SemaphoreType.DMA((2,))],
        ),
        out_shape=jax.ShapeDtypeStruct((n, D), f32),
        compiler_params=pltpu.CompilerParams(dimension_semantics=("arbitrary",)),
        name="moe_combine",
    )(dest_flat, x1, gw_t, mod_l, y)


def _moe_experts_sorted(h2, eidx, w1, w3, w2, layer):
    n = h2.shape[0]
    p_rows = (n * 2 // MOE_BLK + N_EXP) * MOE_BLK
    nb = p_rows // MOE_BLK
    rank, cnt = _ranks(eidx)
    counts = cnt[:, 0].astype(i32)
    padded = (counts + MOE_BLK - 1) // MOE_BLK * MOE_BLK
    pad_end = jnp.cumsum(padded)
    pad_start = pad_end - padded
    start_of = jnp.sum(jnp.where(eidx[..., None] == jnp.arange(N_EXP, dtype=i32), pad_start, 0), axis=-1)
    dest_flat = (start_of + rank).reshape(-1)
    nb_used = (pad_end[-1] // MOE_BLK).astype(i32)
    spare = nb_used + jnp.arange(N_EXP, dtype=i32)
    tails = jnp.concatenate([(pad_start + counts) // ZERO_ROWS, pad_end // ZERO_ROWS,
                             jnp.where(spare < nb, spare * MOE_BLK, -1)]).astype(i32)
    blk = jnp.minimum(jnp.arange(nb, dtype=i32), nb_used - 1)
    seg_done = (pad_end[None, :] <= (blk * MOE_BLK)[:, None]).astype(i32)
    blk_e = jnp.minimum(jnp.sum(seg_done, axis=1), N_EXP - 1).astype(i32)
    xb = _dispatch(h2, dest_flat, tails, p_rows)
    y = _experts(xb, blk_e, nb_used.reshape(1), w1, w3, w2, layer)
    return y, dest_flat


def kernel(x, c, w_ada, b_ada, attn_norm, ffn_norm, w_in, gla_gate_w, gla_gate_b, gla_out_norm,
           na_q_norm, na_k_norm, na_rpb, w_out, router_w, router_b, w1, w3, w2):
    bsz, seq, dim = x.shape
    depth = w_ada.shape[0]
    assert dim == D and seq == GRID_W * GRID_W, (dim, seq)
    assert bsz % GLA_NB == 0 and bsz <= 8, bsz
    assert seq % TM_PROJ == 0 and seq % TM_COMB_PROJ == 0 and seq % (NA_SUB * NA_QROWS * GRID_W) == 0
    assert (bsz * seq) % TM_DISP == 0 and (bsz * seq) % TM_COMB == 0 and (bsz * seq) % TM_RANK == 0
    assert MOE_BLK % ZERO_ROWS == 0 and (2 * bsz * seq) % MOE_BLK == 0
    assert w_in.shape == (depth, D, 2 * QK_W + 2 * V_W + 2 * GLA_RANK + 3 * NA_W), w_in.shape
    assert w1.shape == (depth, N_EXP, D, D_EXP) and router_w.shape == (D, N_EXP)
    mod = _adaln_mod(c, w_ada, b_ada)
    xc = x.reshape(bsz * seq, D)
    pending = None
    for l in range(depth):
        if pending is None:
            (qg, kg, vg, sg, la_f, la_b, qn, kn, vn), bounded_f, bounded_b = _inproj(
                xc, mod[l], attn_norm[l], w_in[l], gla_gate_w[l], gla_gate_b[l], bsz, seq)
        else:
            y, dest_flat, x1, gw_t = pending
            xc, (qg, kg, vg, sg, la_f, la_b, qn, kn, vn), bounded_f, bounded_b = _combine_inproj(
                y, dest_flat, x1, gw_t, mod[l - 1], mod[l], attn_norm[l], w_in[l], gla_gate_w[l],
                gla_gate_b[l], bsz, seq)
        o_f = _gla_dir(False, qg, kg, vg, la_f, bounded_f, bsz, seq)
        o_gla = _gla_dir(True, qg, kg, vg, la_b, bounded_b, bsz, seq, extra=(o_f, sg, gla_out_norm[l]))
        o_na = _natten(qn, kn, vn, na_q_norm[l], na_k_norm[l], na_rpb[l], bsz, seq)
        x1, h2, eidx, gw = _outproj(o_gla, o_na, w_out[l], xc, mod[l], ffn_norm[l],
                                    router_w, router_b, bsz, seq)
        y, dest_flat = _moe_experts_sorted(h2, eidx, w1, w3, w2, l)
        pending = (y, dest_flat, x1, gw.T)
    y, dest_flat, x1, gw_t = pending
    xc = _combine(y, dest_flat, x1, gw_t, mod[depth - 1], bsz, seq)
    return xc.reshape(bsz, seq, D)
```

```python
import functools

import numpy as np
import jax
import jax.numpy as jnp
from jax import lax
from jax.experimental import pallas as pl
from jax.experimental.pallas import tpu as pltpu

f32 = jnp.float32
bf16 = jnp.bfloat16
i32 = jnp.int32

D = 1024
GRID_W = 64
GLA_H, GLA_DK, GLA_DV = 4, 64, 128
GLA_RANK = 16
GLA_TAU = 16.0
CHUNK = 64
NA_H, NA_DH = 8, 64
WIN_H, WIN_W = 8, 16
N_EXP, N_GRP, EPG = 32, 4, 8
D_EXP = 512
QK_W = GLA_H * GLA_DK
V_W = GLA_H * GLA_DV
NA_W = NA_H * NA_DH
EPS = 1e-6
NEG = -1e30

TM_PROJ = 1024
TB_GLA = 256
GLA_NB = 4
NA_QROWS = 4
NA_KROWS = 12
NA_SUB = 8
TM_RANK = 1024
TM_DISP = 512
TM_COMB = 512
TM_COMB_PROJ = 512
MOE_BLK = 512
ZERO_ROWS = 64
N_LEVELS = 6
GLA_SAFE_RANGE = 40.0


def _nt(a, b):
    return lax.dot_general(a, b, (((1,), (1,)), ((), ())), preferred_element_type=f32)


def _tn(a, b):
    return lax.dot_general(a, b, (((0,), (0,)), ((), ())), preferred_element_type=f32)


def _split3(x):
    hi = x.astype(bf16)
    r = x - hi.astype(f32)
    mid = r.astype(bf16)
    lo = (r - mid.astype(f32)).astype(bf16)
    return hi, mid, lo


def _sel_l(m01, x):
    hi, mid, lo = _split3(x)
    d = lambda p: jnp.dot(m01, p, preferred_element_type=f32)
    return d(hi) + d(mid) + d(lo)


def _sel_r(x, m01):
    hi, mid, lo = _split3(x)
    d = lambda p: jnp.dot(p, m01, preferred_element_type=f32)
    return d(hi) + d(mid) + d(lo)


def _mod_kernel(c_ref, w_ref, b_ref, o_ref):
    c = c_ref[...]
    ca = c * jax.nn.sigmoid(c)
    w = w_ref[0]
    c_hi = ca.astype(bf16)
    c_lo = (ca - c_hi.astype(f32)).astype(bf16)
    w_hi = w.astype(bf16)
    acc = jnp.dot(c_hi, w_hi, preferred_element_type=f32) + jnp.dot(c_lo, w_hi, preferred_element_type=f32)
    o_ref[0, 0] = acc + b_ref[0, 0]


def _adaln_mod(c, w_ada, b_ada):
    depth = w_ada.shape[0]
    bsz = c.shape[0]
    cp = jnp.zeros((8, D), f32).at[:bsz].set(c)
    out = pl.pallas_call(
        _mod_kernel,
        grid=(depth, 6),
        in_specs=[pl.BlockSpec((8, D), lambda l, j: (0, 0)),
                  pl.BlockSpec((1, D, D), lambda l, j: (l, 0, j)),
                  pl.BlockSpec((1, 1, 1, D), lambda l, j: (l, j, 0, 0))],
        out_specs=pl.BlockSpec((1, 1, 8, D), lambda l, j: (l, j, 0, 0)),
        out_shape=jax.ShapeDtypeStruct((depth, 6, 8, D), f32),
        name="adaln_mod",
    )(cp, w_ada, b_ada.reshape(depth, 6, 1, D))
    return out[:, :, :bsz].reshape(depth, 6, bsz, 1, D)


def _log_sigmoid(z):
    return jnp.minimum(z, 0.0) - jnp.log1p(jnp.exp(-jnp.abs(z)))


def _inproj_body(x, mod_ref, nrm_ref, wm_ref, wlr_ref, gf_ref, bf_ref,
                 qg_ref, kg_ref, vg_ref, sg_ref, laf_ref, lab_ref, qn_ref, kn_ref, vn_ref, lamin_ref,
                 wg_scr=None):
    y = x * lax.rsqrt(jnp.mean(x * x, axis=-1, keepdims=True) + EPS) * nrm_ref[...]
    h = y * (1.0 + mod_ref[1]) + mod_ref[0]
    hb = h.astype(bf16)
    proj = lambda lo, hi: jnp.dot(hb, wm_ref[:, lo:hi], preferred_element_type=f32)
    qg_ref[...] = (proj(0, 256) * (GLA_DK ** -0.5)).astype(bf16)
    kg_ref[...] = proj(256, 512).astype(bf16)
    vg_ref[...] = proj(512, 1024).astype(bf16)
    gg = proj(1024, 1536)
    sg_ref[...] = (gg * jax.nn.sigmoid(gg)).astype(bf16)
    qn_ref[...] = proj(1536, 2048).astype(bf16)
    kn_ref[...] = proj(2048, 2560).astype(bf16)
    vn_ref[...] = proj(2560, 3072).astype(bf16)
    if wg_scr is None:
        lr = jnp.dot(hb, wlr_ref[...], preferred_element_type=f32)
        z = jnp.dot(lr, gf_ref[...], precision=lax.Precision.HIGHEST, preferred_element_type=f32)
    else:
        z = jnp.dot(hb, wg_scr[...], preferred_element_type=f32)
    z = z + bf_ref[...]
    la = _log_sigmoid(z) * (1.0 / GLA_TAU)
    la_f = la[:, 0:QK_W]
    la_b = la[:, QK_W:2 * QK_W]
    laf_ref[...] = la_f
    lab_ref[...] = la_b
    rows = []
    for la in (la_f, la_b):
        for blk in range(la.shape[0] // TB_GLA):
            tot = jnp.sum(la[blk * TB_GLA:(blk + 1) * TB_GLA], axis=0, keepdims=True)
            rows.append(jnp.broadcast_to(jnp.min(tot, axis=1, keepdims=True), (1, 128)))
    if len(rows) < 8:
        rows.append(jnp.zeros((8 - len(rows), 128), f32))
    lamin_ref[...] = jnp.concatenate(rows, axis=0)


def _inproj_kernel(x_ref, *refs):
    wlr_ref, gf_ref, wg_scr = refs[3], refs[4], refs[-1]

    @pl.when((pl.program_id(0) == 0) & (pl.program_id(1) == 0))
    def _():
        wg_scr[...] = jnp.dot(wlr_ref[...].astype(f32), gf_ref[...], precision=lax.Precision.HIGHEST,
                              preferred_element_type=f32).astype(bf16)

    _inproj_body(x_ref[...], *refs[:-1], wg_scr=wg_scr)


def _combine_inproj_kernel(dest_ref, x1_ref, gw_ref, modp_ref, y_ref, *refs):
    x2_ref = refs[6]
    ybuf, sems = refs[-2], refs[-1]
    proj_refs = refs[:6] + refs[7:-2]
    i = pl.program_id(0) * pl.num_programs(1) + pl.program_id(1)
    nsteps = pl.num_programs(0) * pl.num_programs(1)
    tm = x1_ref.shape[0]
    n = nsteps * tm

    nslot = ybuf.shape[0]

    def row_copies(j, slot, t):
        for k in range(2):
            d = dest_ref[k * n + j * tm + t]
            pltpu.make_async_copy(y_ref.at[pl.ds(d, 1)], ybuf.at[slot, k, pl.ds(t, 1)], sems.at[slot]).start()

    def issue_tile(j, slot):
        for t in range(tm):
            row_copies(j, slot, t)

    def wait_tile(slot):
        for k in range(2):
            pltpu.make_async_copy(y_ref.at[pl.ds(0, tm)], ybuf.at[slot, k], sems.at[slot]).wait()

    slot = i % nslot

    @pl.when(i == 0)
    def _():
        for j in range(nslot - 1):
            lax.fori_loop(0, tm, lambda t, c, j=j: (row_copies(j, j, t), c)[1], 0)

    wait_tile(slot)
    gw = gw_ref[...]
    y = ybuf[slot, 0] * gw[:, 0:1] + ybuf[slot, 1] * gw[:, 1:2]
    x2 = x1_ref[...] + modp_ref[5] * y
    x2_ref[...] = x2
    ahead = i + nslot - 1
    issue_tile(jnp.where(ahead < nsteps, ahead, ahead - nsteps), ahead % nslot)
    _inproj_body(x2, *proj_refs)

    @pl.when(i == nsteps - 1)
    def _():
        for back in range(1, nslot):
            wait_tile((i + back) % nslot)


_PROJ_WIDTHS = [(QK_W, bf16), (QK_W, bf16), (V_W, bf16), (V_W, bf16), (QK_W, f32), (QK_W, f32),
                (NA_W, bf16), (NA_W, bf16), (NA_W, bf16)]


def _inproj_operands(nrm, w_in, gate_w, gate_b, imap):
    lo, hi = 2 * QK_W + 2 * V_W, 2 * QK_W + 2 * V_W + 2 * GLA_RANK
    wm = jnp.concatenate([w_in[:, :lo], w_in[:, hi:]], axis=1).astype(bf16)
    wlr = w_in[:, lo:hi].astype(bf16)
    zpad = jnp.zeros((GLA_RANK, QK_W), f32)
    gmat = jnp.concatenate([jnp.concatenate([gate_w[0], zpad], axis=1),
                            jnp.concatenate([zpad, gate_w[1]], axis=1)], axis=0)
    gbias = jnp.concatenate([gate_b[0], gate_b[1]]).reshape(1, 2 * QK_W)
    args = [nrm.reshape(1, D), wm, wlr, gmat, gbias]
    return args, [pl.BlockSpec(a.shape, imap) for a in args]


def _bounded_flags(lamin, tm):
    per = tm // TB_GLA
    lamin = lamin[:, :2 * per, 0]
    bounded_f = (lamin[:, :per].reshape(-1) > -GLA_SAFE_RANGE).astype(i32)
    bounded_b = (lamin[:, per:].reshape(-1) > -GLA_SAFE_RANGE).astype(i32)
    return bounded_f, bounded_b


def _inproj(x2d, mod_l, nrm, w_in, gate_w, gate_b, bsz, seq):
    n = x2d.shape[0]
    tm = TM_PROJ
    nt = seq // tm
    row = lambda b, i: (b * nt + i, 0)
    w_args, w_specs = _inproj_operands(nrm, w_in, gate_w, gate_b, lambda b, i: (0, 0))
    outs = pl.pallas_call(
        _inproj_kernel,
        grid=(bsz, nt),
        in_specs=[pl.BlockSpec((tm, D), row),
                  pl.BlockSpec((6, None, 1, D), lambda b, i: (0, b, 0, 0))] + w_specs,
        out_specs=[pl.BlockSpec((tm, w), row) for w, _ in _PROJ_WIDTHS]
        + [pl.BlockSpec((None, 8, 128), lambda b, i: (b * nt + i, 0, 0))],
        out_shape=[jax.ShapeDtypeStruct((n, w), dt) for w, dt in _PROJ_WIDTHS]
        + [jax.ShapeDtypeStruct((n // tm, 8, 128), f32)],
        scratch_shapes=[pltpu.VMEM((D, 2 * QK_W), bf16)],
        compiler_params=pltpu.CompilerParams(dimension_semantics=("arbitrary", "arbitrary")),
        name="inproj",
    )(x2d, mod_l, *w_args)
    return outs[:-1], *_bounded_flags(outs[-1], tm)


def _combine_inproj(y, dest_flat, x1, gw_t, mod_prev, mod_l, nrm, w_in, gate_w, gate_b, bsz, seq):
    n = x1.shape[0]
    tm = TM_COMB_PROJ
    nt = seq // tm
    row = lambda b, i, *_: (b * nt + i, 0)
    mods = lambda b, i, *_: (0, b, 0, 0)
    w_args, w_specs = _inproj_operands(nrm, w_in, gate_w, gate_b, lambda b, i, *_: (0, 0))
    outs = pl.pallas_call(
        _combine_inproj_kernel,
        grid_spec=pltpu.PrefetchScalarGridSpec(
            num_scalar_prefetch=1,
            grid=(bsz, nt),
            in_specs=[pl.BlockSpec((tm, D), row),
                      pl.BlockSpec((tm, 2), row),
                      pl.BlockSpec((6, None, 1, D), mods),
                      pl.BlockSpec(memory_space=pl.ANY),
                      pl.BlockSpec((6, None, 1, D), mods)] + w_specs,
            out_specs=[pl.BlockSpec((tm, D), row)]
            + [pl.BlockSpec((tm, w), row) for w, _ in _PROJ_WIDTHS]
            + [pl.BlockSpec((None, 8, 128), lambda b, i, *_: (b * nt + i, 0, 0))],
            scratch_shapes=[pltpu.VMEM((3, 2, tm, D), f32), pltpu.SemaphoreType.DMA((3,))],
        ),
        out_shape=[jax.ShapeDtypeStruct((n, D), f32)]
        + [jax.ShapeDtypeStruct((n, w), dt) for w, dt in _PROJ_WIDTHS]
        + [jax.ShapeDtypeStruct((n // tm, 8, 128), f32)],
        compiler_params=pltpu.CompilerParams(dimension_semantics=("arbitrary", "arbitrary"),
                                             vmem_limit_bytes=56 * 1024 * 1024),
        name="combine_inproj",
    )(dest_flat, x1, gw_t, mod_prev, y, mod_l, *w_args)
    return outs[0], outs[1:-1], *_bounded_flags(outs[-1], tm)


def _gla_consts(rev, tb):
    c = CHUNK
    idx = np.arange(c)
    if not rev:
        cum = (idx[None, :] <= idx[:, None]).astype(np.float32)
    else:
        cum = (idx[None, :] >= idx[:, None]).astype(np.float32)
    mats = [cum]
    masks = []
    for lvl in range(N_LEVELS):
        s = c >> (lvl + 1)
        blk = idx // (2 * s)
        second = (idx % (2 * s)) >= s
        ref_row = blk * 2 * s + (s if rev else s - 1)
        mats.append(cum[ref_row])
        same = blk[:, None] == blk[None, :]
        if not rev:
            m = same & second[:, None] & (~second[None, :])
        else:
            m = same & (~second[:, None]) & second[None, :]
        masks.append(m)
    masks.append(np.eye(c, dtype=bool) & (not rev))
    mst = np.concatenate(mats, axis=0)
    msk = np.stack([np.tile(m, (1, GLA_H)) for m in masks]).astype(np.float32)
    tix = np.arange(tb)
    tri = (tix[None, :] >= tix[:, None]) if rev else (tix[None, :] <= tix[:, None])
    cmask = (tix[:, None] < tix[None, :]) if rev else (tix[:, None] >= tix[None, :])
    nch = tb // c
    selb = np.zeros((tb, nch * 128), np.float32)
    for ch in range(nch):
        selb[ch * c:(ch + 1) * c, ch * 128:(ch + 1) * 128] = 1.0
    hk = np.arange(QK_W) // GLA_DK
    hv = np.arange(V_W) // GLA_DV
    kmask = (hk[:, None] == hk[None, :]).astype(np.float32)
    vmask = (hk[:, None] == hv[None, :]).astype(np.float32)
    return (jnp.asarray(mst, bf16), jnp.asarray(msk, f32), jnp.asarray(selb, bf16),
            jnp.asarray(kmask, bf16), jnp.asarray(vmask, bf16), jnp.asarray(vmask, f32),
            jnp.asarray(tri, bf16), jnp.asarray(cmask, f32), jnp.ones((tb, 128), bf16))


def _gla_kernel(rev, tb, *refs):
    if rev:
        (bounded_ref, q_ref, k_ref, v_ref, la_ref, mst_ref, msk_ref, selb_ref, kmask_ref, vmask_ref,
         smask_ref, tri_ref, cmask_ref, ones_ref, of_ref, sg_ref, gn_ref, out_ref, s_scr, o_scr) = refs
    else:
        (bounded_ref, q_ref, k_ref, v_ref, la_ref, mst_ref, msk_ref, selb_ref, kmask_ref, vmask_ref,
         smask_ref, tri_ref, cmask_ref, ones_ref, out_ref, s_scr) = refs
        o_scr = out_ref
    c = CHUNK
    nch = tb // c
    elems = range(GLA_NB)

    @pl.when(pl.program_id(1) == 0)
    def _():
        s_scr[...] = jnp.zeros_like(s_scr)

    def blocks_bounded():
        tri = tri_ref[...]
        cmask = cmask_ref[...]
        las = [la_ref[e] for e in elems]
        tots = [_sel_r(la.T, ones_ref[...]) for la in las]
        bcums = []
        for la in las:
            la_hi = la.astype(bf16)
            la_lo = (la - la_hi.astype(f32)).astype(bf16)
            bcums.append(jnp.dot(tri, la_hi, preferred_element_type=f32)
                         + jnp.dot(tri, la_lo, preferred_element_type=f32))
        ks = [k_ref[e].astype(f32) for e in elems]
        qes = [(q_ref[e].astype(f32) * jnp.exp(bcums[e])).astype(bf16) for e in elems]
        kxs = [(ks[e] * jnp.exp(-bcums[e])).astype(bf16) for e in elems]
        s_prevs = [s_scr[e] for e in elems]
        o_inters = [jnp.dot(qes[e], s_prevs[e].astype(bf16), preferred_element_type=f32) for e in elems]
        lane_head = lax.broadcasted_iota(i32, (tb, QK_W), 1) // GLA_DK
        for hd in range(GLA_H):
            cols = slice(hd * GLA_DV, (hd + 1) * GLA_DV)
            phs = []
            for e in elems:
                qh = jnp.where(lane_head == hd, qes[e], jnp.zeros_like(qes[e]))
                phs.append(jnp.where(cmask > 0.0, _nt(qh, kxs[e]), 0.0).astype(bf16))
            for e in elems:
                o_blk = o_inters[e][:, cols] + jnp.dot(phs[e], v_ref[e, :, cols], preferred_element_type=f32)
                o_scr[e, :, cols] = o_blk.astype(o_scr.dtype)
        for e in elems:
            blast = bcums[e][0:1] if rev else bcums[e][tb - 1:tb]
            ke = (ks[e] * jnp.exp(blast - bcums[e])).astype(bf16)
            dec = jnp.exp(tots[e])
            s_scr[e] = (s_prevs[e] * jnp.concatenate([dec] * GLA_H, axis=1)
                        + _tn(ke, v_ref[e]) * smask_ref[...])

    def chunks_robust():
        mst = mst_ref[...]
        kmask = kmask_ref[...]
        vmask = vmask_ref[...]
        for e in elems:
            tot = _sel_r(la_ref[e].T, selb_ref[...])
            for ch in (range(nch - 1, -1, -1) if rev else range(nch)):
                rows = slice(ch * c, (ch + 1) * c)
                q = q_ref[e, rows, :].astype(f32)
                k = k_ref[e, rows, :].astype(f32)
                v = v_ref[e, rows, :]
                r_all = _sel_l(mst, la_ref[e, rows, :])
                bcum = r_all[0:c]
                qe = (q * jnp.exp(bcum)).astype(bf16)
                o = jnp.dot(qe, s_scr[e].astype(bf16), preferred_element_type=f32)
                sall = jnp.zeros((c, QK_W), f32)
                for lvl in range(N_LEVELS):
                    ref = r_all[(lvl + 1) * c:(lvl + 2) * c]
                    a = (q * jnp.exp(jnp.minimum(bcum - ref, 0.0))).astype(bf16)
                    bm = (k * jnp.exp(jnp.minimum(ref - bcum, 0.0))).astype(bf16)
                    bbd = jnp.concatenate([bm] * GLA_H, axis=0) * kmask
                    sall = sall + jnp.where(msk_ref[lvl] > 0.0, _nt(a, bbd), 0.0)
                if not rev:
                    bbd = jnp.concatenate([k_ref[e, rows, :]] * GLA_H, axis=0) * kmask
                    sall = sall + jnp.where(msk_ref[N_LEVELS] > 0.0, _nt(q_ref[e, rows, :], bbd), 0.0)
                vbd = jnp.concatenate([v] * GLA_H, axis=0) * vmask
                o_blk = o + jnp.dot(sall.astype(bf16), vbd, preferred_element_type=f32)
                o_scr[e, rows, :] = o_blk.astype(o_scr.dtype)
                blast = bcum[0:1] if rev else bcum[c - 1:c]
                ke = (k * jnp.exp(blast - bcum)).astype(bf16)
                dec = jnp.exp(tot[:, ch * 128:(ch + 1) * 128])
                s_scr[e] = (s_scr[e] * jnp.concatenate([dec] * GLA_H, axis=1)
                            + _tn(ke, v) * smask_ref[...])

    nb = pl.num_programs(1)
    pos = nb - 1 - pl.program_id(1) if rev else pl.program_id(1)
    bounded = bounded_ref[(pl.program_id(0) * GLA_NB) * nb + pos] > 0
    for e in range(1, GLA_NB):
        bounded = bounded & (bounded_ref[(pl.program_id(0) * GLA_NB + e) * nb + pos] > 0)
    pl.when(bounded)(blocks_bounded)
    pl.when(jnp.logical_not(bounded))(chunks_robust)

    if rev:
        gn = gn_ref[...]
        for e in elems:
            for hd in range(GLA_H):
                cols = slice(hd * GLA_DV, (hd + 1) * GLA_DV)
                t = of_ref[e, :, cols].astype(f32) + o_scr[e, :, cols]
                y = t * lax.rsqrt(jnp.mean(t * t, axis=-1, keepdims=True) + EPS) * gn
                out_ref[e, :, cols] = (y * sg_ref[e, :, cols].astype(f32)).astype(bf16)


def _gla_dir(rev, q, k, v, la, bounded, bsz, seq, extra=None):
    n = q.shape[0]
    tb = TB_GLA
    nb = seq // tb
    consts = _gla_consts(rev, tb)
    if rev:
        row = lambda b, i, flags: (b, nb - 1 - i, 0)
    else:
        row = lambda b, i, flags: (b, i, 0)
    const2 = lambda b, i, flags: (0, 0)
    const3 = lambda b, i, flags: (0, 0, 0)
    per_batch = lambda a: a.reshape(bsz, seq, a.shape[-1])
    q, k, v, la = per_batch(q), per_batch(k), per_batch(v), per_batch(la)
    in_specs = [pl.BlockSpec((GLA_NB, tb, QK_W), row), pl.BlockSpec((GLA_NB, tb, QK_W), row),
                pl.BlockSpec((GLA_NB, tb, V_W), row), pl.BlockSpec((GLA_NB, tb, QK_W), row),
                pl.BlockSpec(consts[0].shape, const2), pl.BlockSpec(consts[1].shape, const3),
                pl.BlockSpec(consts[2].shape, const2), pl.BlockSpec(consts[3].shape, const2),
                pl.BlockSpec(consts[4].shape, const2), pl.BlockSpec(consts[5].shape, const2),
                pl.BlockSpec(consts[6].shape, const2), pl.BlockSpec(consts[7].shape, const2),
                pl.BlockSpec(consts[8].shape, const2)]
    args = [q, k, v, la, *consts]
    scratch = [pltpu.VMEM((GLA_NB, QK_W, V_W), f32)]
    if rev:
        o_f, sg, gn = extra
        in_specs += [pl.BlockSpec((GLA_NB, tb, V_W), row), pl.BlockSpec((GLA_NB, tb, V_W), row),
                     pl.BlockSpec((1, GLA_DV), const2)]
        args += [per_batch(o_f), per_batch(sg), gn.reshape(1, GLA_DV)]
        scratch.append(pltpu.VMEM((GLA_NB, tb, V_W), f32))
    out = pl.pallas_call(
        functools.partial(_gla_kernel, rev, tb),
        grid_spec=pltpu.PrefetchScalarGridSpec(
            num_scalar_prefetch=1,
            grid=(bsz // GLA_NB, nb),
            in_specs=in_specs,
            out_specs=pl.BlockSpec((GLA_NB, tb, V_W), row),
            scratch_shapes=scratch,
        ),
        out_shape=jax.ShapeDtypeStruct((bsz, seq, V_W), bf16),
        compiler_params=pltpu.CompilerParams(dimension_semantics=("arbitrary", "arbitrary")),
        name="gla_bwd" if rev else "gla_fwd",
    )(bounded, *args)
    return out.reshape(n, V_W)


def _rpb_expand_kernel(rpb_ref, sel_ref, valid_ref, o_ref):
    e = _sel_r(rpb_ref[...], sel_ref[...])
    o_ref[...] = jnp.where(valid_ref[...] > 0.0, e, NEG)


def _na_bias_table(rpb):
    nri, nci = 2 * WIN_H - 1, 2 * WIN_W - 1
    qc = np.arange(GRID_W)
    col_start = np.clip(qc - WIN_W // 2, 0, GRID_W - WIN_W)
    kc = np.arange(GRID_W)
    valid = (kc[None, :] >= col_start[:, None]) & (kc[None, :] < col_start[:, None] + WIN_W)
    ci = kc[None, :] - qc[:, None] + WIN_W - 1
    sel = np.zeros((32, GRID_W * GRID_W), np.float32)
    flat_ci = ci.reshape(-1)
    ok = (flat_ci >= 0) & (flat_ci < nci)
    sel[flat_ci[ok], np.nonzero(ok)[0]] = 1.0
    rpb2 = jnp.zeros((NA_H * nri + 8 - (NA_H * nri) % 8, 32), f32).at[:NA_H * nri, :nci].set(
        rpb.reshape(NA_H * nri, nci))
    nrow = rpb2.shape[0]
    e = pl.pallas_call(
        _rpb_expand_kernel,
        out_shape=jax.ShapeDtypeStruct((nrow, GRID_W * GRID_W), f32),
        name="rpb_expand",
    )(rpb2, jnp.asarray(sel, bf16), jnp.asarray(valid.reshape(1, -1), f32))
    return e[:NA_H * nri].reshape(NA_H // 2, 2, nri, GRID_W, GRID_W)


def _na_row_classes():
    rows = GRID_W
    ri = -np.ones((3, NA_QROWS, NA_KROWS), np.int64)
    for cls, rb in enumerate((0, 1, rows // NA_QROWS - 1)):
        r0 = rb * NA_QROWS
        kb = int(np.clip(r0 - WIN_H // 2, 0, rows - NA_KROWS))
        for rq in range(NA_QROWS):
            r = r0 + rq
            rs = int(np.clip(r - WIN_H // 2, 0, rows - WIN_H))
            for m in range(NA_KROWS):
                kr = kb + m
                if rs <= kr < rs + WIN_H:
                    ri[cls, rq, m] = kr - r + WIN_H - 1
    return ri


def _natten_kernel(q_ref, k_ref, v_ref, qn_ref, kn_ref, gsum_ref, e_ref, o_ref, kn_scr, tbl_scr):
    b = pl.program_id(1)
    rb = pl.program_id(2)
    gsum = gsum_ref[...]
    log2e = 1.4426950408889634

    def headnorm(x, g):
        sq = x * x
        hi = sq.astype(bf16)
        lo = (sq - hi.astype(f32)).astype(bf16)
        ms = (jnp.dot(hi, gsum, preferred_element_type=f32)
              + jnp.dot(lo, gsum, preferred_element_type=f32)) * (1.0 / NA_DH)
        return x * lax.rsqrt(ms + EPS) * g

    @pl.when((b == 0) & (rb == 0))
    def _():
        ri = _na_row_classes()
        neg = jnp.full((GRID_W, GRID_W), NEG, f32)
        for hh in range(2):
            for cls in range(3):
                for rq in range(NA_QROWS):
                    for mp in range(NA_KROWS // 2):
                        parts = []
                        for m in (2 * mp, 2 * mp + 1):
                            r = int(ri[cls, rq, m])
                            parts.append(neg if r < 0 else e_ref[hh, r] * log2e)
                        r0 = (hh * NA_QROWS + rq) * GRID_W
                        tbl_scr[cls, r0:r0 + GRID_W, mp * 128:(mp + 1) * 128] = (
                            jnp.concatenate(parts, axis=1))

    @pl.when(rb == 0)
    def _():
        kn_scr[...] = headnorm(k_ref[...].astype(f32), kn_ref[...]).astype(bf16)

    nq = NA_QROWS * GRID_W
    nk = NA_KROWS * GRID_W
    nblk = GRID_W // NA_QROWS
    first = lax.broadcasted_iota(i32, (nq, 128), 1) < NA_DH
    qs, wins, tbls = [], [], []
    for sub in range(NA_SUB):
        qb = rb * NA_SUB + sub
        q = headnorm(q_ref[sub * nq:(sub + 1) * nq, :].astype(f32), qn_ref[...]) * (NA_DH ** -0.5 * log2e)
        kb = jnp.clip(qb * NA_QROWS - WIN_H // 2, 0, GRID_W - NA_KROWS)
        start = pl.multiple_of(kb * GRID_W, GRID_W)
        cls = jnp.where(qb == 0, 0, jnp.where(qb == nblk - 1, 2, 1))
        qs.append([jnp.where(first, q, 0.0).astype(bf16), jnp.where(first, 0.0, q).astype(bf16)])
        wins.append((kn_scr[pl.ds(start, nk), :], v_ref[pl.ds(start, nk), :]))
        tbls.append(cls)
    scores = [[_nt(qs[sub][hh], wins[sub][0]) + tbl_scr[tbls[sub], hh * nq:(hh + 1) * nq, :]
               for hh in range(2)] for sub in range(NA_SUB)]
    for sub in range(NA_SUB):
        outs = []
        for s in scores[sub]:
            m = jnp.max(s, axis=-1, keepdims=True)
            p = jnp.exp2(s - m)
            l = jnp.sum(p, axis=-1, keepdims=True)
            outs.append(jnp.dot(p.astype(bf16), wins[sub][1], preferred_element_type=f32) / l)
        o_ref[sub * nq:(sub + 1) * nq, :] = jnp.where(first, outs[0], outs[1]).astype(bf16)


def _natten(qn, kn, vn, q_norm, k_norm, rpb, bsz, seq):
    n = qn.shape[0]
    e5 = _na_bias_table(rpb)
    nq = NA_QROWS * GRID_W
    nqs = NA_SUB * nq
    nrb = seq // nqs
    lane_h = np.arange(128) // NA_DH
    gsum = jnp.asarray(lane_h[:, None] == lane_h[None, :], bf16)
    qn2 = jnp.tile(q_norm.reshape(1, NA_DH), (1, 2))
    kn2 = jnp.tile(k_norm.reshape(1, NA_DH), (1, 2))
    const2 = lambda p, b, r: (0, 0)
    return pl.pallas_call(
        _natten_kernel,
        grid=(NA_H // 2, bsz, nrb),
        in_specs=[pl.BlockSpec((nqs, 128), lambda p, b, r: (b * nrb + r, p)),
                  pl.BlockSpec((seq, 128), lambda p, b, r: (b, p)),
                  pl.BlockSpec((seq, 128), lambda p, b, r: (b, p)),
                  pl.BlockSpec((1, 128), const2),
                  pl.BlockSpec((1, 128), const2),
                  pl.BlockSpec((128, 128), const2),
                  pl.BlockSpec((None,) + e5.shape[1:], lambda p, b, r: (p, 0, 0, 0, 0))],
        out_specs=pl.BlockSpec((nqs, 128), lambda p, b, r: (b * nrb + r, p)),
        out_shape=jax.ShapeDtypeStruct((n, NA_W), bf16),
        scratch_shapes=[pltpu.VMEM((seq, 128), bf16),
                        pltpu.VMEM((3, 2 * nq, NA_KROWS * GRID_W), f32)],
        compiler_params=pltpu.CompilerParams(dimension_semantics=("arbitrary", "arbitrary", "arbitrary")),
        name="natten",
    )(qn, kn, vn, qn2, kn2, gsum, e5)


def _top2(vals):
    io = lax.broadcasted_iota(i32, vals.shape, 0)
    m1 = jnp.max(vals, axis=0, keepdims=True)
    i1 = jnp.min(jnp.where(vals == m1, io, EPG), axis=0, keepdims=True)
    v2 = jnp.where(io == i1, -jnp.inf, vals)
    m2 = jnp.max(v2, axis=0, keepdims=True)
    i2 = jnp.min(jnp.where(v2 == m2, io, EPG), axis=0, keepdims=True)
    return m1, i1, m2, i2


def _outproj_kernel(og_ref, on_ref, wo_ref, x_ref, mod_ref, nrm_ref, rw_ref, rb_ref,
                    x1_ref, h2_ref, eidx_ref, gw_ref):
    mix = (jnp.dot(og_ref[...], wo_ref[0:V_W, :], preferred_element_type=f32)
           + jnp.dot(on_ref[...], wo_ref[V_W:V_W + NA_W, :], preferred_element_type=f32))
    x1 = x_ref[...] + mod_ref[2] * mix
    x1_ref[...] = x1
    y = x1 * lax.rsqrt(jnp.mean(x1 * x1, axis=-1, keepdims=True) + EPS) * nrm_ref[...]
    h2 = y * (1.0 + mod_ref[4]) + mod_ref[3]
    h2_ref[...] = h2
    rw = rw_ref[...]
    rw_hi = rw.astype(bf16)
    rw_lo = (rw - rw_hi.astype(f32)).astype(bf16)
    h_hi = h2.astype(bf16)
    h_lo = (h2 - h_hi.astype(f32)).astype(bf16)
    logits = _nt(rw_hi, h_hi) + _nt(rw_hi, h_lo) + _nt(rw_lo, h_hi)
    scores = jax.nn.sigmoid(logits)
    sel = scores + rb_ref[...]
    tops = [_top2(sel[g * EPG:(g + 1) * EPG]) for g in range(N_GRP)]
    best = jnp.zeros_like(tops[0][1])
    bs = tops[0][0] + tops[0][2]
    for g in range(1, N_GRP):
        gs = tops[g][0] + tops[g][2]
        take = gs > bs
        best = jnp.where(take, g, best)
        bs = jnp.where(take, gs, bs)
    io = lax.broadcasted_iota(i32, (EPG, sel.shape[1]), 0)
    i1 = jnp.zeros_like(best)
    i2 = jnp.zeros_like(best)
    s1 = jnp.zeros(best.shape, f32)
    s2 = jnp.zeros(best.shape, f32)
    for g in range(N_GRP):
        sc = scores[g * EPG:(g + 1) * EPG]
        pick = best == g
        i1 = jnp.where(pick, tops[g][1], i1)
        i2 = jnp.where(pick, tops[g][3], i2)
        s1 = jnp.where(pick, jnp.sum(jnp.where(io == tops[g][1], sc, 0.0), axis=0, keepdims=True), s1)
        s2 = jnp.where(pick, jnp.sum(jnp.where(io == tops[g][3], sc, 0.0), axis=0, keepdims=True), s2)
    eidx_ref[0:1, :] = best * EPG + i1
    eidx_ref[1:2, :] = best * EPG + i2
    tot = s1 + s2
    gw_ref[0:1, :] = s1 / tot
    gw_ref[1:2, :] = s2 / tot


def _outproj(o_gla, o_na, w_out, x2d, mod_l, nrm, router_w, router_b, bsz, seq):
    n = x2d.shape[0]
    tm = TM_PROJ
    nt = seq // tm
    row = lambda b, i: (b * nt + i, 0)
    col = lambda b, i: (0, b * nt + i)
    const = lambda b, i: (0, 0)
    return pl.pallas_call(
        _outproj_kernel,
        grid=(bsz, nt),
        in_specs=[pl.BlockSpec((tm, V_W), row),
                  pl.BlockSpec((tm, NA_W), row),
                  pl.BlockSpec((V_W + NA_W, D), const),
                  pl.BlockSpec((tm, D), row),
                  pl.BlockSpec((6, None, 1, D), lambda b, i: (0, b, 0, 0)),
                  pl.BlockSpec((1, D), const),
                  pl.BlockSpec((N_EXP, D), const),
                  pl.BlockSpec((N_EXP, 1), const)],
        out_specs=[pl.BlockSpec((tm, D), row), pl.BlockSpec((tm, D), row),
                   pl.BlockSpec((2, tm), col), pl.BlockSpec((2, tm), col)],
        out_shape=[jax.ShapeDtypeStruct((n, D), f32), jax.ShapeDtypeStruct((n, D), f32),
                   jax.ShapeDtypeStruct((2, n), i32), jax.ShapeDtypeStruct((2, n), f32)],
        compiler_params=pltpu.CompilerParams(dimension_semantics=("arbitrary", "arbitrary")),
        name="outproj_router",
    )(o_gla, o_na, w_out.astype(bf16), x2d, mod_l, nrm.reshape(1, D),
      router_w.T, router_b.reshape(N_EXP, 1))


def _rank_kernel(eidx_ref, tri_ref, rank_ref, cnt_ref, carry):
    @pl.when(pl.program_id(0) == 0)
    def _():
        carry[...] = jnp.zeros_like(carry)

    e = eidx_ref[...]
    tm = e.shape[1]
    io = lax.broadcasted_iota(i32, (N_EXP, tm), 0)
    run = carry[...]
    for k in range(2):
        oh = io == e[k:k + 1, :]
        ohf = oh.astype(f32)
        pre = jnp.dot(ohf.astype(bf16), tri_ref[...], preferred_element_type=f32) + run[:, 0:1]
        rank_ref[k:k + 1, :] = jnp.sum(jnp.where(oh, pre, 0.0), axis=0, keepdims=True).astype(i32)
        run = run + jnp.sum(ohf, axis=1, keepdims=True)
    carry[...] = run
    cnt_ref[...] = run


def _ranks(eidx):
    n = eidx.shape[1]
    tm = TM_RANK
    t = np.arange(tm)
    tri = jnp.asarray(t[:, None] < t[None, :], bf16)
    return pl.pallas_call(
        _rank_kernel,
        grid=(n // tm,),
        in_specs=[pl.BlockSpec((2, tm), lambda i: (0, i)), pl.BlockSpec((tm, tm), lambda i: (0, 0))],
        out_specs=[pl.BlockSpec((2, tm), lambda i: (0, i)), pl.BlockSpec((N_EXP, 128), lambda i: (0, 0))],
        out_shape=[jax.ShapeDtypeStruct((2, n), i32), jax.ShapeDtypeStruct((N_EXP, 128), f32)],
        scratch_shapes=[pltpu.VMEM((N_EXP, 128), f32)],
        compiler_params=pltpu.CompilerParams(dimension_semantics=("arbitrary",)),
        name="moe_rank",
    )(eidx, tri)


def _dispatch_kernel(dest_ref, tails_ref, h_ref, xb_ref, zbuf, hbuf, lsem, rsem, zsem, ssem):
    i = pl.program_id(0)
    nsteps = pl.num_programs(0)
    tm = TM_DISP
    n = nsteps * tm
    nbuf = hbuf.shape[0]

    def load(j):
        start = pl.multiple_of(j * tm, tm)
        return pltpu.make_async_copy(h_ref.at[pl.ds(start, tm)], hbuf.at[j % nbuf], lsem.at[j % nbuf])

    def drain_rows(j):
        for _ in range(2):
            pltpu.make_async_copy(hbuf.at[j % nbuf], xb_ref.at[pl.ds(0, tm)], rsem.at[j % nbuf]).wait()

    def pad_copy(chunk):
        start = pl.multiple_of(chunk * ZERO_ROWS, ZERO_ROWS)
        return pltpu.make_async_copy(zbuf.at[pl.ds(0, ZERO_ROWS)], xb_ref.at[pl.ds(start, ZERO_ROWS)], zsem)

    def spare_copy(j):
        start = pl.multiple_of(tails_ref[2 * N_EXP + j], MOE_BLK)
        return pltpu.make_async_copy(zbuf, xb_ref.at[pl.ds(start, MOE_BLK)], ssem)

    def for_pad_chunks(fn):
        for e in range(N_EXP):
            lax.fori_loop(tails_ref[e], tails_ref[N_EXP + e], lambda ch, c: (fn(ch), c)[1], 0)

    def for_spare_blocks(fn):
        for j in range(N_EXP):
            @pl.when(tails_ref[2 * N_EXP + j] >= 0)
            def _():
                fn(j)

    @pl.when(i == 0)
    def _():
        zbuf[...] = jnp.zeros_like(zbuf)
        for_pad_chunks(lambda ch: pad_copy(ch).start())
        for_spare_blocks(lambda j: spare_copy(j).start())
        for_pad_chunks(lambda ch: pad_copy(ch).wait())
        load(i).start()

    @pl.when(i >= nbuf - 1)
    def _():
        drain_rows(i + 1 - nbuf)

    @pl.when(i + 1 < nsteps)
    def _():
        load(i + 1).start()

    load(i).wait()
    slot = i % nbuf

    def issue(t, carry):
        src = hbuf.at[slot, pl.ds(t, 1)]
        for k in range(2):
            d = dest_ref[k * n + i * tm + t]
            pltpu.make_async_copy(src, xb_ref.at[pl.ds(d, 1)], rsem.at[slot]).start()
        return carry

    lax.fori_loop(0, tm, issue, 0, unroll=True)

    @pl.when(i == nsteps - 1)
    def _():
        for back in range(nbuf - 2, -1, -1):
            drain_rows(i - back)
        for_spare_blocks(lambda j: spare_copy(j).wait())


def _dispatch(h2, dest_flat, tails, p_rows):
    n = h2.shape[0]
    tm = TM_DISP
    return pl.pallas_call(
        _dispatch_kernel,
        grid_spec=pltpu.PrefetchScalarGridSpec(
            num_scalar_prefetch=2,
            grid=(n // tm,),
            in_specs=[pl.BlockSpec(memory_space=pl.ANY)],
            out_specs=pl.BlockSpec(memory_space=pl.ANY),
            scratch_shapes=[pltpu.VMEM((MOE_BLK, D), f32), pltpu.VMEM((3, tm, D), f32),
                            pltpu.SemaphoreType.DMA((3,)), pltpu.SemaphoreType.DMA((3,)),
                            pltpu.SemaphoreType.DMA(()), pltpu.SemaphoreType.DMA(())],
        ),
        out_shape=jax.ShapeDtypeStruct((p_rows, D), f32),
        compiler_params=pltpu.CompilerParams(dimension_semantics=("arbitrary",)),
        name="moe_dispatch",
    )(dest_flat, tails, h2)


def _expert_kernel(blk_e_ref, nb_ref, xb_ref, w1_ref, w3_ref, w2_ref, y_ref, w1b, w3b, w2b):
    i = pl.program_id(0)

    @pl.when((i < nb_ref[0]) & ((i == 0) | (blk_e_ref[i] != blk_e_ref[jnp.maximum(i - 1, 0)])))
    def _():
        w1b[...] = w1_ref[...].astype(bf16)
        w3b[...] = w3_ref[...].astype(bf16)
        w2b[...] = w2_ref[...].astype(bf16)

    @pl.when(i < nb_ref[0])
    def _():
        x = xb_ref[...].astype(bf16)
        a = jnp.dot(x, w1b[...], preferred_element_type=f32)
        b = jnp.dot(x, w3b[...], preferred_element_type=f32)
        h = (a * jax.nn.sigmoid(a) * b).astype(bf16)
        y_ref[...] = jnp.dot(h, w2b[...], preferred_element_type=f32)

    @pl.when(i >= nb_ref[0])
    def _():
        y_ref[...] = jnp.zeros_like(y_ref)


def _experts(xb, blk_e, nb_used, w1, w3, w2, layer):
    p_rows = xb.shape[0]
    nb = p_rows // MOE_BLK

    def xmap(i, be, nbu):
        return (jnp.minimum(i, nbu[0] - 1), 0)

    def wmap(i, be, nbu):
        return (layer, be[i], 0, 0)

    return pl.pallas_call(
        _expert_kernel,
        grid_spec=pltpu.PrefetchScalarGridSpec(
            num_scalar_prefetch=2,
            grid=(nb,),
            in_specs=[pl.BlockSpec((MOE_BLK, D), xmap),
                      pl.BlockSpec((None, None, D, D_EXP), wmap),
                      pl.BlockSpec((None, None, D, D_EXP), wmap),
                      pl.BlockSpec((None, None, D_EXP, D), wmap)],
            out_specs=pl.BlockSpec((MOE_BLK, D), lambda i, be, nbu: (i, 0)),
            scratch_shapes=[pltpu.VMEM((D, D_EXP), bf16), pltpu.VMEM((D, D_EXP), bf16),
                            pltpu.VMEM((D_EXP, D), bf16)],
        ),
        out_shape=jax.ShapeDtypeStruct((p_rows, D), f32),
        compiler_params=pltpu.CompilerParams(dimension_semantics=("arbitrary",),
                                             vmem_limit_bytes=48 * 1024 * 1024),
        name="moe_experts",
    )(blk_e, nb_used, xb, w1, w3, w2)


def _combine_kernel(dest_ref, x1_ref, gw_ref, mod_ref, y_ref, o_ref, ybuf, sems):
    i = pl.program_id(0)
    nsteps = pl.num_programs(0)
    tm = x1_ref.shape[0]
    n = nsteps * tm

    def issue_tile(j):
        slot = j % 2

        def issue(t, carry):
            for k in range(2):
                d = dest_ref[k * n + j * tm + t]
                pltpu.make_async_copy(y_ref.at[pl.ds(d, 1)], ybuf.at[slot, k, pl.ds(t, 1)],
                                      sems.at[slot]).start()
            return carry

        lax.fori_loop(0, tm, issue, 0, unroll=True)

    @pl.when(i == 0)
    def _():
        issue_tile(i)

    @pl.when(i + 1 < nsteps)
    def _():
        issue_tile(i + 1)

    slot = i % 2
    for k in range(2):
        pltpu.make_async_copy(y_ref.at[pl.ds(0, tm)], ybuf.at[slot, k], sems.at[slot]).wait()
    gw = gw_ref[...]
    y = ybuf[slot, 0] * gw[:, 0:1] + ybuf[slot, 1] * gw[:, 1:2]
    o_ref[...] = x1_ref[...] + mod_ref[5] * y


def _combine(y, dest_flat, x1, gw_t, mod_l, bsz, seq):
    n = x1.shape[0]
    tm = TM_COMB
    nt = seq // tm
    return pl.pallas_call(
        _combine_kernel,
        grid_spec=pltpu.PrefetchScalarGridSpec(
            num_scalar_prefetch=1,
            grid=(n // tm,),
            in_specs=[pl.BlockSpec((tm, D), lambda i, *_: (i, 0)),
                      pl.BlockSpec((tm, 2), lambda i, *_: (i, 0)),
                      pl.BlockSpec((6, None, 1, D), lambda i, *_: (0, i // nt, 0, 0)),
                      pl.BlockSpec(memory_space=pl.ANY)],
            out_specs=pl.BlockSpec((tm, D), lambda i, *_: (i, 0)),
            scratch_shapes=[pltpu.VMEM((2, 2, tm, D), f32), pltpu.SemaphoreType.DMA((2,))],
        ),
        out_shape=jax.ShapeDtypeStruct((n, D), f32),
        compiler_params=pltpu.CompilerParams(dimension_semantics=("arbitrary",)),
        name="moe_combine",
    )(dest_flat, x1, gw_t, mod_l, y)


def _moe_experts_sorted(h2, eidx, w1, w3, w2, layer):
    n = h2.shape[0]
    p_rows = (n * 2 // MOE_BLK + N_EXP) * MOE_BLK
    nb = p_rows // MOE_BLK
    rank, cnt = _ranks(eidx)
    counts = cnt[:, 0].astype(i32)
    padded = (counts + MOE_BLK - 1) // MOE_BLK * MOE_BLK
    pad_end = jnp.cumsum(padded)
    pad_start = pad_end - padded
    start_of = jnp.sum(jnp.where(eidx[..., None] == jnp.arange(N_EXP, dtype=i32), pad_start, 0), axis=-1)
    dest_flat = (start_of + rank).reshape(-1)
    nb_used = (pad_end[-1] // MOE_BLK).astype(i32)
    spare = nb_used + jnp.arange(N_EXP, dtype=i32)
    tails = jnp.concatenate([(pad_start + counts) // ZERO_ROWS, pad_end // ZERO_ROWS,
                             jnp.where(spare < nb, spare * MOE_BLK, -1)]).astype(i32)
    blk = jnp.minimum(jnp.arange(nb, dtype=i32), nb_used - 1)
    seg_done = (pad_end[None, :] <= (blk * MOE_BLK)[:, None]).astype(i32)
    blk_e = jnp.minimum(jnp.sum(seg_done, axis=1), N_EXP - 1).astype(i32)
    xb = _dispatch(h2, dest_flat, tails, p_rows)
    y = _experts(xb, blk_e, nb_used.reshape(1), w1, w3, w2, layer)
    return y, dest_flat


def kernel(x, c, w_ada, b_ada, attn_norm, ffn_norm, w_in, gla_gate_w, gla_gate_b, gla_out_norm,
           na_q_norm, na_k_norm, na_rpb, w_out, router_w, router_b, w1, w3, w2):
    bsz, seq, dim = x.shape
    depth = w_ada.shape[0]
    assert dim == D and seq == GRID_W * GRID_W, (dim, seq)
    assert bsz % GLA_NB == 0 and bsz <= 8, bsz
    assert seq % TM_PROJ == 0 and seq % TM_COMB_PROJ == 0 and seq % (NA_SUB * NA_QROWS * GRID_W) == 0
    assert (bsz * seq) % TM_DISP == 0 and (bsz * seq) % TM_COMB == 0 and (bsz * seq) % TM_RANK == 0
    assert MOE_BLK % ZERO_ROWS == 0 and (2 * bsz * seq) % MOE_BLK == 0
    assert w_in.shape == (depth, D, 2 * QK_W + 2 * V_W + 2 * GLA_RANK + 3 * NA_W), w_in.shape
    assert w1.shape == (depth, N_EXP, D, D_EXP) and router_w.shape == (D, N_EXP)
    mod = _adaln_mod(c, w_ada, b_ada)
    xc = x.reshape(bsz * seq, D)
    pending = None
    for l in range(depth):
        if pending is None:
            (qg, kg, vg, sg, la_f, la_b, qn, kn, vn), bounded_f, bounded_b = _inproj(
                xc, mod[l], attn_norm[l], w_in[l], gla_gate_w[l], gla_gate_b[l], bsz, seq)
        else:
            y, dest_flat, x1, gw_t = pending
            xc, (qg, kg, vg, sg, la_f, la_b, qn, kn, vn), bounded_f, bounded_b = _combine_inproj(
                y, dest_flat, x1, gw_t, mod[l - 1], mod[l], attn_norm[l], w_in[l], gla_gate_w[l],
                gla_gate_b[l], bsz, seq)
        o_f = _gla_dir(False, qg, kg, vg, la_f, bounded_f, bsz, seq)
        o_gla = _gla_dir(True, qg, kg, vg, la_b, bounded_b, bsz, seq, extra=(o_f, sg, gla_out_norm[l]))
        o_na = _natten(qn, kn, vn, na_q_norm[l], na_k_norm[l], na_rpb[l], bsz, seq)
        x1, h2, eidx, gw = _outproj(o_gla, o_na, w_out[l], xc, mod[l], ffn_norm[l],
                                    router_w, router_b, bsz, seq)
        y, dest_flat = _moe_experts_sorted(h2, eidx, w1, w3, w2, l)
        pending = (y, dest_flat, x1, gw.T)
    y, dest_flat, x1, gw_t = pending
    xc = _combine(y, dest_flat, x1, gw_t, mod[depth - 1], bsz, seq)
    return xc.reshape(bsz, seq, D)
```

```python
import functools

import numpy as np
import jax
import jax.numpy as jnp
from jax import lax
from jax.experimental import pallas as pl
from jax.experimental.pallas import tpu as pltpu

f32 = jnp.float32
bf16 = jnp.bfloat16
i32 = jnp.int32

D = 1024
GRID_W = 64
GLA_H, GLA_DK, GLA_DV = 4, 64, 128
GLA_RANK = 16
GLA_TAU = 16.0
CHUNK = 64
NA_H, NA_DH = 8, 64
WIN_H, WIN_W = 8, 16
N_EXP, N_GRP, EPG = 32, 4, 8
D_EXP = 512
QK_W = GLA_H * GLA_DK
V_W = GLA_H * GLA_DV
NA_W = NA_H * NA_DH
EPS = 1e-6
NEG = -1e30

TM_PROJ = 1024
TB_GLA = 256
GLA_NB = 4
NA_QROWS = 4
NA_KROWS = 12
NA_SUB = 8
TM_RANK = 1024
TM_DISP = 512
TM_COMB = 512
TM_COMB_PROJ = 512
MOE_BLK = 512
ZERO_ROWS = 64
N_LEVELS = 6
GLA_SAFE_RANGE = 40.0


def _nt(a, b):
    return lax.dot_general(a, b, (((1,), (1,)), ((), ())), preferred_element_type=f32)


def _tn(a, b):
    return lax.dot_general(a, b, (((0,), (0,)), ((), ())), preferred_element_type=f32)


def _split3(x):
    hi = x.astype(bf16)
    r = x - hi.astype(f32)
    mid = r.astype(bf16)
    lo = (r - mid.astype(f32)).astype(bf16)
    return hi, mid, lo


def _sel_l(m01, x):
    hi, mid, lo = _split3(x)
    d = lambda p: jnp.dot(m01, p, preferred_element_type=f32)
    return d(hi) + d(mid) + d(lo)


def _sel_r(x, m01):
    hi, mid, lo = _split3(x)
    d = lambda p: jnp.dot(p, m01, preferred_element_type=f32)
    return d(hi) + d(mid) + d(lo)


def _mod_kernel(c_ref, w_ref, b_ref, o_ref):
    c = c_ref[...]
    ca = c * jax.nn.sigmoid(c)
    w = w_ref[0]
    c_hi = ca.astype(bf16)
    c_lo = (ca - c_hi.astype(f32)).astype(bf16)
    w_hi = w.astype(bf16)
    acc = jnp.dot(c_hi, w_hi, preferred_element_type=f32) + jnp.dot(c_lo, w_hi, preferred_element_type=f32)
    o_ref[0, 0] = acc + b_ref[0, 0]


def _adaln_mod(c, w_ada, b_ada):
    depth = w_ada.shape[0]
    bsz = c.shape[0]
    cp = jnp.zeros((8, D), f32).at[:bsz].set(c)
    out = pl.pallas_call(
        _mod_kernel,
        grid=(depth, 6),
        in_specs=[pl.BlockSpec((8, D), lambda l, j: (0, 0)),
                  pl.BlockSpec((1, D, D), lambda l, j: (l, 0, j)),
                  pl.BlockSpec((1, 1, 1, D), lambda l, j: (l, j, 0, 0))],
        out_specs=pl.BlockSpec((1, 1, 8, D), lambda l, j: (l, j, 0, 0)),
        out_shape=jax.ShapeDtypeStruct((depth, 6, 8, D), f32),
        name="adaln_mod",
    )(cp, w_ada, b_ada.reshape(depth, 6, 1, D))
    return out[:, :, :bsz].reshape(depth, 6, bsz, 1, D)


def _log_sigmoid(z):
    return jnp.minimum(z, 0.0) - jnp.log1p(jnp.exp(-jnp.abs(z)))


def _inproj_body(x, mod_ref, nrm_ref, wm_ref, wlr_ref, gf_ref, bf_ref,
                 qg_ref, kg_ref, vg_ref, sg_ref, laf_ref, lab_ref, qn_ref, kn_ref, vn_ref, lamin_ref,
                 wg_scr=None):
    y = x * lax.rsqrt(jnp.mean(x * x, axis=-1, keepdims=True) + EPS) * nrm_ref[...]
    h = y * (1.0 + mod_ref[1]) + mod_ref[0]
    hb = h.astype(bf16)
    proj = lambda lo, hi: jnp.dot(hb, wm_ref[:, lo:hi], preferred_element_type=f32)
    qg_ref[...] = (proj(0, 256) * (GLA_DK ** -0.5)).astype(bf16)
    kg_ref[...] = proj(256, 512).astype(bf16)
    vg_ref[...] = proj(512, 1024).astype(bf16)
    gg = proj(1024, 1536)
    sg_ref[...] = (gg * jax.nn.sigmoid(gg)).astype(bf16)
    qn_ref[...] = proj(1536, 2048).astype(bf16)
    kn_ref[...] = proj(2048, 2560).astype(bf16)
    vn_ref[...] = proj(2560, 3072).astype(bf16)
    if wg_scr is None:
        lr = jnp.dot(hb, wlr_ref[...], preferred_element_type=f32)
        z = jnp.dot(lr, gf_ref[...], precision=lax.Precision.HIGHEST, preferred_element_type=f32)
    else:
        z = jnp.dot(hb, wg_scr[...], preferred_element_type=f32)
    z = z + bf_ref[...]
    la = _log_sigmoid(z) * (1.0 / GLA_TAU)
    la_f = la[:, 0:QK_W]
    la_b = la[:, QK_W:2 * QK_W]
    laf_ref[...] = la_f
    lab_ref[...] = la_b
    rows = []
    for la in (la_f, la_b):
        for blk in range(la.shape[0] // TB_GLA):
            tot = jnp.sum(la[blk * TB_GLA:(blk + 1) * TB_GLA], axis=0, keepdims=True)
            rows.append(jnp.broadcast_to(jnp.min(tot, axis=1, keepdims=True), (1, 128)))
    if len(rows) < 8:
        rows.append(jnp.zeros((8 - len(rows), 128), f32))
    lamin_ref[...] = jnp.concatenate(rows, axis=0)


def _inproj_kernel(x_ref, *refs):
    wlr_ref, gf_ref, wg_scr = refs[3], refs[4], refs[-1]

    @pl.when((pl.program_id(0) == 0) & (pl.program_id(1) == 0))
    def _():
        wg_scr[...] = jnp.dot(wlr_ref[...].astype(f32), gf_ref[...], precision=lax.Precision.HIGHEST,
                              preferred_element_type=f32).astype(bf16)

    _inproj_body(x_ref[...], *refs[:-1], wg_scr=wg_scr)


def _combine_inproj_kernel(dest_ref, x1_ref, gw_ref, modp_ref, y_ref, *refs):
    x2_ref = refs[6]
    ybuf, sems = refs[-2], refs[-1]
    proj_refs = refs[:6] + refs[7:-2]
    i = pl.program_id(0) * pl.num_programs(1) + pl.program_id(1)
    nsteps = pl.num_programs(0) * pl.num_programs(1)
    tm = x1_ref.shape[0]
    n = nsteps * tm

    nslot = ybuf.shape[0]

    def row_copies(j, slot, t):
        for k in range(2):
            d = dest_ref[k * n + j * tm + t]
            pltpu.make_async_copy(y_ref.at[pl.ds(d, 1)], ybuf.at[slot, k, pl.ds(t, 1)],
                                  sems.at[slot]).start(priority=k)

    def issue_tile(j, slot):
        for t in range(tm):
            row_copies(j, slot, t)

    def wait_tile(slot):
        for k in range(2):
            pltpu.make_async_copy(y_ref.at[pl.ds(0, tm)], ybuf.at[slot, k], sems.at[slot]).wait()

    slot = i % nslot

    @pl.when(i == 0)
    def _():
        for j in range(nslot - 1):
            lax.fori_loop(0, tm, lambda t, c, j=j: (row_copies(j, j, t), c)[1], 0)

    wait_tile(slot)
    gw = gw_ref[...]
    y = ybuf[slot, 0] * gw[:, 0:1] + ybuf[slot, 1] * gw[:, 1:2]
    x2 = x1_ref[...] + modp_ref[5] * y
    x2_ref[...] = x2
    ahead = i + nslot - 1
    issue_tile(jnp.where(ahead < nsteps, ahead, ahead - nsteps), ahead % nslot)
    _inproj_body(x2, *proj_refs)

    @pl.when(i == nsteps - 1)
    def _():
        for back in range(1, nslot):
            wait_tile((i + back) % nslot)


_PROJ_WIDTHS = [(QK_W, bf16), (QK_W, bf16), (V_W, bf16), (V_W, bf16), (QK_W, f32), (QK_W, f32),
                (NA_W, bf16), (NA_W, bf16), (NA_W, bf16)]


def _inproj_operands(nrm, w_in, gate_w, gate_b, imap):
    lo, hi = 2 * QK_W + 2 * V_W, 2 * QK_W + 2 * V_W + 2 * GLA_RANK
    wm = jnp.concatenate([w_in[:, :lo], w_in[:, hi:]], axis=1).astype(bf16)
    wlr = w_in[:, lo:hi].astype(bf16)
    zpad = jnp.zeros((GLA_RANK, QK_W), f32)
    gmat = jnp.concatenate([jnp.concatenate([gate_w[0], zpad], axis=1),
                            jnp.concatenate([zpad, gate_w[1]], axis=1)], axis=0)
    gbias = jnp.concatenate([gate_b[0], gate_b[1]]).reshape(1, 2 * QK_W)
    args = [nrm.reshape(1, D), wm, wlr, gmat, gbias]
    return args, [pl.BlockSpec(a.shape, imap) for a in args]


def _bounded_flags(lamin, tm):
    per = tm // TB_GLA
    lamin = lamin[:, :2 * per, 0]
    bounded_f = (lamin[:, :per].reshape(-1) > -GLA_SAFE_RANGE).astype(i32)
    bounded_b = (lamin[:, per:].reshape(-1) > -GLA_SAFE_RANGE).astype(i32)
    return bounded_f, bounded_b


def _inproj(x2d, mod_l, nrm, w_in, gate_w, gate_b, bsz, seq):
    n = x2d.shape[0]
    tm = TM_PROJ
    nt = seq // tm
    row = lambda b, i: (b * nt + i, 0)
    w_args, w_specs = _inproj_operands(nrm, w_in, gate_w, gate_b, lambda b, i: (0, 0))
    outs = pl.pallas_call(
        _inproj_kernel,
        grid=(bsz, nt),
        in_specs=[pl.BlockSpec((tm, D), row),
                  pl.BlockSpec((6, None, 1, D), lambda b, i: (0, b, 0, 0))] + w_specs,
        out_specs=[pl.BlockSpec((tm, w), row) for w, _ in _PROJ_WIDTHS]
        + [pl.BlockSpec((None, 8, 128), lambda b, i: (b * nt + i, 0, 0))],
        out_shape=[jax.ShapeDtypeStruct((n, w), dt) for w, dt in _PROJ_WIDTHS]
        + [jax.ShapeDtypeStruct((n // tm, 8, 128), f32)],
        scratch_shapes=[pltpu.VMEM((D, 2 * QK_W), bf16)],
        compiler_params=pltpu.CompilerParams(dimension_semantics=("arbitrary", "arbitrary")),
        name="inproj",
    )(x2d, mod_l, *w_args)
    return outs[:-1], *_bounded_flags(outs[-1], tm)


def _combine_inproj(y, dest_flat, x1, gw_t, mod_prev, mod_l, nrm, w_in, gate_w, gate_b, bsz, seq):
    n = x1.shape[0]
    tm = TM_COMB_PROJ
    nt = seq // tm
    row = lambda b, i, *_: (b * nt + i, 0)
    mods = lambda b, i, *_: (0, b, 0, 0)
    w_args, w_specs = _inproj_operands(nrm, w_in, gate_w, gate_b, lambda b, i, *_: (0, 0))
    outs = pl.pallas_call(
        _combine_inproj_kernel,
        grid_spec=pltpu.PrefetchScalarGridSpec(
            num_scalar_prefetch=1,
            grid=(bsz, nt),
            in_specs=[pl.BlockSpec((tm, D), row),
                      pl.BlockSpec((tm, 2), row),
                      pl.BlockSpec((6, None, 1, D), mods),
                      pl.BlockSpec(memory_space=pl.ANY),
                      pl.BlockSpec((6, None, 1, D), mods)] + w_specs,
            out_specs=[pl.BlockSpec((tm, D), row)]
            + [pl.BlockSpec((tm, w), row) for w, _ in _PROJ_WIDTHS]
            + [pl.BlockSpec((None, 8, 128), lambda b, i, *_: (b * nt + i, 0, 0))],
            scratch_shapes=[pltpu.VMEM((3, 2, tm, D), f32), pltpu.SemaphoreType.DMA((3,))],
        ),
        out_shape=[jax.ShapeDtypeStruct((n, D), f32)]
        + [jax.ShapeDtypeStruct((n, w), dt) for w, dt in _PROJ_WIDTHS]
        + [jax.ShapeDtypeStruct((n // tm, 8, 128), f32)],
        compiler_params=pltpu.CompilerParams(dimension_semantics=("arbitrary", "arbitrary"),
                                             vmem_limit_bytes=56 * 1024 * 1024),
        name="combine_inproj",
    )(dest_flat, x1, gw_t, mod_prev, y, mod_l, *w_args)
    return outs[0], outs[1:-1], *_bounded_flags(outs[-1], tm)


def _gla_consts(rev, tb):
    c = CHUNK
    idx = np.arange(c)
    if not rev:
        cum = (idx[None, :] <= idx[:, None]).astype(np.float32)
    else:
        cum = (idx[None, :] >= idx[:, None]).astype(np.float32)
    mats = [cum]
    masks = []
    for lvl in range(N_LEVELS):
        s = c >> (lvl + 1)
        blk = idx // (2 * s)
        second = (idx % (2 * s)) >= s
        ref_row = blk * 2 * s + (s if rev else s - 1)
        mats.append(cum[ref_row])
        same = blk[:, None] == blk[None, :]
        if not rev:
            m = same & second[:, None] & (~second[None, :])
        else:
            m = same & (~second[:, None]) & second[None, :]
        masks.append(m)
    masks.append(np.eye(c, dtype=bool) & (not rev))
    mst = np.concatenate(mats, axis=0)
    msk = np.stack([np.tile(m, (1, GLA_H)) for m in masks]).astype(np.float32)
    tix = np.arange(tb)
    tri = (tix[None, :] >= tix[:, None]) if rev else (tix[None, :] <= tix[:, None])
    cmask = (tix[:, None] < tix[None, :]) if rev else (tix[:, None] >= tix[None, :])
    nch = tb // c
    selb = np.zeros((tb, nch * 128), np.float32)
    for ch in range(nch):
        selb[ch * c:(ch + 1) * c, ch * 128:(ch + 1) * 128] = 1.0
    hk = np.arange(QK_W) // GLA_DK
    hv = np.arange(V_W) // GLA_DV
    kmask = (hk[:, None] == hk[None, :]).astype(np.float32)
    vmask = (hk[:, None] == hv[None, :]).astype(np.float32)
    return (jnp.asarray(mst, bf16), jnp.asarray(msk, f32), jnp.asarray(selb, bf16),
            jnp.asarray(kmask, bf16), jnp.asarray(vmask, bf16), jnp.asarray(vmask, f32),
            jnp.asarray(tri, bf16), jnp.asarray(cmask, f32), jnp.ones((tb, 128), bf16))


def _gla_kernel(rev, tb, *refs):
    if rev:
        (bounded_ref, q_ref, k_ref, v_ref, la_ref, mst_ref, msk_ref, selb_ref, kmask_ref, vmask_ref,
         smask_ref, tri_ref, cmask_ref, ones_ref, of_ref, sg_ref, gn_ref, out_ref, s_scr, o_scr) = refs
    else:
        (bounded_ref, q_ref, k_ref, v_ref, la_ref, mst_ref, msk_ref, selb_ref, kmask_ref, vmask_ref,
         smask_ref, tri_ref, cmask_ref, ones_ref, out_ref, s_scr) = refs
        o_scr = out_ref
    c = CHUNK
    nch = tb // c
    elems = range(GLA_NB)

    @pl.when(pl.program_id(1) == 0)
    def _():
        s_scr[...] = jnp.zeros_like(s_scr)

    def blocks_bounded():
        tri = tri_ref[...]
        cmask = cmask_ref[...]
        las = [la_ref[e] for e in elems]
        tots = [_sel_r(la.T, ones_ref[...]) for la in las]
        bcums = []
        for la in las:
            la_hi = la.astype(bf16)
            la_lo = (la - la_hi.astype(f32)).astype(bf16)
            bcums.append(jnp.dot(tri, la_hi, preferred_element_type=f32)
                         + jnp.dot(tri, la_lo, preferred_element_type=f32))
        ks = [k_ref[e].astype(f32) for e in elems]
        qes = [(q_ref[e].astype(f32) * jnp.exp(bcums[e])).astype(bf16) for e in elems]
        kxs = [(ks[e] * jnp.exp(-bcums[e])).astype(bf16) for e in elems]
        s_prevs = [s_scr[e] for e in elems]
        o_inters = [jnp.dot(qes[e], s_prevs[e].astype(bf16), preferred_element_type=f32) for e in elems]
        lane_head = lax.broadcasted_iota(i32, (tb, QK_W), 1) // GLA_DK
        for hd in range(GLA_H):
            cols = slice(hd * GLA_DV, (hd + 1) * GLA_DV)
            phs = []
            for e in elems:
                qh = jnp.where(lane_head == hd, qes[e], jnp.zeros_like(qes[e]))
                phs.append(jnp.where(cmask > 0.0, _nt(qh, kxs[e]), 0.0).astype(bf16))
            for e in elems:
                o_blk = o_inters[e][:, cols] + jnp.dot(phs[e], v_ref[e, :, cols], preferred_element_type=f32)
                o_scr[e, :, cols] = o_blk.astype(o_scr.dtype)
        for e in elems:
            blast = bcums[e][0:1] if rev else bcums[e][tb - 1:tb]
            ke = (ks[e] * jnp.exp(blast - bcums[e])).astype(bf16)
            dec = jnp.exp(tots[e])
            s_scr[e] = (s_prevs[e] * jnp.concatenate([dec] * GLA_H, axis=1)
                        + _tn(ke, v_ref[e]) * smask_ref[...])

    def chunks_robust():
        mst = mst_ref[...]
        kmask = kmask_ref[...]
        vmask = vmask_ref[...]
        for e in elems:
            tot = _sel_r(la_ref[e].T, selb_ref[...])
            for ch in (range(nch - 1, -1, -1) if rev else range(nch)):
                rows = slice(ch * c, (ch + 1) * c)
                q = q_ref[e, rows, :].astype(f32)
                k = k_ref[e, rows, :].astype(f32)
                v = v_ref[e, rows, :]
                r_all = _sel_l(mst, la_ref[e, rows, :])
                bcum = r_all[0:c]
                qe = (q * jnp.exp(bcum)).astype(bf16)
                o = jnp.dot(qe, s_scr[e].astype(bf16), preferred_element_type=f32)
                sall = jnp.zeros((c, QK_W), f32)
                for lvl in range(N_LEVELS):
                    ref = r_all[(lvl + 1) * c:(lvl + 2) * c]
                    a = (q * jnp.exp(jnp.minimum(bcum - ref, 0.0))).astype(bf16)
                    bm = (k * jnp.exp(jnp.minimum(ref - bcum, 0.0))).astype(bf16)
                    bbd = jnp.concatenate([bm] * GLA_H, axis=0) * kmask
                    sall = sall + jnp.where(msk_ref[lvl] > 0.0, _nt(a, bbd), 0.0)
                if not rev:
                    bbd = jnp.concatenate([k_ref[e, rows, :]] * GLA_H, axis=0) * kmask
                    sall = sall + jnp.where(msk_ref[N_LEVELS] > 0.0, _nt(q_ref[e, rows, :], bbd), 0.0)
                vbd = jnp.concatenate([v] * GLA_H, axis=0) * vmask
                o_blk = o + jnp.dot(sall.astype(bf16), vbd, preferred_element_type=f32)
                o_scr[e, rows, :] = o_blk.astype(o_scr.dtype)
                blast = bcum[0:1] if rev else bcum[c - 1:c]
                ke = (k * jnp.exp(blast - bcum)).astype(bf16)
                dec = jnp.exp(tot[:, ch * 128:(ch + 1) * 128])
                s_scr[e] = (s_scr[e] * jnp.concatenate([dec] * GLA_H, axis=1)
                            + _tn(ke, v) * smask_ref[...])

    nb = pl.num_programs(1)
    pos = nb - 1 - pl.program_id(1) if rev else pl.program_id(1)
    bounded = bounded_ref[(pl.program_id(0) * GLA_NB) * nb + pos] > 0
    for e in range(1, GLA_NB):
        bounded = bounded & (bounded_ref[(pl.program_id(0) * GLA_NB + e) * nb + pos] > 0)
    pl.when(bounded)(blocks_bounded)
    pl.when(jnp.logical_not(bounded))(chunks_robust)

    if rev:
        gn = gn_ref[...]
        for e in elems:
            for hd in range(GLA_H):
                cols = slice(hd * GLA_DV, (hd + 1) * GLA_DV)
                t = of_ref[e, :, cols].astype(f32) + o_scr[e, :, cols]
                y = t * lax.rsqrt(jnp.mean(t * t, axis=-1, keepdims=True) + EPS) * gn
                out_ref[e, :, cols] = (y * sg_ref[e, :, cols].astype(f32)).astype(bf16)


def _gla_dir(rev, q, k, v, la, bounded, bsz, seq, extra=None):
    n = q.shape[0]
    tb = TB_GLA
    nb = seq // tb
    consts = _gla_consts(rev, tb)
    if rev:
        row = lambda b, i, flags: (b, nb - 1 - i, 0)
    else:
        row = lambda b, i, flags: (b, i, 0)
    const2 = lambda b, i, flags: (0, 0)
    const3 = lambda b, i, flags: (0, 0, 0)
    per_batch = lambda a: a.reshape(bsz, seq, a.shape[-1])
    q, k, v, la = per_batch(q), per_batch(k), per_batch(v), per_batch(la)
    in_specs = [pl.BlockSpec((GLA_NB, tb, QK_W), row), pl.BlockSpec((GLA_NB, tb, QK_W), row),
                pl.BlockSpec((GLA_NB, tb, V_W), row), pl.BlockSpec((GLA_NB, tb, QK_W), row),
                pl.BlockSpec(consts[0].shape, const2), pl.BlockSpec(consts[1].shape, const3),
                pl.BlockSpec(consts[2].shape, const2), pl.BlockSpec(consts[3].shape, const2),
                pl.BlockSpec(consts[4].shape, const2), pl.BlockSpec(consts[5].shape, const2),
                pl.BlockSpec(consts[6].shape, const2), pl.BlockSpec(consts[7].shape, const2),
                pl.BlockSpec(consts[8].shape, const2)]
    args = [q, k, v, la, *consts]
    scratch = [pltpu.VMEM((GLA_NB, QK_W, V_W), f32)]
    if rev:
        o_f, sg, gn = extra
        in_specs += [pl.BlockSpec((GLA_NB, tb, V_W), row), pl.BlockSpec((GLA_NB, tb, V_W), row),
                     pl.BlockSpec((1, GLA_DV), const2)]
        args += [per_batch(o_f), per_batch(sg), gn.reshape(1, GLA_DV)]
        scratch.append(pltpu.VMEM((GLA_NB, tb, V_W), f32))
    out = pl.pallas_call(
        functools.partial(_gla_kernel, rev, tb),
        grid_spec=pltpu.PrefetchScalarGridSpec(
            num_scalar_prefetch=1,
            grid=(bsz // GLA_NB, nb),
            in_specs=in_specs,
            out_specs=pl.BlockSpec((GLA_NB, tb, V_W), row),
            scratch_shapes=scratch,
        ),
        out_shape=jax.ShapeDtypeStruct((bsz, seq, V_W), bf16),
        compiler_params=pltpu.CompilerParams(dimension_semantics=("arbitrary", "arbitrary")),
        name="gla_bwd" if rev else "gla_fwd",
    )(bounded, *args)
    return out.reshape(n, V_W)


def _rpb_expand_kernel(rpb_ref, sel_ref, valid_ref, o_ref):
    e = _sel_r(rpb_ref[...], sel_ref[...])
    o_ref[...] = jnp.where(valid_ref[...] > 0.0, e, NEG)


def _na_bias_table(rpb):
    nri, nci = 2 * WIN_H - 1, 2 * WIN_W - 1
    qc = np.arange(GRID_W)
    col_start = np.clip(qc - WIN_W // 2, 0, GRID_W - WIN_W)
    kc = np.arange(GRID_W)
    valid = (kc[None, :] >= col_start[:, None]) & (kc[None, :] < col_start[:, None] + WIN_W)
    ci = kc[None, :] - qc[:, None] + WIN_W - 1
    sel = np.zeros((32, GRID_W * GRID_W), np.float32)
    flat_ci = ci.reshape(-1)
    ok = (flat_ci >= 0) & (flat_ci < nci)
    sel[flat_ci[ok], np.nonzero(ok)[0]] = 1.0
    rpb2 = jnp.zeros((NA_H * nri + 8 - (NA_H * nri) % 8, 32), f32).at[:NA_H * nri, :nci].set(
        rpb.reshape(NA_H * nri, nci))
    nrow = rpb2.shape[0]
    e = pl.pallas_call(
        _rpb_expand_kernel,
        out_shape=jax.ShapeDtypeStruct((nrow, GRID_W * GRID_W), f32),
        name="rpb_expand",
    )(rpb2, jnp.asarray(sel, bf16), jnp.asarray(valid.reshape(1, -1), f32))
    return e[:NA_H * nri].reshape(NA_H // 2, 2, nri, GRID_W, GRID_W)


def _na_row_classes():
    rows = GRID_W
    ri = -np.ones((3, NA_QROWS, NA_KROWS), np.int64)
    for cls, rb in enumerate((0, 1, rows // NA_QROWS - 1)):
        r0 = rb * NA_QROWS
        kb = int(np.clip(r0 - WIN_H // 2, 0, rows - NA_KROWS))
        for rq in range(NA_QROWS):
            r = r0 + rq
            rs = int(np.clip(r - WIN_H // 2, 0, rows - WIN_H))
            for m in range(NA_KROWS):
                kr = kb + m
                if rs <= kr < rs + WIN_H:
                    ri[cls, rq, m] = kr - r + WIN_H - 1
    return ri


def _natten_kernel(q_ref, k_ref, v_ref, qn_ref, kn_ref, gsum_ref, e_ref, o_ref, kn_scr, tbl_scr):
    b = pl.program_id(1)
    rb = pl.program_id(2)
    gsum = gsum_ref[...]
    log2e = 1.4426950408889634

    def headnorm(x, g):
        sq = x * x
        hi = sq.astype(bf16)
        lo = (sq - hi.astype(f32)).astype(bf16)
        ms = (jnp.dot(hi, gsum, preferred_element_type=f32)
              + jnp.dot(lo, gsum, preferred_element_type=f32)) * (1.0 / NA_DH)
        return x * lax.rsqrt(ms + EPS) * g

    @pl.when((b == 0) & (rb == 0))
    def _():
        ri = _na_row_classes()
        neg = jnp.full((GRID_W, GRID_W), NEG, f32)
        for hh in range(2):
            for cls in range(3):
                for rq in range(NA_QROWS):
                    for mp in range(NA_KROWS // 2):
                        parts = []
                        for m in (2 * mp, 2 * mp + 1):
                            r = int(ri[cls, rq, m])
                            parts.append(neg if r < 0 else e_ref[hh, r] * log2e)
                        r0 = (hh * NA_QROWS + rq) * GRID_W
                        tbl_scr[cls, r0:r0 + GRID_W, mp * 128:(mp + 1) * 128] = (
                            jnp.concatenate(parts, axis=1))

    @pl.when(rb == 0)
    def _():
        kn_scr[...] = headnorm(k_ref[...].astype(f32), kn_ref[...]).astype(bf16)

    nq = NA_QROWS * GRID_W
    nk = NA_KROWS * GRID_W
    nblk = GRID_W // NA_QROWS
    first = lax.broadcasted_iota(i32, (nq, 128), 1) < NA_DH
    qs, wins, tbls = [], [], []
    for sub in range(NA_SUB):
        qb = rb * NA_SUB + sub
        q = headnorm(q_ref[sub * nq:(sub + 1) * nq, :].astype(f32), qn_ref[...]) * (NA_DH ** -0.5 * log2e)
        kb = jnp.clip(qb * NA_QROWS - WIN_H // 2, 0, GRID_W - NA_KROWS)
        start = pl.multiple_of(kb * GRID_W, GRID_W)
        cls = jnp.where(qb == 0, 0, jnp.where(qb == nblk - 1, 2, 1))
        qs.append([jnp.where(first, q, 0.0).astype(bf16), jnp.where(first, 0.0, q).astype(bf16)])
        wins.append((kn_scr[pl.ds(start, nk), :], v_ref[pl.ds(start, nk), :]))
        tbls.append(cls)
    scores = [[_nt(qs[sub][hh], wins[sub][0]) + tbl_scr[tbls[sub], hh * nq:(hh + 1) * nq, :]
               for hh in range(2)] for sub in range(NA_SUB)]
    for sub in range(NA_SUB):
        outs = []
        for s in scores[sub]:
            m = jnp.max(s, axis=-1, keepdims=True)
            p = jnp.exp2(s - m)
            l = jnp.sum(p, axis=-1, keepdims=True)
            outs.append(jnp.dot(p.astype(bf16), wins[sub][1], preferred_element_type=f32) / l)
        o_ref[sub * nq:(sub + 1) * nq, :] = jnp.where(first, outs[0], outs[1]).astype(bf16)


def _natten(qn, kn, vn, q_norm, k_norm, rpb, bsz, seq):
    n = qn.shape[0]
    e5 = _na_bias_table(rpb)
    nq = NA_QROWS * GRID_W
    nqs = NA_SUB * nq
    nrb = seq // nqs
    lane_h = np.arange(128) // NA_DH
    gsum = jnp.asarray(lane_h[:, None] == lane_h[None, :], bf16)
    qn2 = jnp.tile(q_norm.reshape(1, NA_DH), (1, 2))
    kn2 = jnp.tile(k_norm.reshape(1, NA_DH), (1, 2))
    const2 = lambda p, b, r: (0, 0)
    return pl.pallas_call(
        _natten_kernel,
        grid=(NA_H // 2, bsz, nrb),
        in_specs=[pl.BlockSpec((nqs, 128), lambda p, b, r: (b * nrb + r, p)),
                  pl.BlockSpec((seq, 128), lambda p, b, r: (b, p)),
                  pl.BlockSpec((seq, 128), lambda p, b, r: (b, p)),
                  pl.BlockSpec((1, 128), const2),
                  pl.BlockSpec((1, 128), const2),
                  pl.BlockSpec((128, 128), const2),
                  pl.BlockSpec((None,) + e5.shape[1:], lambda p, b, r: (p, 0, 0, 0, 0))],
        out_specs=pl.BlockSpec((nqs, 128), lambda p, b, r: (b * nrb + r, p)),
        out_shape=jax.ShapeDtypeStruct((n, NA_W), bf16),
        scratch_shapes=[pltpu.VMEM((seq, 128), bf16),
                        pltpu.VMEM((3, 2 * nq, NA_KROWS * GRID_W), f32)],
        compiler_params=pltpu.CompilerParams(dimension_semantics=("arbitrary", "arbitrary", "arbitrary")),
        name="natten",
    )(qn, kn, vn, qn2, kn2, gsum, e5)


def _top2(vals):
    io = lax.broadcasted_iota(i32, vals.shape, 0)
    m1 = jnp.max(vals, axis=0, keepdims=True)
    i1 = jnp.min(jnp.where(vals == m1, io, EPG), axis=0, keepdims=True)
    v2 = jnp.where(io == i1, -jnp.inf, vals)
    m2 = jnp.max(v2, axis=0, keepdims=True)
    i2 = jnp.min(jnp.where(v2 == m2, io, EPG), axis=0, keepdims=True)
    return m1, i1, m2, i2


def _outproj_kernel(og_ref, on_ref, wo_ref, x_ref, mod_ref, nrm_ref, rw_ref, rb_ref,
                    x1_ref, h2_ref, eidx_ref, gw_ref):
    mix = (jnp.dot(og_ref[...], wo_ref[0:V_W, :], preferred_element_type=f32)
           + jnp.dot(on_ref[...], wo_ref[V_W:V_W + NA_W, :], preferred_element_type=f32))
    x1 = x_ref[...] + mod_ref[2] * mix
    x1_ref[...] = x1
    y = x1 * lax.rsqrt(jnp.mean(x1 * x1, axis=-1, keepdims=True) + EPS) * nrm_ref[...]
    h2 = y * (1.0 + mod_ref[4]) + mod_ref[3]
    h2_ref[...] = h2
    rw = rw_ref[...]
    rw_hi = rw.astype(bf16)
    rw_lo = (rw - rw_hi.astype(f32)).astype(bf16)
    h_hi = h2.astype(bf16)
    h_lo = (h2 - h_hi.astype(f32)).astype(bf16)
    logits = _nt(rw_hi, h_hi) + _nt(rw_hi, h_lo) + _nt(rw_lo, h_hi)
    scores = jax.nn.sigmoid(logits)
    sel = scores + rb_ref[...]
    tops = [_top2(sel[g * EPG:(g + 1) * EPG]) for g in range(N_GRP)]
    best = jnp.zeros_like(tops[0][1])
    bs = tops[0][0] + tops[0][2]
    for g in range(1, N_GRP):
        gs = tops[g][0] + tops[g][2]
        take = gs > bs
        best = jnp.where(take, g, best)
        bs = jnp.where(take, gs, bs)
    io = lax.broadcasted_iota(i32, (EPG, sel.shape[1]), 0)
    i1 = jnp.zeros_like(best)
    i2 = jnp.zeros_like(best)
    s1 = jnp.zeros(best.shape, f32)
    s2 = jnp.zeros(best.shape, f32)
    for g in range(N_GRP):
        sc = scores[g * EPG:(g + 1) * EPG]
        pick = best == g
        i1 = jnp.where(pick, tops[g][1], i1)
        i2 = jnp.where(pick, tops[g][3], i2)
        s1 = jnp.where(pick, jnp.sum(jnp.where(io == tops[g][1], sc, 0.0), axis=0, keepdims=True), s1)
        s2 = jnp.where(pick, jnp.sum(jnp.where(io == tops[g][3], sc, 0.0), axis=0, keepdims=True), s2)
    eidx_ref[0:1, :] = best * EPG + i1
    eidx_ref[1:2, :] = best * EPG + i2
    tot = s1 + s2
    gw_ref[0:1, :] = s1 / tot
    gw_ref[1:2, :] = s2 / tot


def _outproj(o_gla, o_na, w_out, x2d, mod_l, nrm, router_w, router_b, bsz, seq):
    n = x2d.shape[0]
    tm = TM_PROJ
    nt = seq // tm
    row = lambda b, i: (b * nt + i, 0)
    col = lambda b, i: (0, b * nt + i)
    const = lambda b, i: (0, 0)
    return pl.pallas_call(
        _outproj_kernel,
        grid=(bsz, nt),
        in_specs=[pl.BlockSpec((tm, V_W), row),
                  pl.BlockSpec((tm, NA_W), row),
                  pl.BlockSpec((V_W + NA_W, D), const),
                  pl.BlockSpec((tm, D), row),
                  pl.BlockSpec((6, None, 1, D), lambda b, i: (0, b, 0, 0)),
                  pl.BlockSpec((1, D), const),
                  pl.BlockSpec((N_EXP, D), const),
                  pl.BlockSpec((N_EXP, 1), const)],
        out_specs=[pl.BlockSpec((tm, D), row), pl.BlockSpec((tm, D), row),
                   pl.BlockSpec((2, tm), col), pl.BlockSpec((2, tm), col)],
        out_shape=[jax.ShapeDtypeStruct((n, D), f32), jax.ShapeDtypeStruct((n, D), f32),
                   jax.ShapeDtypeStruct((2, n), i32), jax.ShapeDtypeStruct((2, n), f32)],
        compiler_params=pltpu.CompilerParams(dimension_semantics=("arbitrary", "arbitrary")),
        name="outproj_router",
    )(o_gla, o_na, w_out.astype(bf16), x2d, mod_l, nrm.reshape(1, D),
      router_w.T, router_b.reshape(N_EXP, 1))


def _rank_kernel(eidx_ref, tri_ref, rank_ref, cnt_ref, carry):
    @pl.when(pl.program_id(0) == 0)
    def _():
        carry[...] = jnp.zeros_like(carry)

    e = eidx_ref[...]
    tm = e.shape[1]
    io = lax.broadcasted_iota(i32, (N_EXP, tm), 0)
    run = carry[...]
    for k in range(2):
        oh = io == e[k:k + 1, :]
        ohf = oh.astype(f32)
        pre = jnp.dot(ohf.astype(bf16), tri_ref[...], preferred_element_type=f32) + run[:, 0:1]
        rank_ref[k:k + 1, :] = jnp.sum(jnp.where(oh, pre, 0.0), axis=0, keepdims=True).astype(i32)
        run = run + jnp.sum(ohf, axis=1, keepdims=True)
    carry[...] = run
    cnt_ref[...] = run


def _ranks(eidx):
    n = eidx.shape[1]
    tm = TM_RANK
    t = np.arange(tm)
    tri = jnp.asarray(t[:, None] < t[None, :], bf16)
    return pl.pallas_call(
        _rank_kernel,
        grid=(n // tm,),
        in_specs=[pl.BlockSpec((2, tm), lambda i: (0, i)), pl.BlockSpec((tm, tm), lambda i: (0, 0))],
        out_specs=[pl.BlockSpec((2, tm), lambda i: (0, i)), pl.BlockSpec((N_EXP, 128), lambda i: (0, 0))],
        out_shape=[jax.ShapeDtypeStruct((2, n), i32), jax.ShapeDtypeStruct((N_EXP, 128), f32)],
        scratch_shapes=[pltpu.VMEM((N_EXP, 128), f32)],
        compiler_params=pltpu.CompilerParams(dimension_semantics=("arbitrary",)),
        name="moe_rank",
    )(eidx, tri)


def _dispatch_kernel(dest_ref, tails_ref, h_ref, xb_ref, zbuf, hbuf, lsem, rsem, zsem, ssem):
    i = pl.program_id(0)
    nsteps = pl.num_programs(0)
    tm = TM_DISP
    n = nsteps * tm
    nbuf = hbuf.shape[0]

    def load(j):
        start = pl.multiple_of(j * tm, tm)
        return pltpu.make_async_copy(h_ref.at[pl.ds(start, tm)], hbuf.at[j % nbuf], lsem.at[j % nbuf])

    def drain_rows(j):
        for _ in range(2):
            pltpu.make_async_copy(hbuf.at[j % nbuf], xb_ref.at[pl.ds(0, tm)], rsem.at[j % nbuf]).wait()

    def pad_copy(chunk):
        start = pl.multiple_of(chunk * ZERO_ROWS, ZERO_ROWS)
        return pltpu.make_async_copy(zbuf.at[pl.ds(0, ZERO_ROWS)], xb_ref.at[pl.ds(start, ZERO_ROWS)], zsem)

    def spare_copy(j):
        start = pl.multiple_of(tails_ref[2 * N_EXP + j], MOE_BLK)
        return pltpu.make_async_copy(zbuf, xb_ref.at[pl.ds(start, MOE_BLK)], ssem)

    def for_pad_chunks(fn):
        for e in range(N_EXP):
            lax.fori_loop(tails_ref[e], tails_ref[N_EXP + e], lambda ch, c: (fn(ch), c)[1], 0)

    def for_spare_blocks(fn):
        for j in range(N_EXP):
            @pl.when(tails_ref[2 * N_EXP + j] >= 0)
            def _():
                fn(j)

    @pl.when(i == 0)
    def _():
        zbuf[...] = jnp.zeros_like(zbuf)
        for_pad_chunks(lambda ch: pad_copy(ch).start())
        for_spare_blocks(lambda j: spare_copy(j).start())
        for_pad_chunks(lambda ch: pad_copy(ch).wait())
        load(i).start()

    @pl.when(i >= nbuf - 1)
    def _():
        drain_rows(i + 1 - nbuf)

    @pl.when(i + 1 < nsteps)
    def _():
        load(i + 1).start()

    load(i).wait()
    slot = i % nbuf

    def issue(t, carry):
        src = hbuf.at[slot, pl.ds(t, 1)]
        for k in range(2):
            d = dest_ref[k * n + i * tm + t]
            pltpu.make_async_copy(src, xb_ref.at[pl.ds(d, 1)], rsem.at[slot]).start(priority=k)
        return carry

    lax.fori_loop(0, tm, issue, 0, unroll=True)

    @pl.when(i == nsteps - 1)
    def _():
        for back in range(nbuf - 2, -1, -1):
            drain_rows(i - back)
        for_spare_blocks(lambda j: spare_copy(j).wait())


def _dispatch(h2, dest_flat, tails, p_rows):
    n = h2.shape[0]
    tm = TM_DISP
    return pl.pallas_call(
        _dispatch_kernel,
        grid_spec=pltpu.PrefetchScalarGridSpec(
            num_scalar_prefetch=2,
            grid=(n // tm,),
            in_specs=[pl.BlockSpec(memory_space=pl.ANY)],
            out_specs=pl.BlockSpec(memory_space=pl.ANY),
            scratch_shapes=[pltpu.VMEM((MOE_BLK, D), f32), pltpu.VMEM((3, tm, D), f32),
                            pltpu.SemaphoreType.DMA((3,)), pltpu.SemaphoreType.DMA((3,)),
                            pltpu.SemaphoreType.DMA(()), pltpu.SemaphoreType.DMA(())],
        ),
        out_shape=jax.ShapeDtypeStruct((p_rows, D), f32),
        compiler_params=pltpu.CompilerParams(dimension_semantics=("arbitrary",)),
        name="moe_dispatch",
    )(dest_flat, tails, h2)


def _expert_kernel(blk_e_ref, nb_ref, xb_ref, w1_ref, w3_ref, w2_ref, y_ref, w1b, w3b, w2b):
    i = pl.program_id(0)

    @pl.when((i < nb_ref[0]) & ((i == 0) | (blk_e_ref[i] != blk_e_ref[jnp.maximum(i - 1, 0)])))
    def _():
        w1b[...] = w1_ref[...].astype(bf16)
        w3b[...] = w3_ref[...].astype(bf16)
        w2b[...] = w2_ref[...].astype(bf16)

    @pl.when(i < nb_ref[0])
    def _():
        x = xb_ref[...].astype(bf16)
        a = jnp.dot(x, w1b[...], preferred_element_type=f32)
        b = jnp.dot(x, w3b[...], preferred_element_type=f32)
        h = (a * jax.nn.sigmoid(a) * b).astype(bf16)
        y_ref[...] = jnp.dot(h, w2b[...], preferred_element_type=f32)

    @pl.when(i >= nb_ref[0])
    def _():
        y_ref[...] = jnp.zeros_like(y_ref)


def _experts(xb, blk_e, nb_used, w1, w3, w2, layer):
    p_rows = xb.shape[0]
    nb = p_rows // MOE_BLK

    def xmap(i, be, nbu):
        return (jnp.minimum(i, nbu[0] - 1), 0)

    def wmap(i, be, nbu):
        return (layer, be[i], 0, 0)

    return pl.pallas_call(
        _expert_kernel,
        grid_spec=pltpu.PrefetchScalarGridSpec(
            num_scalar_prefetch=2,
            grid=(nb,),
            in_specs=[pl.BlockSpec((MOE_BLK, D), xmap),
                      pl.BlockSpec((None, None, D, D_EXP), wmap),
                      pl.BlockSpec((None, None, D, D_EXP), wmap),
                      pl.BlockSpec((None, None, D_EXP, D), wmap)],
            out_specs=pl.BlockSpec((MOE_BLK, D), lambda i, be, nbu: (i, 0)),
            scratch_shapes=[pltpu.VMEM((D, D_EXP), bf16), pltpu.VMEM((D, D_EXP), bf16),
                            pltpu.VMEM((D_EXP, D), bf16)],
        ),
        out_shape=jax.ShapeDtypeStruct((p_rows, D), f32),
        compiler_params=pltpu.CompilerParams(dimension_semantics=("arbitrary",),
                                             vmem_limit_bytes=48 * 1024 * 1024),
        name="moe_experts",
    )(blk_e, nb_used, xb, w1, w3, w2)


def _combine_kernel(dest_ref, x1_ref, gw_ref, mod_ref, y_ref, o_ref, ybuf, sems):
    i = pl.program_id(0)
    nsteps = pl.num_programs(0)
    tm = x1_ref.shape[0]
    n = nsteps * tm

    def issue_tile(j):
        slot = j % 2

        def issue(t, carry):
            for k in range(2):
                d = dest_ref[k * n + j * tm + t]
                pltpu.make_async_copy(y_ref.at[pl.ds(d, 1)], ybuf.at[slot, k, pl.ds(t, 1)],
                                      sems.at[slot]).start(priority=k)
            return carry

        lax.fori_loop(0, tm, issue, 0, unroll=True)

    @pl.when(i == 0)
    def _():
        issue_tile(i)

    @pl.when(i + 1 < nsteps)
    def _():
        issue_tile(i + 1)

    slot = i % 2
    for k in range(2):
        pltpu.make_async_copy(y_ref.at[pl.ds(0, tm)], ybuf.at[slot, k], sems.at[slot]).wait()
    gw = gw_ref[...]
    y = ybuf[slot, 0] * gw[:, 0:1] + ybuf[slot, 1] * gw[:, 1:2]
    o_ref[...] = x1_ref[...] + mod_ref[5] * y


def _combine(y, dest_flat, x1, gw_t, mod_l, bsz, seq):
    n = x1.shape[0]
    tm = TM_COMB
    nt = seq // tm
    return pl.pallas_call(
        _combine_kernel,
        grid_spec=pltpu.PrefetchScalarGridSpec(
            num_scalar_prefetch=1,
            grid=(n // tm,),
            in_specs=[pl.BlockSpec((tm, D), lambda i, *_: (i, 0)),
                      pl.BlockSpec((tm, 2), lambda i, *_: (i, 0)),
                      pl.BlockSpec((6, None, 1, D), lambda i, *_: (0, i // nt, 0, 0)),
                      pl.BlockSpec(memory_space=pl.ANY)],
            out_specs=pl.BlockSpec((tm, D), lambda i, *_: (i, 0)),
            scratch_shapes=[pltpu.VMEM((2, 2, tm, D), f32), pltpu.SemaphoreType.DMA((2,))],
        ),
        out_shape=jax.ShapeDtypeStruct((n, D), f32),
        compiler_params=pltpu.CompilerParams(dimension_semantics=("arbitrary",)),
        name="moe_combine",
    )(dest_flat, x1, gw_t, mod_l, y)


def _moe_experts_sorted(h2, eidx, w1, w3, w2, layer):
    n = h2.shape[0]
    p_rows = (n * 2 // MOE_BLK + N_EXP) * MOE_BLK
    nb = p_rows // MOE_BLK
    rank, cnt = _ranks(eidx)
    counts = cnt[:, 0].astype(i32)
    padded = (counts + MOE_BLK - 1) // MOE_BLK * MOE_BLK
    pad_end = jnp.cumsum(padded)
    pad_start = pad_end - padded
    start_of = jnp.sum(jnp.where(eidx[..., None] == jnp.arange(N_EXP, dtype=i32), pad_start, 0), axis=-1)
    dest_flat = (start_of + rank).reshape(-1)
    nb_used = (pad_end[-1] // MOE_BLK).astype(i32)
    spare = nb_used + jnp.arange(N_EXP, dtype=i32)
    tails = jnp.concatenate([(pad_start + counts) // ZERO_ROWS, pad_end // ZERO_ROWS,
                             jnp.where(spare < nb, spare * MOE_BLK, -1)]).astype(i32)
    blk = jnp.minimum(jnp.arange(nb, dtype=i32), nb_used - 1)
    seg_done = (pad_end[None, :] <= (blk * MOE_BLK)[:, None]).astype(i32)
    blk_e = jnp.minimum(jnp.sum(seg_done, axis=1), N_EXP - 1).astype(i32)
    xb = _dispatch(h2, dest_flat, tails, p_rows)
    y = _experts(xb, blk_e, nb_used.reshape(1), w1, w3, w2, layer)
    return y, dest_flat


def kernel(x, c, w_ada, b_ada, attn_norm, ffn_norm, w_in, gla_gate_w, gla_gate_b, gla_out_norm,
           na_q_norm, na_k_norm, na_rpb, w_out, router_w, router_b, w1, w3, w2):
    bsz, seq, dim = x.shape
    depth = w_ada.shape[0]
    assert dim == D and seq == GRID_W * GRID_W, (dim, seq)
    assert bsz % GLA_NB == 0 and bsz <= 8, bsz
    assert seq % TM_PROJ == 0 and seq % TM_COMB_PROJ == 0 and seq % (NA_SUB * NA_QROWS * GRID_W) == 0
    assert (bsz * seq) % TM_DISP == 0 and (bsz * seq) % TM_COMB == 0 and (bsz * seq) % TM_RANK == 0
    assert MOE_BLK % ZERO_ROWS == 0 and (2 * bsz * seq) % MOE_BLK == 0
    assert w_in.shape == (depth, D, 2 * QK_W + 2 * V_W + 2 * GLA_RANK + 3 * NA_W), w_in.shape
    assert w1.shape == (depth, N_EXP, D, D_EXP) and router_w.shape == (D, N_EXP)
    mod = _adaln_mod(c, w_ada, b_ada)
    xc = x.reshape(bsz * seq, D)
    pending = None
    for l in range(depth):
        if pending is None:
            (qg, kg, vg, sg, la_f, la_b, qn, kn, vn), bounded_f, bounded_b = _inproj(
                xc, mod[l], attn_norm[l], w_in[l], gla_gate_w[l], gla_gate_b[l], bsz, seq)
        else:
            y, dest_flat, x1, gw_t = pending
            xc, (qg, kg, vg, sg, la_f, la_b, qn, kn, vn), bounded_f, bounded_b = _combine_inproj(
                y, dest_flat, x1, gw_t, mod[l - 1], mod[l], attn_norm[l], w_in[l], gla_gate_w[l],
                gla_gate_b[l], bsz, seq)
        o_f = _gla_dir(False, qg, kg, vg, la_f, bounded_f, bsz, seq)
        o_gla = _gla_dir(True, qg, kg, vg, la_b, bounded_b, bsz, seq, extra=(o_f, sg, gla_out_norm[l]))
        o_na = _natten(qn, kn, vn, na_q_norm[l], na_k_norm[l], na_rpb[l], bsz, seq)
        x1, h2, eidx, gw = _outproj(o_gla, o_na, w_out[l], xc, mod[l], ffn_norm[l],
                                    router_w, router_b, bsz, seq)
        y, dest_flat = _moe_experts_sorted(h2, eidx, w1, w3, w2, l)
        pending = (y, dest_flat, x1, gw.T)
    y, dest_flat, x1, gw_t = pending
    xc = _combine(y, dest_flat, x1, gw_t, mod[depth - 1], bsz, seq)
    return xc.reshape(bsz, seq, D)
```
